```python
import math
import jax
import jax.numpy as jnp
from jax import lax
import numpy as np

D_MODEL = 2048
BATCH = 4
SEQ = 2048
DEPTH = 2

PLE_DIM = 256
ROPE_THETA = 500000.0
NORM_EPS = 1e-6
ATTN_BLOCK = 128

CONV_DIM = 448
CONV_K = 3

SSM_DIM = 448
SSM_GROUP = 16
SSM_GROUPS = SSM_DIM // SSM_GROUP
SSM_STATE = 64

DIL_HEAD_DIM = 64
DIL_ROT = DIL_HEAD_DIM // 4
DIL_PATTERNS = ((128, 1), (512, 4), (2048, 16))
DIL_HEADS_PER_GROUP = 3
DIL_HEADS = DIL_HEADS_PER_GROUP * len(DIL_PATTERNS)
DIL_DIM = DIL_HEADS * DIL_HEAD_DIM

MLA_HEADS = 9
MLA_Q_RANK = 384
MLA_KV_RANK = 256
MLA_NOPE = 64
MLA_ROPE = 32
MLA_V = 64
MLA_QK = MLA_NOPE + MLA_ROPE
MLA_DIM = MLA_HEADS * MLA_V

D_MIX = CONV_DIM + SSM_DIM + DIL_DIM + MLA_DIM
IN_SPLITS = (CONV_DIM, CONV_DIM, CONV_DIM, SSM_DIM, DIL_DIM, DIL_DIM, DIL_DIM, MLA_Q_RANK, MLA_KV_RANK, MLA_ROPE)
D_IN = sum(IN_SPLITS)

MOE_GROUPS = 8
MOE_EPG = 8
N_EXPERTS = MOE_GROUPS * MOE_EPG
MOE_TOPK = 2
MOE_FF = 512
MOE_BLOCK = 128

kernel_name = 'hymba_style_hybrid_hmoe_trunk'


def rmsnorm(x, g):
    xf = x.astype(jnp.float32)
    y = xf * lax.rsqrt(jnp.mean(xf * xf, axis=-1, keepdims=True) + NORM_EPS)
    return (y * g.astype(jnp.float32)).astype(x.dtype)


def rope_cos_sin(positions, rot_dim):
    inv_freq = ROPE_THETA ** (-jnp.arange(0, rot_dim, 2, dtype=jnp.float32) / rot_dim)
    ang = positions.astype(jnp.float32)[..., None] * inv_freq
    return jnp.cos(ang), jnp.sin(ang)


def apply_rope(x, cos, sin):
    half = cos.shape[-1]
    c = cos[:, :, None, :].astype(x.dtype)
    s = sin[:, :, None, :].astype(x.dtype)
    x1 = x[..., :half]
    x2 = x[..., half:2 * half]
    return jnp.concatenate([x1 * c - x2 * s, x2 * c + x1 * s, x[..., 2 * half:]], axis=-1)


def short_conv_mixer(b, c, h, conv_w):
    u = c * h
    y = lax.conv_general_dilated(u, conv_w[:, None, :].astype(u.dtype), window_strides=(1,),
                                 padding=((CONV_K - 1, 0),), dimension_numbers=('NWC', 'WIO', 'NWC'),
                                 feature_group_count=CONV_DIM)
    return b * y


def s5_mixer(u, lam_re, lam_im, log_dt, b_re, b_im, c_re, c_im, d_skip, glu_w, glu_b):
    Bn, S, _ = u.shape
    f32 = jnp.float32
    uf = u.astype(f32)
    ug = uf.reshape(Bn, S, SSM_GROUPS, SSM_GROUP)
    lr = lam_re.astype(f32)
    li = lam_im.astype(f32)
    dt = jnp.exp(log_dt.astype(f32))[:, None]
    mag = jnp.exp(lr * dt)
    lbar_re = mag * jnp.cos(li * dt)
    lbar_im = mag * jnp.sin(li * dt)
    nr = lbar_re - 1.0
    ni = lbar_im
    den = lr * lr + li * li
    f_re = (nr * lr + ni * li) / den
    f_im = (ni * lr - nr * li) / den
    bu_re = jnp.einsum('bsgc,gpc->bsgp', ug, b_re.astype(f32))
    bu_im = jnp.einsum('bsgc,gpc->bsgp', ug, b_im.astype(f32))
    in_re = f_re * bu_re - f_im * bu_im
    in_im = f_re * bu_im + f_im * bu_re
    a_re = jnp.broadcast_to(lbar_re, in_re.shape)
    a_im = jnp.broadcast_to(lbar_im, in_im.shape)

    def combine(e1, e2):
        a1r, a1i, b1r, b1i = e1
        a2r, a2i, b2r, b2i = e2
        return (a2r * a1r - a2i * a1i, a2r * a1i + a2i * a1r,
                a2r * b1r - a2i * b1i + b2r, a2r * b1i + a2i * b1r + b2i)

    _, _, xr, xi = lax.associative_scan(combine, (a_re, a_im, in_re, in_im), axis=1)
    y = (jnp.einsum('bsgp,gcp->bsgc', xr, c_re.astype(f32))
         - jnp.einsum('bsgp,gcp->bsgc', xi, c_im.astype(f32)))
    y = y.reshape(Bn, S, SSM_DIM) + d_skip.astype(f32) * uf
    g = jax.nn.gelu(y)
    out = g * jax.nn.sigmoid(g @ glu_w.astype(f32) + glu_b.astype(f32))
    return out.astype(u.dtype)


def banded_causal_attention(q, k, v, steps):
    N, L, H, hd = q.shape
    blk = steps
    nb = -(-L // blk)
    Lp = nb * blk
    pad = Lp - L
    scale = 1.0 / math.sqrt(hd)
    qp = jnp.pad(q, ((0, 0), (0, pad), (0, 0), (0, 0)))
    kp = jnp.pad(k, ((0, 0), (blk, pad), (0, 0), (0, 0)))
    vp = jnp.pad(v, ((0, 0), (blk, pad), (0, 0), (0, 0)))
    qb = qp.reshape(N, nb, blk, H, hd)
    kb = jnp.concatenate([kp[:, :Lp].reshape(N, nb, blk, H, hd), kp[:, blk:].reshape(N, nb, blk, H, hd)], axis=2)
    vb = jnp.concatenate([vp[:, :Lp].reshape(N, nb, blk, H, hd), vp[:, blk:].reshape(N, nb, blk, H, hd)], axis=2)
    s = jnp.einsum('nbqhd,nbkhd->nbhqk', qb, kb).astype(jnp.float32) * scale
    qi = jnp.arange(blk)[:, None]
    ki = jnp.arange(2 * blk)[None, :]
    dist = qi + blk - ki
    key_abs = jnp.arange(nb)[:, None, None] * blk - blk + ki[None]
    valid = (dist >= 0)[None] & (dist <= steps)[None] & (key_abs >= 0)
    s = jnp.where(valid[None, :, None], s, -jnp.inf)
    lse = jax.nn.logsumexp(s, axis=-1)
    pr = jnp.exp(s - lse[..., None]).astype(v.dtype)
    o = jnp.einsum('nbhqk,nbkhd->nbqhd', pr, vb).reshape(N, Lp, H, hd)[:, :L]
    lse = lse.transpose(0, 1, 3, 2).reshape(N, Lp, H)[:, :L]
    return o, lse


def dilated_group_attention(q, k, v, dil, steps):
    Bn, S, H, hd = q.shape
    L = S // dil

    def to_res(t):
        return t.reshape(Bn, L, dil, H, hd).transpose(0, 2, 1, 3, 4).reshape(Bn * dil, L, H, hd)

    o, lse = banded_causal_attention(to_res(q), to_res(k), to_res(v), steps)
    o = o.reshape(Bn, dil, L, H, hd).transpose(0, 2, 1, 3, 4).reshape(Bn, S, H, hd)
    lse = lse.reshape(Bn, dil, L, H).transpose(0, 2, 1, 3).reshape(Bn, S, H)
    return o, lse


def dilated_mixer(q, k, v, cos, sin):
    Bn, S, _ = q.shape
    q = apply_rope(q.reshape(Bn, S, DIL_HEADS, DIL_HEAD_DIM), cos, sin)
    k = apply_rope(k.reshape(Bn, S, DIL_HEADS, DIL_HEAD_DIM), cos, sin)
    v = v.reshape(Bn, S, DIL_HEADS, DIL_HEAD_DIM)
    outs = []
    lses = []
    for g, (window, dil) in enumerate(DIL_PATTERNS):
        sl = slice(g * DIL_HEADS_PER_GROUP, (g + 1) * DIL_HEADS_PER_GROUP)
        o, lse = dilated_group_attention(q[:, :, sl], k[:, :, sl], v[:, :, sl], dil, window // dil)
        outs.append(o)
        lses.append(lse)
    alpha = jax.nn.softmax(jnp.stack(lses, axis=0), axis=0)
    o = jnp.stack(outs, axis=0) * alpha[..., None].astype(q.dtype)
    return o.transpose(1, 2, 0, 3, 4).reshape(Bn, S, DIL_DIM)


def causal_attention_blocks(q, k, v):
    Bn, S, H, dq = q.shape
    nb = S // ATTN_BLOCK
    scale = 1.0 / math.sqrt(dq)
    qb = q.reshape(Bn, nb, ATTN_BLOCK, H, dq).transpose(1, 0, 2, 3, 4)
    kpos = jnp.arange(S)

    def one_block(args):
        qblk, i = args
        s = jnp.einsum('bqhd,bkhd->bhqk', qblk, k).astype(jnp.float32) * scale
        qpos = i * ATTN_BLOCK + jnp.arange(ATTN_BLOCK)
        s = jnp.where(kpos[None, :] <= qpos[:, None], s, -jnp.inf)
        pr = jax.nn.softmax(s, axis=-1).astype(v.dtype)
        return jnp.einsum('bhqk,bkhd->bqhd', pr, v)

    o = lax.map(one_block, (qb, jnp.arange(nb)))
    return o.transpose(1, 0, 2, 3, 4).reshape(Bn, S, H, v.shape[-1])


def mla_mixer(cq, ckv, k_rope, q_norm_g, w_uq, kv_norm_g, w_ukv, cos, sin):
    Bn, S, _ = cq.shape
    q = (rmsnorm(cq, q_norm_g) @ w_uq).reshape(Bn, S, MLA_HEADS, MLA_QK)
    kv = (rmsnorm(ckv, kv_norm_g) @ w_ukv).reshape(Bn, S, MLA_HEADS, MLA_NOPE + MLA_V)
    q = jnp.concatenate([q[..., :MLA_NOPE], apply_rope(q[..., MLA_NOPE:], cos, sin)], axis=-1)
    kr = apply_rope(k_rope[:, :, None, :], cos, sin)
    k = jnp.concatenate([kv[..., :MLA_NOPE], jnp.broadcast_to(kr, (Bn, S, MLA_HEADS, MLA_ROPE))], axis=-1)
    v = kv[..., MLA_NOPE:]
    return causal_attention_blocks(q, k, v).reshape(Bn, S, MLA_DIM)


def hierarchical_moe(h, w_group, b_group, w_router, b_router, w_gate, w_up, w_down):
    T, D = h.shape
    g_logits = (h @ w_group).astype(jnp.float32) + b_group.astype(jnp.float32)
    g_prob = jax.nn.softmax(g_logits, axis=-1)
    g_top_p, g_top_i = lax.top_k(g_prob, 1)
    e_logits = ((h @ w_router).astype(jnp.float32) + b_router.astype(jnp.float32)).reshape(T, MOE_GROUPS, MOE_EPG)
    e_in = e_logits[jnp.arange(T), g_top_i[:, 0]]
    e_top_v, e_top_i = lax.top_k(e_in, MOE_TOPK)
    gates = jax.nn.softmax(e_top_v, axis=-1) * g_top_p
    expert_idx = g_top_i * MOE_EPG + e_top_i

    A = T * MOE_TOPK
    nblk = (A + N_EXPERTS * (MOE_BLOCK - 1) + MOE_BLOCK - 1) // MOE_BLOCK
    P = nblk * MOE_BLOCK
    flat_e = expert_idx.reshape(-1).astype(jnp.int32)
    order = jnp.argsort(flat_e)
    sorted_e = flat_e[order]
    tok = order // MOE_TOPK
    counts = jnp.bincount(flat_e, length=N_EXPERTS)
    padded = ((counts + MOE_BLOCK - 1) // MOE_BLOCK) * MOE_BLOCK
    pend = jnp.cumsum(padded)
    pstart = pend - padded
    start = jnp.cumsum(counts) - counts
    dest = pstart[sorted_e] + (jnp.arange(A) - start[sorted_e])
    buf_tok = jnp.zeros((P,), jnp.int32).at[dest].set(tok.astype(jnp.int32))
    block_expert = jnp.minimum(jnp.searchsorted(pend, jnp.arange(nblk) * MOE_BLOCK, side='right'), N_EXPERTS - 1)
    xb = h[buf_tok].reshape(nblk, MOE_BLOCK, D)

    def one_block(args):
        xblk, e = args
        return (jax.nn.silu(xblk @ w_gate[e]) * (xblk @ w_up[e])) @ w_down[e]

    yb = lax.map(one_block, (xb, block_expert)).reshape(P, D)
    contrib = yb[dest] * gates.reshape(-1)[order][:, None].astype(h.dtype)
    return jnp.zeros((T, D), h.dtype).at[tok].add(contrib)


def setup_inputs(seed: int = 0) -> dict:
    key = jax.random.key(seed)
    ks = iter(list(jax.random.split(key, 40)))
    f32 = jnp.float32

    def nrm(shape, scale):
        return jax.random.normal(next(ks), shape, f32) * scale

    def gain(shape):
        return 1.0 + 0.02 * jax.random.normal(next(ks), shape, f32)

    x = nrm((BATCH, SEQ, D_MODEL), 1.0)
    p = nrm((DEPTH, BATCH, SEQ, PLE_DIM), 1.0)
    positions = (jnp.arange(SEQ, dtype=jnp.int32)[None, :]
                 + jax.random.randint(next(ks), (BATCH, 1), 0, 1024, dtype=jnp.int32))
    norm_mix_g = gain((DEPTH, D_MODEL))
    w_in = nrm((DEPTH, D_MODEL, D_IN), D_MODEL ** -0.5)
    conv_w = nrm((DEPTH, CONV_K, CONV_DIM), CONV_K ** -0.5)
    ssm_lam_re = -0.5 * jnp.exp(nrm((DEPTH, SSM_GROUPS, SSM_STATE), 0.05))
    ssm_lam_im = math.pi * jnp.arange(SSM_STATE, dtype=f32) + nrm((DEPTH, SSM_GROUPS, SSM_STATE), 0.05)
    ssm_log_dt = jax.random.uniform(next(ks), (DEPTH, SSM_GROUPS), f32, math.log(1e-3), math.log(1e-1))
    ssm_b_re = nrm((DEPTH, SSM_GROUPS, SSM_STATE, SSM_GROUP), (2 * SSM_GROUP) ** -0.5)
    ssm_b_im = nrm((DEPTH, SSM_GROUPS, SSM_STATE, SSM_GROUP), (2 * SSM_GROUP) ** -0.5)
    ssm_c_re = nrm((DEPTH, SSM_GROUPS, SSM_GROUP, SSM_STATE), (2 * SSM_STATE) ** -0.5)
    ssm_c_im = nrm((DEPTH, SSM_GROUPS, SSM_GROUP, SSM_STATE), (2 * SSM_STATE) ** -0.5)
    ssm_d = nrm((DEPTH, SSM_DIM), 1.0)
    ssm_glu_w = nrm((DEPTH, SSM_DIM, SSM_DIM), SSM_DIM ** -0.5)
    ssm_glu_b = nrm((DEPTH, SSM_DIM), 0.02)
    mla_q_norm_g = gain((DEPTH, MLA_Q_RANK))
    mla_w_uq = nrm((DEPTH, MLA_Q_RANK, MLA_HEADS * MLA_QK), MLA_Q_RANK ** -0.5)
    mla_kv_norm_g = gain((DEPTH, MLA_KV_RANK))
    mla_w_ukv = nrm((DEPTH, MLA_KV_RANK, MLA_HEADS * (MLA_NOPE + MLA_V)), MLA_KV_RANK ** -0.5)
    w_out = nrm((DEPTH, D_MIX, D_MODEL), D_MIX ** -0.5)
    norm_ffn_g = gain((DEPTH, D_MODEL))
    w_group = nrm((DEPTH, D_MODEL, MOE_GROUPS), D_MODEL ** -0.5)
    b_group = nrm((DEPTH, MOE_GROUPS), 0.01)
    w_router = nrm((DEPTH, D_MODEL, N_EXPERTS), D_MODEL ** -0.5)
    b_router = nrm((DEPTH, N_EXPERTS), 0.01)
    moe_w_gate = nrm((DEPTH, N_EXPERTS, D_MODEL, MOE_FF), D_MODEL ** -0.5)
    moe_w_up = nrm((DEPTH, N_EXPERTS, D_MODEL, MOE_FF), D_MODEL ** -0.5)
    moe_w_down = nrm((DEPTH, N_EXPERTS, MOE_FF, D_MODEL), MOE_FF ** -0.5)
    norm_ple_g = gain((DEPTH, D_MODEL))
    ple_w_proj = nrm((DEPTH, PLE_DIM, D_MODEL), PLE_DIM ** -0.5)
    ple_w_gate = nrm((DEPTH, D_MODEL, D_MODEL), D_MODEL ** -0.5)
    final_norm_g = gain((D_MODEL,))
    return {'x': x, 'p': p, 'positions': positions, 'norm_mix_g': norm_mix_g, 'w_in': w_in,
            'conv_w': conv_w, 'ssm_lam_re': ssm_lam_re, 'ssm_lam_im': ssm_lam_im, 'ssm_log_dt': ssm_log_dt,
            'ssm_b_re': ssm_b_re, 'ssm_b_im': ssm_b_im, 'ssm_c_re': ssm_c_re, 'ssm_c_im': ssm_c_im,
            'ssm_d': ssm_d, 'ssm_glu_w': ssm_glu_w, 'ssm_glu_b': ssm_glu_b,
            'mla_q_norm_g': mla_q_norm_g, 'mla_w_uq': mla_w_uq, 'mla_kv_norm_g': mla_kv_norm_g,
            'mla_w_ukv': mla_w_ukv, 'w_out': w_out, 'norm_ffn_g': norm_ffn_g,
            'w_group': w_group, 'b_group': b_group, 'w_router': w_router, 'b_router': b_router,
            'moe_w_gate': moe_w_gate, 'moe_w_up': moe_w_up, 'moe_w_down': moe_w_down,
            'norm_ple_g': norm_ple_g, 'ple_w_proj': ple_w_proj, 'ple_w_gate': ple_w_gate,
            'final_norm_g': final_norm_g}


def reference(x, p, positions, norm_mix_g, w_in, conv_w, ssm_lam_re, ssm_lam_im, ssm_log_dt,
              ssm_b_re, ssm_b_im, ssm_c_re, ssm_c_im, ssm_d, ssm_glu_w, ssm_glu_b,
              mla_q_norm_g, mla_w_uq, mla_kv_norm_g, mla_w_ukv, w_out, norm_ffn_g,
              w_group, b_group, w_router, b_router, moe_w_gate, moe_w_up, moe_w_down,
              norm_ple_g, ple_w_proj, ple_w_gate, final_norm_g):
    Bn, S, D = x.shape
    cos_d, sin_d = rope_cos_sin(positions, DIL_ROT)
    cos_m, sin_m = rope_cos_sin(positions, MLA_ROPE)
    split_at = list(np.cumsum(IN_SPLITS)[:-1])
    h = x
    for i in range(DEPTH):
        hn = rmsnorm(h, norm_mix_g[i])
        z = hn @ w_in[i]
        cb, cc, ch, su, dq, dk, dv, mcq, mckv, mkr = jnp.split(z, split_at, axis=-1)
        y_conv = short_conv_mixer(cb, cc, ch, conv_w[i])
        y_ssm = s5_mixer(su, ssm_lam_re[i], ssm_lam_im[i], ssm_log_dt[i], ssm_b_re[i], ssm_b_im[i],
                         ssm_c_re[i], ssm_c_im[i], ssm_d[i], ssm_glu_w[i], ssm_glu_b[i])
        y_dil = dilated_mixer(dq, dk, dv, cos_d, sin_d)
        y_mla = mla_mixer(mcq, mckv, mkr, mla_q_norm_g[i], mla_w_uq[i], mla_kv_norm_g[i], mla_w_ukv[i], cos_m, sin_m)
        mix = jnp.concatenate([y_conv, y_ssm, y_dil, y_mla], axis=-1) @ w_out[i]
        h = h + mix
        hn = rmsnorm(h, norm_ffn_g[i])
        ffn = hierarchical_moe(hn.reshape(Bn * S, D), w_group[i], b_group[i], w_router[i], b_router[i],
                               moe_w_gate[i], moe_w_up[i], moe_w_down[i])
        h = h + ffn.reshape(Bn, S, D)
        hn = rmsnorm(h, norm_ple_g[i])
        h = h + (p[i].astype(h.dtype) @ ple_w_proj[i]) * jax.nn.sigmoid(hn @ ple_w_gate[i])
    return rmsnorm(h, final_norm_g)
```

```python
import functools
import math

import jax
import jax.numpy as jnp
from jax import lax
from jax.experimental import pallas as pl
from jax.experimental.pallas import tpu as pltpu

D_MODEL = 2048
PLE_DIM = 256
ROPE_THETA = 500000.0
NORM_EPS = 1e-6

CONV_DIM = 448
CONV_K = 3

SSM_DIM = 448
SSM_GROUP = 16
SSM_GROUPS = SSM_DIM // SSM_GROUP
SSM_STATE = 64
SSM_N = SSM_GROUPS * SSM_STATE
SSM_PAD = 512

DIL_HEAD_DIM = 64
DIL_ROT = DIL_HEAD_DIM // 4
DIL_PATTERNS = ((128, 1), (512, 4), (2048, 16))
DIL_HPG = 3
DIL_GW = DIL_HPG * DIL_HEAD_DIM
DIL_BLK = 128

MLA_HEADS = 9
MLA_Q_RANK = 384
MLA_KV_RANK = 256
MLA_NOPE = 64
MLA_ROPE = 32
MLA_V = 64
MLA_QK = MLA_NOPE + MLA_ROPE
MLA_HW = 128
MLA_VW = 640

MOE_GROUPS = 8
MOE_EPG = 8
N_EXPERTS = MOE_GROUPS * MOE_EPG
MOE_TOPK = 2
MOE_FF = 512
MOE_BLOCK = 128
ROUTE_W = 128

LANE = 128
ROW_SLABS = D_MODEL // LANE
NEG_BIG = -1e30

BF16 = jnp.bfloat16
F32 = jnp.float32


def _cparams(sem, vmem_mb):
    return pltpu.CompilerParams(dimension_semantics=sem, vmem_limit_bytes=vmem_mb * 1024 * 1024)


def _rms(x, g):
    ms = jnp.mean(x * x, axis=-1, keepdims=True)
    return (x * lax.rsqrt(ms + NORM_EPS)) * g


def _sigmoid(x):
    return 1.0 / (1.0 + jnp.exp(-x))


def _full(a):
    return pl.BlockSpec(a.shape, lambda *_: (0,) * a.ndim)


def _rows_to_slabs(ref, base, x):
    n = x.shape[0]
    for c in range(ROW_SLABS):
        ref[pl.ds(base + c, n, stride=ROW_SLABS), :] = x[:, c * LANE:(c + 1) * LANE]


def _slabs_to_rows(ref, base, n):
    return jnp.concatenate([ref[pl.ds(base + c, n, stride=ROW_SLABS), :] for c in range(ROW_SLABS)], axis=1)


def _in_proj_kernel(x_ref, g_ref, *refs, n_out):
    w_refs, o_refs = refs[:n_out], refs[n_out:]
    xn = _rms(x_ref[...], g_ref[...]).astype(BF16)
    for w_ref, o_ref in zip(w_refs, o_refs):
        o_ref[...] = jnp.dot(xn, w_ref[...], preferred_element_type=F32).astype(o_ref.dtype)


def _in_proj(h, g, weights, su_index, batch, seq, tm=256):
    n_s = seq // tm
    in_specs = [pl.BlockSpec((tm, D_MODEL), lambda b, i: (b * n_s + i, 0)), _full(g)]
    in_specs += [_full(w) for w in weights]
    out_shape, out_specs = [], []
    for k, w in enumerate(weights):
        n = w.shape[1]
        if k == su_index:
            out_shape.append(jax.ShapeDtypeStruct((seq, batch * n), F32))
            out_specs.append(pl.BlockSpec((tm, n), lambda b, i: (i, b)))
        else:
            out_shape.append(jax.ShapeDtypeStruct((batch * seq, n), F32))
            out_specs.append(pl.BlockSpec((tm, n), lambda b, i: (b * n_s + i, 0)))
    return pl.pallas_call(
        functools.partial(_in_proj_kernel, n_out=len(weights)),
        grid=(batch, n_s), in_specs=in_specs, out_specs=out_specs, out_shape=out_shape,
        compiler_params=_cparams(("parallel", "parallel"), 56), name="in_proj",
    )(h, g, *weights)


def _conv_kernel(cb_ref, cc_ref, ch_ref, w_ref, o_ref, u_ref, *, rows):
    seq = cb_ref.shape[0]
    u_ref[0:8, :] = jnp.zeros((8, CONV_DIM), F32)
    for r0 in range(0, seq, rows):
        u_ref[8 + r0:8 + r0 + rows, :] = cc_ref[r0:r0 + rows, :] * ch_ref[r0:r0 + rows, :]
    w0, w1, w2 = w_ref[0:1, :], w_ref[1:2, :], w_ref[2:3, :]
    for r0 in range(0, seq, rows):
        acc = (w2 * u_ref[8 + r0:8 + r0 + rows, :] + w1 * u_ref[7 + r0:7 + r0 + rows, :]
               + w0 * u_ref[6 + r0:6 + r0 + rows, :])
        o_ref[r0:r0 + rows, :] = cb_ref[r0:r0 + rows, :] * acc


def _conv_mixer(cb, cc, ch, conv_w, batch, seq):
    spec = pl.BlockSpec((seq, CONV_DIM), lambda b: (b, 0))
    return pl.pallas_call(
        functools.partial(_conv_kernel, rows=256),
        grid=(batch,), in_specs=[spec, spec, spec, _full(conv_w)], out_specs=spec,
        out_shape=jax.ShapeDtypeStruct((batch * seq, CONV_DIM), F32),
        scratch_shapes=[pltpu.VMEM((seq + 8, CONV_DIM), F32)],
        compiler_params=_cparams(("parallel",), 48), name="conv_mixer",
    )(cb, cc, ch, conv_w)


_SSM_CHUNKS = ((0, 512), (512, 512), (1024, 512), (1536, 256))


def _gelu_tanh(x):
    return 0.5 * x * (1.0 + jnp.tanh(math.sqrt(2.0 / math.pi) * (x + 0.044715 * (x * x * x))))


def _ssm_kernel(u_ref, bf_ref, are_ref, aim_ref, cc_ref, d_ref, gw_ref, gb_ref, o_ref, xs_ref, st_ref,
                *, batch, tc):
    @pl.when(pl.program_id(0) == 0)
    def _():
        st_ref[...] = jnp.zeros_like(st_ref)

    u = u_ref[...]
    xs_ref[...] = jnp.dot(u.astype(BF16), bf_ref[...], preferred_element_type=F32)
    per_tile = 8 // batch
    for c0, cw in _SSM_CHUNKS:
        re_l, im_l = slice(c0, c0 + cw), slice(SSM_N + c0, SSM_N + c0 + cw)
        ar = jnp.broadcast_to(are_ref[:, re_l], (batch, cw))
        ai = jnp.broadcast_to(aim_ref[:, re_l], (batch, cw))

        def body(k, carry, re_l=re_l, im_l=im_l, ar=ar, ai=ai):
            sr, si = carry
            rows = pl.ds(pl.multiple_of(k * 8, 8), 8)
            tile_r, tile_i = xs_ref[rows, re_l], xs_ref[rows, im_l]
            out_r, out_i = [], []
            for j in range(per_tile):
                step = slice(j * batch, (j + 1) * batch)
                sr, si = ar * sr - ai * si + tile_r[step], ar * si + ai * sr + tile_i[step]
                out_r.append(sr)
                out_i.append(si)
            xs_ref[rows, re_l] = jnp.concatenate(out_r, axis=0)
            xs_ref[rows, im_l] = jnp.concatenate(out_i, axis=0)
            return sr, si

        sr, si = lax.fori_loop(0, tc // per_tile, body, (st_ref[0:batch, re_l], st_ref[0:batch, im_l]))
        st_ref[0:batch, re_l] = sr
        st_ref[0:batch, im_l] = si
    y = jnp.dot(xs_ref[...].astype(BF16), cc_ref[...], preferred_element_type=F32) + d_ref[...] * u
    g = _gelu_tanh(y)
    o_ref[...] = g * _sigmoid(jnp.dot(g.astype(BF16), gw_ref[...], preferred_element_type=F32) + gb_ref[...])


def _ssm_mixer(su_tm, prm, batch, seq, tc=128):
    rows = tc * batch
    spec = pl.BlockSpec((rows, SSM_PAD), lambda c: (c, 0))
    args = (prm["bf"], prm["a_re"], prm["a_im"], prm["cc"], prm["d"], prm["gw"], prm["gb"])
    return pl.pallas_call(
        functools.partial(_ssm_kernel, batch=batch, tc=tc),
        grid=(seq // tc,), in_specs=[spec] + [_full(a) for a in args], out_specs=spec,
        out_shape=jax.ShapeDtypeStruct((seq * batch, SSM_PAD), F32),
        scratch_shapes=[pltpu.VMEM((rows, 2 * SSM_N), F32), pltpu.VMEM((8, 2 * SSM_N), F32)],
        compiler_params=_cparams(("arbitrary",), 48), name="ssm_mixer",
    )(su_tm, *args)


def _ssm_params(lam_re, lam_im, log_dt, b_re, b_im, c_re, c_im, d_skip, glu_w, glu_b):
    dt = jnp.exp(log_dt)[:, None]
    mag = jnp.exp(lam_re * dt)
    a_re = mag * jnp.cos(lam_im * dt)
    a_im = mag * jnp.sin(lam_im * dt)
    nr, ni = a_re - 1.0, a_im
    den = lam_re * lam_re + lam_im * lam_im
    f_re = (nr * lam_re + ni * lam_im) / den
    f_im = (ni * lam_re - nr * lam_im) / den
    bfr = f_re[:, :, None] * b_re - f_im[:, :, None] * b_im
    bfi = f_re[:, :, None] * b_im + f_im[:, :, None] * b_re
    eye = jnp.eye(SSM_GROUPS, dtype=F32)

    def in_blockdiag(m):
        return jnp.einsum("gpc,gh->gchp", m, eye).reshape(SSM_DIM, SSM_N)

    def out_blockdiag(m):
        return jnp.einsum("gcp,gh->gphc", m, eye).reshape(SSM_N, SSM_DIM)

    pad = SSM_PAD - SSM_DIM
    bf = jnp.concatenate([in_blockdiag(bfr), in_blockdiag(bfi)], axis=1)
    cc = jnp.concatenate([out_blockdiag(c_re), -out_blockdiag(c_im)], axis=0)
    return {
        "bf": jnp.pad(bf, ((0, pad), (0, 0))).astype(BF16),
        "cc": jnp.pad(cc, ((0, 0), (0, pad))).astype(BF16),
        "a_re": a_re.reshape(1, SSM_N), "a_im": a_im.reshape(1, SSM_N),
        "d": jnp.pad(d_skip, (0, pad)).reshape(1, SSM_PAD),
        "gw": jnp.pad(glu_w, ((0, pad), (0, pad))).astype(BF16),
        "gb": jnp.pad(glu_b, (0, pad)).reshape(1, SSM_PAD),
    }


def _rope_lanes(x, c, s1, s2, half, width):
    return x * c + pltpu.roll(x, width - half, 1) * s1 + pltpu.roll(x, half, 1) * s2


def _dil_kernel(q_ref, k_ref, v_ref, c_ref, s1_ref, s2_ref, o_ref, l_ref):
    length, width = q_ref.shape
    nb = length // DIL_BLK
    ncomb = width // DIL_HEAD_DIM
    lane = lax.broadcasted_iota(jnp.int32, (1, width), 1)
    cmask = [(lane // DIL_HEAD_DIM == c).astype(F32) for c in range(ncomb)]
    qi = lax.broadcasted_iota(jnp.int32, (ncomb * DIL_BLK, 2 * DIL_BLK), 0) % DIL_BLK
    ki = lax.broadcasted_iota(jnp.int32, (ncomb * DIL_BLK, 2 * DIL_BLK), 1)
    band = (ki >= qi) & (ki <= qi + DIL_BLK)

    def roped(ref, rows):
        return _rope_lanes(ref[rows, :], c_ref[rows, :], s1_ref[rows, :], s2_ref[rows, :], DIL_ROT // 2, width)

    def block(i, carry):
        rows_q = pl.ds(pl.multiple_of(i * DIL_BLK, DIL_BLK), DIL_BLK)
        rows_p = pl.ds(pl.multiple_of(jnp.maximum(i - 1, 0) * DIL_BLK, DIL_BLK), DIL_BLK)
        q = roped(q_ref, rows_q) * (1.0 / math.sqrt(DIL_HEAD_DIM))
        kcat = jnp.concatenate([roped(k_ref, rows_p), roped(k_ref, rows_q)], axis=0).astype(BF16)
        vcat = jnp.concatenate([v_ref[rows_p, :], v_ref[rows_q, :]], axis=0).astype(BF16)
        qs = jnp.concatenate([q * cm for cm in cmask], axis=0).astype(BF16)
        s = lax.dot_general(qs, kcat, (((1,), (1,)), ((), ())), preferred_element_type=F32)
        s = jnp.where(band & ((ki >= DIL_BLK) | (i > 0)), s, NEG_BIG)
        mx = jnp.max(s, axis=-1, keepdims=True)
        p = jnp.exp(s - mx)
        den = jnp.sum(p, axis=-1, keepdims=True)
        o = jnp.dot(p.astype(BF16), vcat, preferred_element_type=F32) / den
        lse = mx + jnp.log(den)
        out = jnp.zeros((DIL_BLK, width), F32)
        lout = jnp.zeros((DIL_BLK, width), F32)
        for c, cm in enumerate(cmask):
            out = out + cm * o[c * DIL_BLK:(c + 1) * DIL_BLK, :]
            lout = lout + cm * lse[c * DIL_BLK:(c + 1) * DIL_BLK, :]
        o_ref[rows_q, :] = out
        l_ref[rows_q, :] = lout
        return carry

    lax.fori_loop(0, nb, block, 0)


def _dil_group(q, k, v, tabs, dil, batch, seq):
    length = seq // dil
    nres = min(dil, 2)
    width = nres * DIL_GW
    view = lambda a: a.reshape(batch * length, dil * DIL_GW)
    spec = pl.BlockSpec((length, width), lambda b, r: (b, r))
    sds = jax.ShapeDtypeStruct((batch * length, dil * DIL_GW), F32)
    o, l = pl.pallas_call(
        _dil_kernel, grid=(batch, dil // nres), in_specs=[spec] * 6, out_specs=[spec, spec], out_shape=[sds, sds],
        compiler_params=_cparams(("parallel", "parallel"), 48), name=f"dil_attn_d{dil}",
    )(view(q), view(k), view(v), *[view(t) for t in tabs])
    return o.reshape(batch * seq, DIL_GW), l.reshape(batch * seq, DIL_GW)


def _dil_combine_kernel(o0, o1, o2, l0, l1, l2, y0, y1, y2):
    a, b, c = l0[...], l1[...], l2[...]
    mx = jnp.maximum(jnp.maximum(a, b), c)
    ea, eb, ec = jnp.exp(a - mx), jnp.exp(b - mx), jnp.exp(c - mx)
    inv = 1.0 / (ea + eb + ec)
    y0[...] = o0[...] * (ea * inv)
    y1[...] = o1[...] * (eb * inv)
    y2[...] = o2[...] * (ec * inv)


def _dil_combine(outs, lses, tm=1024):
    rows = outs[0].shape[0]
    spec = pl.BlockSpec((tm, DIL_GW), lambda i: (i, 0))
    sds = jax.ShapeDtypeStruct((rows, DIL_GW), F32)
    return pl.pallas_call(
        _dil_combine_kernel, grid=(rows // tm,), in_specs=[spec] * 6, out_specs=[spec] * 3,
        out_shape=[sds] * 3, compiler_params=_cparams(("parallel",), 32), name="dil_combine",
    )(*outs, *lses)


def _mla_prep_kernel(cq_ref, ckv_ref, kr_ref, gq_ref, gkv_ref, wq_ref, wk_ref, wv_ref, c_ref, s1_ref, s2_ref,
                     q_ref, k_ref, v_ref):
    c, s1, s2 = c_ref[...], s1_ref[...], s2_ref[...]
    rope = functools.partial(_rope_lanes, c=c, s1=s1, s2=s2, half=MLA_ROPE // 2, width=MLA_HW)
    qn = _rms(cq_ref[...], gq_ref[...]).astype(BF16)
    kvn = _rms(ckv_ref[...], gkv_ref[...]).astype(BF16)
    q = jnp.dot(qn, wq_ref[...], preferred_element_type=F32)
    kn = jnp.dot(kvn, wk_ref[...], preferred_element_type=F32)
    v_ref[...] = jnp.dot(kvn, wv_ref[...], preferred_element_type=F32).astype(BF16)
    kr = rope(kr_ref[...])
    scale = 1.0 / math.sqrt(MLA_QK)
    for h in range(MLA_HEADS):
        sl = slice(h * MLA_HW, (h + 1) * MLA_HW)
        q_ref[:, sl] = (rope(q[:, sl]) * scale).astype(BF16)
        k_ref[:, sl] = (kn[:, sl] + kr).astype(BF16)


def _mla_prep(cq, ckv, kr, gq, gkv, wq, wk, wv, tabs, tm=512):
    rows = cq.shape[0]

    def rspec(n):
        return pl.BlockSpec((tm, n), lambda i: (i, 0))

    hw = MLA_HEADS * MLA_HW
    return pl.pallas_call(
        _mla_prep_kernel, grid=(rows // tm,),
        in_specs=[rspec(MLA_Q_RANK), rspec(MLA_KV_RANK), rspec(MLA_HW), _full(gq), _full(gkv), _full(wq),
                  _full(wk), _full(wv), rspec(MLA_HW), rspec(MLA_HW), rspec(MLA_HW)],
        out_specs=[rspec(hw), rspec(hw), rspec(MLA_VW)],
        out_shape=[jax.ShapeDtypeStruct((rows, hw), BF16), jax.ShapeDtypeStruct((rows, hw), BF16),
                   jax.ShapeDtypeStruct((rows, MLA_VW), BF16)],
        compiler_params=_cparams(("parallel",), 48), name="mla_prep",
    )(cq, ckv, kr, gq, gkv, wq, wk, wv, *tabs)


def _mla_attn_kernel(q_ref, k_ref, v_ref, o_ref, *, tq):
    qi = pl.program_id(1)
    row = lax.broadcasted_iota(jnp.int32, (tq, tq), 0)
    col = lax.broadcasted_iota(jnp.int32, (tq, tq), 1)
    causal = col <= row
    lane = lax.broadcasted_iota(jnp.int32, (1, MLA_HW), 1)

    def head_out(h):
        q = q_ref[:, h * MLA_HW:(h + 1) * MLA_HW]
        v_l = slice((h // 2) * MLA_HW, (h // 2 + 1) * MLA_HW)

        def step(kj, carry, masked):
            m, l, acc = carry
            rows = pl.ds(pl.multiple_of(kj * tq, tq), tq)
            k = k_ref[rows, h * MLA_HW:(h + 1) * MLA_HW]
            s = lax.dot_general(q, k, (((1,), (1,)), ((), ())), preferred_element_type=F32)
            if masked:
                s = jnp.where(causal, s, NEG_BIG)
            mn = jnp.maximum(m, jnp.max(s, axis=-1, keepdims=True))
            alpha = jnp.exp(m - mn)
            p = jnp.exp(s - mn)
            l = alpha * l + jnp.sum(p, axis=-1, keepdims=True)
            acc = alpha * acc + jnp.dot(p.astype(BF16), v_ref[rows, v_l], preferred_element_type=F32)
            return mn, l, acc

        init = (jnp.full((tq, 1), NEG_BIG, F32), jnp.zeros((tq, 1), F32), jnp.zeros((tq, MLA_HW), F32))
        carry = lax.fori_loop(0, qi, functools.partial(step, masked=False), init)
        _, l, acc = step(qi, carry, True)
        return acc / l

    for pair in range(MLA_VW // MLA_HW):
        o = head_out(2 * pair)
        if 2 * pair + 1 < MLA_HEADS:
            o = jnp.where(lane < MLA_V, o, head_out(2 * pair + 1))
        o_ref[:, pair * MLA_HW:(pair + 1) * MLA_HW] = o


def _mla_attn(q, k, v, batch, seq, tq=256):
    hw = MLA_HEADS * MLA_HW
    n_q = seq // tq
    return pl.pallas_call(
        functools.partial(_mla_attn_kernel, tq=tq), grid=(batch, n_q),
        in_specs=[pl.BlockSpec((tq, hw), lambda b, i: (b * n_q + i, 0)),
                  pl.BlockSpec((seq, hw), lambda b, i: (b, 0)),
                  pl.BlockSpec((seq, MLA_VW), lambda b, i: (b, 0))],
        out_specs=pl.BlockSpec((tq, MLA_VW), lambda b, i: (b * n_q + i, 0)),
        out_shape=jax.ShapeDtypeStruct((batch * seq, MLA_VW), F32),
        compiler_params=_cparams(("parallel", "arbitrary"), 48), name="mla_attn",
    )(q, k, v)


def _out_proj_kernel(yc_ref, ys_ref, yd0_ref, yd1_ref, yd2_ref, ym_ref, h_ref,
                     wc_ref, ws_ref, wd0_ref, wd1_ref, wd2_ref, wm_ref,
                     gf_ref, wrh_ref, wrl_ref, br_ref,
                     h1_ref, xn_ref, ri_ref, rg_ref):
    def mm(y_ref, w_ref):
        return jnp.dot(y_ref[...].astype(BF16), w_ref[...], preferred_element_type=F32)

    mix = (mm(yc_ref, wc_ref) + mm(ys_ref, ws_ref) + mm(yd0_ref, wd0_ref) + mm(yd1_ref, wd1_ref)
           + mm(yd2_ref, wd2_ref) + mm(ym_ref, wm_ref))
    h1 = h_ref[...] + mix
    h1_ref[...] = h1
    hn = _rms(h1, gf_ref[...])
    hi = hn.astype(BF16)
    hi32 = hi.astype(F32)
    lo = (hn - hi32).astype(BF16)
    _rows_to_slabs(xn_ref, 0, hi32)
    logits = (jnp.dot(hi, wrh_ref[...], preferred_element_type=F32)
              + jnp.dot(hi, wrl_ref[...], preferred_element_type=F32)
              + jnp.dot(lo, wrh_ref[...], preferred_element_type=F32)) + br_ref[...]
    lane = lax.broadcasted_iota(jnp.int32, logits.shape, 1)
    gl = jnp.where(lane < MOE_GROUPS, logits, NEG_BIG)
    gmax = jnp.max(gl, axis=-1, keepdims=True)
    g_top_p = 1.0 / jnp.sum(jnp.exp(gl - gmax), axis=-1, keepdims=True)
    g_idx = jnp.min(jnp.where(gl == gmax, lane, ROUTE_W), axis=-1, keepdims=True)
    in_group = (lane >= MOE_GROUPS) & (lane < MOE_GROUPS + N_EXPERTS) & (((lane - MOE_GROUPS) >> 3) == g_idx)
    el = jnp.where(in_group, logits, NEG_BIG)
    v1 = jnp.max(el, axis=-1, keepdims=True)
    i1 = jnp.min(jnp.where(el == v1, lane, ROUTE_W), axis=-1, keepdims=True)
    el2 = jnp.where(lane == i1, NEG_BIG, el)
    v2 = jnp.max(el2, axis=-1, keepdims=True)
    i2 = jnp.min(jnp.where(el2 == v2, lane, ROUTE_W), axis=-1, keepdims=True)
    e2 = jnp.exp(v2 - v1)
    w1 = g_top_p / (1.0 + e2)
    w2 = g_top_p * e2 / (1.0 + e2)
    ri_ref[...] = jnp.where(lane == 0, i1 - MOE_GROUPS, jnp.where(lane == 1, i2 - MOE_GROUPS, 0))
    rg_ref[...] = jnp.where(lane == 0, w1, jnp.where(lane == 1, w2, 0.0))


def _out_proj(ys, h, ws, gf, wrh, wrl, br, batch, seq, tm=256):
    rows = batch * seq
    n_s = seq // tm
    yc, ys_tm, yd0, yd1, yd2, ym = ys

    def rspec(n):
        return pl.BlockSpec((tm, n), lambda b, i: (b * n_s + i, 0))

    in_specs = [rspec(CONV_DIM), pl.BlockSpec((tm, SSM_PAD), lambda b, i: (i, b)), rspec(DIL_GW), rspec(DIL_GW),
                rspec(DIL_GW), rspec(MLA_VW), rspec(D_MODEL)]
    in_specs += [_full(w) for w in ws] + [_full(gf), _full(wrh), _full(wrl), _full(br)]
    return pl.pallas_call(
        _out_proj_kernel, grid=(batch, n_s), in_specs=in_specs,
        out_specs=[rspec(D_MODEL), pl.BlockSpec((tm * ROW_SLABS, LANE), lambda b, i: (b * n_s + i, 0)),
                   rspec(ROUTE_W), rspec(ROUTE_W)],
        out_shape=[jax.ShapeDtypeStruct((rows, D_MODEL), F32), jax.ShapeDtypeStruct((rows * ROW_SLABS, LANE), F32),
                   jax.ShapeDtypeStruct((rows, ROUTE_W), jnp.int32), jax.ShapeDtypeStruct((rows, ROUTE_W), F32)],
        compiler_params=_cparams(("parallel", "parallel"), 48), name="out_proj_router",
    )(yc, ys_tm, yd0, yd1, yd2, ym, h, *ws, gf, wrh, wrl, br)


def _moe_plan(route_i, n_tok):
    n_asg = n_tok * MOE_TOPK
    nblk = (n_asg + N_EXPERTS * (MOE_BLOCK - 1) + MOE_BLOCK - 1) // MOE_BLOCK
    flat_e = route_i[:, :MOE_TOPK].reshape(-1)
    onehot = (flat_e[:, None] == jnp.arange(N_EXPERTS, dtype=jnp.int32)[None, :]).astype(jnp.int32)
    csum = jnp.cumsum(onehot, axis=0)
    rank = jnp.take_along_axis(csum, flat_e[:, None], axis=1)[:, 0] - 1
    counts = csum[-1]
    padded = ((counts + MOE_BLOCK - 1) // MOE_BLOCK) * MOE_BLOCK
    pend = jnp.cumsum(padded)
    dest = (pend - padded)[flat_e] + rank
    slot_asg = jnp.full((nblk * MOE_BLOCK,), -1, jnp.int32).at[dest].set(jnp.arange(n_asg, dtype=jnp.int32))
    block_expert = jnp.minimum(
        jnp.searchsorted(pend, jnp.arange(nblk, dtype=jnp.int32) * MOE_BLOCK, side="right"), N_EXPERTS - 1
    ).astype(jnp.int32)
    n_used = (pend[-1] // MOE_BLOCK).astype(jnp.int32).reshape(1)
    return block_expert, slot_asg, n_used, nblk


def _moe_kernel(be_ref, asg_ref, nu_ref, x_hbm, wg_ref, wu_ref, wd_ref, y_hbm,
                xbuf, ybuf, wgb, wub, wdb, gsem, ssem, *, n_tok):
    i = pl.program_id(0)
    n_used = nu_ref[0]
    slot = i % 2
    buf_rows = MOE_BLOCK * ROW_SLABS

    def slab_rows(row):
        return pl.ds(pl.multiple_of(row * ROW_SLABS, ROW_SLABS), ROW_SLABS)

    def gather(blk, sl):
        for r in range(MOE_BLOCK):
            tok = jnp.maximum(asg_ref[blk * MOE_BLOCK + r], 0) >> 1
            yield pltpu.make_async_copy(x_hbm.at[slab_rows(tok), :], xbuf.at[slab_rows(sl * MOE_BLOCK + r), :],
                                        gsem.at[sl])

    def scatter(blk, sl):
        for r in range(MOE_BLOCK):
            a = asg_ref[blk * MOE_BLOCK + r]
            dst = jnp.where(a >= 0, (a & 1) * n_tok + (a >> 1), MOE_TOPK * n_tok + sl * MOE_BLOCK + r)
            yield pltpu.make_async_copy(ybuf.at[slab_rows(sl * MOE_BLOCK + r), :], y_hbm.at[slab_rows(dst), :],
                                        ssem.at[sl])

    @pl.when(i == 0)
    def _():
        ybuf[...] = jnp.zeros_like(ybuf)
        dump = pltpu.make_async_copy(ybuf, y_hbm.at[pl.ds(MOE_TOPK * n_tok * ROW_SLABS, 2 * buf_rows), :], ssem.at[0])
        dump.start()
        dump.wait()

    @pl.when((i == 0) & (n_used > 0))
    def _():
        for cp in gather(0, 0):
            cp.start()

    @pl.when(i < n_used)
    def _():
        for cp in gather(i, slot):
            cp.wait()

    @pl.when(i + 1 < n_used)
    def _():
        for cp in gather(i + 1, 1 - slot):
            cp.start()

    @pl.when((i < n_used) & ((i == 0) | (be_ref[i] != be_ref[jnp.maximum(i - 1, 0)])))
    def _():
        wgb[...] = wg_ref[...].astype(BF16)
        wub[...] = wu_ref[...].astype(BF16)
        wdb[...] = wd_ref[...].astype(BF16)

    @pl.when((i < n_used) & (i >= 2))
    def _():
        for cp in scatter(i - 2, slot):
            cp.wait()

    @pl.when(i < n_used)
    def _():
        x = _slabs_to_rows(xbuf, slot * buf_rows, MOE_BLOCK).astype(BF16)
        a = jnp.dot(x, wgb[...], preferred_element_type=F32)
        b = jnp.dot(x, wub[...], preferred_element_type=F32)
        act = (a * _sigmoid(a) * b).astype(BF16)
        _rows_to_slabs(ybuf, slot * buf_rows, jnp.dot(act, wdb[...], preferred_element_type=F32))
        for cp in scatter(i, slot):
            cp.start()

    @pl.when(i == n_used - 1)
    def _():
        for cp in scatter(i, slot):
            cp.wait()

    @pl.when((i == n_used - 1) & (i >= 1))
    def _():
        for cp in scatter(i - 1, 1 - slot):
            cp.wait()


def _moe_experts(xn, block_expert, slot_asg, n_used, nblk, w_gate, w_up, w_down):
    n_tok = xn.shape[0] // ROW_SLABS
    buf_rows = 2 * MOE_BLOCK * ROW_SLABS
    grid_spec = pltpu.PrefetchScalarGridSpec(
        num_scalar_prefetch=3, grid=(nblk,),
        in_specs=[pl.BlockSpec(memory_space=pl.ANY),
                  pl.BlockSpec((None, D_MODEL, MOE_FF), lambda i, be, asg, nu: (be[i], 0, 0)),
                  pl.BlockSpec((None, D_MODEL, MOE_FF), lambda i, be, asg, nu: (be[i], 0, 0)),
                  pl.BlockSpec((None, MOE_FF, D_MODEL), lambda i, be, asg, nu: (be[i], 0, 0))],
        out_specs=pl.BlockSpec(memory_space=pl.ANY),
        scratch_shapes=[pltpu.VMEM((buf_rows, LANE), F32), pltpu.VMEM((buf_rows, LANE), F32),
                        pltpu.VMEM((D_MODEL, MOE_FF), BF16), pltpu.VMEM((D_MODEL, MOE_FF), BF16),
                        pltpu.VMEM((MOE_FF, D_MODEL), BF16),
                        pltpu.SemaphoreType.DMA((2,)), pltpu.SemaphoreType.DMA((2,))])
    return pl.pallas_call(
        functools.partial(_moe_kernel, n_tok=n_tok), grid_spec=grid_spec,
        out_shape=jax.ShapeDtypeStruct(((MOE_TOPK * n_tok + 2 * MOE_BLOCK) * ROW_SLABS, LANE), F32),
        compiler_params=_cparams(("arbitrary",), 56), name="moe_experts",
    )(block_expert, slot_asg, n_used, xn, w_gate, w_up, w_down)


def _ple_kernel(h1_ref, y0_ref, y1_ref, rg_ref, p_ref, gp_ref, wg_ref, wp_ref, gfin_ref, o_ref, *, final):
    rg = rg_ref[...]
    tm = rg.shape[0]
    ffn = rg[:, 0:1] * _slabs_to_rows(y0_ref, 0, tm) + rg[:, 1:2] * _slabs_to_rows(y1_ref, 0, tm)
    h2 = h1_ref[...] + ffn
    hn = _rms(h2, gp_ref[...]).astype(BF16)
    gate = _sigmoid(jnp.dot(hn, wg_ref[...], preferred_element_type=F32))
    pe = jnp.dot(p_ref[...].astype(BF16), wp_ref[...], preferred_element_type=F32)
    h3 = h2 + pe * gate
    o_ref[...] = _rms(h3, gfin_ref[...]) if final else h3


def _ple(h1, y2, rg, p, gp, wg, wp, gfin, final, tm=256):
    rows = h1.shape[0]
    n_t = rows // tm

    def rspec(n):
        return pl.BlockSpec((tm, n), lambda i: (i, 0))

    def yspec(k):
        return pl.BlockSpec((tm * ROW_SLABS, LANE), lambda i: (k * n_t + i, 0))

    return pl.pallas_call(
        functools.partial(_ple_kernel, final=final), grid=(n_t,),
        in_specs=[rspec(D_MODEL), yspec(0), yspec(1), rspec(ROUTE_W), rspec(PLE_DIM), _full(gp), _full(wg),
                  _full(wp), _full(gfin)],
        out_specs=rspec(D_MODEL), out_shape=jax.ShapeDtypeStruct((rows, D_MODEL), F32),
        compiler_params=_cparams(("parallel",), 48), name="moe_combine_ple",
    )(h1, y2, y2, rg, p, gp, wg, wp, gfin)


def _rope_tables(positions, rot_dim, lead, period, reps):
    half = rot_dim // 2
    inv_freq = ROPE_THETA ** (-jnp.arange(0, rot_dim, 2, dtype=F32) / rot_dim)
    ang = positions.astype(F32).reshape(-1, 1) * inv_freq
    cos, sin = jnp.cos(ang), jnp.sin(ang)
    n = cos.shape[0]
    tail = period - lead - rot_dim
    c = jnp.concatenate([jnp.ones((n, lead), F32), cos, cos, jnp.ones((n, tail), F32)], axis=1)
    s1 = jnp.concatenate([jnp.zeros((n, lead), F32), -sin, jnp.zeros((n, half + tail), F32)], axis=1)
    s2 = jnp.concatenate([jnp.zeros((n, lead + half), F32), sin, jnp.zeros((n, tail), F32)], axis=1)
    return tuple(jnp.tile(t, (1, reps)) for t in (c, s1, s2))


def _split_cols(w, sizes):
    out, o = [], 0
    for n in sizes:
        out.append(w[:, o:o + n])
        o += n
    return out


def _in_weights(w_in):
    cb, cc, ch, su, dq, dk, dv, cq, ckv, kr = _split_cols(
        w_in, (CONV_DIM, CONV_DIM, CONV_DIM, SSM_DIM, 3 * DIL_GW, 3 * DIL_GW, 3 * DIL_GW, MLA_Q_RANK, MLA_KV_RANK,
               MLA_ROPE))
    ws = [cb, cc, ch, jnp.pad(su, ((0, 0), (0, SSM_PAD - SSM_DIM)))]
    for g in range(len(DIL_PATTERNS)):
        ws += [m[:, g * DIL_GW:(g + 1) * DIL_GW] for m in (dq, dk, dv)]
    ws += [cq, ckv, jnp.pad(kr, ((0, 0), (MLA_NOPE, MLA_HW - MLA_NOPE - MLA_ROPE)))]
    return [w.astype(BF16) for w in ws]


def _mla_weights(w_uq, w_ukv):
    q = w_uq.reshape(MLA_Q_RANK, MLA_HEADS, MLA_QK)
    wq = jnp.pad(q, ((0, 0), (0, 0), (0, MLA_HW - MLA_QK))).reshape(MLA_Q_RANK, MLA_HEADS * MLA_HW)
    kv = w_ukv.reshape(MLA_KV_RANK, MLA_HEADS, MLA_NOPE + MLA_V)
    wk = jnp.pad(kv[:, :, :MLA_NOPE], ((0, 0), (0, 0), (0, MLA_HW - MLA_NOPE))).reshape(MLA_KV_RANK, -1)
    wv = jnp.pad(kv[:, :, MLA_NOPE:].reshape(MLA_KV_RANK, MLA_HEADS * MLA_V),
                 ((0, 0), (0, MLA_VW - MLA_HEADS * MLA_V)))
    return wq.astype(BF16), wk.astype(BF16), wv.astype(BF16)


def _out_weights(w_out):
    o_ssm, o_dil, o_mla = CONV_DIM, CONV_DIM + SSM_DIM, CONV_DIM + SSM_DIM + 3 * DIL_GW
    ws = [w_out[:o_ssm], jnp.pad(w_out[o_ssm:o_dil], ((0, SSM_PAD - SSM_DIM), (0, 0)))]
    ws += [w_out[o_dil + g * DIL_GW:o_dil + (g + 1) * DIL_GW] for g in range(3)]
    ws += [jnp.pad(w_out[o_mla:], ((0, MLA_VW - MLA_HEADS * MLA_V), (0, 0)))]
    return [w.astype(BF16) for w in ws]


def _router_weights(w_group, b_group, w_router, b_router):
    w = jnp.pad(jnp.concatenate([w_group, w_router], axis=1), ((0, 0), (0, ROUTE_W - MOE_GROUPS - N_EXPERTS)))
    hi = w.astype(BF16)
    lo = (w - hi.astype(F32)).astype(BF16)
    b = jnp.pad(jnp.concatenate([b_group, b_router]), (0, ROUTE_W - MOE_GROUPS - N_EXPERTS)).reshape(1, ROUTE_W)
    return hi, lo, b


def kernel(x, p, positions, norm_mix_g, w_in, conv_w, ssm_lam_re, ssm_lam_im, ssm_log_dt, ssm_b_re, ssm_b_im, ssm_c_re, ssm_c_im, ssm_d, ssm_glu_w, ssm_glu_b, mla_q_norm_g, mla_w_uq, mla_kv_norm_g, mla_w_ukv, w_out, norm_ffn_g, w_group, b_group, w_router, b_router, moe_w_gate, moe_w_up, moe_w_down, norm_ple_g, ple_w_proj, ple_w_gate, final_norm_g):
    batch, seq, _ = x.shape
    n_tok = batch * seq
    depth = w_in.shape[0]
    dil_tabs = _rope_tables(positions, DIL_ROT, 0, DIL_HEAD_DIM, DIL_HPG)
    mla_tabs = _rope_tables(positions, MLA_ROPE, MLA_NOPE, MLA_HW, 1)
    h = x.reshape(n_tok, D_MODEL)
    for i in range(depth):
        z = _in_proj(h, norm_mix_g[i].reshape(1, -1), _in_weights(w_in[i]), 3, batch, seq)
        cb, cc, ch, su = z[:4]
        y_conv = _conv_mixer(cb, cc, ch, conv_w[i], batch, seq)
        prm = _ssm_params(ssm_lam_re[i], ssm_lam_im[i], ssm_log_dt[i], ssm_b_re[i], ssm_b_im[i], ssm_c_re[i],
                          ssm_c_im[i], ssm_d[i], ssm_glu_w[i], ssm_glu_b[i])
        y_ssm = _ssm_mixer(su.reshape(seq * batch, SSM_PAD), prm, batch, seq).reshape(seq, batch * SSM_PAD)
        outs, lses = [], []
        for g, (window, dil) in enumerate(DIL_PATTERNS):
            assert window // dil == DIL_BLK and (seq // dil) % DIL_BLK == 0
            o, l = _dil_group(z[4 + 3 * g], z[5 + 3 * g], z[6 + 3 * g], dil_tabs, dil, batch, seq)
            outs.append(o)
            lses.append(l)
        y_dil = _dil_combine(outs, lses)
        wq, wk, wv = _mla_weights(mla_w_uq[i], mla_w_ukv[i])
        q, k, v = _mla_prep(z[13], z[14], z[15], mla_q_norm_g[i].reshape(1, -1), mla_kv_norm_g[i].reshape(1, -1),
                            wq, wk, wv, mla_tabs)
        y_mla = _mla_attn(q, k, v, batch, seq)
        wrh, wrl, br = _router_weights(w_group[i], b_group[i], w_router[i], b_router[i])
        h1, xn, route_i, route_g = _out_proj(
            (y_conv, y_ssm, y_dil[0], y_dil[1], y_dil[2], y_mla), h, _out_weights(w_out[i]),
            norm_ffn_g[i].reshape(1, -1), wrh, wrl, br, batch, seq)
        block_expert, slot_asg, n_used, nblk = _moe_plan(route_i, n_tok)
        y2 = _moe_experts(xn, block_expert, slot_asg, n_used, nblk, moe_w_gate[i], moe_w_up[i], moe_w_down[i])
        h = _ple(h1, y2, route_g, p[i].reshape(n_tok, PLE_DIM), norm_ple_g[i].reshape(1, -1),
                 ple_w_gate[i].astype(BF16), ple_w_proj[i].astype(BF16), final_norm_g.reshape(1, -1),
                 final=(i == depth - 1))
    return h.reshape(batch, seq, D_MODEL)
```

```python
import functools
import math

import jax
import jax.numpy as jnp
from jax import lax
from jax.experimental import pallas as pl
from jax.experimental.pallas import tpu as pltpu

D_MODEL = 2048
PLE_DIM = 256
ROPE_THETA = 500000.0
NORM_EPS = 1e-6

CONV_DIM = 448
CONV_K = 3

SSM_DIM = 448
SSM_GROUP = 16
SSM_GROUPS = SSM_DIM // SSM_GROUP
SSM_STATE = 64
SSM_N = SSM_GROUPS * SSM_STATE
SSM_PAD = 512

DIL_HEAD_DIM = 64
DIL_ROT = DIL_HEAD_DIM // 4
DIL_PATTERNS = ((128, 1), (512, 4), (2048, 16))
DIL_HPG = 3
DIL_GW = DIL_HPG * DIL_HEAD_DIM
DIL_BLK = 128

MLA_HEADS = 9
MLA_Q_RANK = 384
MLA_KV_RANK = 256
MLA_NOPE = 64
MLA_ROPE = 32
MLA_V = 64
MLA_QK = MLA_NOPE + MLA_ROPE
MLA_HW = 128
MLA_VW = 640

MOE_GROUPS = 8
MOE_EPG = 8
N_EXPERTS = MOE_GROUPS * MOE_EPG
MOE_TOPK = 2
MOE_FF = 512
MOE_BLOCK = 128
ROUTE_W = 128

LANE = 128
ROW_SLABS = D_MODEL // LANE
NEG_BIG = -1e30

BF16 = jnp.bfloat16
F32 = jnp.float32


def _cparams(sem, vmem_mb):
    return pltpu.CompilerParams(dimension_semantics=sem, vmem_limit_bytes=vmem_mb * 1024 * 1024)


def _rms(x, g):
    ms = jnp.mean(x * x, axis=-1, keepdims=True)
    return (x * lax.rsqrt(ms + NORM_EPS)) * g


def _sigmoid(x):
    return 1.0 / (1.0 + jnp.exp(-x))


def _full(a):
    return pl.BlockSpec(a.shape, lambda *_: (0,) * a.ndim)


def _rows_to_slabs(ref, base, x):
    n = x.shape[0]
    for c in range(ROW_SLABS):
        ref[pl.ds(base + c, n, stride=ROW_SLABS), :] = x[:, c * LANE:(c + 1) * LANE]


def _slabs_to_rows(ref, base, n):
    return jnp.concatenate([ref[pl.ds(base + c, n, stride=ROW_SLABS), :] for c in range(ROW_SLABS)], axis=1)


def _in_proj_kernel(x_ref, g_ref, *refs, n_out):
    w_refs, o_refs = refs[:n_out], refs[n_out:]
    xn = _rms(x_ref[...], g_ref[...]).astype(BF16)
    for w_ref, o_ref in zip(w_refs, o_refs):
        o_ref[...] = jnp.dot(xn, w_ref[...], preferred_element_type=F32).astype(o_ref.dtype)


def _in_proj(h, g, weights, su_index, batch, seq, tm=256):
    n_s = seq // tm
    in_specs = [pl.BlockSpec((tm, D_MODEL), lambda b, i: (b * n_s + i, 0)), _full(g)]
    in_specs += [_full(w) for w in weights]
    out_shape, out_specs = [], []
    for k, w in enumerate(weights):
        n = w.shape[1]
        if k == su_index:
            out_shape.append(jax.ShapeDtypeStruct((seq, batch * n), F32))
            out_specs.append(pl.BlockSpec((tm, n), lambda b, i: (i, b)))
        else:
            out_shape.append(jax.ShapeDtypeStruct((batch * seq, n), F32))
            out_specs.append(pl.BlockSpec((tm, n), lambda b, i: (b * n_s + i, 0)))
    return pl.pallas_call(
        functools.partial(_in_proj_kernel, n_out=len(weights)),
        grid=(batch, n_s), in_specs=in_specs, out_specs=out_specs, out_shape=out_shape,
        compiler_params=_cparams(("parallel", "parallel"), 56), name="in_proj",
    )(h, g, *weights)


def _conv_kernel(cb_ref, cc_ref, ch_ref, w_ref, o_ref, u_ref, *, rows):
    seq = cb_ref.shape[0]
    u_ref[0:8, :] = jnp.zeros((8, CONV_DIM), F32)
    for r0 in range(0, seq, rows):
        u_ref[8 + r0:8 + r0 + rows, :] = cc_ref[r0:r0 + rows, :] * ch_ref[r0:r0 + rows, :]
    w0, w1, w2 = w_ref[0:1, :], w_ref[1:2, :], w_ref[2:3, :]
    for r0 in range(0, seq, rows):
        acc = (w2 * u_ref[8 + r0:8 + r0 + rows, :] + w1 * u_ref[7 + r0:7 + r0 + rows, :]
               + w0 * u_ref[6 + r0:6 + r0 + rows, :])
        o_ref[r0:r0 + rows, :] = cb_ref[r0:r0 + rows, :] * acc


def _conv_mixer(cb, cc, ch, conv_w, batch, seq):
    spec = pl.BlockSpec((seq, CONV_DIM), lambda b: (b, 0))
    return pl.pallas_call(
        functools.partial(_conv_kernel, rows=256),
        grid=(batch,), in_specs=[spec, spec, spec, _full(conv_w)], out_specs=spec,
        out_shape=jax.ShapeDtypeStruct((batch * seq, CONV_DIM), F32),
        scratch_shapes=[pltpu.VMEM((seq + 8, CONV_DIM), F32)],
        compiler_params=_cparams(("parallel",), 48), name="conv_mixer",
    )(cb, cc, ch, conv_w)


_SSM_CHUNKS = ((0, 512), (512, 512), (1024, 512), (1536, 256))


def _gelu_tanh(x):
    return 0.5 * x * (1.0 + jnp.tanh(math.sqrt(2.0 / math.pi) * (x + 0.044715 * (x * x * x))))


def _ssm_kernel(u_ref, bf_ref, are_ref, aim_ref, cc_ref, d_ref, gw_ref, gb_ref, o_ref, xs_ref, st_ref,
                *, batch, tc):
    @pl.when(pl.program_id(0) == 0)
    def _():
        st_ref[...] = jnp.zeros_like(st_ref)

    u = u_ref[...]
    xs_ref[...] = jnp.dot(u.astype(BF16), bf_ref[...], preferred_element_type=F32)
    per_tile = 8 // batch
    for c0, cw in _SSM_CHUNKS:
        re_l, im_l = slice(c0, c0 + cw), slice(SSM_N + c0, SSM_N + c0 + cw)
        ar = jnp.broadcast_to(are_ref[:, re_l], (batch, cw))
        ai = jnp.broadcast_to(aim_ref[:, re_l], (batch, cw))

        def body(k, carry, re_l=re_l, im_l=im_l, ar=ar, ai=ai):
            sr, si = carry
            rows = pl.ds(pl.multiple_of(k * 8, 8), 8)
            tile_r, tile_i = xs_ref[rows, re_l], xs_ref[rows, im_l]
            out_r, out_i = [], []
            for j in range(per_tile):
                step = slice(j * batch, (j + 1) * batch)
                sr, si = ar * sr - ai * si + tile_r[step], ar * si + ai * sr + tile_i[step]
                out_r.append(sr)
                out_i.append(si)
            xs_ref[rows, re_l] = jnp.concatenate(out_r, axis=0)
            xs_ref[rows, im_l] = jnp.concatenate(out_i, axis=0)
            return sr, si

        sr, si = lax.fori_loop(0, tc // per_tile, body, (st_ref[0:batch, re_l], st_ref[0:batch, im_l]))
        st_ref[0:batch, re_l] = sr
        st_ref[0:batch, im_l] = si
    y = jnp.dot(xs_ref[...].astype(BF16), cc_ref[...], preferred_element_type=F32) + d_ref[...] * u
    g = _gelu_tanh(y)
    o_ref[...] = g * _sigmoid(jnp.dot(g.astype(BF16), gw_ref[...], preferred_element_type=F32) + gb_ref[...])


def _ssm_mixer(su_tm, prm, batch, seq, tc=128):
    rows = tc * batch
    spec = pl.BlockSpec((rows, SSM_PAD), lambda c: (c, 0))
    args = (prm["bf"], prm["a_re"], prm["a_im"], prm["cc"], prm["d"], prm["gw"], prm["gb"])
    return pl.pallas_call(
        functools.partial(_ssm_kernel, batch=batch, tc=tc),
        grid=(seq // tc,), in_specs=[spec] + [_full(a) for a in args], out_specs=spec,
        out_shape=jax.ShapeDtypeStruct((seq * batch, SSM_PAD), F32),
        scratch_shapes=[pltpu.VMEM((rows, 2 * SSM_N), F32), pltpu.VMEM((8, 2 * SSM_N), F32)],
        compiler_params=_cparams(("arbitrary",), 48), name="ssm_mixer",
    )(su_tm, *args)


def _ssm_params(lam_re, lam_im, log_dt, b_re, b_im, c_re, c_im, d_skip, glu_w, glu_b):
    dt = jnp.exp(log_dt)[:, None]
    mag = jnp.exp(lam_re * dt)
    a_re = mag * jnp.cos(lam_im * dt)
    a_im = mag * jnp.sin(lam_im * dt)
    nr, ni = a_re - 1.0, a_im
    den = lam_re * lam_re + lam_im * lam_im
    f_re = (nr * lam_re + ni * lam_im) / den
    f_im = (ni * lam_re - nr * lam_im) / den
    bfr = f_re[:, :, None] * b_re - f_im[:, :, None] * b_im
    bfi = f_re[:, :, None] * b_im + f_im[:, :, None] * b_re
    eye = jnp.eye(SSM_GROUPS, dtype=F32)

    def in_blockdiag(m):
        return jnp.einsum("gpc,gh->gchp", m, eye).reshape(SSM_DIM, SSM_N)

    def out_blockdiag(m):
        return jnp.einsum("gcp,gh->gphc", m, eye).reshape(SSM_N, SSM_DIM)

    pad = SSM_PAD - SSM_DIM
    bf = jnp.concatenate([in_blockdiag(bfr), in_blockdiag(bfi)], axis=1)
    cc = jnp.concatenate([out_blockdiag(c_re), -out_blockdiag(c_im)], axis=0)
    return {
        "bf": jnp.pad(bf, ((0, pad), (0, 0))).astype(BF16),
        "cc": jnp.pad(cc, ((0, 0), (0, pad))).astype(BF16),
        "a_re": a_re.reshape(1, SSM_N), "a_im": a_im.reshape(1, SSM_N),
        "d": jnp.pad(d_skip, (0, pad)).reshape(1, SSM_PAD),
        "gw": jnp.pad(glu_w, ((0, pad), (0, pad))).astype(BF16),
        "gb": jnp.pad(glu_b, (0, pad)).reshape(1, SSM_PAD),
    }


def _rope_lanes(x, c, s1, s2, half, width):
    return x * c + pltpu.roll(x, width - half, 1) * s1 + pltpu.roll(x, half, 1) * s2


def _dil_kernel(q_ref, k_ref, v_ref, c_ref, s1_ref, s2_ref, o_ref, l_ref):
    length, width = q_ref.shape
    nb = length // DIL_BLK
    ncomb = width // DIL_HEAD_DIM
    lane = lax.broadcasted_iota(jnp.int32, (1, width), 1)
    cmask = [(lane // DIL_HEAD_DIM == c).astype(F32) for c in range(ncomb)]
    qi = lax.broadcasted_iota(jnp.int32, (ncomb * DIL_BLK, 2 * DIL_BLK), 0) % DIL_BLK
    ki = lax.broadcasted_iota(jnp.int32, (ncomb * DIL_BLK, 2 * DIL_BLK), 1)
    band = (ki >= qi) & (ki <= qi + DIL_BLK)

    def roped(ref, rows):
        return _rope_lanes(ref[rows, :], c_ref[rows, :], s1_ref[rows, :], s2_ref[rows, :], DIL_ROT // 2, width)

    def block(i, carry):
        rows_q = pl.ds(pl.multiple_of(i * DIL_BLK, DIL_BLK), DIL_BLK)
        rows_p = pl.ds(pl.multiple_of(jnp.maximum(i - 1, 0) * DIL_BLK, DIL_BLK), DIL_BLK)
        q = roped(q_ref, rows_q) * (1.0 / math.sqrt(DIL_HEAD_DIM))
        kcat = jnp.concatenate([roped(k_ref, rows_p), roped(k_ref, rows_q)], axis=0).astype(BF16)
        vcat = jnp.concatenate([v_ref[rows_p, :], v_ref[rows_q, :]], axis=0).astype(BF16)
        qs = jnp.concatenate([q * cm for cm in cmask], axis=0).astype(BF16)
        s = lax.dot_general(qs, kcat, (((1,), (1,)), ((), ())), preferred_element_type=F32)
        s = jnp.where(band & ((ki >= DIL_BLK) | (i > 0)), s, NEG_BIG)
        mx = jnp.max(s, axis=-1, keepdims=True)
        p = jnp.exp(s - mx)
        den = jnp.sum(p, axis=-1, keepdims=True)
        o = jnp.dot(p.astype(BF16), vcat, preferred_element_type=F32) / den
        lse = mx + jnp.log(den)
        out = jnp.zeros((DIL_BLK, width), F32)
        lout = jnp.zeros((DIL_BLK, width), F32)
        for c, cm in enumerate(cmask):
            out = out + cm * o[c * DIL_BLK:(c + 1) * DIL_BLK, :]
            lout = lout + cm * lse[c * DIL_BLK:(c + 1) * DIL_BLK, :]
        o_ref[rows_q, :] = out
        l_ref[rows_q, :] = lout
        return carry

    lax.fori_loop(0, nb, block, 0)


def _dil_group(q, k, v, tabs, dil, batch, seq):
    length = seq // dil
    nres = min(dil, 2)
    width = nres * DIL_GW
    view = lambda a: a.reshape(batch * length, dil * DIL_GW)
    spec = pl.BlockSpec((length, width), lambda b, r: (b, r))
    sds = jax.ShapeDtypeStruct((batch * length, dil * DIL_GW), F32)
    o, l = pl.pallas_call(
        _dil_kernel, grid=(batch, dil // nres), in_specs=[spec] * 6, out_specs=[spec, spec], out_shape=[sds, sds],
        compiler_params=_cparams(("parallel", "parallel"), 48), name=f"dil_attn_d{dil}",
    )(view(q), view(k), view(v), *[view(t) for t in tabs])
    return o.reshape(batch * seq, DIL_GW), l.reshape(batch * seq, DIL_GW)


def _dil_combine_kernel(o0, o1, o2, l0, l1, l2, y0, y1, y2):
    a, b, c = l0[...], l1[...], l2[...]
    mx = jnp.maximum(jnp.maximum(a, b), c)
    ea, eb, ec = jnp.exp(a - mx), jnp.exp(b - mx), jnp.exp(c - mx)
    inv = 1.0 / (ea + eb + ec)
    y0[...] = o0[...] * (ea * inv)
    y1[...] = o1[...] * (eb * inv)
    y2[...] = o2[...] * (ec * inv)


def _dil_combine(outs, lses, tm=1024):
    rows = outs[0].shape[0]
    spec = pl.BlockSpec((tm, DIL_GW), lambda i: (i, 0))
    sds = jax.ShapeDtypeStruct((rows, DIL_GW), F32)
    return pl.pallas_call(
        _dil_combine_kernel, grid=(rows // tm,), in_specs=[spec] * 6, out_specs=[spec] * 3,
        out_shape=[sds] * 3, compiler_params=_cparams(("parallel",), 32), name="dil_combine",
    )(*outs, *lses)


def _mla_prep_kernel(cq_ref, ckv_ref, kr_ref, gq_ref, gkv_ref, wq_ref, wk_ref, wv_ref, c_ref, s1_ref, s2_ref,
                     q_ref, k_ref, v_ref):
    c, s1, s2 = c_ref[...], s1_ref[...], s2_ref[...]
    rope = functools.partial(_rope_lanes, c=c, s1=s1, s2=s2, half=MLA_ROPE // 2, width=MLA_HW)
    qn = _rms(cq_ref[...], gq_ref[...]).astype(BF16)
    kvn = _rms(ckv_ref[...], gkv_ref[...]).astype(BF16)
    q = jnp.dot(qn, wq_ref[...], preferred_element_type=F32)
    kn = jnp.dot(kvn, wk_ref[...], preferred_element_type=F32)
    v_ref[...] = jnp.dot(kvn, wv_ref[...], preferred_element_type=F32).astype(BF16)
    kr = rope(kr_ref[...])
    scale = math.log2(math.e) / math.sqrt(MLA_QK)
    for h in range(MLA_HEADS):
        sl = slice(h * MLA_HW, (h + 1) * MLA_HW)
        q_ref[:, sl] = (rope(q[:, sl]) * scale).astype(BF16)
        k_ref[:, sl] = (kn[:, sl] + kr).astype(BF16)


def _mla_prep(cq, ckv, kr, gq, gkv, wq, wk, wv, tabs, tm=512):
    rows = cq.shape[0]

    def rspec(n):
        return pl.BlockSpec((tm, n), lambda i: (i, 0))

    hw = MLA_HEADS * MLA_HW
    return pl.pallas_call(
        _mla_prep_kernel, grid=(rows // tm,),
        in_specs=[rspec(MLA_Q_RANK), rspec(MLA_KV_RANK), rspec(MLA_HW), _full(gq), _full(gkv), _full(wq),
                  _full(wk), _full(wv), rspec(MLA_HW), rspec(MLA_HW), rspec(MLA_HW)],
        out_specs=[rspec(hw), rspec(hw), rspec(MLA_VW)],
        out_shape=[jax.ShapeDtypeStruct((rows, hw), BF16), jax.ShapeDtypeStruct((rows, hw), BF16),
                   jax.ShapeDtypeStruct((rows, MLA_VW), BF16)],
        compiler_params=_cparams(("parallel",), 48), name="mla_prep",
    )(cq, ckv, kr, gq, gkv, wq, wk, wv, *tabs)


def _mla_attn_kernel(q_ref, k_ref, v_ref, o_ref, *, tq, group):
    qi = pl.program_id(1)
    row = lax.broadcasted_iota(jnp.int32, (tq, tq), 0)
    col = lax.broadcasted_iota(jnp.int32, (tq, tq), 1)
    causal = col <= row

    def heads_out(heads):
        qs = [q_ref[:, h * MLA_HW:(h + 1) * MLA_HW] for h in heads]

        def step(kj, carry, masked):
            rows = pl.ds(pl.multiple_of(kj * tq, tq), tq)
            scores = [lax.dot_general(q, k_ref[rows, h * MLA_HW:(h + 1) * MLA_HW], (((1,), (1,)), ((), ())),
                                      preferred_element_type=F32) for h, q in zip(heads, qs)]
            stats = []
            for s, (m, l, acc) in zip(scores, carry):
                if masked:
                    s = jnp.where(causal, s, NEG_BIG)
                mn = jnp.maximum(m, jnp.max(s, axis=-1, keepdims=True))
                alpha = jnp.exp2(m - mn)
                p = jnp.exp2(s - mn)
                stats.append((mn, alpha, alpha * l + jnp.sum(p, axis=-1, keepdims=True), p.astype(BF16)))
            out = []
            for h, (mn, alpha, l, p), (_, _, acc) in zip(heads, stats, carry):
                v = v_ref[rows, (h // 2) * MLA_HW:(h // 2 + 1) * MLA_HW]
                out.append((mn, l, alpha * acc + jnp.dot(p, v, preferred_element_type=F32)))
            return tuple(out)

        init = tuple((jnp.full((tq, 1), NEG_BIG, F32), jnp.zeros((tq, 1), F32), jnp.zeros((tq, MLA_HW), F32))
                     for _ in heads)
        carry = lax.fori_loop(0, qi, functools.partial(step, masked=False), init)
        return [acc / l for _, l, acc in step(qi, carry, True)]

    for h0 in range(0, MLA_HEADS, group):
        heads = list(range(h0, min(h0 + group, MLA_HEADS)))
        for h, o in zip(heads, heads_out(heads)):
            lo = (h // 2) * MLA_HW + (h % 2) * MLA_V
            o_ref[:, lo:lo + MLA_V] = o[:, (h % 2) * MLA_V:(h % 2 + 1) * MLA_V]
    if MLA_HEADS % 2:
        o_ref[:, MLA_HEADS * MLA_V:] = jnp.zeros((tq, MLA_VW - MLA_HEADS * MLA_V), F32)


def _mla_attn(q, k, v, batch, seq, tq=256, group=5):
    hw = MLA_HEADS * MLA_HW
    n_q = seq // tq
    return pl.pallas_call(
        functools.partial(_mla_attn_kernel, tq=tq, group=group), grid=(batch, n_q),
        in_specs=[pl.BlockSpec((tq, hw), lambda b, i: (b * n_q + i, 0)),
                  pl.BlockSpec((seq, hw), lambda b, i: (b, 0)),
                  pl.BlockSpec((seq, MLA_VW), lambda b, i: (b, 0))],
        out_specs=pl.BlockSpec((tq, MLA_VW), lambda b, i: (b * n_q + i, 0)),
        out_shape=jax.ShapeDtypeStruct((batch * seq, MLA_VW), F32),
        compiler_params=_cparams(("parallel", "arbitrary"), 48), name="mla_attn",
    )(q, k, v)


def _out_proj_kernel(yc_ref, ys_ref, yd0_ref, yd1_ref, yd2_ref, ym_ref, h_ref,
                     wc_ref, ws_ref, wd0_ref, wd1_ref, wd2_ref, wm_ref,
                     gf_ref, wrh_ref, wrl_ref, br_ref,
                     h1_ref, xn_ref, ri_ref, rg_ref):
    def mm(y_ref, w_ref):
        return jnp.dot(y_ref[...].astype(BF16), w_ref[...], preferred_element_type=F32)

    mix = (mm(yc_ref, wc_ref) + mm(ys_ref, ws_ref) + mm(yd0_ref, wd0_ref) + mm(yd1_ref, wd1_ref)
           + mm(yd2_ref, wd2_ref) + mm(ym_ref, wm_ref))
    h1 = h_ref[...] + mix
    h1_ref[...] = h1
    hn = _rms(h1, gf_ref[...])
    hi = hn.astype(BF16)
    hi32 = hi.astype(F32)
    lo = (hn - hi32).astype(BF16)
    _rows_to_slabs(xn_ref, 0, hi32)
    logits = (jnp.dot(hi, wrh_ref[...], preferred_element_type=F32)
              + jnp.dot(hi, wrl_ref[...], preferred_element_type=F32)
              + jnp.dot(lo, wrh_ref[...], preferred_element_type=F32)) + br_ref[...]
    lane = lax.broadcasted_iota(jnp.int32, logits.shape, 1)
    gl = jnp.where(lane < MOE_GROUPS, logits, NEG_BIG)
    gmax = jnp.max(gl, axis=-1, keepdims=True)
    g_top_p = 1.0 / jnp.sum(jnp.exp(gl - gmax), axis=-1, keepdims=True)
    g_idx = jnp.min(jnp.where(gl == gmax, lane, ROUTE_W), axis=-1, keepdims=True)
    in_group = (lane >= MOE_GROUPS) & (lane < MOE_GROUPS + N_EXPERTS) & (((lane - MOE_GROUPS) >> 3) == g_idx)
    el = jnp.where(in_group, logits, NEG_BIG)
    v1 = jnp.max(el, axis=-1, keepdims=True)
    i1 = jnp.min(jnp.where(el == v1, lane, ROUTE_W), axis=-1, keepdims=True)
    el2 = jnp.where(lane == i1, NEG_BIG, el)
    v2 = jnp.max(el2, axis=-1, keepdims=True)
    i2 = jnp.min(jnp.where(el2 == v2, lane, ROUTE_W), axis=-1, keepdims=True)
    e2 = jnp.exp(v2 - v1)
    w1 = g_top_p / (1.0 + e2)
    w2 = g_top_p * e2 / (1.0 + e2)
    ri_ref[...] = jnp.where(lane == 0, i1 - MOE_GROUPS, jnp.where(lane == 1, i2 - MOE_GROUPS, 0))
    rg_ref[...] = jnp.where(lane == 0, w1, jnp.where(lane == 1, w2, 0.0))


def _out_proj(ys, h, ws, gf, wrh, wrl, br, batch, seq, tm=256):
    rows = batch * seq
    n_s = seq // tm
    yc, ys_tm, yd0, yd1, yd2, ym = ys

    def rspec(n):
        return pl.BlockSpec((tm, n), lambda b, i: (b * n_s + i, 0))

    in_specs = [rspec(CONV_DIM), pl.BlockSpec((tm, SSM_PAD), lambda b, i: (i, b)), rspec(DIL_GW), rspec(DIL_GW),
                rspec(DIL_GW), rspec(MLA_VW), rspec(D_MODEL)]
    in_specs += [_full(w) for w in ws] + [_full(gf), _full(wrh), _full(wrl), _full(br)]
    return pl.pallas_call(
        _out_proj_kernel, grid=(batch, n_s), in_specs=in_specs,
        out_specs=[rspec(D_MODEL), pl.BlockSpec((tm * ROW_SLABS, LANE), lambda b, i: (b * n_s + i, 0)),
                   rspec(ROUTE_W), rspec(ROUTE_W)],
        out_shape=[jax.ShapeDtypeStruct((rows, D_MODEL), F32), jax.ShapeDtypeStruct((rows * ROW_SLABS, LANE), F32),
                   jax.ShapeDtypeStruct((rows, ROUTE_W), jnp.int32), jax.ShapeDtypeStruct((rows, ROUTE_W), F32)],
        compiler_params=_cparams(("parallel", "parallel"), 48), name="out_proj_router",
    )(yc, ys_tm, yd0, yd1, yd2, ym, h, *ws, gf, wrh, wrl, br)


def _moe_plan(route_i, n_tok):
    n_asg = n_tok * MOE_TOPK
    nblk = (n_asg + N_EXPERTS * (MOE_BLOCK - 1) + MOE_BLOCK - 1) // MOE_BLOCK
    flat_e = route_i[:, :MOE_TOPK].reshape(-1)
    onehot = (flat_e[:, None] == jnp.arange(N_EXPERTS, dtype=jnp.int32)[None, :]).astype(jnp.int32)
    csum = jnp.cumsum(onehot, axis=0)
    rank = jnp.take_along_axis(csum, flat_e[:, None], axis=1)[:, 0] - 1
    counts = csum[-1]
    padded = ((counts + MOE_BLOCK - 1) // MOE_BLOCK) * MOE_BLOCK
    pend = jnp.cumsum(padded)
    dest = (pend - padded)[flat_e] + rank
    slot_asg = jnp.full((nblk * MOE_BLOCK,), -1, jnp.int32).at[dest].set(jnp.arange(n_asg, dtype=jnp.int32))
    block_expert = jnp.minimum(
        jnp.searchsorted(pend, jnp.arange(nblk, dtype=jnp.int32) * MOE_BLOCK, side="right"), N_EXPERTS - 1
    ).astype(jnp.int32)
    n_used = (pend[-1] // MOE_BLOCK).astype(jnp.int32).reshape(1)
    return block_expert, slot_asg, n_used, nblk


def _moe_kernel(be_ref, asg_ref, nu_ref, x_hbm, wg_ref, wu_ref, wd_ref, y_hbm,
                xbuf, ybuf, wgb, wub, wdb, gsem, ssem, *, n_tok):
    i = pl.program_id(0)
    n_used = nu_ref[0]
    slot = i % 2
    buf_rows = MOE_BLOCK * ROW_SLABS

    def slab_rows(row):
        return pl.ds(pl.multiple_of(row * ROW_SLABS, ROW_SLABS), ROW_SLABS)

    def gather(blk, sl):
        for r in range(MOE_BLOCK):
            tok = jnp.maximum(asg_ref[blk * MOE_BLOCK + r], 0) >> 1
            yield pltpu.make_async_copy(x_hbm.at[slab_rows(tok), :], xbuf.at[slab_rows(sl * MOE_BLOCK + r), :],
                                        gsem.at[sl])

    def scatter(blk, sl):
        for r in range(MOE_BLOCK):
            a = asg_ref[blk * MOE_BLOCK + r]
            dst = jnp.where(a >= 0, (a & 1) * n_tok + (a >> 1), MOE_TOPK * n_tok + sl * MOE_BLOCK + r)
            yield pltpu.make_async_copy(ybuf.at[slab_rows(sl * MOE_BLOCK + r), :], y_hbm.at[slab_rows(dst), :],
                                        ssem.at[sl])

    @pl.when(i == 0)
    def _():
        ybuf[...] = jnp.zeros_like(ybuf)
        dump = pltpu.make_async_copy(ybuf, y_hbm.at[pl.ds(MOE_TOPK * n_tok * ROW_SLABS, 2 * buf_rows), :], ssem.at[0])
        dump.start()
        dump.wait()

    @pl.when((i == 0) & (n_used > 0))
    def _():
        for cp in gather(0, 0):
            cp.start()

    @pl.when(i < n_used)
    def _():
        for cp in gather(i, slot):
            cp.wait()

    @pl.when(i + 1 < n_used)
    def _():
        for cp in gather(i + 1, 1 - slot):
            cp.start()

    @pl.when((i < n_used) & ((i == 0) | (be_ref[i] != be_ref[jnp.maximum(i - 1, 0)])))
    def _():
        wgb[...] = wg_ref[...].astype(BF16)
        wub[...] = wu_ref[...].astype(BF16)
        wdb[...] = wd_ref[...].astype(BF16)

    @pl.when((i < n_used) & (i >= 2))
    def _():
        for cp in scatter(i - 2, slot):
            cp.wait()

    @pl.when(i < n_used)
    def _():
        x = _slabs_to_rows(xbuf, slot * buf_rows, MOE_BLOCK).astype(BF16)
        a = jnp.dot(x, wgb[...], preferred_element_type=F32)
        b = jnp.dot(x, wub[...], preferred_element_type=F32)
        act = (a * _sigmoid(a) * b).astype(BF16)
        _rows_to_slabs(ybuf, slot * buf_rows, jnp.dot(act, wdb[...], preferred_element_type=F32))
        for cp in scatter(i, slot):
            cp.start()

    @pl.when(i == n_used - 1)
    def _():
        for cp in scatter(i, slot):
            cp.wait()

    @pl.when((i == n_used - 1) & (i >= 1))
    def _():
        for cp in scatter(i - 1, 1 - slot):
            cp.wait()


def _moe_experts(xn, block_expert, slot_asg, n_used, nblk, w_gate, w_up, w_down, layer):
    n_tok = xn.shape[0] // ROW_SLABS
    buf_rows = 2 * MOE_BLOCK * ROW_SLABS
    grid_spec = pltpu.PrefetchScalarGridSpec(
        num_scalar_prefetch=3, grid=(nblk,),
        in_specs=[pl.BlockSpec(memory_space=pl.ANY),
                  pl.BlockSpec((None, None, D_MODEL, MOE_FF), lambda i, be, asg, nu: (layer, be[i], 0, 0)),
                  pl.BlockSpec((None, None, D_MODEL, MOE_FF), lambda i, be, asg, nu: (layer, be[i], 0, 0)),
                  pl.BlockSpec((None, None, MOE_FF, D_MODEL), lambda i, be, asg, nu: (layer, be[i], 0, 0))],
        out_specs=pl.BlockSpec(memory_space=pl.ANY),
        scratch_shapes=[pltpu.VMEM((buf_rows, LANE), F32), pltpu.VMEM((buf_rows, LANE), F32),
                        pltpu.VMEM((D_MODEL, MOE_FF), BF16), pltpu.VMEM((D_MODEL, MOE_FF), BF16),
                        pltpu.VMEM((MOE_FF, D_MODEL), BF16),
                        pltpu.SemaphoreType.DMA((2,)), pltpu.SemaphoreType.DMA((2,))])
    return pl.pallas_call(
        functools.partial(_moe_kernel, n_tok=n_tok), grid_spec=grid_spec,
        out_shape=jax.ShapeDtypeStruct(((MOE_TOPK * n_tok + 2 * MOE_BLOCK) * ROW_SLABS, LANE), F32),
        compiler_params=_cparams(("arbitrary",), 56), name="moe_experts",
    )(block_expert, slot_asg, n_used, xn, w_gate, w_up, w_down)


def _ple_kernel(h1_ref, y0_ref, y1_ref, rg_ref, p_ref, gp_ref, wg_ref, wp_ref, gfin_ref, o_ref, *, final):
    rg = rg_ref[...]
    tm = rg.shape[0]
    ffn = rg[:, 0:1] * _slabs_to_rows(y0_ref, 0, tm) + rg[:, 1:2] * _slabs_to_rows(y1_ref, 0, tm)
    h2 = h1_ref[...] + ffn
    hn = _rms(h2, gp_ref[...]).astype(BF16)
    gate = _sigmoid(jnp.dot(hn, wg_ref[...], preferred_element_type=F32))
    pe = jnp.dot(p_ref[...].astype(BF16), wp_ref[...], preferred_element_type=F32)
    h3 = h2 + pe * gate
    o_ref[...] = _rms(h3, gfin_ref[...]) if final else h3


def _ple(h1, y2, rg, p, gp, wg, wp, gfin, final, tm=256):
    rows = h1.shape[0]
    n_t = rows // tm

    def rspec(n):
        return pl.BlockSpec((tm, n), lambda i: (i, 0))

    def yspec(k):
        return pl.BlockSpec((tm * ROW_SLABS, LANE), lambda i: (k * n_t + i, 0))

    return pl.pallas_call(
        functools.partial(_ple_kernel, final=final), grid=(n_t,),
        in_specs=[rspec(D_MODEL), yspec(0), yspec(1), rspec(ROUTE_W), rspec(PLE_DIM), _full(gp), _full(wg),
                  _full(wp), _full(gfin)],
        out_specs=rspec(D_MODEL), out_shape=jax.ShapeDtypeStruct((rows, D_MODEL), F32),
        compiler_params=_cparams(("parallel",), 48), name="moe_combine_ple",
    )(h1, y2, y2, rg, p, gp, wg, wp, gfin)


def _rope_tables(positions, rot_dim, lead, period, reps):
    half = rot_dim // 2
    inv_freq = ROPE_THETA ** (-jnp.arange(0, rot_dim, 2, dtype=F32) / rot_dim)
    ang = positions.astype(F32).reshape(-1, 1) * inv_freq
    cos, sin = jnp.cos(ang), jnp.sin(ang)
    n = cos.shape[0]
    tail = period - lead - rot_dim
    c = jnp.concatenate([jnp.ones((n, lead), F32), cos, cos, jnp.ones((n, tail), F32)], axis=1)
    s1 = jnp.concatenate([jnp.zeros((n, lead), F32), -sin, jnp.zeros((n, half + tail), F32)], axis=1)
    s2 = jnp.concatenate([jnp.zeros((n, lead + half), F32), sin, jnp.zeros((n, tail), F32)], axis=1)
    return tuple(jnp.tile(t, (1, reps)) for t in (c, s1, s2))


def _split_cols(w, sizes):
    out, o = [], 0
    for n in sizes:
        out.append(w[:, o:o + n])
        o += n
    return out


def _in_weights(w_in):
    cb, cc, ch, su, dq, dk, dv, cq, ckv, kr = _split_cols(
        w_in, (CONV_DIM, CONV_DIM, CONV_DIM, SSM_DIM, 3 * DIL_GW, 3 * DIL_GW, 3 * DIL_GW, MLA_Q_RANK, MLA_KV_RANK,
               MLA_ROPE))
    ws = [cb, cc, ch, jnp.pad(su, ((0, 0), (0, SSM_PAD - SSM_DIM)))]
    for g in range(len(DIL_PATTERNS)):
        ws += [m[:, g * DIL_GW:(g + 1) * DIL_GW] for m in (dq, dk, dv)]
    ws += [cq, ckv, jnp.pad(kr, ((0, 0), (MLA_NOPE, MLA_HW - MLA_NOPE - MLA_ROPE)))]
    return [w.astype(BF16) for w in ws]


def _mla_weights(w_uq, w_ukv):
    q = w_uq.reshape(MLA_Q_RANK, MLA_HEADS, MLA_QK)
    wq = jnp.pad(q, ((0, 0), (0, 0), (0, MLA_HW - MLA_QK))).reshape(MLA_Q_RANK, MLA_HEADS * MLA_HW)
    kv = w_ukv.reshape(MLA_KV_RANK, MLA_HEADS, MLA_NOPE + MLA_V)
    wk = jnp.pad(kv[:, :, :MLA_NOPE], ((0, 0), (0, 0), (0, MLA_HW - MLA_NOPE))).reshape(MLA_KV_RANK, -1)
    wv = jnp.pad(kv[:, :, MLA_NOPE:].reshape(MLA_KV_RANK, MLA_HEADS * MLA_V),
                 ((0, 0), (0, MLA_VW - MLA_HEADS * MLA_V)))
    return wq.astype(BF16), wk.astype(BF16), wv.astype(BF16)


def _out_weights(w_out):
    o_ssm, o_dil, o_mla = CONV_DIM, CONV_DIM + SSM_DIM, CONV_DIM + SSM_DIM + 3 * DIL_GW
    ws = [w_out[:o_ssm], jnp.pad(w_out[o_ssm:o_dil], ((0, SSM_PAD - SSM_DIM), (0, 0)))]
    ws += [w_out[o_dil + g * DIL_GW:o_dil + (g + 1) * DIL_GW] for g in range(3)]
    ws += [jnp.pad(w_out[o_mla:], ((0, MLA_VW - MLA_HEADS * MLA_V), (0, 0)))]
    return [w.astype(BF16) for w in ws]


def _router_weights(w_group, b_group, w_router, b_router):
    w = jnp.pad(jnp.concatenate([w_group, w_router], axis=1), ((0, 0), (0, ROUTE_W - MOE_GROUPS - N_EXPERTS)))
    hi = w.astype(BF16)
    lo = (w - hi.astype(F32)).astype(BF16)
    b = jnp.pad(jnp.concatenate([b_group, b_router]), (0, ROUTE_W - MOE_GROUPS - N_EXPERTS)).reshape(1, ROUTE_W)
    return hi, lo, b


def kernel(x, p, positions, norm_mix_g, w_in, conv_w, ssm_lam_re, ssm_lam_im, ssm_log_dt, ssm_b_re, ssm_b_im, ssm_c_re, ssm_c_im, ssm_d, ssm_glu_w, ssm_glu_b, mla_q_norm_g, mla_w_uq, mla_kv_norm_g, mla_w_ukv, w_out, norm_ffn_g, w_group, b_group, w_router, b_router, moe_w_gate, moe_w_up, moe_w_down, norm_ple_g, ple_w_proj, ple_w_gate, final_norm_g):
    batch, seq, _ = x.shape
    n_tok = batch * seq
    depth = w_in.shape[0]
    dil_tabs = _rope_tables(positions, DIL_ROT, 0, DIL_HEAD_DIM, DIL_HPG)
    mla_tabs = _rope_tables(positions, MLA_ROPE, MLA_NOPE, MLA_HW, 1)
    h = x.reshape(n_tok, D_MODEL)
    for i in range(depth):
        z = _in_proj(h, norm_mix_g[i].reshape(1, -1), _in_weights(w_in[i]), 3, batch, seq)
        cb, cc, ch, su = z[:4]
        y_conv = _conv_mixer(cb, cc, ch, conv_w[i], batch, seq)
        prm = _ssm_params(ssm_lam_re[i], ssm_lam_im[i], ssm_log_dt[i], ssm_b_re[i], ssm_b_im[i], ssm_c_re[i],
                          ssm_c_im[i], ssm_d[i], ssm_glu_w[i], ssm_glu_b[i])
        y_ssm = _ssm_mixer(su.reshape(seq * batch, SSM_PAD), prm, batch, seq).reshape(seq, batch * SSM_PAD)
        outs, lses = [], []
        for g, (window, dil) in enumerate(DIL_PATTERNS):
            assert window // dil == DIL_BLK and (seq // dil) % DIL_BLK == 0
            o, l = _dil_group(z[4 + 3 * g], z[5 + 3 * g], z[6 + 3 * g], dil_tabs, dil, batch, seq)
            outs.append(o)
            lses.append(l)
        y_dil = _dil_combine(outs, lses)
        wq, wk, wv = _mla_weights(mla_w_uq[i], mla_w_ukv[i])
        q, k, v = _mla_prep(z[13], z[14], z[15], mla_q_norm_g[i].reshape(1, -1), mla_kv_norm_g[i].reshape(1, -1),
                            wq, wk, wv, mla_tabs)
        y_mla = _mla_attn(q, k, v, batch, seq)
        wrh, wrl, br = _router_weights(w_group[i], b_group[i], w_router[i], b_router[i])
        h1, xn, route_i, route_g = _out_proj(
            (y_conv, y_ssm, y_dil[0], y_dil[1], y_dil[2], y_mla), h, _out_weights(w_out[i]),
            norm_ffn_g[i].reshape(1, -1), wrh, wrl, br, batch, seq)
        block_expert, slot_asg, n_used, nblk = _moe_plan(route_i, n_tok)
        y2 = _moe_experts(xn, block_expert, slot_asg, n_used, nblk, moe_w_gate, moe_w_up, moe_w_down, i)
        h = _ple(h1, y2, route_g, p[i].reshape(n_tok, PLE_DIM), norm_ple_g[i].reshape(1, -1),
                 ple_w_gate[i].astype(BF16), ple_w_proj[i].astype(BF16), final_norm_g.reshape(1, -1),
                 final=(i == depth - 1))
    return h.reshape(batch, seq, D_MODEL)
```

```python
import functools
import math

import jax
import jax.numpy as jnp
from jax import lax
from jax.experimental import pallas as pl
from jax.experimental.pallas import tpu as pltpu

D_MODEL = 2048
PLE_DIM = 256
ROPE_THETA = 500000.0
NORM_EPS = 1e-6

CONV_DIM = 448
CONV_K = 3

SSM_DIM = 448
SSM_GROUP = 16
SSM_GROUPS = SSM_DIM // SSM_GROUP
SSM_STATE = 64
SSM_N = SSM_GROUPS * SSM_STATE
SSM_PAD = 512

DIL_HEAD_DIM = 64
DIL_ROT = DIL_HEAD_DIM // 4
DIL_PATTERNS = ((128, 1), (512, 4), (2048, 16))
DIL_HPG = 3
DIL_GW = DIL_HPG * DIL_HEAD_DIM
DIL_PLANES = 2
DIL_BLK = 128

MLA_HEADS = 9
MLA_Q_RANK = 384
MLA_KV_RANK = 256
MLA_NOPE = 64
MLA_ROPE = 32
MLA_V = 64
MLA_QK = MLA_NOPE + MLA_ROPE
MLA_HW = 128
MLA_VW = 640

MOE_GROUPS = 8
MOE_EPG = 8
N_EXPERTS = MOE_GROUPS * MOE_EPG
MOE_TOPK = 2
MOE_FF = 512
MOE_BLOCK = 128
ROUTE_W = 128

LANE = 128
ROW_SLABS = D_MODEL // LANE
NEG_BIG = -1e30

BF16 = jnp.bfloat16
F32 = jnp.float32


def _cparams(sem, vmem_mb):
    return pltpu.CompilerParams(dimension_semantics=sem, vmem_limit_bytes=vmem_mb * 1024 * 1024)


def _rms(x, g):
    ms = jnp.mean(x * x, axis=-1, keepdims=True)
    return (x * lax.rsqrt(ms + NORM_EPS)) * g


def _sigmoid(x):
    return 1.0 / (1.0 + jnp.exp(-x))


def _full(a):
    return pl.BlockSpec(a.shape, lambda *_: (0,) * a.ndim)


def _rows_to_slabs(ref, base, x):
    n = x.shape[0]
    for c in range(ROW_SLABS):
        ref[pl.ds(base + c, n, stride=ROW_SLABS), :] = x[:, c * LANE:(c + 1) * LANE]


def _slabs_to_rows(ref, base, n):
    return jnp.concatenate([ref[pl.ds(base + c, n, stride=ROW_SLABS), :] for c in range(ROW_SLABS)], axis=1)


def _in_proj_kernel(x_ref, g_ref, *refs, n_out):
    w_refs, o_refs = refs[:n_out], refs[n_out:]
    xn = _rms(x_ref[...], g_ref[...]).astype(BF16)
    for w_ref, o_ref in zip(w_refs, o_refs):
        res = jnp.dot(xn, w_ref[...], preferred_element_type=F32)
        if len(o_ref.shape) == 3:
            for j in range(o_ref.shape[0]):
                o_ref[j] = res[:, j * LANE:(j + 1) * LANE]
        else:
            o_ref[...] = res


def _in_proj(h, g, weights, kinds, batch, seq, tm=256):
    n_s = seq // tm
    in_specs = [pl.BlockSpec((tm, D_MODEL), lambda b, i: (b * n_s + i, 0)), _full(g)]
    in_specs += [_full(w) for w in weights]
    out_shape, out_specs = [], []
    for kind, w in zip(kinds, weights):
        n = w.shape[1]
        if kind == "time_major":
            out_shape.append(jax.ShapeDtypeStruct((seq, batch * n), F32))
            out_specs.append(pl.BlockSpec((tm, n), lambda b, i: (i, b)))
        elif kind == "planes":
            out_shape.append(jax.ShapeDtypeStruct((n // LANE, batch * seq, LANE), F32))
            out_specs.append(pl.BlockSpec((n // LANE, tm, LANE), lambda b, i: (0, b * n_s + i, 0)))
        else:
            out_shape.append(jax.ShapeDtypeStruct((batch * seq, n), F32))
            out_specs.append(pl.BlockSpec((tm, n), lambda b, i: (b * n_s + i, 0)))
    return pl.pallas_call(
        functools.partial(_in_proj_kernel, n_out=len(weights)),
        grid=(batch, n_s), in_specs=in_specs, out_specs=out_specs, out_shape=out_shape,
        compiler_params=_cparams(("parallel", "parallel"), 56), name="in_proj",
    )(h, g, *weights)


def _conv_kernel(cb_ref, cc_ref, ch_ref, w_ref, o_ref, u_ref, *, rows):
    seq = cb_ref.shape[0]
    u_ref[0:8, :] = jnp.zeros((8, CONV_DIM), F32)
    for r0 in range(0, seq, rows):
        u_ref[8 + r0:8 + r0 + rows, :] = cc_ref[r0:r0 + rows, :] * ch_ref[r0:r0 + rows, :]
    w0, w1, w2 = w_ref[0:1, :], w_ref[1:2, :], w_ref[2:3, :]
    for r0 in range(0, seq, rows):
        acc = (w2 * u_ref[8 + r0:8 + r0 + rows, :] + w1 * u_ref[7 + r0:7 + r0 + rows, :]
               + w0 * u_ref[6 + r0:6 + r0 + rows, :])
        o_ref[r0:r0 + rows, :] = cb_ref[r0:r0 + rows, :] * acc


def _conv_mixer(cb, cc, ch, conv_w, batch, seq):
    spec = pl.BlockSpec((seq, CONV_DIM), lambda b: (b, 0))
    return pl.pallas_call(
        functools.partial(_conv_kernel, rows=256),
        grid=(batch,), in_specs=[spec, spec, spec, _full(conv_w)], out_specs=spec,
        out_shape=jax.ShapeDtypeStruct((batch * seq, CONV_DIM), F32),
        scratch_shapes=[pltpu.VMEM((seq + 8, CONV_DIM), F32)],
        compiler_params=_cparams(("parallel",), 48), name="conv_mixer",
    )(cb, cc, ch, conv_w)


_SSM_CHUNKS = ((0, 512), (512, 512), (1024, 512), (1536, 256))


def _gelu_tanh(x):
    return 0.5 * x * (1.0 + jnp.tanh(math.sqrt(2.0 / math.pi) * (x + 0.044715 * (x * x * x))))


def _ssm_kernel(u_ref, bf_ref, are_ref, aim_ref, cc_ref, d_ref, gw_ref, gb_ref, o_ref, xs_ref, st_ref,
                *, batch, tc):
    @pl.when(pl.program_id(0) == 0)
    def _():
        st_ref[...] = jnp.zeros_like(st_ref)

    u = u_ref[...]
    xs_ref[...] = jnp.dot(u.astype(BF16), bf_ref[...], preferred_element_type=F32)
    per_tile = 8 // batch
    for c0, cw in _SSM_CHUNKS:
        re_l, im_l = slice(c0, c0 + cw), slice(SSM_N + c0, SSM_N + c0 + cw)
        ar = jnp.broadcast_to(are_ref[:, re_l], (batch, cw))
        ai = jnp.broadcast_to(aim_ref[:, re_l], (batch, cw))

        def body(k, carry, re_l=re_l, im_l=im_l, ar=ar, ai=ai):
            sr, si = carry
            rows = pl.ds(pl.multiple_of(k * 8, 8), 8)
            tile_r, tile_i = xs_ref[rows, re_l], xs_ref[rows, im_l]
            out_r, out_i = [], []
            for j in range(per_tile):
                step = slice(j * batch, (j + 1) * batch)
                sr, si = ar * sr - ai * si + tile_r[step], ar * si + ai * sr + tile_i[step]
                out_r.append(sr)
                out_i.append(si)
            xs_ref[rows, re_l] = jnp.concatenate(out_r, axis=0)
            xs_ref[rows, im_l] = jnp.concatenate(out_i, axis=0)
            return sr, si

        sr, si = lax.fori_loop(0, tc // per_tile, body, (st_ref[0:batch, re_l], st_ref[0:batch, im_l]))
        st_ref[0:batch, re_l] = sr
        st_ref[0:batch, im_l] = si
    y = jnp.dot(xs_ref[...].astype(BF16), cc_ref[...], preferred_element_type=F32) + d_ref[...] * u
    g = _gelu_tanh(y)
    o_ref[...] = g * _sigmoid(jnp.dot(g.astype(BF16), gw_ref[...], preferred_element_type=F32) + gb_ref[...])


def _ssm_mixer(su_tm, prm, batch, seq, tc=128):
    rows = tc * batch
    spec = pl.BlockSpec((rows, SSM_PAD), lambda c: (c, 0))
    args = (prm["bf"], prm["a_re"], prm["a_im"], prm["cc"], prm["d"], prm["gw"], prm["gb"])
    return pl.pallas_call(
        functools.partial(_ssm_kernel, batch=batch, tc=tc),
        grid=(seq // tc,), in_specs=[spec] + [_full(a) for a in args], out_specs=spec,
        out_shape=jax.ShapeDtypeStruct((seq * batch, SSM_PAD), F32),
        scratch_shapes=[pltpu.VMEM((rows, 2 * SSM_N), F32), pltpu.VMEM((8, 2 * SSM_N), F32)],
        compiler_params=_cparams(("arbitrary",), 48), name="ssm_mixer",
    )(su_tm, *args)


def _ssm_params(lam_re, lam_im, log_dt, b_re, b_im, c_re, c_im, d_skip, glu_w, glu_b):
    dt = jnp.exp(log_dt)[:, None]
    mag = jnp.exp(lam_re * dt)
    a_re = mag * jnp.cos(lam_im * dt)
    a_im = mag * jnp.sin(lam_im * dt)
    nr, ni = a_re - 1.0, a_im
    den = lam_re * lam_re + lam_im * lam_im
    f_re = (nr * lam_re + ni * lam_im) / den
    f_im = (ni * lam_re - nr * lam_im) / den
    bfr = f_re[:, :, None] * b_re - f_im[:, :, None] * b_im
    bfi = f_re[:, :, None] * b_im + f_im[:, :, None] * b_re
    eye = jnp.eye(SSM_GROUPS, dtype=F32)

    def in_blockdiag(m):
        return jnp.einsum("gpc,gh->gchp", m, eye).reshape(SSM_DIM, SSM_N)

    def out_blockdiag(m):
        return jnp.einsum("gcp,gh->gphc", m, eye).reshape(SSM_N, SSM_DIM)

    pad = SSM_PAD - SSM_DIM
    bf = jnp.concatenate([in_blockdiag(bfr), in_blockdiag(bfi)], axis=1)
    cc = jnp.concatenate([out_blockdiag(c_re), -out_blockdiag(c_im)], axis=0)
    return {
        "bf": jnp.pad(bf, ((0, pad), (0, 0))).astype(BF16),
        "cc": jnp.pad(cc, ((0, 0), (0, pad))).astype(BF16),
        "a_re": a_re.reshape(1, SSM_N), "a_im": a_im.reshape(1, SSM_N),
        "d": jnp.pad(d_skip, (0, pad)).reshape(1, SSM_PAD),
        "gw": jnp.pad(glu_w, ((0, pad), (0, pad))).astype(BF16),
        "gb": jnp.pad(glu_b, (0, pad)).reshape(1, SSM_PAD),
    }


def _rope_lanes(x, c, s1, s2, half, width):
    return x * c + pltpu.roll(x, width - half, 1) * s1 + pltpu.roll(x, half, 1) * s2


def _dil_kernel(q_ref, k_ref, v_ref, c_ref, s1_ref, s2_ref, o_ref, l_ref, *, dil):
    seq = q_ref.shape[1]
    nb = (seq // dil) // DIL_BLK
    width = DIL_PLANES * LANE
    lane = lax.broadcasted_iota(jnp.int32, (1, width), 1)
    cmask = [(lane // DIL_HEAD_DIM == c).astype(F32) for c in range(DIL_HPG)]
    qi = lax.broadcasted_iota(jnp.int32, (DIL_HPG * DIL_BLK, 2 * DIL_BLK), 0) % DIL_BLK
    ki = lax.broadcasted_iota(jnp.int32, (DIL_HPG * DIL_BLK, 2 * DIL_BLK), 1)
    band = (ki >= qi) & (ki <= qi + DIL_BLK)

    def planes(ref, rows):
        return jnp.concatenate([ref[j, rows, :] for j in range(DIL_PLANES)], axis=1)

    def table(ref, rows):
        return jnp.concatenate([ref[rows, :]] * DIL_PLANES, axis=1)

    def roped(ref, rows):
        return _rope_lanes(planes(ref, rows), table(c_ref, rows), table(s1_ref, rows), table(s2_ref, rows),
                           DIL_ROT // 2, width)

    def block(idx, carry):
        m, i = idx // nb, idx % nb

        def rows_of(blk):
            if dil == 1:
                return pl.ds(pl.multiple_of(blk * DIL_BLK, DIL_BLK), DIL_BLK)
            return pl.ds(blk * DIL_BLK * dil + m, DIL_BLK, stride=dil)

        rows_q, rows_p = rows_of(i), rows_of(jnp.maximum(i - 1, 0))
        q = roped(q_ref, rows_q) * (1.0 / math.sqrt(DIL_HEAD_DIM))
        kcat = jnp.concatenate([roped(k_ref, rows_p), roped(k_ref, rows_q)], axis=0).astype(BF16)
        vcat = jnp.concatenate([planes(v_ref, rows_p), planes(v_ref, rows_q)], axis=0).astype(BF16)
        qs = jnp.concatenate([q * cm for cm in cmask], axis=0).astype(BF16)
        s = lax.dot_general(qs, kcat, (((1,), (1,)), ((), ())), preferred_element_type=F32)
        s = jnp.where(band & ((ki >= DIL_BLK) | (i > 0)), s, NEG_BIG)
        mx = jnp.max(s, axis=-1, keepdims=True)
        p = jnp.exp(s - mx)
        den = jnp.sum(p, axis=-1, keepdims=True)
        o = jnp.dot(p.astype(BF16), vcat, preferred_element_type=F32) / den
        lse = mx + jnp.log(den)
        out = jnp.zeros((DIL_BLK, width), F32)
        lout = jnp.zeros((DIL_BLK, width), F32)
        for c, cm in enumerate(cmask):
            out = out + cm * o[c * DIL_BLK:(c + 1) * DIL_BLK, :]
            lout = lout + cm * lse[c * DIL_BLK:(c + 1) * DIL_BLK, :]
        for j in range(DIL_PLANES):
            o_ref[j, rows_q, :] = out[:, j * LANE:(j + 1) * LANE]
            l_ref[j, rows_q, :] = lout[:, j * LANE:(j + 1) * LANE]
        return carry

    lax.fori_loop(0, dil * nb, block, 0)


def _dil_group(q, k, v, tabs, dil, batch, seq):
    spec = pl.BlockSpec((DIL_PLANES, seq, LANE), lambda b: (0, b, 0))
    tspec = pl.BlockSpec((seq, LANE), lambda b: (b, 0))
    sds = jax.ShapeDtypeStruct((DIL_PLANES, batch * seq, LANE), F32)
    return pl.pallas_call(
        functools.partial(_dil_kernel, dil=dil), grid=(batch,), in_specs=[spec] * 3 + [tspec] * 3,
        out_specs=[spec, spec], out_shape=[sds, sds],
        compiler_params=_cparams(("parallel",), 48), name=f"dil_attn_d{dil}",
    )(q, k, v, *tabs)


def _dil_combine_kernel(o0, o1, o2, l0, l1, l2, y0, y1, y2):
    a, b, c = l0[...], l1[...], l2[...]
    mx = jnp.maximum(jnp.maximum(a, b), c)
    ea, eb, ec = jnp.exp(a - mx), jnp.exp(b - mx), jnp.exp(c - mx)
    inv = 1.0 / (ea + eb + ec)
    y0[...] = o0[...] * (ea * inv)
    y1[...] = o1[...] * (eb * inv)
    y2[...] = o2[...] * (ec * inv)


def _dil_combine(outs, lses, tm=2048):
    shape = outs[0].shape
    rows = shape[0] * shape[1]
    flat = lambda a: a.reshape(rows, LANE)
    spec = pl.BlockSpec((tm, LANE), lambda i: (i, 0))
    sds = jax.ShapeDtypeStruct((rows, LANE), F32)
    ys = pl.pallas_call(
        _dil_combine_kernel, grid=(rows // tm,), in_specs=[spec] * 6, out_specs=[spec] * 3,
        out_shape=[sds] * 3, compiler_params=_cparams(("parallel",), 32), name="dil_combine",
    )(*[flat(a) for a in outs], *[flat(a) for a in lses])
    return [y.reshape(shape) for y in ys]


def _mla_prep_kernel(cq_ref, ckv_ref, kr_ref, gq_ref, gkv_ref, wq_ref, wk_ref, wv_ref, c_ref, s1_ref, s2_ref,
                     q_ref, k_ref, v_ref):
    c, s1, s2 = c_ref[...], s1_ref[...], s2_ref[...]
    rope = functools.partial(_rope_lanes, c=c, s1=s1, s2=s2, half=MLA_ROPE // 2, width=MLA_HW)
    qn = _rms(cq_ref[...], gq_ref[...]).astype(BF16)
    kvn = _rms(ckv_ref[...], gkv_ref[...]).astype(BF16)
    q = jnp.dot(qn, wq_ref[...], preferred_element_type=F32)
    kn = jnp.dot(kvn, wk_ref[...], preferred_element_type=F32)
    v_ref[...] = jnp.dot(kvn, wv_ref[...], preferred_element_type=F32).astype(BF16)
    kr = rope(kr_ref[...])
    scale = math.log2(math.e) / math.sqrt(MLA_QK)
    for h in range(MLA_HEADS):
        sl = slice(h * MLA_HW, (h + 1) * MLA_HW)
        q_ref[:, sl] = (rope(q[:, sl]) * scale).astype(BF16)
        k_ref[:, sl] = (kn[:, sl] + kr).astype(BF16)


def _mla_prep(cq, ckv, kr, gq, gkv, wq, wk, wv, tabs, tm=512):
    rows = cq.shape[0]

    def rspec(n):
        return pl.BlockSpec((tm, n), lambda i: (i, 0))

    hw = MLA_HEADS * MLA_HW
    return pl.pallas_call(
        _mla_prep_kernel, grid=(rows // tm,),
        in_specs=[rspec(MLA_Q_RANK), rspec(MLA_KV_RANK), rspec(MLA_HW), _full(gq), _full(gkv), _full(wq),
                  _full(wk), _full(wv), rspec(MLA_HW), rspec(MLA_HW), rspec(MLA_HW)],
        out_specs=[rspec(hw), rspec(hw), rspec(MLA_VW)],
        out_shape=[jax.ShapeDtypeStruct((rows, hw), BF16), jax.ShapeDtypeStruct((rows, hw), BF16),
                   jax.ShapeDtypeStruct((rows, MLA_VW), BF16)],
        compiler_params=_cparams(("parallel",), 48), name="mla_prep",
    )(cq, ckv, kr, gq, gkv, wq, wk, wv, *tabs)


def _mla_attn_kernel(q_ref, k_ref, v_ref, o_ref, *, tq, group):
    qi = pl.program_id(1)
    row = lax.broadcasted_iota(jnp.int32, (tq, tq), 0)
    col = lax.broadcasted_iota(jnp.int32, (tq, tq), 1)
    causal = col <= row

    def heads_out(heads):
        qs = [q_ref[:, h * MLA_HW:(h + 1) * MLA_HW] for h in heads]

        def step(kj, carry, masked):
            rows = pl.ds(pl.multiple_of(kj * tq, tq), tq)
            scores = [lax.dot_general(q, k_ref[rows, h * MLA_HW:(h + 1) * MLA_HW], (((1,), (1,)), ((), ())),
                                      preferred_element_type=F32) for h, q in zip(heads, qs)]
            stats = []
            for s, (m, l, acc) in zip(scores, carry):
                if masked:
                    s = jnp.where(causal, s, NEG_BIG)
                mn = jnp.maximum(m, jnp.max(s, axis=-1, keepdims=True))
                alpha = jnp.exp2(m - mn)
                p = jnp.exp2(s - mn)
                stats.append((mn, alpha, alpha * l + jnp.sum(p, axis=-1, keepdims=True), p.astype(BF16)))
            out = []
            for h, (mn, alpha, l, p), (_, _, acc) in zip(heads, stats, carry):
                v = v_ref[rows, (h // 2) * MLA_HW:(h // 2 + 1) * MLA_HW]
                out.append((mn, l, alpha * acc + jnp.dot(p, v, preferred_element_type=F32)))
            return tuple(out)

        init = tuple((jnp.full((tq, 1), NEG_BIG, F32), jnp.zeros((tq, 1), F32), jnp.zeros((tq, MLA_HW), F32))
                     for _ in heads)
        carry = lax.fori_loop(0, qi, functools.partial(step, masked=False), init)
        return [acc / l for _, l, acc in step(qi, carry, True)]

    for h0 in range(0, MLA_HEADS, group):
        heads = list(range(h0, min(h0 + group, MLA_HEADS)))
        for h, o in zip(heads, heads_out(heads)):
            lo = (h // 2) * MLA_HW + (h % 2) * MLA_V
            o_ref[:, lo:lo + MLA_V] = o[:, (h % 2) * MLA_V:(h % 2 + 1) * MLA_V]
    if MLA_HEADS % 2:
        o_ref[:, MLA_HEADS * MLA_V:] = jnp.zeros((tq, MLA_VW - MLA_HEADS * MLA_V), F32)


def _mla_attn(q, k, v, batch, seq, tq=256, group=5):
    hw = MLA_HEADS * MLA_HW
    n_q = seq // tq
    return pl.pallas_call(
        functools.partial(_mla_attn_kernel, tq=tq, group=group), grid=(batch, n_q),
        in_specs=[pl.BlockSpec((tq, hw), lambda b, i: (b * n_q + i, 0)),
                  pl.BlockSpec((seq, hw), lambda b, i: (b, 0)),
                  pl.BlockSpec((seq, MLA_VW), lambda b, i: (b, 0))],
        out_specs=pl.BlockSpec((tq, MLA_VW), lambda b, i: (b * n_q + i, 0)),
        out_shape=jax.ShapeDtypeStruct((batch * seq, MLA_VW), F32),
        compiler_params=_cparams(("parallel", "arbitrary"), 48), name="mla_attn",
    )(q, k, v)


def _out_proj_kernel(yc_ref, ys_ref, yd0_ref, yd1_ref, yd2_ref, ym_ref, h_ref,
                     wc_ref, ws_ref, wd0_ref, wd1_ref, wd2_ref, wm_ref,
                     gf_ref, wrh_ref, wrl_ref, br_ref,
                     h1_ref, xn_ref, ri_ref, rg_ref):
    def mm(y_ref, w_ref):
        if len(y_ref.shape) == 3:
            return sum(jnp.dot(y_ref[j].astype(BF16), w_ref[j * LANE:(j + 1) * LANE, :], preferred_element_type=F32)
                       for j in range(y_ref.shape[0]))
        return jnp.dot(y_ref[...].astype(BF16), w_ref[...], preferred_element_type=F32)

    mix = (mm(yc_ref, wc_ref) + mm(ys_ref, ws_ref) + mm(yd0_ref, wd0_ref) + mm(yd1_ref, wd1_ref)
           + mm(yd2_ref, wd2_ref) + mm(ym_ref, wm_ref))
    h1 = h_ref[...] + mix
    h1_ref[...] = h1
    hn = _rms(h1, gf_ref[...])
    hi = hn.astype(BF16)
    hi32 = hi.astype(F32)
    lo = (hn - hi32).astype(BF16)
    _rows_to_slabs(xn_ref, 0, hi32)
    logits = (jnp.dot(hi, wrh_ref[...], preferred_element_type=F32)
              + jnp.dot(hi, wrl_ref[...], preferred_element_type=F32)
              + jnp.dot(lo, wrh_ref[...], preferred_element_type=F32)) + br_ref[...]
    lane = lax.broadcasted_iota(jnp.int32, logits.shape, 1)
    gl = jnp.where(lane < MOE_GROUPS, logits, NEG_BIG)
    gmax = jnp.max(gl, axis=-1, keepdims=True)
    g_top_p = 1.0 / jnp.sum(jnp.exp(gl - gmax), axis=-1, keepdims=True)
    g_idx = jnp.min(jnp.where(gl == gmax, lane, ROUTE_W), axis=-1, keepdims=True)
    in_group = (lane >= MOE_GROUPS) & (lane < MOE_GROUPS + N_EXPERTS) & (((lane - MOE_GROUPS) >> 3) == g_idx)
    el = jnp.where(in_group, logits, NEG_BIG)
    v1 = jnp.max(el, axis=-1, keepdims=True)
    i1 = jnp.min(jnp.where(el == v1, lane, ROUTE_W), axis=-1, keepdims=True)
    el2 = jnp.where(lane == i1, NEG_BIG, el)
    v2 = jnp.max(el2, axis=-1, keepdims=True)
    i2 = jnp.min(jnp.where(el2 == v2, lane, ROUTE_W), axis=-1, keepdims=True)
    e2 = jnp.exp(v2 - v1)
    w1 = g_top_p / (1.0 + e2)
    w2 = g_top_p * e2 / (1.0 + e2)
    ri_ref[...] = jnp.where(lane == 0, i1 - MOE_GROUPS, jnp.where(lane == 1, i2 - MOE_GROUPS, 0))
    rg_ref[...] = jnp.where(lane == 0, w1, jnp.where(lane == 1, w2, 0.0))


def _out_proj(ys, h, ws, gf, wrh, wrl, br, batch, seq, tm=256):
    rows = batch * seq
    n_s = seq // tm
    yc, ys_tm, yd0, yd1, yd2, ym = ys

    def rspec(n):
        return pl.BlockSpec((tm, n), lambda b, i: (b * n_s + i, 0))

    pspec = pl.BlockSpec((DIL_PLANES, tm, LANE), lambda b, i: (0, b * n_s + i, 0))
    in_specs = [rspec(CONV_DIM), pl.BlockSpec((tm, SSM_PAD), lambda b, i: (i, b)), pspec, pspec, pspec,
                rspec(MLA_VW), rspec(D_MODEL)]
    in_specs += [_full(w) for w in ws] + [_full(gf), _full(wrh), _full(wrl), _full(br)]
    return pl.pallas_call(
        _out_proj_kernel, grid=(batch, n_s), in_specs=in_specs,
        out_specs=[rspec(D_MODEL), pl.BlockSpec((tm * ROW_SLABS, LANE), lambda b, i: (b * n_s + i, 0)),
                   rspec(ROUTE_W), rspec(ROUTE_W)],
        out_shape=[jax.ShapeDtypeStruct((rows, D_MODEL), F32), jax.ShapeDtypeStruct((rows * ROW_SLABS, LANE), F32),
                   jax.ShapeDtypeStruct((rows, ROUTE_W), jnp.int32), jax.ShapeDtypeStruct((rows, ROUTE_W), F32)],
        compiler_params=_cparams(("parallel", "parallel"), 48), name="out_proj_router",
    )(yc, ys_tm, yd0, yd1, yd2, ym, h, *ws, gf, wrh, wrl, br)


def _moe_plan(route_i, n_tok):
    n_asg = n_tok * MOE_TOPK
    nblk = (n_asg + N_EXPERTS * (MOE_BLOCK - 1) + MOE_BLOCK - 1) // MOE_BLOCK
    flat_e = route_i[:, :MOE_TOPK].reshape(-1)
    onehot = (flat_e[:, None] == jnp.arange(N_EXPERTS, dtype=jnp.int32)[None, :]).astype(jnp.int32)
    csum = jnp.cumsum(onehot, axis=0)
    rank = jnp.take_along_axis(csum, flat_e[:, None], axis=1)[:, 0] - 1
    counts = csum[-1]
    padded = ((counts + MOE_BLOCK - 1) // MOE_BLOCK) * MOE_BLOCK
    pend = jnp.cumsum(padded)
    dest = (pend - padded)[flat_e] + rank
    slot_asg = jnp.full((nblk * MOE_BLOCK,), -1, jnp.int32).at[dest].set(jnp.arange(n_asg, dtype=jnp.int32))
    block_expert = jnp.minimum(
        jnp.searchsorted(pend, jnp.arange(nblk, dtype=jnp.int32) * MOE_BLOCK, side="right"), N_EXPERTS - 1
    ).astype(jnp.int32)
    n_used = (pend[-1] // MOE_BLOCK).astype(jnp.int32)
    prev = jnp.concatenate([jnp.full((1,), -1, jnp.int32), block_expert[:-1]])
    is_first = (block_expert != prev).astype(jnp.int32)
    w_slot = (jnp.cumsum(is_first) - 1) % 2
    run_end = (pend // MOE_BLOCK)[block_expert]
    nxt = jnp.where(run_end < n_used, block_expert[jnp.minimum(run_end, nblk - 1)], -1)
    sched = jnp.stack([block_expert, is_first, w_slot, nxt]).astype(jnp.int32)
    return sched, slot_asg, n_used.reshape(1), nblk


def _moe_kernel(sched_ref, asg_ref, nu_ref, x_hbm, wg_hbm, wu_hbm, wd_hbm, y_hbm,
                xbuf, ybuf, wgf, wuf, wdf, wgb, wub, wdb, gsem, ssem, wsem, *, n_tok, layer):
    i = pl.program_id(0)
    n_used = nu_ref[0]
    slot = i % 2
    buf_rows = MOE_BLOCK * ROW_SLABS
    active = i < n_used
    first = active & (sched_ref[1, i] == 1)
    w_slot = sched_ref[2, i]

    def weights(e, ws):
        return [pltpu.make_async_copy(w_hbm.at[layer, e], buf.at[ws], wsem.at[ws])
                for w_hbm, buf in ((wg_hbm, wgf), (wu_hbm, wuf), (wd_hbm, wdf))]

    def slab_rows(row):
        return pl.ds(pl.multiple_of(row * ROW_SLABS, ROW_SLABS), ROW_SLABS)

    def gather(blk, sl):
        for r in range(MOE_BLOCK):
            tok = jnp.maximum(asg_ref[blk * MOE_BLOCK + r], 0) >> 1
            yield pltpu.make_async_copy(x_hbm.at[slab_rows(tok), :], xbuf.at[slab_rows(sl * MOE_BLOCK + r), :],
                                        gsem.at[sl])

    def scatter(blk, sl):
        for r in range(MOE_BLOCK):
            a = asg_ref[blk * MOE_BLOCK + r]
            dst = jnp.where(a >= 0, (a & 1) * n_tok + (a >> 1), MOE_TOPK * n_tok + sl * MOE_BLOCK + r)
            yield pltpu.make_async_copy(ybuf.at[slab_rows(sl * MOE_BLOCK + r), :], y_hbm.at[slab_rows(dst), :],
                                        ssem.at[sl])

    @pl.when(i == 0)
    def _():
        ybuf[...] = jnp.zeros_like(ybuf)
        dump = pltpu.make_async_copy(ybuf, y_hbm.at[pl.ds(MOE_TOPK * n_tok * ROW_SLABS, 2 * buf_rows), :], ssem.at[0])
        dump.start()
        dump.wait()

    @pl.when((i == 0) & (n_used > 0))
    def _():
        for cp in weights(sched_ref[0, 0], 0):
            cp.start()
        for cp in gather(0, 0):
            cp.start()

    @pl.when(first)
    def _():
        for cp in weights(sched_ref[0, i], w_slot):
            cp.wait()

    @pl.when(first & (sched_ref[3, i] >= 0))
    def _():
        for cp in weights(sched_ref[3, i], 1 - w_slot):
            cp.start()

    @pl.when(first)
    def _():
        wgb[...] = wgf[w_slot].astype(BF16)
        wub[...] = wuf[w_slot].astype(BF16)
        wdb[...] = wdf[w_slot].astype(BF16)

    @pl.when(active)
    def _():
        for cp in gather(i, slot):
            cp.wait()

    @pl.when(i + 1 < n_used)
    def _():
        for cp in gather(i + 1, 1 - slot):
            cp.start()

    @pl.when((i < n_used) & (i >= 2))
    def _():
        for cp in scatter(i - 2, slot):
            cp.wait()

    @pl.when(i < n_used)
    def _():
        x = _slabs_to_rows(xbuf, slot * buf_rows, MOE_BLOCK).astype(BF16)
        a = jnp.dot(x, wgb[...], preferred_element_type=F32)
        b = jnp.dot(x, wub[...], preferred_element_type=F32)
        act = (a * _sigmoid(a) * b).astype(BF16)
        _rows_to_slabs(ybuf, slot * buf_rows, jnp.dot(act, wdb[...], preferred_element_type=F32))
        for cp in scatter(i, slot):
            cp.start()

    @pl.when(i == n_used - 1)
    def _():
        for cp in scatter(i, slot):
            cp.wait()

    @pl.when((i == n_used - 1) & (i >= 1))
    def _():
        for cp in scatter(i - 1, 1 - slot):
            cp.wait()


def _moe_experts(xn, sched, slot_asg, n_used, nblk, w_gate, w_up, w_down, layer):
    n_tok = xn.shape[0] // ROW_SLABS
    buf_rows = 2 * MOE_BLOCK * ROW_SLABS
    any_spec = pl.BlockSpec(memory_space=pl.ANY)
    grid_spec = pltpu.PrefetchScalarGridSpec(
        num_scalar_prefetch=3, grid=(nblk,),
        in_specs=[any_spec, any_spec, any_spec, any_spec], out_specs=any_spec,
        scratch_shapes=[pltpu.VMEM((buf_rows, LANE), F32), pltpu.VMEM((buf_rows, LANE), F32),
                        pltpu.VMEM((2, D_MODEL, MOE_FF), F32), pltpu.VMEM((2, D_MODEL, MOE_FF), F32),
                        pltpu.VMEM((2, MOE_FF, D_MODEL), F32),
                        pltpu.VMEM((D_MODEL, MOE_FF), BF16), pltpu.VMEM((D_MODEL, MOE_FF), BF16),
                        pltpu.VMEM((MOE_FF, D_MODEL), BF16),
                        pltpu.SemaphoreType.DMA((2,)), pltpu.SemaphoreType.DMA((2,)), pltpu.SemaphoreType.DMA((2,))])
    return pl.pallas_call(
        functools.partial(_moe_kernel, n_tok=n_tok, layer=layer), grid_spec=grid_spec,
        out_shape=jax.ShapeDtypeStruct(((MOE_TOPK * n_tok + 2 * MOE_BLOCK) * ROW_SLABS, LANE), F32),
        compiler_params=_cparams(("arbitrary",), 56), name="moe_experts",
    )(sched, slot_asg, n_used, xn, w_gate, w_up, w_down)


def _ple_kernel(h1_ref, y0_ref, y1_ref, rg_ref, p_ref, gp_ref, wg_ref, wp_ref, gfin_ref, o_ref, *, final):
    rg = rg_ref[...]
    tm = rg.shape[0]
    ffn = rg[:, 0:1] * _slabs_to_rows(y0_ref, 0, tm) + rg[:, 1:2] * _slabs_to_rows(y1_ref, 0, tm)
    h2 = h1_ref[...] + ffn
    hn = _rms(h2, gp_ref[...]).astype(BF16)
    gate = _sigmoid(jnp.dot(hn, wg_ref[...], preferred_element_type=F32))
    pe = jnp.dot(p_ref[...].astype(BF16), wp_ref[...], preferred_element_type=F32)
    h3 = h2 + pe * gate
    o_ref[...] = _rms(h3, gfin_ref[...]) if final else h3


def _ple(h1, y2, rg, p, gp, wg, wp, gfin, final, tm=256):
    rows = h1.shape[0]
    n_t = rows // tm

    def rspec(n):
        return pl.BlockSpec((tm, n), lambda i: (i, 0))

    def yspec(k):
        return pl.BlockSpec((tm * ROW_SLABS, LANE), lambda i: (k * n_t + i, 0))

    return pl.pallas_call(
        functools.partial(_ple_kernel, final=final), grid=(n_t,),
        in_specs=[rspec(D_MODEL), yspec(0), yspec(1), rspec(ROUTE_W), rspec(PLE_DIM), _full(gp), _full(wg),
                  _full(wp), _full(gfin)],
        out_specs=rspec(D_MODEL), out_shape=jax.ShapeDtypeStruct((rows, D_MODEL), F32),
        compiler_params=_cparams(("parallel",), 48), name="moe_combine_ple",
    )(h1, y2, y2, rg, p, gp, wg, wp, gfin)


def _rope_tables(positions, rot_dim, lead, period, reps):
    half = rot_dim // 2
    inv_freq = ROPE_THETA ** (-jnp.arange(0, rot_dim, 2, dtype=F32) / rot_dim)
    ang = positions.astype(F32).reshape(-1, 1) * inv_freq
    cos, sin = jnp.cos(ang), jnp.sin(ang)
    n = cos.shape[0]
    tail = period - lead - rot_dim
    c = jnp.concatenate([jnp.ones((n, lead), F32), cos, cos, jnp.ones((n, tail), F32)], axis=1)
    s1 = jnp.concatenate([jnp.zeros((n, lead), F32), -sin, jnp.zeros((n, half + tail), F32)], axis=1)
    s2 = jnp.concatenate([jnp.zeros((n, lead + half), F32), sin, jnp.zeros((n, tail), F32)], axis=1)
    return tuple(jnp.tile(t, (1, reps)) for t in (c, s1, s2))


def _split_cols(w, sizes):
    out, o = [], 0
    for n in sizes:
        out.append(w[:, o:o + n])
        o += n
    return out


def _in_weights(w_in):
    cb, cc, ch, su, dq, dk, dv, cq, ckv, kr = _split_cols(
        w_in, (CONV_DIM, CONV_DIM, CONV_DIM, SSM_DIM, 3 * DIL_GW, 3 * DIL_GW, 3 * DIL_GW, MLA_Q_RANK, MLA_KV_RANK,
               MLA_ROPE))
    ws = [cb, cc, ch, jnp.pad(su, ((0, 0), (0, SSM_PAD - SSM_DIM)))]
    kinds = ["rows", "rows", "rows", "time_major"]
    for g in range(len(DIL_PATTERNS)):
        ws += [jnp.pad(m[:, g * DIL_GW:(g + 1) * DIL_GW], ((0, 0), (0, DIL_PLANES * LANE - DIL_GW)))
               for m in (dq, dk, dv)]
        kinds += ["planes"] * 3
    ws += [cq, ckv, jnp.pad(kr, ((0, 0), (MLA_NOPE, MLA_HW - MLA_NOPE - MLA_ROPE)))]
    kinds += ["rows"] * 3
    return [w.astype(BF16) for w in ws], kinds


def _mla_weights(w_uq, w_ukv):
    q = w_uq.reshape(MLA_Q_RANK, MLA_HEADS, MLA_QK)
    wq = jnp.pad(q, ((0, 0), (0, 0), (0, MLA_HW - MLA_QK))).reshape(MLA_Q_RANK, MLA_HEADS * MLA_HW)
    kv = w_ukv.reshape(MLA_KV_RANK, MLA_HEADS, MLA_NOPE + MLA_V)
    wk = jnp.pad(kv[:, :, :MLA_NOPE], ((0, 0), (0, 0), (0, MLA_HW - MLA_NOPE))).reshape(MLA_KV_RANK, -1)
    wv = jnp.pad(kv[:, :, MLA_NOPE:].reshape(MLA_KV_RANK, MLA_HEADS * MLA_V),
                 ((0, 0), (0, MLA_VW - MLA_HEADS * MLA_V)))
    return wq.astype(BF16), wk.astype(BF16), wv.astype(BF16)


def _out_weights(w_out):
    o_ssm, o_dil, o_mla = CONV_DIM, CONV_DIM + SSM_DIM, CONV_DIM + SSM_DIM + 3 * DIL_GW
    ws = [w_out[:o_ssm], jnp.pad(w_out[o_ssm:o_dil], ((0, SSM_PAD - SSM_DIM), (0, 0)))]
    ws += [jnp.pad(w_out[o_dil + g * DIL_GW:o_dil + (g + 1) * DIL_GW], ((0, DIL_PLANES * LANE - DIL_GW), (0, 0)))
           for g in range(3)]
    ws += [jnp.pad(w_out[o_mla:], ((0, MLA_VW - MLA_HEADS * MLA_V), (0, 0)))]
    return [w.astype(BF16) for w in ws]


def _router_weights(w_group, b_group, w_router, b_router):
    w = jnp.pad(jnp.concatenate([w_group, w_router], axis=1), ((0, 0), (0, ROUTE_W - MOE_GROUPS - N_EXPERTS)))
    hi = w.astype(BF16)
    lo = (w - hi.astype(F32)).astype(BF16)
    b = jnp.pad(jnp.concatenate([b_group, b_router]), (0, ROUTE_W - MOE_GROUPS - N_EXPERTS)).reshape(1, ROUTE_W)
    return hi, lo, b


def kernel(x, p, positions, norm_mix_g, w_in, conv_w, ssm_lam_re, ssm_lam_im, ssm_log_dt, ssm_b_re, ssm_b_im, ssm_c_re, ssm_c_im, ssm_d, ssm_glu_w, ssm_glu_b, mla_q_norm_g, mla_w_uq, mla_kv_norm_g, mla_w_ukv, w_out, norm_ffn_g, w_group, b_group, w_router, b_router, moe_w_gate, moe_w_up, moe_w_down, norm_ple_g, ple_w_proj, ple_w_gate, final_norm_g):
    batch, seq, _ = x.shape
    n_tok = batch * seq
    depth = w_in.shape[0]
    dil_tabs = _rope_tables(positions, DIL_ROT, 0, DIL_HEAD_DIM, LANE // DIL_HEAD_DIM)
    mla_tabs = _rope_tables(positions, MLA_ROPE, MLA_NOPE, MLA_HW, 1)
    h = x.reshape(n_tok, D_MODEL)
    for i in range(depth):
        z = _in_proj(h, norm_mix_g[i].reshape(1, -1), *_in_weights(w_in[i]), batch, seq)
        cb, cc, ch, su = z[:4]
        y_conv = _conv_mixer(cb, cc, ch, conv_w[i], batch, seq)
        prm = _ssm_params(ssm_lam_re[i], ssm_lam_im[i], ssm_log_dt[i], ssm_b_re[i], ssm_b_im[i], ssm_c_re[i],
                          ssm_c_im[i], ssm_d[i], ssm_glu_w[i], ssm_glu_b[i])
        y_ssm = _ssm_mixer(su.reshape(seq * batch, SSM_PAD), prm, batch, seq).reshape(seq, batch * SSM_PAD)
        outs, lses = [], []
        for g, (window, dil) in enumerate(DIL_PATTERNS):
            assert window // dil == DIL_BLK and (seq // dil) % DIL_BLK == 0
            o, l = _dil_group(z[4 + 3 * g], z[5 + 3 * g], z[6 + 3 * g], dil_tabs, dil, batch, seq)
            outs.append(o)
            lses.append(l)
        y_dil = _dil_combine(outs, lses)
        wq, wk, wv = _mla_weights(mla_w_uq[i], mla_w_ukv[i])
        q, k, v = _mla_prep(z[13], z[14], z[15], mla_q_norm_g[i].reshape(1, -1), mla_kv_norm_g[i].reshape(1, -1),
                            wq, wk, wv, mla_tabs)
        y_mla = _mla_attn(q, k, v, batch, seq)
        wrh, wrl, br = _router_weights(w_group[i], b_group[i], w_router[i], b_router[i])
        h1, xn, route_i, route_g = _out_proj(
            (y_conv, y_ssm, y_dil[0], y_dil[1], y_dil[2], y_mla), h, _out_weights(w_out[i]),
            norm_ffn_g[i].reshape(1, -1), wrh, wrl, br, batch, seq)
        block_expert, slot_asg, n_used, nblk = _moe_plan(route_i, n_tok)
        y2 = _moe_experts(xn, block_expert, slot_asg, n_used, nblk, moe_w_gate, moe_w_up, moe_w_down, i)
        h = _ple(h1, y2, route_g, p[i].reshape(n_tok, PLE_DIM), norm_ple_g[i].reshape(1, -1),
                 ple_w_gate[i].astype(BF16), ple_w_proj[i].astype(BF16), final_norm_g.reshape(1, -1),
                 final=(i == depth - 1))
    return h.reshape(batch, seq, D_MODEL)
```

```python
import functools
import math

import jax
import jax.numpy as jnp
from jax import lax
from jax.experimental import pallas as pl
from jax.experimental.pallas import tpu as pltpu

D_MODEL = 2048
PLE_DIM = 256
ROPE_THETA = 500000.0
NORM_EPS = 1e-6

CONV_DIM = 448
CONV_K = 3

SSM_DIM = 448
SSM_GROUP = 16
SSM_GROUPS = SSM_DIM // SSM_GROUP
SSM_STATE = 64
SSM_N = SSM_GROUPS * SSM_STATE
SSM_PAD = 512

DIL_HEAD_DIM = 64
DIL_ROT = DIL_HEAD_DIM // 4
DIL_PATTERNS = ((128, 1), (512, 4), (2048, 16))
DIL_HPG = 3
DIL_GW = DIL_HPG * DIL_HEAD_DIM
DIL_PLANES = 2
DIL_BLK = 128

MLA_HEADS = 9
MLA_Q_RANK = 384
MLA_KV_RANK = 256
MLA_NOPE = 64
MLA_ROPE = 32
MLA_V = 64
MLA_QK = MLA_NOPE + MLA_ROPE
MLA_HW = 128
MLA_VW = 640

MOE_GROUPS = 8
MOE_EPG = 8
N_EXPERTS = MOE_GROUPS * MOE_EPG
MOE_TOPK = 2
MOE_FF = 512
MOE_BLOCK = 128
ROUTE_W = 128

LANE = 128
ROW_SLABS = D_MODEL // LANE
NEG_BIG = -1e30

BF16 = jnp.bfloat16
F32 = jnp.float32


def _cparams(sem, vmem_mb):
    return pltpu.CompilerParams(dimension_semantics=sem, vmem_limit_bytes=vmem_mb * 1024 * 1024)


def _rms(x, g):
    ms = jnp.mean(x * x, axis=-1, keepdims=True)
    return (x * lax.rsqrt(ms + NORM_EPS)) * g


def _sigmoid(x):
    return 1.0 / (1.0 + jnp.exp(-x))


def _full(a):
    return pl.BlockSpec(a.shape, lambda *_: (0,) * a.ndim)


def _rows_to_slabs(ref, base, x):
    n = x.shape[0]
    for c in range(ROW_SLABS):
        ref[pl.ds(base + c, n, stride=ROW_SLABS), :] = x[:, c * LANE:(c + 1) * LANE]


def _slabs_to_rows(ref, base, n):
    return jnp.concatenate([ref[pl.ds(base + c, n, stride=ROW_SLABS), :] for c in range(ROW_SLABS)], axis=1)


def _in_proj_kernel(x_ref, g_ref, *refs, n_out):
    w_refs, o_refs = refs[:n_out], refs[n_out:]
    xn = _rms(x_ref[...], g_ref[...]).astype(BF16)
    for w_ref, o_ref in zip(w_refs, o_refs):
        res = jnp.dot(xn, w_ref[...], preferred_element_type=F32)
        if len(o_ref.shape) == 3:
            for j in range(o_ref.shape[0]):
                o_ref[j] = res[:, j * LANE:(j + 1) * LANE]
        else:
            o_ref[...] = res


def _in_proj(h, g, weights, kinds, batch, seq, tm=256):
    n_s = seq // tm
    in_specs = [pl.BlockSpec((tm, D_MODEL), lambda b, i: (b * n_s + i, 0)), _full(g)]
    in_specs += [_full(w) for w in weights]
    out_shape, out_specs = [], []
    for kind, w in zip(kinds, weights):
        n = w.shape[1]
        if kind == "time_major":
            out_shape.append(jax.ShapeDtypeStruct((seq, batch * n), F32))
            out_specs.append(pl.BlockSpec((tm, n), lambda b, i: (i, b)))
        elif kind == "planes":
            out_shape.append(jax.ShapeDtypeStruct((n // LANE, batch * seq, LANE), F32))
            out_specs.append(pl.BlockSpec((n // LANE, tm, LANE), lambda b, i: (0, b * n_s + i, 0)))
        else:
            out_shape.append(jax.ShapeDtypeStruct((batch * seq, n), F32))
            out_specs.append(pl.BlockSpec((tm, n), lambda b, i: (b * n_s + i, 0)))
    return pl.pallas_call(
        functools.partial(_in_proj_kernel, n_out=len(weights)),
        grid=(batch, n_s), in_specs=in_specs, out_specs=out_specs, out_shape=out_shape,
        compiler_params=_cparams(("parallel", "parallel"), 56), name="in_proj",
    )(h, g, *weights)


def _conv_kernel(cb_ref, cc_ref, ch_ref, w_ref, o_ref, u_ref, *, rows):
    seq = cb_ref.shape[0]
    u_ref[0:8, :] = jnp.zeros((8, CONV_DIM), F32)
    for r0 in range(0, seq, rows):
        u_ref[8 + r0:8 + r0 + rows, :] = cc_ref[r0:r0 + rows, :] * ch_ref[r0:r0 + rows, :]
    w0, w1, w2 = w_ref[0:1, :], w_ref[1:2, :], w_ref[2:3, :]
    for r0 in range(0, seq, rows):
        acc = (w2 * u_ref[8 + r0:8 + r0 + rows, :] + w1 * u_ref[7 + r0:7 + r0 + rows, :]
               + w0 * u_ref[6 + r0:6 + r0 + rows, :])
        o_ref[r0:r0 + rows, :] = cb_ref[r0:r0 + rows, :] * acc


def _conv_mixer(cb, cc, ch, conv_w, batch, seq):
    spec = pl.BlockSpec((seq, CONV_DIM), lambda b: (b, 0))
    return pl.pallas_call(
        functools.partial(_conv_kernel, rows=256),
        grid=(batch,), in_specs=[spec, spec, spec, _full(conv_w)], out_specs=spec,
        out_shape=jax.ShapeDtypeStruct((batch * seq, CONV_DIM), F32),
        scratch_shapes=[pltpu.VMEM((seq + 8, CONV_DIM), F32)],
        compiler_params=_cparams(("parallel",), 48), name="conv_mixer",
    )(cb, cc, ch, conv_w)


_SSM_CHUNKS = ((0, 512), (512, 512), (1024, 512), (1536, 256))


def _gelu_tanh(x):
    return 0.5 * x * (1.0 + jnp.tanh(math.sqrt(2.0 / math.pi) * (x + 0.044715 * (x * x * x))))


def _ssm_kernel(u_ref, bf_ref, are_ref, aim_ref, cc_ref, d_ref, gw_ref, gb_ref, o_ref, xs_ref, st_ref,
                *, batch, tc):
    @pl.when(pl.program_id(0) == 0)
    def _():
        st_ref[...] = jnp.zeros_like(st_ref)

    u = u_ref[...]
    xs_ref[...] = jnp.dot(u.astype(BF16), bf_ref[...], preferred_element_type=F32)
    per_tile = 8 // batch
    for c0, cw in _SSM_CHUNKS:
        re_l, im_l = slice(c0, c0 + cw), slice(SSM_N + c0, SSM_N + c0 + cw)
        ar = jnp.broadcast_to(are_ref[:, re_l], (batch, cw))
        ai = jnp.broadcast_to(aim_ref[:, re_l], (batch, cw))

        def body(k, carry, re_l=re_l, im_l=im_l, ar=ar, ai=ai):
            sr, si = carry
            rows = pl.ds(pl.multiple_of(k * 8, 8), 8)
            tile_r, tile_i = xs_ref[rows, re_l], xs_ref[rows, im_l]
            out_r, out_i = [], []
            for j in range(per_tile):
                step = slice(j * batch, (j + 1) * batch)
                sr, si = ar * sr - ai * si + tile_r[step], ar * si + ai * sr + tile_i[step]
                out_r.append(sr)
                out_i.append(si)
            xs_ref[rows, re_l] = jnp.concatenate(out_r, axis=0)
            xs_ref[rows, im_l] = jnp.concatenate(out_i, axis=0)
            return sr, si

        sr, si = lax.fori_loop(0, tc // per_tile, body, (st_ref[0:batch, re_l], st_ref[0:batch, im_l]))
        st_ref[0:batch, re_l] = sr
        st_ref[0:batch, im_l] = si
    y = jnp.dot(xs_ref[...].astype(BF16), cc_ref[...], preferred_element_type=F32) + d_ref[...] * u
    g = _gelu_tanh(y)
    o_ref[...] = g * _sigmoid(jnp.dot(g.astype(BF16), gw_ref[...], preferred_element_type=F32) + gb_ref[...])


def _ssm_mixer(su_tm, prm, batch, seq, tc=128):
    rows = tc * batch
    spec = pl.BlockSpec((rows, SSM_PAD), lambda c: (c, 0))
    args = (prm["bf"], prm["a_re"], prm["a_im"], prm["cc"], prm["d"], prm["gw"], prm["gb"])
    return pl.pallas_call(
        functools.partial(_ssm_kernel, batch=batch, tc=tc),
        grid=(seq // tc,), in_specs=[spec] + [_full(a) for a in args], out_specs=spec,
        out_shape=jax.ShapeDtypeStruct((seq * batch, SSM_PAD), F32),
        scratch_shapes=[pltpu.VMEM((rows, 2 * SSM_N), F32), pltpu.VMEM((8, 2 * SSM_N), F32)],
        compiler_params=_cparams(("arbitrary",), 48), name="ssm_mixer",
    )(su_tm, *args)


def _ssm_params(lam_re, lam_im, log_dt, b_re, b_im, c_re, c_im, d_skip, glu_w, glu_b):
    dt = jnp.exp(log_dt)[:, None]
    mag = jnp.exp(lam_re * dt)
    a_re = mag * jnp.cos(lam_im * dt)
    a_im = mag * jnp.sin(lam_im * dt)
    nr, ni = a_re - 1.0, a_im
    den = lam_re * lam_re + lam_im * lam_im
    f_re = (nr * lam_re + ni * lam_im) / den
    f_im = (ni * lam_re - nr * lam_im) / den
    bfr = f_re[:, :, None] * b_re - f_im[:, :, None] * b_im
    bfi = f_re[:, :, None] * b_im + f_im[:, :, None] * b_re
    eye = jnp.eye(SSM_GROUPS, dtype=F32)

    def in_blockdiag(m):
        return jnp.einsum("gpc,gh->gchp", m, eye).reshape(SSM_DIM, SSM_N)

    def out_blockdiag(m):
        return jnp.einsum("gcp,gh->gphc", m, eye).reshape(SSM_N, SSM_DIM)

    pad = SSM_PAD - SSM_DIM
    bf = jnp.concatenate([in_blockdiag(bfr), in_blockdiag(bfi)], axis=1)
    cc = jnp.concatenate([out_blockdiag(c_re), -out_blockdiag(c_im)], axis=0)
    return {
        "bf": jnp.pad(bf, ((0, pad), (0, 0))).astype(BF16),
        "cc": jnp.pad(cc, ((0, 0), (0, pad))).astype(BF16),
        "a_re": a_re.reshape(1, SSM_N), "a_im": a_im.reshape(1, SSM_N),
        "d": jnp.pad(d_skip, (0, pad)).reshape(1, SSM_PAD),
        "gw": jnp.pad(glu_w, ((0, pad), (0, pad))).astype(BF16),
        "gb": jnp.pad(glu_b, (0, pad)).reshape(1, SSM_PAD),
    }


def _rope_lanes(x, c, s1, s2, half, width):
    return x * c + pltpu.roll(x, width - half, 1) * s1 + pltpu.roll(x, half, 1) * s2


def _dil_kernel(q_ref, k_ref, v_ref, c_ref, s1_ref, s2_ref, o_ref, l_ref, *, dil):
    seq = q_ref.shape[1]
    nb = (seq // dil) // DIL_BLK
    width = DIL_PLANES * LANE
    lane = lax.broadcasted_iota(jnp.int32, (1, width), 1)
    cmask = [(lane // DIL_HEAD_DIM == c).astype(F32) for c in range(DIL_HPG)]
    qi = lax.broadcasted_iota(jnp.int32, (DIL_HPG * DIL_BLK, 2 * DIL_BLK), 0) % DIL_BLK
    ki = lax.broadcasted_iota(jnp.int32, (DIL_HPG * DIL_BLK, 2 * DIL_BLK), 1)
    band = (ki >= qi) & (ki <= qi + DIL_BLK)

    def planes(ref, rows):
        return jnp.concatenate([ref[j, rows, :] for j in range(DIL_PLANES)], axis=1)

    def table(ref, rows):
        return jnp.concatenate([ref[rows, :]] * DIL_PLANES, axis=1)

    def roped(ref, rows):
        return _rope_lanes(planes(ref, rows), table(c_ref, rows), table(s1_ref, rows), table(s2_ref, rows),
                           DIL_ROT // 2, width)

    def block(idx, carry):
        m, i = idx // nb, idx % nb

        def rows_of(blk):
            if dil == 1:
                return pl.ds(pl.multiple_of(blk * DIL_BLK, DIL_BLK), DIL_BLK)
            return pl.ds(blk * DIL_BLK * dil + m, DIL_BLK, stride=dil)

        rows_q, rows_p = rows_of(i), rows_of(jnp.maximum(i - 1, 0))
        q = roped(q_ref, rows_q) * (1.0 / math.sqrt(DIL_HEAD_DIM))
        kcat = jnp.concatenate([roped(k_ref, rows_p), roped(k_ref, rows_q)], axis=0).astype(BF16)
        vcat = jnp.concatenate([planes(v_ref, rows_p), planes(v_ref, rows_q)], axis=0).astype(BF16)
        qs = jnp.concatenate([q * cm for cm in cmask], axis=0).astype(BF16)
        s = lax.dot_general(qs, kcat, (((1,), (1,)), ((), ())), preferred_element_type=F32)
        s = jnp.where(band & ((ki >= DIL_BLK) | (i > 0)), s, NEG_BIG)
        mx = jnp.max(s, axis=-1, keepdims=True)
        p = jnp.exp(s - mx)
        den = jnp.sum(p, axis=-1, keepdims=True)
        o = jnp.dot(p.astype(BF16), vcat, preferred_element_type=F32) / den
        lse = mx + jnp.log(den)
        out = jnp.zeros((DIL_BLK, width), F32)
        lout = jnp.zeros((DIL_BLK, width), F32)
        for c, cm in enumerate(cmask):
            out = out + cm * o[c * DIL_BLK:(c + 1) * DIL_BLK, :]
            lout = lout + cm * lse[c * DIL_BLK:(c + 1) * DIL_BLK, :]
        for j in range(DIL_PLANES):
            o_ref[j, rows_q, :] = out[:, j * LANE:(j + 1) * LANE]
            l_ref[j, rows_q, :] = lout[:, j * LANE:(j + 1) * LANE]
        return carry

    lax.fori_loop(0, dil * nb, block, 0)


def _dil_group(q, k, v, tabs, dil, batch, seq):
    spec = pl.BlockSpec((DIL_PLANES, seq, LANE), lambda b: (0, b, 0))
    tspec = pl.BlockSpec((seq, LANE), lambda b: (b, 0))
    sds = jax.ShapeDtypeStruct((DIL_PLANES, batch * seq, LANE), F32)
    return pl.pallas_call(
        functools.partial(_dil_kernel, dil=dil), grid=(batch,), in_specs=[spec] * 3 + [tspec] * 3,
        out_specs=[spec, spec], out_shape=[sds, sds],
        compiler_params=_cparams(("parallel",), 48), name=f"dil_attn_d{dil}",
    )(q, k, v, *tabs)


def _dil_combine_kernel(o0, o1, o2, l0, l1, l2, y0, y1, y2):
    a, b, c = l0[...], l1[...], l2[...]
    mx = jnp.maximum(jnp.maximum(a, b), c)
    ea, eb, ec = jnp.exp(a - mx), jnp.exp(b - mx), jnp.exp(c - mx)
    inv = 1.0 / (ea + eb + ec)
    y0[...] = o0[...] * (ea * inv)
    y1[...] = o1[...] * (eb * inv)
    y2[...] = o2[...] * (ec * inv)


def _dil_combine(outs, lses, tm=2048):
    shape = outs[0].shape
    rows = shape[0] * shape[1]
    flat = lambda a: a.reshape(rows, LANE)
    spec = pl.BlockSpec((tm, LANE), lambda i: (i, 0))
    sds = jax.ShapeDtypeStruct((rows, LANE), F32)
    ys = pl.pallas_call(
        _dil_combine_kernel, grid=(rows // tm,), in_specs=[spec] * 6, out_specs=[spec] * 3,
        out_shape=[sds] * 3, compiler_params=_cparams(("parallel",), 32), name="dil_combine",
    )(*[flat(a) for a in outs], *[flat(a) for a in lses])
    return [y.reshape(shape) for y in ys]


def _mla_prep_kernel(cq_ref, ckv_ref, kr_ref, gq_ref, gkv_ref, wq_ref, wk_ref, wv_ref, c_ref, s1_ref, s2_ref,
                     q_ref, k_ref, v_ref):
    c, s1, s2 = c_ref[...], s1_ref[...], s2_ref[...]
    rope = functools.partial(_rope_lanes, c=c, s1=s1, s2=s2, half=MLA_ROPE // 2, width=MLA_HW)
    qn = _rms(cq_ref[...], gq_ref[...]).astype(BF16)
    kvn = _rms(ckv_ref[...], gkv_ref[...]).astype(BF16)
    q = jnp.dot(qn, wq_ref[...], preferred_element_type=F32)
    kn = jnp.dot(kvn, wk_ref[...], preferred_element_type=F32)
    v_ref[...] = jnp.dot(kvn, wv_ref[...], preferred_element_type=F32).astype(BF16)
    kr = rope(kr_ref[...])
    scale = math.log2(math.e) / math.sqrt(MLA_QK)
    for h in range(MLA_HEADS):
        sl = slice(h * MLA_HW, (h + 1) * MLA_HW)
        q_ref[:, sl] = (rope(q[:, sl]) * scale).astype(BF16)
        k_ref[:, sl] = (kn[:, sl] + kr).astype(BF16)


def _mla_prep(cq, ckv, kr, gq, gkv, wq, wk, wv, tabs, tm=512):
    rows = cq.shape[0]

    def rspec(n):
        return pl.BlockSpec((tm, n), lambda i: (i, 0))

    hw = MLA_HEADS * MLA_HW
    return pl.pallas_call(
        _mla_prep_kernel, grid=(rows // tm,),
        in_specs=[rspec(MLA_Q_RANK), rspec(MLA_KV_RANK), rspec(MLA_HW), _full(gq), _full(gkv), _full(wq),
                  _full(wk), _full(wv), rspec(MLA_HW), rspec(MLA_HW), rspec(MLA_HW)],
        out_specs=[rspec(hw), rspec(hw), rspec(MLA_VW)],
        out_shape=[jax.ShapeDtypeStruct((rows, hw), BF16), jax.ShapeDtypeStruct((rows, hw), BF16),
                   jax.ShapeDtypeStruct((rows, MLA_VW), BF16)],
        compiler_params=_cparams(("parallel",), 48), name="mla_prep",
    )(cq, ckv, kr, gq, gkv, wq, wk, wv, *tabs)


def _mla_attn_kernel(q_ref, k_ref, v_ref, o_ref, *, tq, group):
    qi = pl.program_id(1)
    row = lax.broadcasted_iota(jnp.int32, (tq, tq), 0)
    col = lax.broadcasted_iota(jnp.int32, (tq, tq), 1)
    causal = col <= row

    def heads_out(heads):
        qs = [q_ref[:, h * MLA_HW:(h + 1) * MLA_HW] for h in heads]

        def step(kj, carry, masked):
            rows = pl.ds(pl.multiple_of(kj * tq, tq), tq)
            scores = [lax.dot_general(q, k_ref[rows, h * MLA_HW:(h + 1) * MLA_HW], (((1,), (1,)), ((), ())),
                                      preferred_element_type=F32) for h, q in zip(heads, qs)]
            stats = []
            for s, (m, l, acc) in zip(scores, carry):
                if masked:
                    s = jnp.where(causal, s, NEG_BIG)
                mn = jnp.maximum(m, jnp.max(s, axis=-1, keepdims=True))
                alpha = jnp.exp2(m - mn)
                p = jnp.exp2(s - mn)
                stats.append((mn, alpha, alpha * l + jnp.sum(p, axis=-1, keepdims=True), p.astype(BF16)))
            out = []
            for h, (mn, alpha, l, p), (_, _, acc) in zip(heads, stats, carry):
                v = v_ref[rows, (h // 2) * MLA_HW:(h // 2 + 1) * MLA_HW]
                out.append((mn, l, alpha * acc + jnp.dot(p, v, preferred_element_type=F32)))
            return tuple(out)

        init = tuple((jnp.full((tq, 1), NEG_BIG, F32), jnp.zeros((tq, 1), F32), jnp.zeros((tq, MLA_HW), F32))
                     for _ in heads)
        carry = lax.fori_loop(0, qi, functools.partial(step, masked=False), init)
        return [acc / l for _, l, acc in step(qi, carry, True)]

    for h0 in range(0, MLA_HEADS, group):
        heads = list(range(h0, min(h0 + group, MLA_HEADS)))
        for h, o in zip(heads, heads_out(heads)):
            lo = (h // 2) * MLA_HW + (h % 2) * MLA_V
            o_ref[:, lo:lo + MLA_V] = o[:, (h % 2) * MLA_V:(h % 2 + 1) * MLA_V]
    if MLA_HEADS % 2:
        o_ref[:, MLA_HEADS * MLA_V:] = jnp.zeros((tq, MLA_VW - MLA_HEADS * MLA_V), F32)


def _mla_attn(q, k, v, batch, seq, tq=256, group=5):
    hw = MLA_HEADS * MLA_HW
    n_q = seq // tq
    return pl.pallas_call(
        functools.partial(_mla_attn_kernel, tq=tq, group=group), grid=(batch, n_q),
        in_specs=[pl.BlockSpec((tq, hw), lambda b, i: (b * n_q + i, 0)),
                  pl.BlockSpec((seq, hw), lambda b, i: (b, 0)),
                  pl.BlockSpec((seq, MLA_VW), lambda b, i: (b, 0))],
        out_specs=pl.BlockSpec((tq, MLA_VW), lambda b, i: (b * n_q + i, 0)),
        out_shape=jax.ShapeDtypeStruct((batch * seq, MLA_VW), F32),
        compiler_params=_cparams(("parallel", "arbitrary"), 48), name="mla_attn",
    )(q, k, v)


def _out_proj_kernel(yc_ref, ys_ref, yd0_ref, yd1_ref, yd2_ref, ym_ref, h_ref,
                     wc_ref, ws_ref, wd0_ref, wd1_ref, wd2_ref, wm_ref,
                     gf_ref, wrh_ref, wrl_ref, br_ref,
                     h1_ref, xn_ref, ri_ref, rg_ref):
    def mm(y_ref, w_ref):
        if len(y_ref.shape) == 3:
            return sum(jnp.dot(y_ref[j].astype(BF16), w_ref[j * LANE:(j + 1) * LANE, :], preferred_element_type=F32)
                       for j in range(y_ref.shape[0]))
        return jnp.dot(y_ref[...].astype(BF16), w_ref[...], preferred_element_type=F32)

    mix = (mm(yc_ref, wc_ref) + mm(ys_ref, ws_ref) + mm(yd0_ref, wd0_ref) + mm(yd1_ref, wd1_ref)
           + mm(yd2_ref, wd2_ref) + mm(ym_ref, wm_ref))
    h1 = h_ref[...] + mix
    h1_ref[...] = h1
    hn = _rms(h1, gf_ref[...])
    hi = hn.astype(BF16)
    hi32 = hi.astype(F32)
    lo = (hn - hi32).astype(BF16)
    _rows_to_slabs(xn_ref, 0, hi32)
    logits = (jnp.dot(hi, wrh_ref[...], preferred_element_type=F32)
              + jnp.dot(hi, wrl_ref[...], preferred_element_type=F32)
              + jnp.dot(lo, wrh_ref[...], preferred_element_type=F32)) + br_ref[...]
    lane = lax.broadcasted_iota(jnp.int32, logits.shape, 1)
    gl = jnp.where(lane < MOE_GROUPS, logits, NEG_BIG)
    gmax = jnp.max(gl, axis=-1, keepdims=True)
    g_top_p = 1.0 / jnp.sum(jnp.exp(gl - gmax), axis=-1, keepdims=True)
    g_idx = jnp.min(jnp.where(gl == gmax, lane, ROUTE_W), axis=-1, keepdims=True)
    in_group = (lane >= MOE_GROUPS) & (lane < MOE_GROUPS + N_EXPERTS) & (((lane - MOE_GROUPS) >> 3) == g_idx)
    el = jnp.where(in_group, logits, NEG_BIG)
    v1 = jnp.max(el, axis=-1, keepdims=True)
    i1 = jnp.min(jnp.where(el == v1, lane, ROUTE_W), axis=-1, keepdims=True)
    el2 = jnp.where(lane == i1, NEG_BIG, el)
    v2 = jnp.max(el2, axis=-1, keepdims=True)
    i2 = jnp.min(jnp.where(el2 == v2, lane, ROUTE_W), axis=-1, keepdims=True)
    e2 = jnp.exp(v2 - v1)
    w1 = g_top_p / (1.0 + e2)
    w2 = g_top_p * e2 / (1.0 + e2)
    ri_ref[...] = jnp.where(lane == 0, i1 - MOE_GROUPS, jnp.where(lane == 1, i2 - MOE_GROUPS, 0))
    rg_ref[...] = jnp.where(lane == 0, w1, jnp.where(lane == 1, w2, 0.0))


def _out_proj(ys, h, ws, gf, wrh, wrl, br, batch, seq, tm=256):
    rows = batch * seq
    n_s = seq // tm
    yc, ys_tm, yd0, yd1, yd2, ym = ys

    def rspec(n):
        return pl.BlockSpec((tm, n), lambda b, i: (b * n_s + i, 0))

    pspec = pl.BlockSpec((DIL_PLANES, tm, LANE), lambda b, i: (0, b * n_s + i, 0))
    in_specs = [rspec(CONV_DIM), pl.BlockSpec((tm, SSM_PAD), lambda b, i: (i, b)), pspec, pspec, pspec,
                rspec(MLA_VW), rspec(D_MODEL)]
    in_specs += [_full(w) for w in ws] + [_full(gf), _full(wrh), _full(wrl), _full(br)]
    return pl.pallas_call(
        _out_proj_kernel, grid=(batch, n_s), in_specs=in_specs,
        out_specs=[rspec(D_MODEL), pl.BlockSpec((tm * ROW_SLABS, LANE), lambda b, i: (b * n_s + i, 0)),
                   rspec(ROUTE_W), rspec(ROUTE_W)],
        out_shape=[jax.ShapeDtypeStruct((rows, D_MODEL), F32), jax.ShapeDtypeStruct((rows * ROW_SLABS, LANE), F32),
                   jax.ShapeDtypeStruct((rows, ROUTE_W), jnp.int32), jax.ShapeDtypeStruct((rows, ROUTE_W), F32)],
        compiler_params=_cparams(("parallel", "parallel"), 48), name="out_proj_router",
    )(yc, ys_tm, yd0, yd1, yd2, ym, h, *ws, gf, wrh, wrl, br)


def _moe_plan(route_i, n_tok):
    n_asg = n_tok * MOE_TOPK
    nblk = (n_asg + N_EXPERTS * (MOE_BLOCK - 1) + MOE_BLOCK - 1) // MOE_BLOCK
    flat_e = route_i[:, :MOE_TOPK].reshape(-1)
    onehot = (flat_e[:, None] == jnp.arange(N_EXPERTS, dtype=jnp.int32)[None, :]).astype(jnp.int32)
    csum = jnp.cumsum(onehot, axis=0)
    rank = jnp.take_along_axis(csum, flat_e[:, None], axis=1)[:, 0] - 1
    counts = csum[-1]
    padded = ((counts + MOE_BLOCK - 1) // MOE_BLOCK) * MOE_BLOCK
    pend = jnp.cumsum(padded)
    dest = (pend - padded)[flat_e] + rank
    slot_asg = jnp.full(((nblk + 1) * MOE_BLOCK,), -1, jnp.int32).at[dest].set(jnp.arange(n_asg, dtype=jnp.int32))
    block_expert = jnp.minimum(
        jnp.searchsorted(pend, jnp.arange(nblk, dtype=jnp.int32) * MOE_BLOCK, side="right"), N_EXPERTS - 1
    ).astype(jnp.int32)
    n_used = (pend[-1] // MOE_BLOCK).astype(jnp.int32)
    prev = jnp.concatenate([jnp.full((1,), -1, jnp.int32), block_expert[:-1]])
    is_first = (block_expert != prev).astype(jnp.int32)
    w_slot = (jnp.cumsum(is_first) - 1) % 2
    run_end = (pend // MOE_BLOCK)[block_expert]
    nxt = jnp.where(run_end < n_used, block_expert[jnp.minimum(run_end, nblk - 1)], -1)
    sched = jnp.stack([block_expert, is_first, w_slot, nxt]).astype(jnp.int32)
    return sched, slot_asg, n_used.reshape(1), nblk


def _moe_kernel(sched_ref, asg_ref, nu_ref, x_hbm, wg_hbm, wu_hbm, wd_hbm, y_hbm,
                xbuf, ybuf, wgf, wuf, wdf, wgb, wub, wdb, gsem, ssem, wsem, *, n_tok, layer):
    i = pl.program_id(0)
    n_used = nu_ref[0]
    slot = i % 2
    buf_rows = MOE_BLOCK * ROW_SLABS
    active = i < n_used
    first = active & (sched_ref[1, i] == 1)
    w_slot = sched_ref[2, i]

    def weights(e, ws):
        return [pltpu.make_async_copy(w_hbm.at[layer, e], buf.at[ws], wsem.at[ws])
                for w_hbm, buf in ((wg_hbm, wgf), (wu_hbm, wuf), (wd_hbm, wdf))]

    def slab_rows(row):
        return pl.ds(pl.multiple_of(row * ROW_SLABS, ROW_SLABS), ROW_SLABS)

    def gather(blk, sl):
        for r in range(MOE_BLOCK):
            tok = jnp.maximum(asg_ref[blk * MOE_BLOCK + r], 0) >> 1
            yield pltpu.make_async_copy(x_hbm.at[slab_rows(tok), :], xbuf.at[slab_rows(sl * MOE_BLOCK + r), :],
                                        gsem.at[sl])

    def scatter(blk, sl):
        for r in range(MOE_BLOCK):
            a = asg_ref[blk * MOE_BLOCK + r]
            dst = jnp.where(a >= 0, (a & 1) * n_tok + (a >> 1), MOE_TOPK * n_tok + sl * MOE_BLOCK + r)
            yield pltpu.make_async_copy(ybuf.at[slab_rows(sl * MOE_BLOCK + r), :], y_hbm.at[slab_rows(dst), :],
                                        ssem.at[sl])

    @pl.when(i == 0)
    def _():
        ybuf[...] = jnp.zeros_like(ybuf)
        dump = pltpu.make_async_copy(ybuf, y_hbm.at[pl.ds(MOE_TOPK * n_tok * ROW_SLABS, 2 * buf_rows), :], ssem.at[0])
        dump.start()
        dump.wait()

    @pl.when((i == 0) & (n_used > 0))
    def _():
        for cp in weights(sched_ref[0, 0], 0):
            cp.start()
        for cp in gather(0, 0):
            cp.start()

    @pl.when(first)
    def _():
        for cp in weights(sched_ref[0, i], w_slot):
            cp.wait()

    @pl.when(first & (sched_ref[3, i] >= 0))
    def _():
        for cp in weights(sched_ref[3, i], 1 - w_slot):
            cp.start(priority=1)

    @pl.when(first)
    def _():
        wgb[...] = wgf[w_slot].astype(BF16)
        wub[...] = wuf[w_slot].astype(BF16)
        wdb[...] = wdf[w_slot].astype(BF16)

    @pl.when(active & (i >= 2))
    def _():
        for cp in scatter(i - 2, slot):
            cp.wait()

    @pl.when(active)
    def _():
        for cp in gather(i, slot):
            cp.wait()
        for cp in gather(i + 1, 1 - slot):
            cp.start()
        x = _slabs_to_rows(xbuf, slot * buf_rows, MOE_BLOCK).astype(BF16)
        a = jnp.dot(x, wgb[...], preferred_element_type=F32)
        b = jnp.dot(x, wub[...], preferred_element_type=F32)
        act = (a * _sigmoid(a) * b).astype(BF16)
        _rows_to_slabs(ybuf, slot * buf_rows, jnp.dot(act, wdb[...], preferred_element_type=F32))
        for r, cp in enumerate(scatter(i, slot)):
            cp.start(priority=r % 2)

    @pl.when(i == n_used - 1)
    def _():
        for cp in gather(i + 1, 1 - slot):
            cp.wait()
        for cp in scatter(i, slot):
            cp.wait()

    @pl.when((i == n_used - 1) & (i >= 1))
    def _():
        for cp in scatter(i - 1, 1 - slot):
            cp.wait()


def _moe_experts(xn, sched, slot_asg, n_used, nblk, w_gate, w_up, w_down, layer):
    n_tok = xn.shape[0] // ROW_SLABS
    buf_rows = 2 * MOE_BLOCK * ROW_SLABS
    any_spec = pl.BlockSpec(memory_space=pl.ANY)
    grid_spec = pltpu.PrefetchScalarGridSpec(
        num_scalar_prefetch=3, grid=(nblk,),
        in_specs=[any_spec, any_spec, any_spec, any_spec], out_specs=any_spec,
        scratch_shapes=[pltpu.VMEM((buf_rows, LANE), F32), pltpu.VMEM((buf_rows, LANE), F32),
                        pltpu.VMEM((2, D_MODEL, MOE_FF), F32), pltpu.VMEM((2, D_MODEL, MOE_FF), F32),
                        pltpu.VMEM((2, MOE_FF, D_MODEL), F32),
                        pltpu.VMEM((D_MODEL, MOE_FF), BF16), pltpu.VMEM((D_MODEL, MOE_FF), BF16),
                        pltpu.VMEM((MOE_FF, D_MODEL), BF16),
                        pltpu.SemaphoreType.DMA((2,)), pltpu.SemaphoreType.DMA((2,)), pltpu.SemaphoreType.DMA((2,))])
    return pl.pallas_call(
        functools.partial(_moe_kernel, n_tok=n_tok, layer=layer), grid_spec=grid_spec,
        out_shape=jax.ShapeDtypeStruct(((MOE_TOPK * n_tok + 2 * MOE_BLOCK) * ROW_SLABS, LANE), F32),
        compiler_params=_cparams(("arbitrary",), 56), name="moe_experts",
    )(sched, slot_asg, n_used, xn, w_gate, w_up, w_down)


def _ple_kernel(h1_ref, y0_ref, y1_ref, rg_ref, p_ref, gp_ref, wg_ref, wp_ref, gfin_ref, o_ref, *, final):
    rg = rg_ref[...]
    tm = rg.shape[0]
    ffn = rg[:, 0:1] * _slabs_to_rows(y0_ref, 0, tm) + rg[:, 1:2] * _slabs_to_rows(y1_ref, 0, tm)
    h2 = h1_ref[...] + ffn
    hn = _rms(h2, gp_ref[...]).astype(BF16)
    gate = _sigmoid(jnp.dot(hn, wg_ref[...], preferred_element_type=F32))
    pe = jnp.dot(p_ref[...].astype(BF16), wp_ref[...], preferred_element_type=F32)
    h3 = h2 + pe * gate
    o_ref[...] = _rms(h3, gfin_ref[...]) if final else h3


def _ple(h1, y2, rg, p, gp, wg, wp, gfin, final, tm=256):
    rows = h1.shape[0]
    n_t = rows // tm

    def rspec(n):
        return pl.BlockSpec((tm, n), lambda i: (i, 0))

    def yspec(k):
        return pl.BlockSpec((tm * ROW_SLABS, LANE), lambda i: (k * n_t + i, 0))

    return pl.pallas_call(
        functools.partial(_ple_kernel, final=final), grid=(n_t,),
        in_specs=[rspec(D_MODEL), yspec(0), yspec(1), rspec(ROUTE_W), rspec(PLE_DIM), _full(gp), _full(wg),
                  _full(wp), _full(gfin)],
        out_specs=rspec(D_MODEL), out_shape=jax.ShapeDtypeStruct((rows, D_MODEL), F32),
        compiler_params=_cparams(("parallel",), 48), name="moe_combine_ple",
    )(h1, y2, y2, rg, p, gp, wg, wp, gfin)


def _rope_tables(positions, rot_dim, lead, period, reps):
    half = rot_dim // 2
    inv_freq = ROPE_THETA ** (-jnp.arange(0, rot_dim, 2, dtype=F32) / rot_dim)
    ang = positions.astype(F32).reshape(-1, 1) * inv_freq
    cos, sin = jnp.cos(ang), jnp.sin(ang)
    n = cos.shape[0]
    tail = period - lead - rot_dim
    c = jnp.concatenate([jnp.ones((n, lead), F32), cos, cos, jnp.ones((n, tail), F32)], axis=1)
    s1 = jnp.concatenate([jnp.zeros((n, lead), F32), -sin, jnp.zeros((n, half + tail), F32)], axis=1)
    s2 = jnp.concatenate([jnp.zeros((n, lead + half), F32), sin, jnp.zeros((n, tail), F32)], axis=1)
    return tuple(jnp.tile(t, (1, reps)) for t in (c, s1, s2))


def _split_cols(w, sizes):
    out, o = [], 0
    for n in sizes:
        out.append(w[:, o:o + n])
        o += n
    return out


def _in_weights(w_in):
    cb, cc, ch, su, dq, dk, dv, cq, ckv, kr = _split_cols(
        w_in, (CONV_DIM, CONV_DIM, CONV_DIM, SSM_DIM, 3 * DIL_GW, 3 * DIL_GW, 3 * DIL_GW, MLA_Q_RANK, MLA_KV_RANK,
               MLA_ROPE))
    ws = [cb, cc, ch, jnp.pad(su, ((0, 0), (0, SSM_PAD - SSM_DIM)))]
    kinds = ["rows", "rows", "rows", "time_major"]
    for g in range(len(DIL_PATTERNS)):
        ws += [jnp.pad(m[:, g * DIL_GW:(g + 1) * DIL_GW], ((0, 0), (0, DIL_PLANES * LANE - DIL_GW)))
               for m in (dq, dk, dv)]
        kinds += ["planes"] * 3
    ws += [cq, ckv, jnp.pad(kr, ((0, 0), (MLA_NOPE, MLA_HW - MLA_NOPE - MLA_ROPE)))]
    kinds += ["rows"] * 3
    return [w.astype(BF16) for w in ws], kinds


def _mla_weights(w_uq, w_ukv):
    q = w_uq.reshape(MLA_Q_RANK, MLA_HEADS, MLA_QK)
    wq = jnp.pad(q, ((0, 0), (0, 0), (0, MLA_HW - MLA_QK))).reshape(MLA_Q_RANK, MLA_HEADS * MLA_HW)
    kv = w_ukv.reshape(MLA_KV_RANK, MLA_HEADS, MLA_NOPE + MLA_V)
    wk = jnp.pad(kv[:, :, :MLA_NOPE], ((0, 0), (0, 0), (0, MLA_HW - MLA_NOPE))).reshape(MLA_KV_RANK, -1)
    wv = jnp.pad(kv[:, :, MLA_NOPE:].reshape(MLA_KV_RANK, MLA_HEADS * MLA_V),
                 ((0, 0), (0, MLA_VW - MLA_HEADS * MLA_V)))
    return wq.astype(BF16), wk.astype(BF16), wv.astype(BF16)


def _out_weights(w_out):
    o_ssm, o_dil, o_mla = CONV_DIM, CONV_DIM + SSM_DIM, CONV_DIM + SSM_DIM + 3 * DIL_GW
    ws = [w_out[:o_ssm], jnp.pad(w_out[o_ssm:o_dil], ((0, SSM_PAD - SSM_DIM), (0, 0)))]
    ws += [jnp.pad(w_out[o_dil + g * DIL_GW:o_dil + (g + 1) * DIL_GW], ((0, DIL_PLANES * LANE - DIL_GW), (0, 0)))
           for g in range(3)]
    ws += [jnp.pad(w_out[o_mla:], ((0, MLA_VW - MLA_HEADS * MLA_V), (0, 0)))]
    return [w.astype(BF16) for w in ws]


def _router_weights(w_group, b_group, w_router, b_router):
    w = jnp.pad(jnp.concatenate([w_group, w_router], axis=1), ((0, 0), (0, ROUTE_W - MOE_GROUPS - N_EXPERTS)))
    hi = w.astype(BF16)
    lo = (w - hi.astype(F32)).astype(BF16)
    b = jnp.pad(jnp.concatenate([b_group, b_router]), (0, ROUTE_W - MOE_GROUPS - N_EXPERTS)).reshape(1, ROUTE_W)
    return hi, lo, b


def kernel(x, p, positions, norm_mix_g, w_in, conv_w, ssm_lam_re, ssm_lam_im, ssm_log_dt, ssm_b_re, ssm_b_im, ssm_c_re, ssm_c_im, ssm_d, ssm_glu_w, ssm_glu_b, mla_q_norm_g, mla_w_uq, mla_kv_norm_g, mla_w_ukv, w_out, norm_ffn_g, w_group, b_group, w_router, b_router, moe_w_gate, moe_w_up, moe_w_down, norm_ple_g, ple_w_proj, ple_w_gate, final_norm_g):
    batch, seq, _ = x.shape
    n_tok = batch * seq
    depth = w_in.shape[0]
    dil_tabs = _rope_tables(positions, DIL_ROT, 0, DIL_HEAD_DIM, LANE // DIL_HEAD_DIM)
    mla_tabs = _rope_tables(positions, MLA_ROPE, MLA_NOPE, MLA_HW, 1)
    h = x.reshape(n_tok, D_MODEL)
    for i in range(depth):
        z = _in_proj(h, norm_mix_g[i].reshape(1, -1), *_in_weights(w_in[i]), batch, seq)
        cb, cc, ch, su = z[:4]
        y_conv = _conv_mixer(cb, cc, ch, conv_w[i], batch, seq)
        prm = _ssm_params(ssm_lam_re[i], ssm_lam_im[i], ssm_log_dt[i], ssm_b_re[i], ssm_b_im[i], ssm_c_re[i],
                          ssm_c_im[i], ssm_d[i], ssm_glu_w[i], ssm_glu_b[i])
        y_ssm = _ssm_mixer(su.reshape(seq * batch, SSM_PAD), prm, batch, seq).reshape(seq, batch * SSM_PAD)
        outs, lses = [], []
        for g, (window, dil) in enumerate(DIL_PATTERNS):
            assert window // dil == DIL_BLK and (seq // dil) % DIL_BLK == 0
            o, l = _dil_group(z[4 + 3 * g], z[5 + 3 * g], z[6 + 3 * g], dil_tabs, dil, batch, seq)
            outs.append(o)
            lses.append(l)
        y_dil = _dil_combine(outs, lses)
        wq, wk, wv = _mla_weights(mla_w_uq[i], mla_w_ukv[i])
        q, k, v = _mla_prep(z[13], z[14], z[15], mla_q_norm_g[i].reshape(1, -1), mla_kv_norm_g[i].reshape(1, -1),
                            wq, wk, wv, mla_tabs)
        y_mla = _mla_attn(q, k, v, batch, seq)
        wrh, wrl, br = _router_weights(w_group[i], b_group[i], w_router[i], b_router[i])
        h1, xn, route_i, route_g = _out_proj(
            (y_conv, y_ssm, y_dil[0], y_dil[1], y_dil[2], y_mla), h, _out_weights(w_out[i]),
            norm_ffn_g[i].reshape(1, -1), wrh, wrl, br, batch, seq)
        block_expert, slot_asg, n_used, nblk = _moe_plan(route_i, n_tok)
        y2 = _moe_experts(xn, block_expert, slot_asg, n_used, nblk, moe_w_gate, moe_w_up, moe_w_down, i)
        h = _ple(h1, y2, route_g, p[i].reshape(n_tok, PLE_DIM), norm_ple_g[i].reshape(1, -1),
                 ple_w_gate[i].astype(BF16), ple_w_proj[i].astype(BF16), final_norm_g.reshape(1, -1),
                 final=(i == depth - 1))
    return h.reshape(batch, seq, D_MODEL)
```

```python
import functools
import math

import jax
import jax.numpy as jnp
from jax import lax
from jax.experimental import pallas as pl
from jax.experimental.pallas import tpu as pltpu

D_MODEL = 2048
PLE_DIM = 256
ROPE_THETA = 500000.0
NORM_EPS = 1e-6

CONV_DIM = 448
CONV_K = 3

SSM_DIM = 448
SSM_GROUP = 16
SSM_GROUPS = SSM_DIM // SSM_GROUP
SSM_STATE = 64
SSM_N = SSM_GROUPS * SSM_STATE
SSM_PAD = 512

DIL_HEAD_DIM = 64
DIL_ROT = DIL_HEAD_DIM // 4
DIL_PATTERNS = ((128, 1), (512, 4), (2048, 16))
DIL_HPG = 3
DIL_GW = DIL_HPG * DIL_HEAD_DIM
DIL_PLANES = 2
DIL_BLK = 128

MLA_HEADS = 9
MLA_Q_RANK = 384
MLA_KV_RANK = 256
MLA_NOPE = 64
MLA_ROPE = 32
MLA_V = 64
MLA_QK = MLA_NOPE + MLA_ROPE
MLA_HW = 128
MLA_VW = 640

MOE_GROUPS = 8
MOE_EPG = 8
N_EXPERTS = MOE_GROUPS * MOE_EPG
MOE_TOPK = 2
MOE_FF = 512
MOE_BLOCK = 128
ROUTE_W = 128

LANE = 128
PACK_SLABS = D_MODEL // 2 // LANE
ROW_CHUNK = 32
NEG_BIG = -1e30

BF16 = jnp.bfloat16
F32 = jnp.float32


def _cparams(sem, vmem_mb):
    return pltpu.CompilerParams(dimension_semantics=sem, vmem_limit_bytes=vmem_mb * 1024 * 1024)


def _rms(x, g):
    ms = jnp.mean(x * x, axis=-1, keepdims=True)
    return (x * lax.rsqrt(ms + NORM_EPS)) * g


def _sigmoid(x):
    return 1.0 / (1.0 + jnp.exp(-x))


def _full(a):
    return pl.BlockSpec(a.shape, lambda *_: (0,) * a.ndim)


_HI16 = 0xFFFF0000


def _pack_rows(ref, base, x):
    n = x.shape[0]
    bits = lax.bitcast_convert_type(x, jnp.uint32)
    for c in range(PACK_SLABS):
        lo = bits[:, c * LANE:(c + 1) * LANE] >> 16
        hi = bits[:, D_MODEL // 2 + c * LANE:D_MODEL // 2 + (c + 1) * LANE] & jnp.uint32(_HI16)
        ref[pl.ds(base + c, n, stride=PACK_SLABS), :] = lo | hi


def _unpack_rows(ref, base, n):
    lo, hi = [], []
    for c in range(PACK_SLABS):
        w = ref[pl.ds(base + c, n, stride=PACK_SLABS), :]
        lo.append(lax.bitcast_convert_type(w << 16, F32))
        hi.append(lax.bitcast_convert_type(w & jnp.uint32(_HI16), F32))
    return jnp.concatenate(lo + hi, axis=1)


def _in_proj_kernel(x_ref, g_ref, *refs, n_out):
    w_refs, o_refs = refs[:n_out], refs[n_out:]
    xn = _rms(x_ref[...], g_ref[...]).astype(BF16)
    for w_ref, o_ref in zip(w_refs, o_refs):
        res = jnp.dot(xn, w_ref[...], preferred_element_type=F32)
        if len(o_ref.shape) == 3:
            for j in range(o_ref.shape[0]):
                o_ref[j] = res[:, j * LANE:(j + 1) * LANE]
        else:
            o_ref[...] = res


def _in_proj(h, g, weights, kinds, batch, seq, tm=256):
    n_s = seq // tm
    in_specs = [pl.BlockSpec((tm, D_MODEL), lambda b, i: (b * n_s + i, 0)), _full(g)]
    in_specs += [_full(w) for w in weights]
    out_shape, out_specs = [], []
    for kind, w in zip(kinds, weights):
        n = w.shape[1]
        if kind == "time_major":
            out_shape.append(jax.ShapeDtypeStruct((seq, batch * n), F32))
            out_specs.append(pl.BlockSpec((tm, n), lambda b, i: (i, b)))
        elif kind == "planes":
            out_shape.append(jax.ShapeDtypeStruct((n // LANE, batch * seq, LANE), F32))
            out_specs.append(pl.BlockSpec((n // LANE, tm, LANE), lambda b, i: (0, b * n_s + i, 0)))
        else:
            out_shape.append(jax.ShapeDtypeStruct((batch * seq, n), F32))
            out_specs.append(pl.BlockSpec((tm, n), lambda b, i: (b * n_s + i, 0)))
    return pl.pallas_call(
        functools.partial(_in_proj_kernel, n_out=len(weights)),
        grid=(batch, n_s), in_specs=in_specs, out_specs=out_specs, out_shape=out_shape,
        compiler_params=_cparams(("parallel", "parallel"), 56), name="in_proj",
    )(h, g, *weights)


def _conv_kernel(cb_ref, cc_ref, ch_ref, w_ref, o_ref, u_ref, *, rows):
    seq = cb_ref.shape[0]
    u_ref[0:8, :] = jnp.zeros((8, CONV_DIM), F32)
    for r0 in range(0, seq, rows):
        u_ref[8 + r0:8 + r0 + rows, :] = cc_ref[r0:r0 + rows, :] * ch_ref[r0:r0 + rows, :]
    w0, w1, w2 = w_ref[0:1, :], w_ref[1:2, :], w_ref[2:3, :]
    for r0 in range(0, seq, rows):
        acc = (w2 * u_ref[8 + r0:8 + r0 + rows, :] + w1 * u_ref[7 + r0:7 + r0 + rows, :]
               + w0 * u_ref[6 + r0:6 + r0 + rows, :])
        o_ref[r0:r0 + rows, :] = cb_ref[r0:r0 + rows, :] * acc


def _conv_mixer(cb, cc, ch, conv_w, batch, seq):
    spec = pl.BlockSpec((seq, CONV_DIM), lambda b: (b, 0))
    return pl.pallas_call(
        functools.partial(_conv_kernel, rows=256),
        grid=(batch,), in_specs=[spec, spec, spec, _full(conv_w)], out_specs=spec,
        out_shape=jax.ShapeDtypeStruct((batch * seq, CONV_DIM), F32),
        scratch_shapes=[pltpu.VMEM((seq + 8, CONV_DIM), F32)],
        compiler_params=_cparams(("parallel",), 48), name="conv_mixer",
    )(cb, cc, ch, conv_w)


_SSM_CHUNKS = ((0, 512), (512, 512), (1024, 512), (1536, 256))


def _gelu_tanh(x):
    return 0.5 * x * (1.0 + jnp.tanh(math.sqrt(2.0 / math.pi) * (x + 0.044715 * (x * x * x))))


def _ssm_kernel(u_ref, bf_ref, are_ref, aim_ref, cc_ref, d_ref, gw_ref, gb_ref, o_ref, xs_ref, st_ref,
                *, batch, tc):
    @pl.when(pl.program_id(0) == 0)
    def _():
        st_ref[...] = jnp.zeros_like(st_ref)

    u = u_ref[...]
    xs_ref[...] = jnp.dot(u.astype(BF16), bf_ref[...], preferred_element_type=F32)
    per_tile = 8 // batch
    for c0, cw in _SSM_CHUNKS:
        re_l, im_l = slice(c0, c0 + cw), slice(SSM_N + c0, SSM_N + c0 + cw)
        ar = jnp.broadcast_to(are_ref[:, re_l], (batch, cw))
        ai = jnp.broadcast_to(aim_ref[:, re_l], (batch, cw))

        def body(k, carry, re_l=re_l, im_l=im_l, ar=ar, ai=ai):
            sr, si = carry
            rows = pl.ds(pl.multiple_of(k * 8, 8), 8)
            tile_r, tile_i = xs_ref[rows, re_l], xs_ref[rows, im_l]
            out_r, out_i = [], []
            for j in range(per_tile):
                step = slice(j * batch, (j + 1) * batch)
                sr, si = ar * sr - ai * si + tile_r[step], ar * si + ai * sr + tile_i[step]
                out_r.append(sr)
                out_i.append(si)
            xs_ref[rows, re_l] = jnp.concatenate(out_r, axis=0)
            xs_ref[rows, im_l] = jnp.concatenate(out_i, axis=0)
            return sr, si

        sr, si = lax.fori_loop(0, tc // per_tile, body, (st_ref[0:batch, re_l], st_ref[0:batch, im_l]))
        st_ref[0:batch, re_l] = sr
        st_ref[0:batch, im_l] = si
    y = jnp.dot(xs_ref[...].astype(BF16), cc_ref[...], preferred_element_type=F32) + d_ref[...] * u
    g = _gelu_tanh(y)
    o_ref[...] = g * _sigmoid(jnp.dot(g.astype(BF16), gw_ref[...], preferred_element_type=F32) + gb_ref[...])


def _ssm_mixer(su_tm, prm, batch, seq, tc=128):
    rows = tc * batch
    spec = pl.BlockSpec((rows, SSM_PAD), lambda c: (c, 0))
    args = (prm["bf"], prm["a_re"], prm["a_im"], prm["cc"], prm["d"], prm["gw"], prm["gb"])
    return pl.pallas_call(
        functools.partial(_ssm_kernel, batch=batch, tc=tc),
        grid=(seq // tc,), in_specs=[spec] + [_full(a) for a in args], out_specs=spec,
        out_shape=jax.ShapeDtypeStruct((seq * batch, SSM_PAD), F32),
        scratch_shapes=[pltpu.VMEM((rows, 2 * SSM_N), F32), pltpu.VMEM((8, 2 * SSM_N), F32)],
        compiler_params=_cparams(("arbitrary",), 48), name="ssm_mixer",
    )(su_tm, *args)


def _ssm_params(lam_re, lam_im, log_dt, b_re, b_im, c_re, c_im, d_skip, glu_w, glu_b):
    dt = jnp.exp(log_dt)[:, None]
    mag = jnp.exp(lam_re * dt)
    a_re = mag * jnp.cos(lam_im * dt)
    a_im = mag * jnp.sin(lam_im * dt)
    nr, ni = a_re - 1.0, a_im
    den = lam_re * lam_re + lam_im * lam_im
    f_re = (nr * lam_re + ni * lam_im) / den
    f_im = (ni * lam_re - nr * lam_im) / den
    bfr = f_re[:, :, None] * b_re - f_im[:, :, None] * b_im
    bfi = f_re[:, :, None] * b_im + f_im[:, :, None] * b_re
    eye = jnp.eye(SSM_GROUPS, dtype=F32)

    def in_blockdiag(m):
        return jnp.einsum("gpc,gh->gchp", m, eye).reshape(SSM_DIM, SSM_N)

    def out_blockdiag(m):
        return jnp.einsum("gcp,gh->gphc", m, eye).reshape(SSM_N, SSM_DIM)

    pad = SSM_PAD - SSM_DIM
    bf = jnp.concatenate([in_blockdiag(bfr), in_blockdiag(bfi)], axis=1)
    cc = jnp.concatenate([out_blockdiag(c_re), -out_blockdiag(c_im)], axis=0)
    return {
        "bf": jnp.pad(bf, ((0, pad), (0, 0))).astype(BF16),
        "cc": jnp.pad(cc, ((0, 0), (0, pad))).astype(BF16),
        "a_re": a_re.reshape(1, SSM_N), "a_im": a_im.reshape(1, SSM_N),
        "d": jnp.pad(d_skip, (0, pad)).reshape(1, SSM_PAD),
        "gw": jnp.pad(glu_w, ((0, pad), (0, pad))).astype(BF16),
        "gb": jnp.pad(glu_b, (0, pad)).reshape(1, SSM_PAD),
    }


def _rope_lanes(x, c, s1, s2, half, width):
    return x * c + pltpu.roll(x, width - half, 1) * s1 + pltpu.roll(x, half, 1) * s2


def _dil_kernel(q_ref, k_ref, v_ref, c_ref, s1_ref, s2_ref, o_ref, l_ref, *, dil):
    seq = q_ref.shape[1]
    nb = (seq // dil) // DIL_BLK
    width = DIL_PLANES * LANE
    lane = lax.broadcasted_iota(jnp.int32, (1, width), 1)
    cmask = [(lane // DIL_HEAD_DIM == c).astype(F32) for c in range(DIL_HPG)]
    qi = lax.broadcasted_iota(jnp.int32, (DIL_HPG * DIL_BLK, 2 * DIL_BLK), 0) % DIL_BLK
    ki = lax.broadcasted_iota(jnp.int32, (DIL_HPG * DIL_BLK, 2 * DIL_BLK), 1)
    band = (ki >= qi) & (ki <= qi + DIL_BLK)

    def planes(ref, rows):
        return jnp.concatenate([ref[j, rows, :] for j in range(DIL_PLANES)], axis=1)

    def table(ref, rows):
        return jnp.concatenate([ref[rows, :]] * DIL_PLANES, axis=1)

    def roped(ref, rows):
        return _rope_lanes(planes(ref, rows), table(c_ref, rows), table(s1_ref, rows), table(s2_ref, rows),
                           DIL_ROT // 2, width)

    def block(idx, carry):
        m, i = idx // nb, idx % nb

        def rows_of(blk):
            if dil == 1:
                return pl.ds(pl.multiple_of(blk * DIL_BLK, DIL_BLK), DIL_BLK)
            return pl.ds(blk * DIL_BLK * dil + m, DIL_BLK, stride=dil)

        rows_q, rows_p = rows_of(i), rows_of(jnp.maximum(i - 1, 0))
        q = roped(q_ref, rows_q) * (1.0 / math.sqrt(DIL_HEAD_DIM))
        kcat = jnp.concatenate([roped(k_ref, rows_p), roped(k_ref, rows_q)], axis=0).astype(BF16)
        vcat = jnp.concatenate([planes(v_ref, rows_p), planes(v_ref, rows_q)], axis=0).astype(BF16)
        qs = jnp.concatenate([q * cm for cm in cmask], axis=0).astype(BF16)
        s = lax.dot_general(qs, kcat, (((1,), (1,)), ((), ())), preferred_element_type=F32)
        s = jnp.where(band & ((ki >= DIL_BLK) | (i > 0)), s, NEG_BIG)
        mx = jnp.max(s, axis=-1, keepdims=True)
        p = jnp.exp(s - mx)
        den = jnp.sum(p, axis=-1, keepdims=True)
        o = jnp.dot(p.astype(BF16), vcat, preferred_element_type=F32) / den
        lse = mx + jnp.log(den)
        out = jnp.zeros((DIL_BLK, width), F32)
        lout = jnp.zeros((DIL_BLK, width), F32)
        for c, cm in enumerate(cmask):
            out = out + cm * o[c * DIL_BLK:(c + 1) * DIL_BLK, :]
            lout = lout + cm * lse[c * DIL_BLK:(c + 1) * DIL_BLK, :]
        for j in range(DIL_PLANES):
            o_ref[j, rows_q, :] = out[:, j * LANE:(j + 1) * LANE]
            l_ref[j, rows_q, :] = lout[:, j * LANE:(j + 1) * LANE]
        return carry

    lax.fori_loop(0, dil * nb, block, 0)


def _dil_group(q, k, v, tabs, dil, batch, seq):
    spec = pl.BlockSpec((DIL_PLANES, seq, LANE), lambda b: (0, b, 0))
    tspec = pl.BlockSpec((seq, LANE), lambda b: (b, 0))
    sds = jax.ShapeDtypeStruct((DIL_PLANES, batch * seq, LANE), F32)
    return pl.pallas_call(
        functools.partial(_dil_kernel, dil=dil), grid=(batch,), in_specs=[spec] * 3 + [tspec] * 3,
        out_specs=[spec, spec], out_shape=[sds, sds],
        compiler_params=_cparams(("parallel",), 48), name=f"dil_attn_d{dil}",
    )(q, k, v, *tabs)


def _dil_combine_kernel(o0, o1, o2, l0, l1, l2, y0, y1, y2):
    a, b, c = l0[...], l1[...], l2[...]
    mx = jnp.maximum(jnp.maximum(a, b), c)
    ea, eb, ec = jnp.exp(a - mx), jnp.exp(b - mx), jnp.exp(c - mx)
    inv = 1.0 / (ea + eb + ec)
    y0[...] = o0[...] * (ea * inv)
    y1[...] = o1[...] * (eb * inv)
    y2[...] = o2[...] * (ec * inv)


def _dil_combine(outs, lses, tm=2048):
    shape = outs[0].shape
    rows = shape[0] * shape[1]
    flat = lambda a: a.reshape(rows, LANE)
    spec = pl.BlockSpec((tm, LANE), lambda i: (i, 0))
    sds = jax.ShapeDtypeStruct((rows, LANE), F32)
    ys = pl.pallas_call(
        _dil_combine_kernel, grid=(rows // tm,), in_specs=[spec] * 6, out_specs=[spec] * 3,
        out_shape=[sds] * 3, compiler_params=_cparams(("parallel",), 32), name="dil_combine",
    )(*[flat(a) for a in outs], *[flat(a) for a in lses])
    return [y.reshape(shape) for y in ys]


def _mla_prep_kernel(cq_ref, ckv_ref, kr_ref, gq_ref, gkv_ref, wq_ref, wk_ref, wv_ref, c_ref, s1_ref, s2_ref,
                     q_ref, k_ref, v_ref):
    c, s1, s2 = c_ref[...], s1_ref[...], s2_ref[...]
    rope = functools.partial(_rope_lanes, c=c, s1=s1, s2=s2, half=MLA_ROPE // 2, width=MLA_HW)
    qn = _rms(cq_ref[...], gq_ref[...]).astype(BF16)
    kvn = _rms(ckv_ref[...], gkv_ref[...]).astype(BF16)
    q = jnp.dot(qn, wq_ref[...], preferred_element_type=F32)
    kn = jnp.dot(kvn, wk_ref[...], preferred_element_type=F32)
    v_ref[...] = jnp.dot(kvn, wv_ref[...], preferred_element_type=F32).astype(BF16)
    kr = rope(kr_ref[...])
    scale = math.log2(math.e) / math.sqrt(MLA_QK)
    for h in range(MLA_HEADS):
        sl = slice(h * MLA_HW, (h + 1) * MLA_HW)
        q_ref[:, sl] = (rope(q[:, sl]) * scale).astype(BF16)
        k_ref[:, sl] = (kn[:, sl] + kr).astype(BF16)


def _mla_prep(cq, ckv, kr, gq, gkv, wq, wk, wv, tabs, tm=512):
    rows = cq.shape[0]

    def rspec(n):
        return pl.BlockSpec((tm, n), lambda i: (i, 0))

    hw = MLA_HEADS * MLA_HW
    return pl.pallas_call(
        _mla_prep_kernel, grid=(rows // tm,),
        in_specs=[rspec(MLA_Q_RANK), rspec(MLA_KV_RANK), rspec(MLA_HW), _full(gq), _full(gkv), _full(wq),
                  _full(wk), _full(wv), rspec(MLA_HW), rspec(MLA_HW), rspec(MLA_HW)],
        out_specs=[rspec(hw), rspec(hw), rspec(MLA_VW)],
        out_shape=[jax.ShapeDtypeStruct((rows, hw), BF16), jax.ShapeDtypeStruct((rows, hw), BF16),
                   jax.ShapeDtypeStruct((rows, MLA_VW), BF16)],
        compiler_params=_cparams(("parallel",), 48), name="mla_prep",
    )(cq, ckv, kr, gq, gkv, wq, wk, wv, *tabs)


def _mla_attn_kernel(q_ref, k_ref, v_ref, o_ref, *, tq, group):
    qi = pl.program_id(1)
    row = lax.broadcasted_iota(jnp.int32, (tq, tq), 0)
    col = lax.broadcasted_iota(jnp.int32, (tq, tq), 1)
    causal = col <= row

    def heads_out(heads):
        qs = [q_ref[:, h * MLA_HW:(h + 1) * MLA_HW] for h in heads]

        def step(kj, carry, masked):
            rows = pl.ds(pl.multiple_of(kj * tq, tq), tq)
            scores = [lax.dot_general(q, k_ref[rows, h * MLA_HW:(h + 1) * MLA_HW], (((1,), (1,)), ((), ())),
                                      preferred_element_type=F32) for h, q in zip(heads, qs)]
            stats = []
            for s, (m, l, acc) in zip(scores, carry):
                if masked:
                    s = jnp.where(causal, s, NEG_BIG)
                mn = jnp.maximum(m, jnp.max(s, axis=-1, keepdims=True))
                alpha = jnp.exp2(m - mn)
                p = jnp.exp2(s - mn)
                stats.append((mn, alpha, alpha * l + jnp.sum(p, axis=-1, keepdims=True), p.astype(BF16)))
            out = []
            for h, (mn, alpha, l, p), (_, _, acc) in zip(heads, stats, carry):
                v = v_ref[rows, (h // 2) * MLA_HW:(h // 2 + 1) * MLA_HW]
                out.append((mn, l, alpha * acc + jnp.dot(p, v, preferred_element_type=F32)))
            return tuple(out)

        init = tuple((jnp.full((tq, 1), NEG_BIG, F32), jnp.zeros((tq, 1), F32), jnp.zeros((tq, MLA_HW), F32))
                     for _ in heads)
        carry = lax.fori_loop(0, qi, functools.partial(step, masked=False), init)
        return [acc / l for _, l, acc in step(qi, carry, True)]

    for h0 in range(0, MLA_HEADS, group):
        heads = list(range(h0, min(h0 + group, MLA_HEADS)))
        for h, o in zip(heads, heads_out(heads)):
            lo = (h // 2) * MLA_HW + (h % 2) * MLA_V
            o_ref[:, lo:lo + MLA_V] = o[:, (h % 2) * MLA_V:(h % 2 + 1) * MLA_V]
    if MLA_HEADS % 2:
        o_ref[:, MLA_HEADS * MLA_V:] = jnp.zeros((tq, MLA_VW - MLA_HEADS * MLA_V), F32)


def _mla_attn(q, k, v, batch, seq, tq=256, group=5):
    hw = MLA_HEADS * MLA_HW
    n_q = seq // tq
    return pl.pallas_call(
        functools.partial(_mla_attn_kernel, tq=tq, group=group), grid=(batch, n_q),
        in_specs=[pl.BlockSpec((tq, hw), lambda b, i: (b * n_q + i, 0)),
                  pl.BlockSpec((seq, hw), lambda b, i: (b, 0)),
                  pl.BlockSpec((seq, MLA_VW), lambda b, i: (b, 0))],
        out_specs=pl.BlockSpec((tq, MLA_VW), lambda b, i: (b * n_q + i, 0)),
        out_shape=jax.ShapeDtypeStruct((batch * seq, MLA_VW), F32),
        compiler_params=_cparams(("parallel", "arbitrary"), 48), name="mla_attn",
    )(q, k, v)


def _out_proj_kernel(yc_ref, ys_ref, yd0_ref, yd1_ref, yd2_ref, ym_ref, h_ref,
                     wc_ref, ws_ref, wd0_ref, wd1_ref, wd2_ref, wm_ref,
                     gf_ref, wr_ref, br_ref,
                     h1_ref, xn_ref, ri_ref, rg_ref):
    def mm(y_ref, w_ref):
        if len(y_ref.shape) == 3:
            return sum(jnp.dot(y_ref[j].astype(BF16), w_ref[j * LANE:(j + 1) * LANE, :], preferred_element_type=F32)
                       for j in range(y_ref.shape[0]))
        return jnp.dot(y_ref[...].astype(BF16), w_ref[...], preferred_element_type=F32)

    mix = (mm(yc_ref, wc_ref) + mm(ys_ref, ws_ref) + mm(yd0_ref, wd0_ref) + mm(yd1_ref, wd1_ref)
           + mm(yd2_ref, wd2_ref) + mm(ym_ref, wm_ref))
    h1 = h_ref[...] + mix
    h1_ref[...] = h1
    hn = _rms(h1, gf_ref[...])
    hi = hn.astype(BF16)
    hi32 = hi.astype(F32)
    lo = (hn - hi32).astype(BF16)
    _pack_rows(xn_ref, 0, hi32)
    r_hi = jnp.dot(hi, wr_ref[...], preferred_element_type=F32)
    r_lo = jnp.dot(lo, wr_ref[:, 0:ROUTE_W], preferred_element_type=F32)
    logits = r_hi[:, 0:ROUTE_W] + r_hi[:, ROUTE_W:2 * ROUTE_W] + r_lo + br_ref[...]
    lane = lax.broadcasted_iota(jnp.int32, logits.shape, 1)
    gl = jnp.where(lane < MOE_GROUPS, logits, NEG_BIG)
    gmax = jnp.max(gl, axis=-1, keepdims=True)
    g_top_p = 1.0 / jnp.sum(jnp.exp(gl - gmax), axis=-1, keepdims=True)
    g_idx = jnp.min(jnp.where(gl == gmax, lane, ROUTE_W), axis=-1, keepdims=True)
    in_group = (lane >= MOE_GROUPS) & (lane < MOE_GROUPS + N_EXPERTS) & (((lane - MOE_GROUPS) >> 3) == g_idx)
    el = jnp.where(in_group, logits, NEG_BIG)
    v1 = jnp.max(el, axis=-1, keepdims=True)
    i1 = jnp.min(jnp.where(el == v1, lane, ROUTE_W), axis=-1, keepdims=True)
    el2 = jnp.where(lane == i1, NEG_BIG, el)
    v2 = jnp.max(el2, axis=-1, keepdims=True)
    i2 = jnp.min(jnp.where(el2 == v2, lane, ROUTE_W), axis=-1, keepdims=True)
    e2 = jnp.exp(v2 - v1)
    w1 = g_top_p / (1.0 + e2)
    w2 = g_top_p * e2 / (1.0 + e2)
    ri_ref[...] = jnp.where(lane == 0, i1 - MOE_GROUPS, jnp.where(lane == 1, i2 - MOE_GROUPS, 0))
    rg_ref[...] = jnp.where(lane == 0, w1, jnp.where(lane == 1, w2, 0.0))


def _out_proj(ys, h, ws, gf, wr, br, batch, seq, tm=256):
    rows = batch * seq
    n_s = seq // tm
    yc, ys_tm, yd0, yd1, yd2, ym = ys

    def rspec(n):
        return pl.BlockSpec((tm, n), lambda b, i: (b * n_s + i, 0))

    pspec = pl.BlockSpec((DIL_PLANES, tm, LANE), lambda b, i: (0, b * n_s + i, 0))
    in_specs = [rspec(CONV_DIM), pl.BlockSpec((tm, SSM_PAD), lambda b, i: (i, b)), pspec, pspec, pspec,
                rspec(MLA_VW), rspec(D_MODEL)]
    in_specs += [_full(w) for w in ws] + [_full(gf), _full(wr), _full(br)]
    return pl.pallas_call(
        _out_proj_kernel, grid=(batch, n_s), in_specs=in_specs,
        out_specs=[rspec(D_MODEL), pl.BlockSpec((tm * PACK_SLABS, LANE), lambda b, i: (b * n_s + i, 0)),
                   rspec(ROUTE_W), rspec(ROUTE_W)],
        out_shape=[jax.ShapeDtypeStruct((rows, D_MODEL), F32),
                   jax.ShapeDtypeStruct((rows * PACK_SLABS, LANE), jnp.uint32),
                   jax.ShapeDtypeStruct((rows, ROUTE_W), jnp.int32), jax.ShapeDtypeStruct((rows, ROUTE_W), F32)],
        compiler_params=_cparams(("parallel", "parallel"), 48), name="out_proj_router",
    )(yc, ys_tm, yd0, yd1, yd2, ym, h, *ws, gf, wr, br)


def _moe_plan(route_i, n_tok):
    n_asg = n_tok * MOE_TOPK
    nblk = (n_asg + N_EXPERTS * (MOE_BLOCK - 1) + MOE_BLOCK - 1) // MOE_BLOCK
    flat_e = route_i[:, :MOE_TOPK].reshape(-1)
    onehot = (flat_e[:, None] == jnp.arange(N_EXPERTS, dtype=jnp.int32)[None, :]).astype(jnp.int32)
    csum = jnp.cumsum(onehot, axis=0)
    rank = jnp.take_along_axis(csum, flat_e[:, None], axis=1)[:, 0] - 1
    counts = csum[-1]
    padded = ((counts + MOE_BLOCK - 1) // MOE_BLOCK) * MOE_BLOCK
    pend = jnp.cumsum(padded)
    dest = (pend - padded)[flat_e] + rank
    slot_asg = jnp.full((nblk * MOE_BLOCK,), -1, jnp.int32).at[dest].set(jnp.arange(n_asg, dtype=jnp.int32))
    block_expert = jnp.minimum(
        jnp.searchsorted(pend, jnp.arange(nblk, dtype=jnp.int32) * MOE_BLOCK, side="right"), N_EXPERTS - 1
    ).astype(jnp.int32)
    n_used = (pend[-1] // MOE_BLOCK).astype(jnp.int32)
    prev = jnp.concatenate([jnp.full((1,), -1, jnp.int32), block_expert[:-1]])
    is_first = (block_expert != prev).astype(jnp.int32)
    w_slot = (jnp.cumsum(is_first) - 1) % 2
    run_end = (pend // MOE_BLOCK)[block_expert]
    nxt = jnp.where(run_end < n_used, block_expert[jnp.minimum(run_end, nblk - 1)], -1)
    blk = jnp.arange(nblk, dtype=jnp.int32)
    run_start = ((pend - padded) // MOE_BLOCK)[block_expert]
    rows_left = jnp.clip(counts[block_expert] - (blk - run_start) * MOE_BLOCK, 0, MOE_BLOCK)
    sched = jnp.stack([block_expert, is_first, w_slot, nxt, jnp.where(blk < n_used, rows_left, 0)])
    sched = jnp.pad(sched.astype(jnp.int32), ((0, 0), (0, 1)))
    return sched, slot_asg, n_used.reshape(1), nblk


def _moe_kernel(sched_ref, asg_ref, nu_ref, x_hbm, wg_hbm, wu_hbm, wd_hbm, y_hbm,
                xbuf, ybuf, wgf, wuf, wdf, wgb, wub, wdb, gsem, ssem, wsem, *, n_tok, layer):
    i = pl.program_id(0)
    n_used = nu_ref[0]
    slot = i % 2
    buf_rows = MOE_BLOCK * PACK_SLABS
    active = i < n_used
    first = active & (sched_ref[1, i] == 1)
    w_slot = sched_ref[2, i]

    def weights(e, ws):
        return [pltpu.make_async_copy(w_hbm.at[layer, e], buf.at[ws], wsem.at[ws])
                for w_hbm, buf in ((wg_hbm, wgf), (wu_hbm, wuf), (wd_hbm, wdf))]

    def slab_rows(row):
        return pl.ds(pl.multiple_of(row * PACK_SLABS, PACK_SLABS), PACK_SLABS)

    def gather_row(blk, sl, r):
        tok = jnp.maximum(asg_ref[blk * MOE_BLOCK + r], 0) >> 1
        return pltpu.make_async_copy(x_hbm.at[slab_rows(tok), :], xbuf.at[slab_rows(sl * MOE_BLOCK + r), :],
                                     gsem.at[sl])

    def scatter_row(blk, sl, r):
        a = asg_ref[blk * MOE_BLOCK + r]
        dst = jnp.where(a >= 0, (a & 1) * n_tok + (a >> 1), MOE_TOPK * n_tok + sl * MOE_BLOCK + r)
        return pltpu.make_async_copy(ybuf.at[slab_rows(sl * MOE_BLOCK + r), :], y_hbm.at[slab_rows(dst), :],
                                     ssem.at[sl])

    def row_chunks(make, blk, sl, wait):
        n_valid = sched_ref[4, blk]
        for c0 in range(0, MOE_BLOCK, ROW_CHUNK):
            @pl.when(c0 < n_valid)
            def _(c0=c0):
                for r in range(c0, c0 + ROW_CHUNK):
                    if wait:
                        make(blk, sl, r).wait()
                    else:
                        make(blk, sl, r).start(priority=r % 2)

    @pl.when(i == 0)
    def _():
        xbuf[...] = jnp.zeros_like(xbuf)
        ybuf[...] = jnp.zeros_like(ybuf)
        dump = pltpu.make_async_copy(ybuf, y_hbm.at[pl.ds(MOE_TOPK * n_tok * PACK_SLABS, 2 * buf_rows), :], ssem.at[0])
        dump.start()
        dump.wait()

    @pl.when((i == 0) & (n_used > 0))
    def _():
        for cp in weights(sched_ref[0, 0], 0):
            cp.start()
        row_chunks(gather_row, 0, 0, wait=False)

    @pl.when(first)
    def _():
        for cp in weights(sched_ref[0, i], w_slot):
            cp.wait()

    @pl.when(first & (sched_ref[3, i] >= 0))
    def _():
        for cp in weights(sched_ref[3, i], 1 - w_slot):
            cp.start(priority=1)

    @pl.when(first)
    def _():
        wgb[...] = wgf[w_slot].astype(BF16)
        wub[...] = wuf[w_slot].astype(BF16)
        wdb[...] = wdf[w_slot].astype(BF16)

    @pl.when(active)
    def _():
        row_chunks(gather_row, i, slot, wait=True)
        row_chunks(gather_row, i + 1, 1 - slot, wait=False)

    @pl.when(active & (i >= 2))
    def _():
        row_chunks(scatter_row, i - 2, slot, wait=True)

    @pl.when(active)
    def _():
        x = _unpack_rows(xbuf, slot * buf_rows, MOE_BLOCK).astype(BF16)
        a = jnp.dot(x, wgb[...], preferred_element_type=F32)
        b = jnp.dot(x, wub[...], preferred_element_type=F32)
        act = (a * _sigmoid(a) * b).astype(BF16)
        y = jnp.dot(act, wdb[...], preferred_element_type=F32)
        _pack_rows(ybuf, slot * buf_rows, y.astype(BF16).astype(F32))
        row_chunks(scatter_row, i, slot, wait=False)

    @pl.when(i == n_used - 1)
    def _():
        row_chunks(scatter_row, i, slot, wait=True)

    @pl.when((i == n_used - 1) & (i >= 1))
    def _():
        row_chunks(scatter_row, i - 1, 1 - slot, wait=True)


def _moe_experts(xn, sched, slot_asg, n_used, nblk, w_gate, w_up, w_down, layer):
    n_tok = xn.shape[0] // PACK_SLABS
    buf_rows = 2 * MOE_BLOCK * PACK_SLABS
    any_spec = pl.BlockSpec(memory_space=pl.ANY)
    grid_spec = pltpu.PrefetchScalarGridSpec(
        num_scalar_prefetch=3, grid=(nblk,),
        in_specs=[any_spec, any_spec, any_spec, any_spec], out_specs=any_spec,
        scratch_shapes=[pltpu.VMEM((buf_rows, LANE), jnp.uint32), pltpu.VMEM((buf_rows, LANE), jnp.uint32),
                        pltpu.VMEM((2, D_MODEL, MOE_FF), F32), pltpu.VMEM((2, D_MODEL, MOE_FF), F32),
                        pltpu.VMEM((2, MOE_FF, D_MODEL), F32),
                        pltpu.VMEM((D_MODEL, MOE_FF), BF16), pltpu.VMEM((D_MODEL, MOE_FF), BF16),
                        pltpu.VMEM((MOE_FF, D_MODEL), BF16),
                        pltpu.SemaphoreType.DMA((2,)), pltpu.SemaphoreType.DMA((2,)), pltpu.SemaphoreType.DMA((2,))])
    return pl.pallas_call(
        functools.partial(_moe_kernel, n_tok=n_tok, layer=layer), grid_spec=grid_spec,
        out_shape=jax.ShapeDtypeStruct(((MOE_TOPK * n_tok + 2 * MOE_BLOCK) * PACK_SLABS, LANE), jnp.uint32),
        compiler_params=_cparams(("arbitrary",), 56), name="moe_experts",
    )(sched, slot_asg, n_used, xn, w_gate, w_up, w_down)


def _ple_kernel(h1_ref, y0_ref, y1_ref, rg_ref, p_ref, gp_ref, wg_ref, wp_ref, gfin_ref, o_ref, *, final):
    rg = rg_ref[...]
    tm = rg.shape[0]
    ffn = rg[:, 0:1] * _unpack_rows(y0_ref, 0, tm) + rg[:, 1:2] * _unpack_rows(y1_ref, 0, tm)
    h2 = h1_ref[...] + ffn
    hn = _rms(h2, gp_ref[...]).astype(BF16)
    gate = _sigmoid(jnp.dot(hn, wg_ref[...], preferred_element_type=F32))
    pe = jnp.dot(p_ref[...].astype(BF16), wp_ref[...], preferred_element_type=F32)
    h3 = h2 + pe * gate
    o_ref[...] = _rms(h3, gfin_ref[...]) if final else h3


def _ple(h1, y2, rg, p, gp, wg, wp, gfin, final, tm=256):
    rows = h1.shape[0]
    n_t = rows // tm

    def rspec(n):
        return pl.BlockSpec((tm, n), lambda i: (i, 0))

    def yspec(k):
        return pl.BlockSpec((tm * PACK_SLABS, LANE), lambda i: (k * n_t + i, 0))

    return pl.pallas_call(
        functools.partial(_ple_kernel, final=final), grid=(n_t,),
        in_specs=[rspec(D_MODEL), yspec(0), yspec(1), rspec(ROUTE_W), rspec(PLE_DIM), _full(gp), _full(wg),
                  _full(wp), _full(gfin)],
        out_specs=rspec(D_MODEL), out_shape=jax.ShapeDtypeStruct((rows, D_MODEL), F32),
        compiler_params=_cparams(("parallel",), 48), name="moe_combine_ple",
    )(h1, y2, y2, rg, p, gp, wg, wp, gfin)


def _rope_tables(positions, rot_dim, lead, period, reps):
    half = rot_dim // 2
    inv_freq = ROPE_THETA ** (-jnp.arange(0, rot_dim, 2, dtype=F32) / rot_dim)
    ang = positions.astype(F32).reshape(-1, 1) * inv_freq
    cos, sin = jnp.cos(ang), jnp.sin(ang)
    n = cos.shape[0]
    tail = period - lead - rot_dim
    c = jnp.concatenate([jnp.ones((n, lead), F32), cos, cos, jnp.ones((n, tail), F32)], axis=1)
    s1 = jnp.concatenate([jnp.zeros((n, lead), F32), -sin, jnp.zeros((n, half + tail), F32)], axis=1)
    s2 = jnp.concatenate([jnp.zeros((n, lead + half), F32), sin, jnp.zeros((n, tail), F32)], axis=1)
    return tuple(jnp.tile(t, (1, reps)) for t in (c, s1, s2))


def _split_cols(w, sizes):
    out, o = [], 0
    for n in sizes:
        out.append(w[:, o:o + n])
        o += n
    return out


def _in_weights(w_in):
    cb, cc, ch, su, dq, dk, dv, cq, ckv, kr = _split_cols(
        w_in, (CONV_DIM, CONV_DIM, CONV_DIM, SSM_DIM, 3 * DIL_GW, 3 * DIL_GW, 3 * DIL_GW, MLA_Q_RANK, MLA_KV_RANK,
               MLA_ROPE))
    ws = [cb, cc, ch, jnp.pad(su, ((0, 0), (0, SSM_PAD - SSM_DIM)))]
    kinds = ["rows", "rows", "rows", "time_major"]
    for g in range(len(DIL_PATTERNS)):
        ws += [jnp.pad(m[:, g * DIL_GW:(g + 1) * DIL_GW], ((0, 0), (0, DIL_PLANES * LANE - DIL_GW)))
               for m in (dq, dk, dv)]
        kinds += ["planes"] * 3
    ws += [cq, ckv, jnp.pad(kr, ((0, 0), (MLA_NOPE, MLA_HW - MLA_NOPE - MLA_ROPE)))]
    kinds += ["rows"] * 3
    return [w.astype(BF16) for w in ws], kinds


def _mla_weights(w_uq, w_ukv):
    q = w_uq.reshape(MLA_Q_RANK, MLA_HEADS, MLA_QK)
    wq = jnp.pad(q, ((0, 0), (0, 0), (0, MLA_HW - MLA_QK))).reshape(MLA_Q_RANK, MLA_HEADS * MLA_HW)
    kv = w_ukv.reshape(MLA_KV_RANK, MLA_HEADS, MLA_NOPE + MLA_V)
    wk = jnp.pad(kv[:, :, :MLA_NOPE], ((0, 0), (0, 0), (0, MLA_HW - MLA_NOPE))).reshape(MLA_KV_RANK, -1)
    wv = jnp.pad(kv[:, :, MLA_NOPE:].reshape(MLA_KV_RANK, MLA_HEADS * MLA_V),
                 ((0, 0), (0, MLA_VW - MLA_HEADS * MLA_V)))
    return wq.astype(BF16), wk.astype(BF16), wv.astype(BF16)


def _out_weights(w_out):
    o_ssm, o_dil, o_mla = CONV_DIM, CONV_DIM + SSM_DIM, CONV_DIM + SSM_DIM + 3 * DIL_GW
    ws = [w_out[:o_ssm], jnp.pad(w_out[o_ssm:o_dil], ((0, SSM_PAD - SSM_DIM), (0, 0)))]
    ws += [jnp.pad(w_out[o_dil + g * DIL_GW:o_dil + (g + 1) * DIL_GW], ((0, DIL_PLANES * LANE - DIL_GW), (0, 0)))
           for g in range(3)]
    ws += [jnp.pad(w_out[o_mla:], ((0, MLA_VW - MLA_HEADS * MLA_V), (0, 0)))]
    return [w.astype(BF16) for w in ws]


def _router_weights(w_group, b_group, w_router, b_router):
    w = jnp.pad(jnp.concatenate([w_group, w_router], axis=1), ((0, 0), (0, ROUTE_W - MOE_GROUPS - N_EXPERTS)))
    hi = w.astype(BF16)
    lo = (w - hi.astype(F32)).astype(BF16)
    b = jnp.pad(jnp.concatenate([b_group, b_router]), (0, ROUTE_W - MOE_GROUPS - N_EXPERTS)).reshape(1, ROUTE_W)
    return jnp.concatenate([hi, lo], axis=1), b


def kernel(x, p, positions, norm_mix_g, w_in, conv_w, ssm_lam_re, ssm_lam_im, ssm_log_dt, ssm_b_re, ssm_b_im, ssm_c_re, ssm_c_im, ssm_d, ssm_glu_w, ssm_glu_b, mla_q_norm_g, mla_w_uq, mla_kv_norm_g, mla_w_ukv, w_out, norm_ffn_g, w_group, b_group, w_router, b_router, moe_w_gate, moe_w_up, moe_w_down, norm_ple_g, ple_w_proj, ple_w_gate, final_norm_g):
    batch, seq, _ = x.shape
    n_tok = batch * seq
    depth = w_in.shape[0]
    dil_tabs = _rope_tables(positions, DIL_ROT, 0, DIL_HEAD_DIM, LANE // DIL_HEAD_DIM)
    mla_tabs = _rope_tables(positions, MLA_ROPE, MLA_NOPE, MLA_HW, 1)
    h = x.reshape(n_tok, D_MODEL)
    for i in range(depth):
        z = _in_proj(h, norm_mix_g[i].reshape(1, -1), *_in_weights(w_in[i]), batch, seq)
        cb, cc, ch, su = z[:4]
        y_conv = _conv_mixer(cb, cc, ch, conv_w[i], batch, seq)
        prm = _ssm_params(ssm_lam_re[i], ssm_lam_im[i], ssm_log_dt[i], ssm_b_re[i], ssm_b_im[i], ssm_c_re[i],
                          ssm_c_im[i], ssm_d[i], ssm_glu_w[i], ssm_glu_b[i])
        y_ssm = _ssm_mixer(su.reshape(seq * batch, SSM_PAD), prm, batch, seq).reshape(seq, batch * SSM_PAD)
        outs, lses = [], []
        for g, (window, dil) in enumerate(DIL_PATTERNS):
            assert window // dil == DIL_BLK and (seq // dil) % DIL_BLK == 0
            o, l = _dil_group(z[4 + 3 * g], z[5 + 3 * g], z[6 + 3 * g], dil_tabs, dil, batch, seq)
            outs.append(o)
            lses.append(l)
        y_dil = _dil_combine(outs, lses)
        wq, wk, wv = _mla_weights(mla_w_uq[i], mla_w_ukv[i])
        q, k, v = _mla_prep(z[13], z[14], z[15], mla_q_norm_g[i].reshape(1, -1), mla_kv_norm_g[i].reshape(1, -1),
                            wq, wk, wv, mla_tabs)
        y_mla = _mla_attn(q, k, v, batch, seq)
        wr, br = _router_weights(w_group[i], b_group[i], w_router[i], b_router[i])
        h1, xn, route_i, route_g = _out_proj(
            (y_conv, y_ssm, y_dil[0], y_dil[1], y_dil[2], y_mla), h, _out_weights(w_out[i]),
            norm_ffn_g[i].reshape(1, -1), wr, br, batch, seq)
        sched, slot_asg, n_used, nblk = _moe_plan(route_i, n_tok)
        y2 = _moe_experts(xn, sched, slot_asg, n_used, nblk, moe_w_gate, moe_w_up, moe_w_down, i)
        h = _ple(h1, y2, route_g, p[i].reshape(n_tok, PLE_DIM), norm_ple_g[i].reshape(1, -1),
                 ple_w_gate[i].astype(BF16), ple_w_proj[i].astype(BF16), final_norm_g.reshape(1, -1),
                 final=(i == depth - 1))
    return h.reshape(batch, seq, D_MODEL)
```

```python
import functools
import math

import jax
import jax.numpy as jnp
from jax import lax
from jax.experimental import pallas as pl
from jax.experimental.pallas import tpu as pltpu

D_MODEL = 2048
PLE_DIM = 256
ROPE_THETA = 500000.0
NORM_EPS = 1e-6

CONV_DIM = 448
CONV_K = 3

SSM_DIM = 448
SSM_GROUP = 16
SSM_GROUPS = SSM_DIM // SSM_GROUP
SSM_STATE = 64
SSM_N = SSM_GROUPS * SSM_STATE
SSM_PAD = 512

DIL_HEAD_DIM = 64
DIL_ROT = DIL_HEAD_DIM // 4
DIL_PATTERNS = ((128, 1), (512, 4), (2048, 16))
DIL_HPG = 3
DIL_GW = DIL_HPG * DIL_HEAD_DIM
DIL_PLANES = 2
DIL_BLK = 128
DIL_UNROLL = 4

MLA_HEADS = 9
MLA_Q_RANK = 384
MLA_KV_RANK = 256
MLA_NOPE = 64
MLA_ROPE = 32
MLA_V = 64
MLA_QK = MLA_NOPE + MLA_ROPE
MLA_HW = 128
MLA_VW = 640

MOE_GROUPS = 8
MOE_EPG = 8
N_EXPERTS = MOE_GROUPS * MOE_EPG
MOE_TOPK = 2
MOE_FF = 512
MOE_BLOCK = 128
ROUTE_W = 128

LANE = 128
PACK_SLABS = D_MODEL // 2 // LANE
ROW_CHUNK = 32
NEG_BIG = -1e30

BF16 = jnp.bfloat16
F32 = jnp.float32


def _cparams(sem, vmem_mb):
    return pltpu.CompilerParams(dimension_semantics=sem, vmem_limit_bytes=vmem_mb * 1024 * 1024)


def _rms(x, g):
    ms = jnp.mean(x * x, axis=-1, keepdims=True)
    return (x * lax.rsqrt(ms + NORM_EPS)) * g


def _sigmoid(x):
    return 1.0 / (1.0 + jnp.exp(-x))


def _full(a):
    return pl.BlockSpec(a.shape, lambda *_: (0,) * a.ndim)


_HI16 = 0xFFFF0000


def _pack_rows(ref, base, x):
    n = x.shape[0]
    bits = lax.bitcast_convert_type(x, jnp.uint32)
    for c in range(PACK_SLABS):
        lo = bits[:, c * LANE:(c + 1) * LANE] >> 16
        hi = bits[:, D_MODEL // 2 + c * LANE:D_MODEL // 2 + (c + 1) * LANE] & jnp.uint32(_HI16)
        ref[pl.ds(base + c, n, stride=PACK_SLABS), :] = lo | hi


def _unpack_rows(ref, base, n):
    lo, hi = [], []
    for c in range(PACK_SLABS):
        w = ref[pl.ds(base + c, n, stride=PACK_SLABS), :]
        lo.append(lax.bitcast_convert_type(w << 16, F32))
        hi.append(lax.bitcast_convert_type(w & jnp.uint32(_HI16), F32))
    return jnp.concatenate(lo + hi, axis=1)


def _in_proj_kernel(x_ref, g_ref, c_ref, s1_ref, s2_ref, *refs, kinds):
    n_out = len(kinds)
    w_refs, o_refs = refs[:n_out], refs[n_out:]
    xn = _rms(x_ref[...], g_ref[...]).astype(BF16)
    tabs = [jnp.concatenate([t[...]] * DIL_PLANES, axis=1) for t in (c_ref, s1_ref, s2_ref)]
    for kind, w_ref, o_ref in zip(kinds, w_refs, o_refs):
        res = jnp.dot(xn, w_ref[...], preferred_element_type=F32)
        if kind.startswith("planes"):
            if kind != "planes":
                res = _rope_lanes(res, *tabs, DIL_ROT // 2, DIL_PLANES * LANE)
            if kind == "planes_rope_q":
                res = res * (1.0 / math.sqrt(DIL_HEAD_DIM))
            for j in range(o_ref.shape[0]):
                o_ref[j] = res[:, j * LANE:(j + 1) * LANE]
        else:
            o_ref[...] = res


def _in_proj(h, g, tabs, weights, kinds, batch, seq, tm=256):
    n_s = seq // tm
    tspec = pl.BlockSpec((tm, LANE), lambda b, i: (b * n_s + i, 0))
    in_specs = [pl.BlockSpec((tm, D_MODEL), lambda b, i: (b * n_s + i, 0)), _full(g), tspec, tspec, tspec]
    in_specs += [_full(w) for w in weights]
    out_shape, out_specs = [], []
    for kind, w in zip(kinds, weights):
        n = w.shape[1]
        if kind == "time_major":
            out_shape.append(jax.ShapeDtypeStruct((seq, batch * n), F32))
            out_specs.append(pl.BlockSpec((tm, n), lambda b, i: (i, b)))
        elif kind.startswith("planes"):
            out_shape.append(jax.ShapeDtypeStruct((n // LANE, batch * seq, LANE), F32))
            out_specs.append(pl.BlockSpec((n // LANE, tm, LANE), lambda b, i: (0, b * n_s + i, 0)))
        else:
            out_shape.append(jax.ShapeDtypeStruct((batch * seq, n), F32))
            out_specs.append(pl.BlockSpec((tm, n), lambda b, i: (b * n_s + i, 0)))
    return pl.pallas_call(
        functools.partial(_in_proj_kernel, kinds=tuple(kinds)),
        grid=(batch, n_s), in_specs=in_specs, out_specs=out_specs, out_shape=out_shape,
        compiler_params=_cparams(("parallel", "parallel"), 56), name="in_proj",
    )(h, g, *tabs, *weights)


def _conv_kernel(cb_ref, cc_ref, ch_ref, w_ref, o_ref, u_ref, *, rows):
    seq = cb_ref.shape[0]
    u_ref[0:8, :] = jnp.zeros((8, CONV_DIM), F32)
    for r0 in range(0, seq, rows):
        u_ref[8 + r0:8 + r0 + rows, :] = cc_ref[r0:r0 + rows, :] * ch_ref[r0:r0 + rows, :]
    w0, w1, w2 = w_ref[0:1, :], w_ref[1:2, :], w_ref[2:3, :]
    for r0 in range(0, seq, rows):
        acc = (w2 * u_ref[8 + r0:8 + r0 + rows, :] + w1 * u_ref[7 + r0:7 + r0 + rows, :]
               + w0 * u_ref[6 + r0:6 + r0 + rows, :])
        o_ref[r0:r0 + rows, :] = cb_ref[r0:r0 + rows, :] * acc


def _conv_mixer(cb, cc, ch, conv_w, batch, seq):
    spec = pl.BlockSpec((seq, CONV_DIM), lambda b: (b, 0))
    return pl.pallas_call(
        functools.partial(_conv_kernel, rows=256),
        grid=(batch,), in_specs=[spec, spec, spec, _full(conv_w)], out_specs=spec,
        out_shape=jax.ShapeDtypeStruct((batch * seq, CONV_DIM), F32),
        scratch_shapes=[pltpu.VMEM((seq + 8, CONV_DIM), F32)],
        compiler_params=_cparams(("parallel",), 48), name="conv_mixer",
    )(cb, cc, ch, conv_w)


_SSM_CHUNKS = ((0, 512), (512, 512), (1024, 512), (1536, 256))


def _gelu_tanh(x):
    return 0.5 * x * (1.0 + jnp.tanh(math.sqrt(2.0 / math.pi) * (x + 0.044715 * (x * x * x))))


def _ssm_kernel(u_ref, bf_ref, are_ref, aim_ref, cc_ref, d_ref, gw_ref, gb_ref, o_ref, xs_ref, st_ref,
                *, batch, tc):
    @pl.when(pl.program_id(0) == 0)
    def _():
        st_ref[...] = jnp.zeros_like(st_ref)

    u = u_ref[...]
    xs_ref[...] = jnp.dot(u.astype(BF16), bf_ref[...], preferred_element_type=F32)
    per_tile = 8 // batch
    for c0, cw in _SSM_CHUNKS:
        re_l, im_l = slice(c0, c0 + cw), slice(SSM_N + c0, SSM_N + c0 + cw)
        ar = jnp.broadcast_to(are_ref[:, re_l], (batch, cw))
        ai = jnp.broadcast_to(aim_ref[:, re_l], (batch, cw))

        def body(k, carry, re_l=re_l, im_l=im_l, ar=ar, ai=ai):
            sr, si = carry
            rows = pl.ds(pl.multiple_of(k * 8, 8), 8)
            tile_r, tile_i = xs_ref[rows, re_l], xs_ref[rows, im_l]
            out_r, out_i = [], []
            for j in range(per_tile):
                step = slice(j * batch, (j + 1) * batch)
                sr, si = ar * sr - ai * si + tile_r[step], ar * si + ai * sr + tile_i[step]
                out_r.append(sr)
                out_i.append(si)
            xs_ref[rows, re_l] = jnp.concatenate(out_r, axis=0)
            xs_ref[rows, im_l] = jnp.concatenate(out_i, axis=0)
            return sr, si

        sr, si = lax.fori_loop(0, tc // per_tile, body, (st_ref[0:batch, re_l], st_ref[0:batch, im_l]))
        st_ref[0:batch, re_l] = sr
        st_ref[0:batch, im_l] = si
    y = jnp.dot(xs_ref[...].astype(BF16), cc_ref[...], preferred_element_type=F32) + d_ref[...] * u
    g = _gelu_tanh(y)
    o_ref[...] = g * _sigmoid(jnp.dot(g.astype(BF16), gw_ref[...], preferred_element_type=F32) + gb_ref[...])


def _ssm_mixer(su_tm, prm, batch, seq, tc=128):
    rows = tc * batch
    spec = pl.BlockSpec((rows, SSM_PAD), lambda c: (c, 0))
    args = (prm["bf"], prm["a_re"], prm["a_im"], prm["cc"], prm["d"], prm["gw"], prm["gb"])
    return pl.pallas_call(
        functools.partial(_ssm_kernel, batch=batch, tc=tc),
        grid=(seq // tc,), in_specs=[spec] + [_full(a) for a in args], out_specs=spec,
        out_shape=jax.ShapeDtypeStruct((seq * batch, SSM_PAD), F32),
        scratch_shapes=[pltpu.VMEM((rows, 2 * SSM_N), F32), pltpu.VMEM((8, 2 * SSM_N), F32)],
        compiler_params=_cparams(("arbitrary",), 48), name="ssm_mixer",
    )(su_tm, *args)


def _ssm_params(lam_re, lam_im, log_dt, b_re, b_im, c_re, c_im, d_skip, glu_w, glu_b):
    dt = jnp.exp(log_dt)[:, None]
    mag = jnp.exp(lam_re * dt)
    a_re = mag * jnp.cos(lam_im * dt)
    a_im = mag * jnp.sin(lam_im * dt)
    nr, ni = a_re - 1.0, a_im
    den = lam_re * lam_re + lam_im * lam_im
    f_re = (nr * lam_re + ni * lam_im) / den
    f_im = (ni * lam_re - nr * lam_im) / den
    bfr = f_re[:, :, None] * b_re - f_im[:, :, None] * b_im
    bfi = f_re[:, :, None] * b_im + f_im[:, :, None] * b_re
    eye = jnp.eye(SSM_GROUPS, dtype=F32)

    def in_blockdiag(m):
        return jnp.einsum("gpc,gh->gchp", m, eye).reshape(SSM_DIM, SSM_N)

    def out_blockdiag(m):
        return jnp.einsum("gcp,gh->gphc", m, eye).reshape(SSM_N, SSM_DIM)

    pad = SSM_PAD - SSM_DIM
    bf = jnp.concatenate([in_blockdiag(bfr), in_blockdiag(bfi)], axis=1)
    cc = jnp.concatenate([out_blockdiag(c_re), -out_blockdiag(c_im)], axis=0)
    return {
        "bf": jnp.pad(bf, ((0, pad), (0, 0))).astype(BF16),
        "cc": jnp.pad(cc, ((0, 0), (0, pad))).astype(BF16),
        "a_re": a_re.reshape(1, SSM_N), "a_im": a_im.reshape(1, SSM_N),
        "d": jnp.pad(d_skip, (0, pad)).reshape(1, SSM_PAD),
        "gw": jnp.pad(glu_w, ((0, pad), (0, pad))).astype(BF16),
        "gb": jnp.pad(glu_b, (0, pad)).reshape(1, SSM_PAD),
    }


def _rope_lanes(x, c, s1, s2, half, width):
    return x * c + pltpu.roll(x, width - half, 1) * s1 + pltpu.roll(x, half, 1) * s2


def _dil_kernel(q_ref, k_ref, v_ref, o_ref, l_ref, *, dil):
    seq = q_ref.shape[1]
    nb = (seq // dil) // DIL_BLK
    width = DIL_PLANES * LANE
    lane = lax.broadcasted_iota(jnp.int32, (1, width), 1)
    cmask = [(lane // DIL_HEAD_DIM == c).astype(F32) for c in range(DIL_HPG)]
    qi = lax.broadcasted_iota(jnp.int32, (DIL_HPG * DIL_BLK, 2 * DIL_BLK), 0) % DIL_BLK
    ki = lax.broadcasted_iota(jnp.int32, (DIL_HPG * DIL_BLK, 2 * DIL_BLK), 1)
    band = (ki >= qi) & (ki <= qi + DIL_BLK)

    def planes(ref, rows):
        return jnp.concatenate([ref[j, rows, :] for j in range(DIL_PLANES)], axis=1)

    def rows_of(m, blk):
        if dil == 1:
            return pl.ds(pl.multiple_of(blk * DIL_BLK, DIL_BLK), DIL_BLK)
        return pl.ds(blk * DIL_BLK * dil + m, DIL_BLK, stride=dil)

    def blocks(it, carry):
        ids = [it * DIL_UNROLL + u for u in range(DIL_UNROLL)]
        mis = [(idx // nb, idx % nb) for idx in ids]
        rows_q = [rows_of(m, i) for m, i in mis]
        rows_p = [rows_of(m, jnp.maximum(i - 1, 0)) for m, i in mis]
        scores, vcats = [], []
        for rq, rp in zip(rows_q, rows_p):
            q = planes(q_ref, rq)
            kcat = jnp.concatenate([planes(k_ref, rp), planes(k_ref, rq)], axis=0).astype(BF16)
            vcats.append(jnp.concatenate([planes(v_ref, rp), planes(v_ref, rq)], axis=0).astype(BF16))
            qs = jnp.concatenate([q * cm for cm in cmask], axis=0).astype(BF16)
            scores.append(lax.dot_general(qs, kcat, (((1,), (1,)), ((), ())), preferred_element_type=F32))
        soft = []
        for s, (m, i) in zip(scores, mis):
            s = jnp.where(band & ((ki >= DIL_BLK) | (i > 0)), s, NEG_BIG)
            mx = jnp.max(s, axis=-1, keepdims=True)
            p = jnp.exp(s - mx)
            den = jnp.sum(p, axis=-1, keepdims=True)
            soft.append((p.astype(BF16), den, mx + jnp.log(den)))
        for (p, den, lse), vcat, rq in zip(soft, vcats, rows_q):
            o = jnp.dot(p, vcat, preferred_element_type=F32) / den
            out = jnp.zeros((DIL_BLK, width), F32)
            lout = jnp.zeros((DIL_BLK, width), F32)
            for c, cm in enumerate(cmask):
                out = out + cm * o[c * DIL_BLK:(c + 1) * DIL_BLK, :]
                lout = lout + cm * lse[c * DIL_BLK:(c + 1) * DIL_BLK, :]
            for j in range(DIL_PLANES):
                o_ref[j, rq, :] = out[:, j * LANE:(j + 1) * LANE]
                l_ref[j, rq, :] = lout[:, j * LANE:(j + 1) * LANE]
        return carry

    lax.fori_loop(0, dil * nb // DIL_UNROLL, blocks, 0)


def _dil_group(q, k, v, dil, batch, seq):
    spec = pl.BlockSpec((DIL_PLANES, seq, LANE), lambda b: (0, b, 0))
    sds = jax.ShapeDtypeStruct((DIL_PLANES, batch * seq, LANE), F32)
    return pl.pallas_call(
        functools.partial(_dil_kernel, dil=dil), grid=(batch,), in_specs=[spec] * 3,
        out_specs=[spec, spec], out_shape=[sds, sds],
        compiler_params=_cparams(("parallel",), 48), name=f"dil_attn_d{dil}",
    )(q, k, v)


def _mla_prep_kernel(cq_ref, ckv_ref, kr_ref, gq_ref, gkv_ref, wq_ref, wk_ref, wv_ref, c_ref, s1_ref, s2_ref,
                     q_ref, k_ref, v_ref):
    c, s1, s2 = c_ref[...], s1_ref[...], s2_ref[...]
    rope = functools.partial(_rope_lanes, c=c, s1=s1, s2=s2, half=MLA_ROPE // 2, width=MLA_HW)
    qn = _rms(cq_ref[...], gq_ref[...]).astype(BF16)
    kvn = _rms(ckv_ref[...], gkv_ref[...]).astype(BF16)
    q = jnp.dot(qn, wq_ref[...], preferred_element_type=F32)
    kn = jnp.dot(kvn, wk_ref[...], preferred_element_type=F32)
    v_ref[...] = jnp.dot(kvn, wv_ref[...], preferred_element_type=F32).astype(BF16)
    kr = rope(kr_ref[...])
    scale = math.log2(math.e) / math.sqrt(MLA_QK)
    for h in range(MLA_HEADS):
        sl = slice(h * MLA_HW, (h + 1) * MLA_HW)
        q_ref[:, sl] = (rope(q[:, sl]) * scale).astype(BF16)
        k_ref[:, sl] = (kn[:, sl] + kr).astype(BF16)


def _mla_prep(cq, ckv, kr, gq, gkv, wq, wk, wv, tabs, tm=512):
    rows = cq.shape[0]

    def rspec(n):
        return pl.BlockSpec((tm, n), lambda i: (i, 0))

    hw = MLA_HEADS * MLA_HW
    return pl.pallas_call(
        _mla_prep_kernel, grid=(rows // tm,),
        in_specs=[rspec(MLA_Q_RANK), rspec(MLA_KV_RANK), rspec(MLA_HW), _full(gq), _full(gkv), _full(wq),
                  _full(wk), _full(wv), rspec(MLA_HW), rspec(MLA_HW), rspec(MLA_HW)],
        out_specs=[rspec(hw), rspec(hw), rspec(MLA_VW)],
        out_shape=[jax.ShapeDtypeStruct((rows, hw), BF16), jax.ShapeDtypeStruct((rows, hw), BF16),
                   jax.ShapeDtypeStruct((rows, MLA_VW), BF16)],
        compiler_params=_cparams(("parallel",), 48), name="mla_prep",
    )(cq, ckv, kr, gq, gkv, wq, wk, wv, *tabs)


def _mla_attn_kernel(q_ref, k_ref, v_ref, o_ref, *, tq, group):
    qi = pl.program_id(1)
    row = lax.broadcasted_iota(jnp.int32, (tq, tq), 0)
    col = lax.broadcasted_iota(jnp.int32, (tq, tq), 1)
    causal = col <= row

    def heads_out(heads):
        qs = [q_ref[:, h * MLA_HW:(h + 1) * MLA_HW] for h in heads]

        def step(kj, carry, masked):
            rows = pl.ds(pl.multiple_of(kj * tq, tq), tq)
            scores = [lax.dot_general(q, k_ref[rows, h * MLA_HW:(h + 1) * MLA_HW], (((1,), (1,)), ((), ())),
                                      preferred_element_type=F32) for h, q in zip(heads, qs)]
            stats = []
            for s, (m, l, acc) in zip(scores, carry):
                if masked:
                    s = jnp.where(causal, s, NEG_BIG)
                mn = jnp.maximum(m, jnp.max(s, axis=-1, keepdims=True))
                alpha = jnp.exp2(m - mn)
                p = jnp.exp2(s - mn)
                stats.append((mn, alpha, alpha * l + jnp.sum(p, axis=-1, keepdims=True), p.astype(BF16)))
            out = []
            for h, (mn, alpha, l, p), (_, _, acc) in zip(heads, stats, carry):
                v = v_ref[rows, (h // 2) * MLA_HW:(h // 2 + 1) * MLA_HW]
                out.append((mn, l, alpha * acc + jnp.dot(p, v, preferred_element_type=F32)))
            return tuple(out)

        init = tuple((jnp.full((tq, 1), NEG_BIG, F32), jnp.zeros((tq, 1), F32), jnp.zeros((tq, MLA_HW), F32))
                     for _ in heads)
        carry = lax.fori_loop(0, qi, functools.partial(step, masked=False), init)
        return [acc / l for _, l, acc in step(qi, carry, True)]

    for h0 in range(0, MLA_HEADS, group):
        heads = list(range(h0, min(h0 + group, MLA_HEADS)))
        for h, o in zip(heads, heads_out(heads)):
            lo = (h // 2) * MLA_HW + (h % 2) * MLA_V
            o_ref[:, lo:lo + MLA_V] = o[:, (h % 2) * MLA_V:(h % 2 + 1) * MLA_V]
    if MLA_HEADS % 2:
        o_ref[:, MLA_HEADS * MLA_V:] = jnp.zeros((tq, MLA_VW - MLA_HEADS * MLA_V), F32)


def _mla_attn(q, k, v, batch, seq, tq=256, group=5):
    hw = MLA_HEADS * MLA_HW
    n_q = seq // tq
    return pl.pallas_call(
        functools.partial(_mla_attn_kernel, tq=tq, group=group), grid=(batch, n_q),
        in_specs=[pl.BlockSpec((tq, hw), lambda b, i: (b * n_q + i, 0)),
                  pl.BlockSpec((seq, hw), lambda b, i: (b, 0)),
                  pl.BlockSpec((seq, MLA_VW), lambda b, i: (b, 0))],
        out_specs=pl.BlockSpec((tq, MLA_VW), lambda b, i: (b * n_q + i, 0)),
        out_shape=jax.ShapeDtypeStruct((batch * seq, MLA_VW), F32),
        compiler_params=_cparams(("parallel", "arbitrary"), 48), name="mla_attn",
    )(q, k, v)


def _out_proj_kernel(yc_ref, ys_ref, od0_ref, od1_ref, od2_ref, ld0_ref, ld1_ref, ld2_ref, ym_ref, h_ref,
                     wc_ref, ws_ref, wd0_ref, wd1_ref, wd2_ref, wm_ref,
                     gf_ref, wr_ref, br_ref,
                     h1_ref, xn_ref, ri_ref, rg_ref):
    def mm(y_ref, w_ref):
        return jnp.dot(y_ref[...].astype(BF16), w_ref[...], preferred_element_type=F32)

    mix = mm(yc_ref, wc_ref) + mm(ys_ref, ws_ref) + mm(ym_ref, wm_ref)
    for j in range(DIL_PLANES):
        la, lb, lc = ld0_ref[j], ld1_ref[j], ld2_ref[j]
        mx = jnp.maximum(jnp.maximum(la, lb), lc)
        ea, eb, ec = jnp.exp(la - mx), jnp.exp(lb - mx), jnp.exp(lc - mx)
        inv = 1.0 / (ea + eb + ec)
        for o_ref, e, w_ref in ((od0_ref, ea, wd0_ref), (od1_ref, eb, wd1_ref), (od2_ref, ec, wd2_ref)):
            mix = mix + jnp.dot((o_ref[j] * (e * inv)).astype(BF16), w_ref[j * LANE:(j + 1) * LANE, :],
                                preferred_element_type=F32)
    h1 = h_ref[...] + mix
    h1_ref[...] = h1
    hn = _rms(h1, gf_ref[...])
    hi = hn.astype(BF16)
    hi32 = hi.astype(F32)
    lo = (hn - hi32).astype(BF16)
    _pack_rows(xn_ref, 0, hi32)
    r_hi = jnp.dot(hi, wr_ref[...], preferred_element_type=F32)
    r_lo = jnp.dot(lo, wr_ref[:, 0:ROUTE_W], preferred_element_type=F32)
    logits = r_hi[:, 0:ROUTE_W] + r_hi[:, ROUTE_W:2 * ROUTE_W] + r_lo + br_ref[...]
    lane = lax.broadcasted_iota(jnp.int32, logits.shape, 1)
    gl = jnp.where(lane < MOE_GROUPS, logits, NEG_BIG)
    gmax = jnp.max(gl, axis=-1, keepdims=True)
    g_top_p = 1.0 / jnp.sum(jnp.exp(gl - gmax), axis=-1, keepdims=True)
    g_idx = jnp.min(jnp.where(gl == gmax, lane, ROUTE_W), axis=-1, keepdims=True)
    in_group = (lane >= MOE_GROUPS) & (lane < MOE_GROUPS + N_EXPERTS) & (((lane - MOE_GROUPS) >> 3) == g_idx)
    el = jnp.where(in_group, logits, NEG_BIG)
    v1 = jnp.max(el, axis=-1, keepdims=True)
    i1 = jnp.min(jnp.where(el == v1, lane, ROUTE_W), axis=-1, keepdims=True)
    el2 = jnp.where(lane == i1, NEG_BIG, el)
    v2 = jnp.max(el2, axis=-1, keepdims=True)
    i2 = jnp.min(jnp.where(el2 == v2, lane, ROUTE_W), axis=-1, keepdims=True)
    e2 = jnp.exp(v2 - v1)
    w1 = g_top_p / (1.0 + e2)
    w2 = g_top_p * e2 / (1.0 + e2)
    ri_ref[...] = jnp.where(lane == 0, i1 - MOE_GROUPS, jnp.where(lane == 1, i2 - MOE_GROUPS, 0))
    rg_ref[...] = jnp.where(lane == 0, w1, jnp.where(lane == 1, w2, 0.0))


def _out_proj(ys, h, ws, gf, wr, br, batch, seq, tm=256):
    rows = batch * seq
    n_s = seq // tm
    yc, ys_tm, dil_o, dil_l, ym = ys

    def rspec(n):
        return pl.BlockSpec((tm, n), lambda b, i: (b * n_s + i, 0))

    pspec = pl.BlockSpec((DIL_PLANES, tm, LANE), lambda b, i: (0, b * n_s + i, 0))
    in_specs = [rspec(CONV_DIM), pl.BlockSpec((tm, SSM_PAD), lambda b, i: (i, b))] + [pspec] * 6
    in_specs += [rspec(MLA_VW), rspec(D_MODEL)]
    in_specs += [_full(w) for w in ws] + [_full(gf), _full(wr), _full(br)]
    return pl.pallas_call(
        _out_proj_kernel, grid=(batch, n_s), in_specs=in_specs,
        out_specs=[rspec(D_MODEL), pl.BlockSpec((tm * PACK_SLABS, LANE), lambda b, i: (b * n_s + i, 0)),
                   rspec(ROUTE_W), rspec(ROUTE_W)],
        out_shape=[jax.ShapeDtypeStruct((rows, D_MODEL), F32),
                   jax.ShapeDtypeStruct((rows * PACK_SLABS, LANE), jnp.uint32),
                   jax.ShapeDtypeStruct((rows, ROUTE_W), jnp.int32), jax.ShapeDtypeStruct((rows, ROUTE_W), F32)],
        compiler_params=_cparams(("parallel", "parallel"), 48), name="out_proj_router",
    )(yc, ys_tm, *dil_o, *dil_l, ym, h, *ws, gf, wr, br)


def _moe_plan(route_i, n_tok):
    n_asg = n_tok * MOE_TOPK
    nblk = (n_asg + N_EXPERTS * (MOE_BLOCK - 1) + MOE_BLOCK - 1) // MOE_BLOCK
    flat_e = route_i[:, :MOE_TOPK].reshape(-1)
    onehot = (flat_e[:, None] == jnp.arange(N_EXPERTS, dtype=jnp.int32)[None, :]).astype(F32)
    oh3 = onehot.reshape(n_asg // MOE_BLOCK, MOE_BLOCK, N_EXPERTS)
    within = jnp.einsum("ij,bje->bie", jnp.tril(jnp.ones((MOE_BLOCK, MOE_BLOCK), F32)), oh3)
    chunk_tot = within[:, -1, :]
    chunk_off = jnp.cumsum(chunk_tot, axis=0) - chunk_tot
    rank = jnp.sum((within + chunk_off[:, None, :]) * oh3, axis=-1).reshape(n_asg).astype(jnp.int32) - 1
    counts = (chunk_off[-1] + chunk_tot[-1]).astype(jnp.int32)
    padded = ((counts + MOE_BLOCK - 1) // MOE_BLOCK) * MOE_BLOCK
    pend = jnp.cumsum(padded)
    first_blk = ((pend - padded) // MOE_BLOCK).astype(F32)
    dest = jnp.dot(onehot, first_blk).astype(jnp.int32) * MOE_BLOCK + rank
    in_blk = (dest[:, None] // MOE_BLOCK == jnp.arange(nblk, dtype=jnp.int32)[None, :]).astype(F32)
    at_off = (dest[:, None] % MOE_BLOCK == jnp.arange(MOE_BLOCK, dtype=jnp.int32)[None, :]).astype(F32)
    asg = jnp.arange(n_asg, dtype=jnp.int32)
    a_hi, a_lo = (asg // MOE_BLOCK).astype(F32)[:, None], (asg % MOE_BLOCK).astype(F32)[:, None]
    inv = lambda v: jnp.einsum("ab,ao->bo", in_blk * v, at_off)
    slot_asg = jnp.where(inv(1.0) > 0.5, inv(a_hi) * MOE_BLOCK + inv(a_lo), -1.0).astype(jnp.int32).reshape(-1)
    block_expert = jnp.minimum(
        jnp.searchsorted(pend, jnp.arange(nblk, dtype=jnp.int32) * MOE_BLOCK, side="right"), N_EXPERTS - 1
    ).astype(jnp.int32)
    n_used = (pend[-1] // MOE_BLOCK).astype(jnp.int32)
    prev = jnp.concatenate([jnp.full((1,), -1, jnp.int32), block_expert[:-1]])
    is_first = (block_expert != prev).astype(jnp.int32)
    w_slot = (jnp.cumsum(is_first) - 1) % 2
    run_end = (pend // MOE_BLOCK)[block_expert]
    nxt = jnp.where(run_end < n_used, block_expert[jnp.minimum(run_end, nblk - 1)], -1)
    blk = jnp.arange(nblk, dtype=jnp.int32)
    run_start = ((pend - padded) // MOE_BLOCK)[block_expert]
    rows_left = jnp.clip(counts[block_expert] - (blk - run_start) * MOE_BLOCK, 0, MOE_BLOCK)
    sched = jnp.stack([block_expert, is_first, w_slot, nxt, jnp.where(blk < n_used, rows_left, 0)])
    sched = jnp.pad(sched.astype(jnp.int32), ((0, 0), (0, 1)))
    return sched, slot_asg, n_used.reshape(1), nblk


def _moe_kernel(sched_ref, asg_ref, nu_ref, x_hbm, wg_hbm, wu_hbm, wd_hbm, y_hbm,
                xbuf, ybuf, wgf, wuf, wdf, wgb, wub, wdb, gsem, ssem, wsem, *, n_tok, layer):
    i = pl.program_id(0)
    n_used = nu_ref[0]
    slot = i % 2
    buf_rows = MOE_BLOCK * PACK_SLABS
    active = i < n_used
    first = active & (sched_ref[1, i] == 1)
    w_slot = sched_ref[2, i]

    def weights(e, ws):
        return [pltpu.make_async_copy(w_hbm.at[layer, e], buf.at[ws], wsem.at[ws])
                for w_hbm, buf in ((wg_hbm, wgf), (wu_hbm, wuf), (wd_hbm, wdf))]

    def slab_rows(row):
        return pl.ds(pl.multiple_of(row * PACK_SLABS, PACK_SLABS), PACK_SLABS)

    def gather_row(blk, sl, r):
        tok = jnp.maximum(asg_ref[blk * MOE_BLOCK + r], 0) >> 1
        return pltpu.make_async_copy(x_hbm.at[slab_rows(tok), :], xbuf.at[slab_rows(sl * MOE_BLOCK + r), :],
                                     gsem.at[sl])

    def scatter_row(blk, sl, r):
        a = asg_ref[blk * MOE_BLOCK + r]
        dst = jnp.where(a >= 0, (a & 1) * n_tok + (a >> 1), MOE_TOPK * n_tok + sl * MOE_BLOCK + r)
        return pltpu.make_async_copy(ybuf.at[slab_rows(sl * MOE_BLOCK + r), :], y_hbm.at[slab_rows(dst), :],
                                     ssem.at[sl])

    def row_chunks(make, blk, sl, wait):
        n_valid = sched_ref[4, blk]
        for c0 in range(0, MOE_BLOCK, ROW_CHUNK):
            @pl.when(c0 < n_valid)
            def _(c0=c0):
                for r in range(c0, c0 + ROW_CHUNK):
                    if wait:
                        make(blk, sl, r).wait()
                    else:
                        make(blk, sl, r).start(priority=r % 2)

    @pl.when(i == 0)
    def _():
        xbuf[...] = jnp.zeros_like(xbuf)
        ybuf[...] = jnp.zeros_like(ybuf)
        dump = pltpu.make_async_copy(ybuf, y_hbm.at[pl.ds(MOE_TOPK * n_tok * PACK_SLABS, 2 * buf_rows), :], ssem.at[0])
        dump.start()
        dump.wait()

    @pl.when((i == 0) & (n_used > 0))
    def _():
        for cp in weights(sched_ref[0, 0], 0):
            cp.start()
        row_chunks(gather_row, 0, 0, wait=False)

    @pl.when(first)
    def _():
        for cp in weights(sched_ref[0, i], w_slot):
            cp.wait()

    @pl.when(first & (sched_ref[3, i] >= 0))
    def _():
        for cp in weights(sched_ref[3, i], 1 - w_slot):
            cp.start(priority=1)

    @pl.when(first)
    def _():
        wgb[...] = wgf[w_slot].astype(BF16)
        wub[...] = wuf[w_slot].astype(BF16)
        wdb[...] = wdf[w_slot].astype(BF16)

    @pl.when(active)
    def _():
        row_chunks(gather_row, i, slot, wait=True)
        row_chunks(gather_row, i + 1, 1 - slot, wait=False)

    @pl.when(active & (i >= 2))
    def _():
        row_chunks(scatter_row, i - 2, slot, wait=True)

    @pl.when(active)
    def _():
        x = _unpack_rows(xbuf, slot * buf_rows, MOE_BLOCK).astype(BF16)
        a = jnp.dot(x, wgb[...], preferred_element_type=F32)
        b = jnp.dot(x, wub[...], preferred_element_type=F32)
        act = (a * _sigmoid(a) * b).astype(BF16)
        y = jnp.dot(act, wdb[...], preferred_element_type=F32)
        _pack_rows(ybuf, slot * buf_rows, y.astype(BF16).astype(F32))
        row_chunks(scatter_row, i, slot, wait=False)

    @pl.when(i == n_used - 1)
    def _():
        row_chunks(scatter_row, i, slot, wait=True)

    @pl.when((i == n_used - 1) & (i >= 1))
    def _():
        row_chunks(scatter_row, i - 1, 1 - slot, wait=True)


def _moe_experts(xn, sched, slot_asg, n_used, nblk, w_gate, w_up, w_down, layer):
    n_tok = xn.shape[0] // PACK_SLABS
    buf_rows = 2 * MOE_BLOCK * PACK_SLABS
    any_spec = pl.BlockSpec(memory_space=pl.ANY)
    grid_spec = pltpu.PrefetchScalarGridSpec(
        num_scalar_prefetch=3, grid=(nblk,),
        in_specs=[any_spec, any_spec, any_spec, any_spec], out_specs=any_spec,
        scratch_shapes=[pltpu.VMEM((buf_rows, LANE), jnp.uint32), pltpu.VMEM((buf_rows, LANE), jnp.uint32),
                        pltpu.VMEM((2, D_MODEL, MOE_FF), F32), pltpu.VMEM((2, D_MODEL, MOE_FF), F32),
                        pltpu.VMEM((2, MOE_FF, D_MODEL), F32),
                        pltpu.VMEM((D_MODEL, MOE_FF), BF16), pltpu.VMEM((D_MODEL, MOE_FF), BF16),
                        pltpu.VMEM((MOE_FF, D_MODEL), BF16),
                        pltpu.SemaphoreType.DMA((2,)), pltpu.SemaphoreType.DMA((2,)), pltpu.SemaphoreType.DMA((2,))])
    return pl.pallas_call(
        functools.partial(_moe_kernel, n_tok=n_tok, layer=layer), grid_spec=grid_spec,
        out_shape=jax.ShapeDtypeStruct(((MOE_TOPK * n_tok + 2 * MOE_BLOCK) * PACK_SLABS, LANE), jnp.uint32),
        compiler_params=_cparams(("arbitrary",), 56), name="moe_experts",
    )(sched, slot_asg, n_used, xn, w_gate, w_up, w_down)


def _ple_kernel(h1_ref, y0_ref, y1_ref, rg_ref, p_ref, gp_ref, wg_ref, wp_ref, gfin_ref, o_ref, *, final):
    rg = rg_ref[...]
    tm = rg.shape[0]
    ffn = rg[:, 0:1] * _unpack_rows(y0_ref, 0, tm) + rg[:, 1:2] * _unpack_rows(y1_ref, 0, tm)
    h2 = h1_ref[...] + ffn
    hn = _rms(h2, gp_ref[...]).astype(BF16)
    gate = _sigmoid(jnp.dot(hn, wg_ref[...], preferred_element_type=F32))
    pe = jnp.dot(p_ref[...].astype(BF16), wp_ref[...], preferred_element_type=F32)
    h3 = h2 + pe * gate
    o_ref[...] = _rms(h3, gfin_ref[...]) if final else h3


def _ple(h1, y2, rg, p, gp, wg, wp, gfin, final, tm=256):
    rows = h1.shape[0]
    n_t = rows // tm

    def rspec(n):
        return pl.BlockSpec((tm, n), lambda i: (i, 0))

    def yspec(k):
        return pl.BlockSpec((tm * PACK_SLABS, LANE), lambda i: (k * n_t + i, 0))

    return pl.pallas_call(
        functools.partial(_ple_kernel, final=final), grid=(n_t,),
        in_specs=[rspec(D_MODEL), yspec(0), yspec(1), rspec(ROUTE_W), rspec(PLE_DIM), _full(gp), _full(wg),
                  _full(wp), _full(gfin)],
        out_specs=rspec(D_MODEL), out_shape=jax.ShapeDtypeStruct((rows, D_MODEL), F32),
        compiler_params=_cparams(("parallel",), 48), name="moe_combine_ple",
    )(h1, y2, y2, rg, p, gp, wg, wp, gfin)


def _rope_tables(positions, rot_dim, lead, period, reps):
    half = rot_dim // 2
    inv_freq = ROPE_THETA ** (-jnp.arange(0, rot_dim, 2, dtype=F32) / rot_dim)
    ang = positions.astype(F32).reshape(-1, 1) * inv_freq
    cos, sin = jnp.cos(ang), jnp.sin(ang)
    n = cos.shape[0]
    tail = period - lead - rot_dim
    c = jnp.concatenate([jnp.ones((n, lead), F32), cos, cos, jnp.ones((n, tail), F32)], axis=1)
    s1 = jnp.concatenate([jnp.zeros((n, lead), F32), -sin, jnp.zeros((n, half + tail), F32)], axis=1)
    s2 = jnp.concatenate([jnp.zeros((n, lead + half), F32), sin, jnp.zeros((n, tail), F32)], axis=1)
    return tuple(jnp.tile(t, (1, reps)) for t in (c, s1, s2))


def _split_cols(w, sizes):
    out, o = [], 0
    for n in sizes:
        out.append(w[:, o:o + n])
        o += n
    return out


def _in_weights(w_in):
    cb, cc, ch, su, dq, dk, dv, cq, ckv, kr = _split_cols(
        w_in, (CONV_DIM, CONV_DIM, CONV_DIM, SSM_DIM, 3 * DIL_GW, 3 * DIL_GW, 3 * DIL_GW, MLA_Q_RANK, MLA_KV_RANK,
               MLA_ROPE))
    ws = [cb, cc, ch, jnp.pad(su, ((0, 0), (0, SSM_PAD - SSM_DIM)))]
    kinds = ["rows", "rows", "rows", "time_major"]
    for g in range(len(DIL_PATTERNS)):
        ws += [jnp.pad(m[:, g * DIL_GW:(g + 1) * DIL_GW], ((0, 0), (0, DIL_PLANES * LANE - DIL_GW)))
               for m in (dq, dk, dv)]
        kinds += ["planes_rope_q", "planes_rope_k", "planes"]
    ws += [cq, ckv, jnp.pad(kr, ((0, 0), (MLA_NOPE, MLA_HW - MLA_NOPE - MLA_ROPE)))]
    kinds += ["rows"] * 3
    return [w.astype(BF16) for w in ws], kinds


def _mla_weights(w_uq, w_ukv):
    q = w_uq.reshape(MLA_Q_RANK, MLA_HEADS, MLA_QK)
    wq = jnp.pad(q, ((0, 0), (0, 0), (0, MLA_HW - MLA_QK))).reshape(MLA_Q_RANK, MLA_HEADS * MLA_HW)
    kv = w_ukv.reshape(MLA_KV_RANK, MLA_HEADS, MLA_NOPE + MLA_V)
    wk = jnp.pad(kv[:, :, :MLA_NOPE], ((0, 0), (0, 0), (0, MLA_HW - MLA_NOPE))).reshape(MLA_KV_RANK, -1)
    wv = jnp.pad(kv[:, :, MLA_NOPE:].reshape(MLA_KV_RANK, MLA_HEADS * MLA_V),
                 ((0, 0), (0, MLA_VW - MLA_HEADS * MLA_V)))
    return wq.astype(BF16), wk.astype(BF16), wv.astype(BF16)


def _out_weights(w_out):
    o_ssm, o_dil, o_mla = CONV_DIM, CONV_DIM + SSM_DIM, CONV_DIM + SSM_DIM + 3 * DIL_GW
    ws = [w_out[:o_ssm], jnp.pad(w_out[o_ssm:o_dil], ((0, SSM_PAD - SSM_DIM), (0, 0)))]
    ws += [jnp.pad(w_out[o_dil + g * DIL_GW:o_dil + (g + 1) * DIL_GW], ((0, DIL_PLANES * LANE - DIL_GW), (0, 0)))
           for g in range(3)]
    ws += [jnp.pad(w_out[o_mla:], ((0, MLA_VW - MLA_HEADS * MLA_V), (0, 0)))]
    return [w.astype(BF16) for w in ws]


def _router_weights(w_group, b_group, w_router, b_router):
    w = jnp.pad(jnp.concatenate([w_group, w_router], axis=1), ((0, 0), (0, ROUTE_W - MOE_GROUPS - N_EXPERTS)))
    hi = w.astype(BF16)
    lo = (w - hi.astype(F32)).astype(BF16)
    b = jnp.pad(jnp.concatenate([b_group, b_router]), (0, ROUTE_W - MOE_GROUPS - N_EXPERTS)).reshape(1, ROUTE_W)
    return jnp.concatenate([hi, lo], axis=1), b


def kernel(x, p, positions, norm_mix_g, w_in, conv_w, ssm_lam_re, ssm_lam_im, ssm_log_dt, ssm_b_re, ssm_b_im, ssm_c_re, ssm_c_im, ssm_d, ssm_glu_w, ssm_glu_b, mla_q_norm_g, mla_w_uq, mla_kv_norm_g, mla_w_ukv, w_out, norm_ffn_g, w_group, b_group, w_router, b_router, moe_w_gate, moe_w_up, moe_w_down, norm_ple_g, ple_w_proj, ple_w_gate, final_norm_g):
    batch, seq, _ = x.shape
    n_tok = batch * seq
    depth = w_in.shape[0]
    dil_tabs = _rope_tables(positions, DIL_ROT, 0, DIL_HEAD_DIM, LANE // DIL_HEAD_DIM)
    mla_tabs = _rope_tables(positions, MLA_ROPE, MLA_NOPE, MLA_HW, 1)
    h = x.reshape(n_tok, D_MODEL)
    for i in range(depth):
        z = _in_proj(h, norm_mix_g[i].reshape(1, -1), dil_tabs, *_in_weights(w_in[i]), batch, seq)
        cb, cc, ch, su = z[:4]
        y_conv = _conv_mixer(cb, cc, ch, conv_w[i], batch, seq)
        prm = _ssm_params(ssm_lam_re[i], ssm_lam_im[i], ssm_log_dt[i], ssm_b_re[i], ssm_b_im[i], ssm_c_re[i],
                          ssm_c_im[i], ssm_d[i], ssm_glu_w[i], ssm_glu_b[i])
        y_ssm = _ssm_mixer(su.reshape(seq * batch, SSM_PAD), prm, batch, seq).reshape(seq, batch * SSM_PAD)
        outs, lses = [], []
        for g, (window, dil) in enumerate(DIL_PATTERNS):
            assert window // dil == DIL_BLK and (seq // dil) % DIL_BLK == 0
            o, l = _dil_group(z[4 + 3 * g], z[5 + 3 * g], z[6 + 3 * g], dil, batch, seq)
            outs.append(o)
            lses.append(l)
        wq, wk, wv = _mla_weights(mla_w_uq[i], mla_w_ukv[i])
        q, k, v = _mla_prep(z[13], z[14], z[15], mla_q_norm_g[i].reshape(1, -1), mla_kv_norm_g[i].reshape(1, -1),
                            wq, wk, wv, mla_tabs)
        y_mla = _mla_attn(q, k, v, batch, seq)
        wr, br = _router_weights(w_group[i], b_group[i], w_router[i], b_router[i])
        h1, xn, route_i, route_g = _out_proj(
            (y_conv, y_ssm, outs, lses, y_mla), h, _out_weights(w_out[i]),
            norm_ffn_g[i].reshape(1, -1), wr, br, batch, seq)
        sched, slot_asg, n_used, nblk = _moe_plan(route_i, n_tok)
        y2 = _moe_experts(xn, sched, slot_asg, n_used, nblk, moe_w_gate, moe_w_up, moe_w_down, i)
        h = _ple(h1, y2, route_g, p[i].reshape(n_tok, PLE_DIM), norm_ple_g[i].reshape(1, -1),
                 ple_w_gate[i].astype(BF16), ple_w_proj[i].astype(BF16), final_norm_g.reshape(1, -1),
                 final=(i == depth - 1))
    return h.reshape(batch, seq, D_MODEL)
```

```python
import functools
import math

import jax
import jax.numpy as jnp
from jax import lax
from jax.experimental import pallas as pl
from jax.experimental.pallas import tpu as pltpu

D_MODEL = 2048
PLE_DIM = 256
ROPE_THETA = 500000.0
NORM_EPS = 1e-6

CONV_DIM = 448
CONV_K = 3

SSM_DIM = 448
SSM_GROUP = 16
SSM_GROUPS = SSM_DIM // SSM_GROUP
SSM_STATE = 64
SSM_N = SSM_GROUPS * SSM_STATE
SSM_PAD = 512

DIL_HEAD_DIM = 64
DIL_ROT = DIL_HEAD_DIM // 4
DIL_PATTERNS = ((128, 1), (512, 4), (2048, 16))
DIL_HPG = 3
DIL_GW = DIL_HPG * DIL_HEAD_DIM
DIL_PLANES = 2
DIL_BLK = 128
DIL_UNROLL = 4

MLA_HEADS = 9
MLA_Q_RANK = 384
MLA_KV_RANK = 256
MLA_NOPE = 64
MLA_ROPE = 32
MLA_V = 64
MLA_QK = MLA_NOPE + MLA_ROPE
MLA_HW = 128
MLA_VW = 640

MOE_GROUPS = 8
MOE_EPG = 8
N_EXPERTS = MOE_GROUPS * MOE_EPG
MOE_TOPK = 2
MOE_FF = 512
MOE_BLOCK = 128
ROUTE_W = 128

LANE = 128
PACK_SLABS = D_MODEL // 2 // LANE
ROW_CHUNK = 32
NEG_BIG = -1e30

BF16 = jnp.bfloat16
F32 = jnp.float32


def _cparams(sem, vmem_mb):
    return pltpu.CompilerParams(dimension_semantics=sem, vmem_limit_bytes=vmem_mb * 1024 * 1024)


def _rms(x, g):
    ms = jnp.mean(x * x, axis=-1, keepdims=True)
    return (x * lax.rsqrt(ms + NORM_EPS)) * g


def _sigmoid(x):
    return 1.0 / (1.0 + jnp.exp(-x))


def _full(a):
    return pl.BlockSpec(a.shape, lambda *_: (0,) * a.ndim)


_HI16 = 0xFFFF0000


def _pack_rows(ref, base, x):
    n = x.shape[0]
    bits = lax.bitcast_convert_type(x, jnp.uint32)
    for c in range(PACK_SLABS):
        lo = bits[:, c * LANE:(c + 1) * LANE] >> 16
        hi = bits[:, D_MODEL // 2 + c * LANE:D_MODEL // 2 + (c + 1) * LANE] & jnp.uint32(_HI16)
        ref[pl.ds(base + c, n, stride=PACK_SLABS), :] = lo | hi


def _unpack_rows(ref, base, n):
    lo, hi = [], []
    for c in range(PACK_SLABS):
        w = ref[pl.ds(base + c, n, stride=PACK_SLABS), :]
        lo.append(lax.bitcast_convert_type(w << 16, F32))
        hi.append(lax.bitcast_convert_type(w & jnp.uint32(_HI16), F32))
    return jnp.concatenate(lo + hi, axis=1)


def _in_proj_kernel(x_ref, g_ref, dc_ref, ds1_ref, ds2_ref, mc_ref, ms1_ref, ms2_ref, gq_ref, gkv_ref,
                    wq_ref, wk_ref, wv_ref, *refs, kinds):
    n_out = len(kinds)
    w_refs, o_refs = refs[:n_out], refs[n_out:]
    xn = _rms(x_ref[...], g_ref[...]).astype(BF16)
    tabs = [jnp.concatenate([t[...]] * DIL_PLANES, axis=1) for t in (dc_ref, ds1_ref, ds2_ref)]
    latent = {}
    outs = iter(o_refs)
    for kind, w_ref in zip(kinds, w_refs):
        res = jnp.dot(xn, w_ref[...], preferred_element_type=F32)
        if kind.startswith("mla"):
            latent[kind] = res
            continue
        o_ref = next(outs)
        if kind.startswith("planes"):
            if kind != "planes":
                res = _rope_lanes(res, *tabs, DIL_ROT // 2, DIL_PLANES * LANE)
            if kind == "planes_rope_q":
                res = res * (1.0 / math.sqrt(DIL_HEAD_DIM))
            for j in range(o_ref.shape[0]):
                o_ref[j] = res[:, j * LANE:(j + 1) * LANE]
        else:
            o_ref[...] = res
    q_ref, k_ref, v_ref = outs
    rope = functools.partial(_rope_lanes, c=mc_ref[...], s1=ms1_ref[...], s2=ms2_ref[...], half=MLA_ROPE // 2,
                             width=MLA_HW)
    qn = _rms(latent["mla_cq"], gq_ref[...]).astype(BF16)
    kvn = _rms(latent["mla_ckv"], gkv_ref[...]).astype(BF16)
    q = jnp.dot(qn, wq_ref[...], preferred_element_type=F32)
    kn = jnp.dot(kvn, wk_ref[...], preferred_element_type=F32)
    v_ref[...] = jnp.dot(kvn, wv_ref[...], preferred_element_type=F32).astype(BF16)
    kr = rope(latent["mla_kr"])
    scale = math.log2(math.e) / math.sqrt(MLA_QK)
    for h in range(MLA_HEADS):
        sl = slice(h * MLA_HW, (h + 1) * MLA_HW)
        q_ref[:, sl] = (rope(q[:, sl]) * scale).astype(BF16)
        k_ref[:, sl] = (kn[:, sl] + kr).astype(BF16)


def _in_proj(h, g, dil_tabs, mla_tabs, mla_prm, weights, kinds, batch, seq, tm=256):
    n_s = seq // tm

    def rspec(n):
        return pl.BlockSpec((tm, n), lambda b, i: (b * n_s + i, 0))

    in_specs = [rspec(D_MODEL), _full(g)] + [rspec(LANE)] * 6 + [_full(a) for a in mla_prm]
    in_specs += [_full(w) for w in weights]
    out_shape, out_specs = [], []
    for kind, w in zip(kinds, weights):
        n = w.shape[1]
        if kind.startswith("mla"):
            continue
        if kind == "time_major":
            out_shape.append(jax.ShapeDtypeStruct((seq, batch * n), F32))
            out_specs.append(pl.BlockSpec((tm, n), lambda b, i: (i, b)))
        elif kind.startswith("planes"):
            out_shape.append(jax.ShapeDtypeStruct((n // LANE, batch * seq, LANE), F32))
            out_specs.append(pl.BlockSpec((n // LANE, tm, LANE), lambda b, i: (0, b * n_s + i, 0)))
        else:
            out_shape.append(jax.ShapeDtypeStruct((batch * seq, n), F32))
            out_specs.append(rspec(n))
    for n in (MLA_HEADS * MLA_HW, MLA_HEADS * MLA_HW, MLA_VW):
        out_shape.append(jax.ShapeDtypeStruct((batch * seq, n), BF16))
        out_specs.append(rspec(n))
    return pl.pallas_call(
        functools.partial(_in_proj_kernel, kinds=tuple(kinds)),
        grid=(batch, n_s), in_specs=in_specs, out_specs=out_specs, out_shape=out_shape,
        compiler_params=_cparams(("parallel", "parallel"), 56), name="in_proj",
    )(h, g, *dil_tabs, *mla_tabs, *mla_prm, *weights)


def _conv_kernel(cb_ref, cc_ref, ch_ref, w_ref, o_ref, u_ref, *, rows):
    seq = cb_ref.shape[0]
    u_ref[0:8, :] = jnp.zeros((8, CONV_DIM), F32)
    for r0 in range(0, seq, rows):
        u_ref[8 + r0:8 + r0 + rows, :] = cc_ref[r0:r0 + rows, :] * ch_ref[r0:r0 + rows, :]
    w0, w1, w2 = w_ref[0:1, :], w_ref[1:2, :], w_ref[2:3, :]
    for r0 in range(0, seq, rows):
        acc = (w2 * u_ref[8 + r0:8 + r0 + rows, :] + w1 * u_ref[7 + r0:7 + r0 + rows, :]
               + w0 * u_ref[6 + r0:6 + r0 + rows, :])
        o_ref[r0:r0 + rows, :] = cb_ref[r0:r0 + rows, :] * acc


def _conv_mixer(cb, cc, ch, conv_w, batch, seq):
    spec = pl.BlockSpec((seq, CONV_DIM), lambda b: (b, 0))
    return pl.pallas_call(
        functools.partial(_conv_kernel, rows=256),
        grid=(batch,), in_specs=[spec, spec, spec, _full(conv_w)], out_specs=spec,
        out_shape=jax.ShapeDtypeStruct((batch * seq, CONV_DIM), F32),
        scratch_shapes=[pltpu.VMEM((seq + 8, CONV_DIM), F32)],
        compiler_params=_cparams(("parallel",), 48), name="conv_mixer",
    )(cb, cc, ch, conv_w)


SSM_NP = SSM_N // LANE
SSM_SCAN_PLANES = 4
_SSM_SPLIT = ((0, 256, 0, 1024), (256, 256, 1024, 768))


def _gelu_tanh(x):
    return 0.5 * x * (1.0 + jnp.tanh(math.sqrt(2.0 / math.pi) * (x + 0.044715 * (x * x * x))))


def _ssm_kernel(u_ref, bf0_ref, bf1_ref, are_ref, aim_ref, cc0_ref, cc1_ref, d_ref, gw_ref, gb_ref, o_ref,
                xs_ref, st_ref, *, batch, tc):
    @pl.when(pl.program_id(0) == 0)
    def _():
        st_ref[...] = jnp.zeros_like(st_ref)

    def batch_rows(b):
        return pl.ds(b, tc, stride=batch)

    for b in range(batch):
        for (l0, lw, s0, sw), bf_ref in zip(_SSM_SPLIT, (bf0_ref, bf1_ref)):
            ub = u_ref[:, b * SSM_PAD + l0:b * SSM_PAD + l0 + lw].astype(BF16)
            r = jnp.dot(ub, bf_ref[...], preferred_element_type=F32)
            for q in range(sw // LANE):
                xs_ref[s0 // LANE + q, batch_rows(b), :] = r[:, q * LANE:(q + 1) * LANE]
                xs_ref[SSM_NP + s0 // LANE + q, batch_rows(b), :] = r[:, sw + q * LANE:sw + (q + 1) * LANE]

    per_tile = 8 // batch
    for p0 in range(0, SSM_NP, SSM_SCAN_PLANES):
        ps = list(range(p0, min(p0 + SSM_SCAN_PLANES, SSM_NP)))
        ar = [jnp.broadcast_to(are_ref[:, p * LANE:(p + 1) * LANE], (batch, LANE)) for p in ps]
        ai = [jnp.broadcast_to(aim_ref[:, p * LANE:(p + 1) * LANE], (batch, LANE)) for p in ps]

        def body(k, carry, ps=ps, ar=ar, ai=ai):
            rows = pl.ds(pl.multiple_of(k * 8, 8), 8)
            new = []
            for idx, p in enumerate(ps):
                sr, si = carry[idx]
                tile_r, tile_i = xs_ref[p, rows, :], xs_ref[SSM_NP + p, rows, :]
                out_r, out_i = [], []
                for j in range(per_tile):
                    step = slice(j * batch, (j + 1) * batch)
                    sr, si = (ar[idx] * sr - ai[idx] * si + tile_r[step], ar[idx] * si + ai[idx] * sr + tile_i[step])
                    out_r.append(sr)
                    out_i.append(si)
                xs_ref[p, rows, :] = jnp.concatenate(out_r, axis=0)
                xs_ref[SSM_NP + p, rows, :] = jnp.concatenate(out_i, axis=0)
                new.append((sr, si))
            return tuple(new)

        init = tuple((st_ref[p, 0:batch, :], st_ref[SSM_NP + p, 0:batch, :]) for p in ps)
        for p, (sr, si) in zip(ps, lax.fori_loop(0, tc // per_tile, body, init)):
            st_ref[p, 0:batch, :] = sr
            st_ref[SSM_NP + p, 0:batch, :] = si

    for b in range(batch):
        ys = []
        for (l0, lw, s0, sw), cc_ref in zip(_SSM_SPLIT, (cc0_ref, cc1_ref)):
            planes = list(range(s0 // LANE, (s0 + sw) // LANE))
            xb = jnp.concatenate([xs_ref[p, batch_rows(b), :] for p in planes]
                                 + [xs_ref[SSM_NP + p, batch_rows(b), :] for p in planes], axis=1)
            ys.append(jnp.dot(xb.astype(BF16), cc_ref[...], preferred_element_type=F32))
        y = jnp.concatenate(ys, axis=1) + d_ref[...] * u_ref[:, b * SSM_PAD:(b + 1) * SSM_PAD]
        g = _gelu_tanh(y)
        o_ref[:, b * SSM_PAD:(b + 1) * SSM_PAD] = g * _sigmoid(
            jnp.dot(g.astype(BF16), gw_ref[...], preferred_element_type=F32) + gb_ref[...])


def _ssm_mixer(su, prm, batch, seq, tc=256):
    spec = pl.BlockSpec((tc, batch * SSM_PAD), lambda c: (c, 0))
    args = (prm["bf0"], prm["bf1"], prm["a_re"], prm["a_im"], prm["cc0"], prm["cc1"], prm["d"], prm["gw"], prm["gb"])
    return pl.pallas_call(
        functools.partial(_ssm_kernel, batch=batch, tc=tc),
        grid=(seq // tc,), in_specs=[spec] + [_full(a) for a in args], out_specs=spec,
        out_shape=jax.ShapeDtypeStruct((seq, batch * SSM_PAD), F32),
        scratch_shapes=[pltpu.VMEM((2 * SSM_NP, tc * batch, LANE), F32), pltpu.VMEM((2 * SSM_NP, 8, LANE), F32)],
        compiler_params=_cparams(("arbitrary",), 48), name="ssm_mixer",
    )(su, *args)


def _ssm_params(lam_re, lam_im, log_dt, b_re, b_im, c_re, c_im, d_skip, glu_w, glu_b):
    dt = jnp.exp(log_dt)[:, None]
    mag = jnp.exp(lam_re * dt)
    a_re = mag * jnp.cos(lam_im * dt)
    a_im = mag * jnp.sin(lam_im * dt)
    nr, ni = a_re - 1.0, a_im
    den = lam_re * lam_re + lam_im * lam_im
    f_re = (nr * lam_re + ni * lam_im) / den
    f_im = (ni * lam_re - nr * lam_im) / den
    bfr = f_re[:, :, None] * b_re - f_im[:, :, None] * b_im
    bfi = f_re[:, :, None] * b_im + f_im[:, :, None] * b_re
    eye = jnp.eye(SSM_GROUPS, dtype=F32)

    def in_blockdiag(m):
        return jnp.einsum("gpc,gh->gchp", m, eye).reshape(SSM_DIM, SSM_N)

    def out_blockdiag(m):
        return jnp.einsum("gcp,gh->gphc", m, eye).reshape(SSM_N, SSM_DIM)

    pad = SSM_PAD - SSM_DIM
    b_re_d, b_im_d = (jnp.pad(in_blockdiag(m), ((0, pad), (0, 0))) for m in (bfr, bfi))
    c_re_d, c_im_d = (jnp.pad(out_blockdiag(m), ((0, 0), (0, pad))) for m in (c_re, -c_im))
    prm = {}
    for c, (l0, lw, s0, sw) in enumerate(_SSM_SPLIT):
        prm[f"bf{c}"] = jnp.concatenate([b_re_d[l0:l0 + lw, s0:s0 + sw], b_im_d[l0:l0 + lw, s0:s0 + sw]],
                                        axis=1).astype(BF16)
        prm[f"cc{c}"] = jnp.concatenate([c_re_d[s0:s0 + sw, l0:l0 + lw], c_im_d[s0:s0 + sw, l0:l0 + lw]],
                                        axis=0).astype(BF16)
    return {
        **prm,
        "a_re": a_re.reshape(1, SSM_N), "a_im": a_im.reshape(1, SSM_N),
        "d": jnp.pad(d_skip, (0, pad)).reshape(1, SSM_PAD),
        "gw": jnp.pad(glu_w, ((0, pad), (0, pad))).astype(BF16),
        "gb": jnp.pad(glu_b, (0, pad)).reshape(1, SSM_PAD),
    }


def _rope_lanes(x, c, s1, s2, half, width):
    return x * c + pltpu.roll(x, width - half, 1) * s1 + pltpu.roll(x, half, 1) * s2


def _dil_kernel(q_ref, k_ref, v_ref, o_ref, l_ref, *, dil):
    seq = q_ref.shape[1]
    nb = (seq // dil) // DIL_BLK
    width = DIL_PLANES * LANE
    lane = lax.broadcasted_iota(jnp.int32, (1, width), 1)
    cmask = [(lane // DIL_HEAD_DIM == c).astype(F32) for c in range(DIL_HPG)]
    qi = lax.broadcasted_iota(jnp.int32, (DIL_HPG * DIL_BLK, 2 * DIL_BLK), 0) % DIL_BLK
    ki = lax.broadcasted_iota(jnp.int32, (DIL_HPG * DIL_BLK, 2 * DIL_BLK), 1)
    band = (ki >= qi) & (ki <= qi + DIL_BLK)

    def planes(ref, rows):
        return jnp.concatenate([ref[j, rows, :] for j in range(DIL_PLANES)], axis=1)

    def rows_of(m, blk):
        if dil == 1:
            return pl.ds(pl.multiple_of(blk * DIL_BLK, DIL_BLK), DIL_BLK)
        return pl.ds(blk * DIL_BLK * dil + m, DIL_BLK, stride=dil)

    def blocks(it, carry):
        ids = [it * DIL_UNROLL + u for u in range(DIL_UNROLL)]
        mis = [(idx // nb, idx % nb) for idx in ids]
        rows_q = [rows_of(m, i) for m, i in mis]
        rows_p = [rows_of(m, jnp.maximum(i - 1, 0)) for m, i in mis]
        scores, vcats = [], []
        for rq, rp in zip(rows_q, rows_p):
            q = planes(q_ref, rq)
            kcat = jnp.concatenate([planes(k_ref, rp), planes(k_ref, rq)], axis=0).astype(BF16)
            vcats.append(jnp.concatenate([planes(v_ref, rp), planes(v_ref, rq)], axis=0).astype(BF16))
            qs = jnp.concatenate([q * cm for cm in cmask], axis=0).astype(BF16)
            scores.append(lax.dot_general(qs, kcat, (((1,), (1,)), ((), ())), preferred_element_type=F32))
        soft = []
        for s, (m, i) in zip(scores, mis):
            s = jnp.where(band & ((ki >= DIL_BLK) | (i > 0)), s, NEG_BIG)
            mx = jnp.max(s, axis=-1, keepdims=True)
            p = jnp.exp(s - mx)
            den = jnp.sum(p, axis=-1, keepdims=True)
            soft.append((p.astype(BF16), den, mx + jnp.log(den)))
        for (p, den, lse), vcat, rq in zip(soft, vcats, rows_q):
            o = jnp.dot(p, vcat, preferred_element_type=F32) / den
            out = jnp.zeros((DIL_BLK, width), F32)
            lout = jnp.zeros((DIL_BLK, width), F32)
            for c, cm in enumerate(cmask):
                out = out + cm * o[c * DIL_BLK:(c + 1) * DIL_BLK, :]
                lout = lout + cm * lse[c * DIL_BLK:(c + 1) * DIL_BLK, :]
            for j in range(DIL_PLANES):
                o_ref[j, rq, :] = out[:, j * LANE:(j + 1) * LANE]
                l_ref[j, rq, :] = lout[:, j * LANE:(j + 1) * LANE]
        return carry

    lax.fori_loop(0, dil * nb // DIL_UNROLL, blocks, 0)


def _dil_group(q, k, v, dil, batch, seq):
    spec = pl.BlockSpec((DIL_PLANES, seq, LANE), lambda b: (0, b, 0))
    sds = jax.ShapeDtypeStruct((DIL_PLANES, batch * seq, LANE), F32)
    return pl.pallas_call(
        functools.partial(_dil_kernel, dil=dil), grid=(batch,), in_specs=[spec] * 3,
        out_specs=[spec, spec], out_shape=[sds, sds],
        compiler_params=_cparams(("parallel",), 48), name=f"dil_attn_d{dil}",
    )(q, k, v)


def _mla_attn_kernel(q_ref, k_ref, v_ref, o_ref, *, tq, group):
    qi = pl.program_id(1)
    row = lax.broadcasted_iota(jnp.int32, (tq, tq), 0)
    col = lax.broadcasted_iota(jnp.int32, (tq, tq), 1)
    causal = col <= row

    def heads_out(heads):
        qs = [q_ref[:, h * MLA_HW:(h + 1) * MLA_HW] for h in heads]

        def step(kj, carry, masked):
            rows = pl.ds(pl.multiple_of(kj * tq, tq), tq)
            scores = [lax.dot_general(q, k_ref[rows, h * MLA_HW:(h + 1) * MLA_HW], (((1,), (1,)), ((), ())),
                                      preferred_element_type=F32) for h, q in zip(heads, qs)]
            stats = []
            for s, (m, l, acc) in zip(scores, carry):
                if masked:
                    s = jnp.where(causal, s, NEG_BIG)
                mn = jnp.maximum(m, jnp.max(s, axis=-1, keepdims=True))
                alpha = jnp.exp2(m - mn)
                p = jnp.exp2(s - mn)
                stats.append((mn, alpha, alpha * l + jnp.sum(p, axis=-1, keepdims=True), p.astype(BF16)))
            out = []
            for h, (mn, alpha, l, p), (_, _, acc) in zip(heads, stats, carry):
                v = v_ref[rows, (h // 2) * MLA_HW:(h // 2 + 1) * MLA_HW]
                out.append((mn, l, alpha * acc + jnp.dot(p, v, preferred_element_type=F32)))
            return tuple(out)

        init = tuple((jnp.full((tq, 1), NEG_BIG, F32), jnp.zeros((tq, 1), F32), jnp.zeros((tq, MLA_HW), F32))
                     for _ in heads)
        carry = lax.fori_loop(0, qi, functools.partial(step, masked=False), init)
        return [acc / l for _, l, acc in step(qi, carry, True)]

    for h0 in range(0, MLA_HEADS, group):
        heads = list(range(h0, min(h0 + group, MLA_HEADS)))
        for h, o in zip(heads, heads_out(heads)):
            lo = (h // 2) * MLA_HW + (h % 2) * MLA_V
            o_ref[:, lo:lo + MLA_V] = o[:, (h % 2) * MLA_V:(h % 2 + 1) * MLA_V]
    if MLA_HEADS % 2:
        o_ref[:, MLA_HEADS * MLA_V:] = jnp.zeros((tq, MLA_VW - MLA_HEADS * MLA_V), F32)


def _mla_attn(q, k, v, batch, seq, tq=256, group=5):
    hw = MLA_HEADS * MLA_HW
    n_q = seq // tq
    return pl.pallas_call(
        functools.partial(_mla_attn_kernel, tq=tq, group=group), grid=(batch, n_q),
        in_specs=[pl.BlockSpec((tq, hw), lambda b, i: (b * n_q + i, 0)),
                  pl.BlockSpec((seq, hw), lambda b, i: (b, 0)),
                  pl.BlockSpec((seq, MLA_VW), lambda b, i: (b, 0))],
        out_specs=pl.BlockSpec((tq, MLA_VW), lambda b, i: (b * n_q + i, 0)),
        out_shape=jax.ShapeDtypeStruct((batch * seq, MLA_VW), F32),
        compiler_params=_cparams(("parallel", "arbitrary"), 48), name="mla_attn",
    )(q, k, v)


def _out_proj_kernel(yc_ref, ys_ref, od0_ref, od1_ref, od2_ref, ld0_ref, ld1_ref, ld2_ref, ym_ref, h_ref,
                     wc_ref, ws_ref, wd_ref, wm_ref,
                     gf_ref, wr_ref, br_ref,
                     h1_ref, xn_ref, ri_ref, rg_ref):
    def mm(y, w_ref):
        return jnp.dot(y.astype(BF16), w_ref[...], preferred_element_type=F32)

    y_dil = [None] * (3 * DIL_PLANES)
    for j in range(DIL_PLANES):
        la, lb, lc = ld0_ref[j], ld1_ref[j], ld2_ref[j]
        mx = jnp.maximum(jnp.maximum(la, lb), lc)
        ea, eb, ec = jnp.exp(la - mx), jnp.exp(lb - mx), jnp.exp(lc - mx)
        inv = 1.0 / (ea + eb + ec)
        for g, (o_ref, e) in enumerate(((od0_ref, ea), (od1_ref, eb), (od2_ref, ec))):
            y_dil[g * DIL_PLANES + j] = o_ref[j] * (e * inv)
    mix = (mm(yc_ref[...], wc_ref) + mm(ys_ref[...], ws_ref) + mm(jnp.concatenate(y_dil, axis=1), wd_ref)
           + mm(ym_ref[...], wm_ref))
    h1 = h_ref[...] + mix
    h1_ref[...] = h1
    hn = _rms(h1, gf_ref[...])
    hi = hn.astype(BF16)
    hi32 = hi.astype(F32)
    lo = (hn - hi32).astype(BF16)
    _pack_rows(xn_ref, 0, hi32)
    r_hi = jnp.dot(hi, wr_ref[...], preferred_element_type=F32)
    r_lo = jnp.dot(lo, wr_ref[:, 0:ROUTE_W], preferred_element_type=F32)
    logits = r_hi[:, 0:ROUTE_W] + r_hi[:, ROUTE_W:2 * ROUTE_W] + r_lo + br_ref[...]
    lane = lax.broadcasted_iota(jnp.int32, logits.shape, 1)
    gl = jnp.where(lane < MOE_GROUPS, logits, NEG_BIG)
    gmax = jnp.max(gl, axis=-1, keepdims=True)
    g_top_p = 1.0 / jnp.sum(jnp.exp(gl - gmax), axis=-1, keepdims=True)
    g_idx = jnp.min(jnp.where(gl == gmax, lane, ROUTE_W), axis=-1, keepdims=True)
    in_group = (lane >= MOE_GROUPS) & (lane < MOE_GROUPS + N_EXPERTS) & (((lane - MOE_GROUPS) >> 3) == g_idx)
    el = jnp.where(in_group, logits, NEG_BIG)
    v1 = jnp.max(el, axis=-1, keepdims=True)
    i1 = jnp.min(jnp.where(el == v1, lane, ROUTE_W), axis=-1, keepdims=True)
    el2 = jnp.where(lane == i1, NEG_BIG, el)
    v2 = jnp.max(el2, axis=-1, keepdims=True)
    i2 = jnp.min(jnp.where(el2 == v2, lane, ROUTE_W), axis=-1, keepdims=True)
    e2 = jnp.exp(v2 - v1)
    w1 = g_top_p / (1.0 + e2)
    w2 = g_top_p * e2 / (1.0 + e2)
    ri_ref[...] = jnp.where(lane == 0, i1 - MOE_GROUPS, jnp.where(lane == 1, i2 - MOE_GROUPS, 0))
    rg_ref[...] = jnp.where(lane == 0, w1, jnp.where(lane == 1, w2, 0.0))


def _out_proj(ys, h, ws, gf, wr, br, batch, seq, tm=256):
    rows = batch * seq
    n_s = seq // tm
    yc, ys_tm, dil_o, dil_l, ym = ys

    def rspec(n):
        return pl.BlockSpec((tm, n), lambda b, i: (b * n_s + i, 0))

    pspec = pl.BlockSpec((DIL_PLANES, tm, LANE), lambda b, i: (0, b * n_s + i, 0))
    in_specs = [rspec(CONV_DIM), pl.BlockSpec((tm, SSM_PAD), lambda b, i: (i, b))] + [pspec] * 6
    in_specs += [rspec(MLA_VW), rspec(D_MODEL)]
    in_specs += [_full(w) for w in ws] + [_full(gf), _full(wr), _full(br)]
    return pl.pallas_call(
        _out_proj_kernel, grid=(batch, n_s), in_specs=in_specs,
        out_specs=[rspec(D_MODEL), pl.BlockSpec((tm * PACK_SLABS, LANE), lambda b, i: (b * n_s + i, 0)),
                   rspec(ROUTE_W), rspec(ROUTE_W)],
        out_shape=[jax.ShapeDtypeStruct((rows, D_MODEL), F32),
                   jax.ShapeDtypeStruct((rows * PACK_SLABS, LANE), jnp.uint32),
                   jax.ShapeDtypeStruct((rows, ROUTE_W), jnp.int32), jax.ShapeDtypeStruct((rows, ROUTE_W), F32)],
        compiler_params=_cparams(("parallel", "parallel"), 48), name="out_proj_router",
    )(yc, ys_tm, *dil_o, *dil_l, ym, h, *ws, gf, wr, br)


def _moe_plan(route_i, n_tok):
    n_asg = n_tok * MOE_TOPK
    nblk = (n_asg + N_EXPERTS * (MOE_BLOCK - 1) + MOE_BLOCK - 1) // MOE_BLOCK
    flat_e = route_i[:, :MOE_TOPK].reshape(-1)
    onehot = (flat_e[:, None] == jnp.arange(N_EXPERTS, dtype=jnp.int32)[None, :]).astype(F32)
    oh3 = onehot.reshape(n_asg // MOE_BLOCK, MOE_BLOCK, N_EXPERTS)
    within = jnp.einsum("ij,bje->bie", jnp.tril(jnp.ones((MOE_BLOCK, MOE_BLOCK), F32)), oh3)
    chunk_tot = within[:, -1, :]
    chunk_off = jnp.cumsum(chunk_tot, axis=0) - chunk_tot
    rank = jnp.sum((within + chunk_off[:, None, :]) * oh3, axis=-1).reshape(n_asg).astype(jnp.int32) - 1
    counts = (chunk_off[-1] + chunk_tot[-1]).astype(jnp.int32)
    padded = ((counts + MOE_BLOCK - 1) // MOE_BLOCK) * MOE_BLOCK
    pend = jnp.cumsum(padded)
    first_blk = ((pend - padded) // MOE_BLOCK).astype(F32)
    dest = jnp.dot(onehot, first_blk).astype(jnp.int32) * MOE_BLOCK + rank
    in_blk = (dest[:, None] // MOE_BLOCK == jnp.arange(nblk, dtype=jnp.int32)[None, :]).astype(F32)
    at_off = (dest[:, None] % MOE_BLOCK == jnp.arange(MOE_BLOCK, dtype=jnp.int32)[None, :]).astype(F32)
    asg = jnp.arange(n_asg, dtype=jnp.int32)
    a_hi, a_lo = (asg // MOE_BLOCK).astype(F32)[:, None], (asg % MOE_BLOCK).astype(F32)[:, None]
    inv = lambda v: jnp.einsum("ab,ao->bo", in_blk * v, at_off)
    slot_asg = jnp.where(inv(1.0) > 0.5, inv(a_hi) * MOE_BLOCK + inv(a_lo), -1.0).astype(jnp.int32).reshape(-1)
    block_expert = jnp.minimum(
        jnp.searchsorted(pend, jnp.arange(nblk, dtype=jnp.int32) * MOE_BLOCK, side="right"), N_EXPERTS - 1
    ).astype(jnp.int32)
    n_used = (pend[-1] // MOE_BLOCK).astype(jnp.int32)
    prev = jnp.concatenate([jnp.full((1,), -1, jnp.int32), block_expert[:-1]])
    is_first = (block_expert != prev).astype(jnp.int32)
    w_slot = (jnp.cumsum(is_first) - 1) % 2
    run_end = (pend // MOE_BLOCK)[block_expert]
    nxt = jnp.where(run_end < n_used, block_expert[jnp.minimum(run_end, nblk - 1)], -1)
    blk = jnp.arange(nblk, dtype=jnp.int32)
    run_start = ((pend - padded) // MOE_BLOCK)[block_expert]
    rows_left = jnp.clip(counts[block_expert] - (blk - run_start) * MOE_BLOCK, 0, MOE_BLOCK)
    sched = jnp.stack([block_expert, is_first, w_slot, nxt, jnp.where(blk < n_used, rows_left, 0)])
    sched = jnp.pad(sched.astype(jnp.int32), ((0, 0), (0, 1)))
    return sched, slot_asg, n_used.reshape(1), nblk


def _moe_kernel(sched_ref, asg_ref, nu_ref, x_hbm, wg_hbm, wu_hbm, wd_hbm, y_hbm,
                xbuf, ybuf, wgf, wuf, wdf, wgb, wub, wdb, gsem, ssem, wsem, *, n_tok, layer):
    i = pl.program_id(0)
    n_used = nu_ref[0]
    slot = i % 2
    buf_rows = MOE_BLOCK * PACK_SLABS
    active = i < n_used
    first = active & (sched_ref[1, i] == 1)
    w_slot = sched_ref[2, i]

    def weights(e, ws):
        return [pltpu.make_async_copy(w_hbm.at[layer, e], buf.at[ws], wsem.at[ws])
                for w_hbm, buf in ((wg_hbm, wgf), (wu_hbm, wuf), (wd_hbm, wdf))]

    def slab_rows(row):
        return pl.ds(pl.multiple_of(row * PACK_SLABS, PACK_SLABS), PACK_SLABS)

    def gather_row(blk, sl, r):
        tok = jnp.maximum(asg_ref[blk * MOE_BLOCK + r], 0) >> 1
        return pltpu.make_async_copy(x_hbm.at[slab_rows(tok), :], xbuf.at[slab_rows(sl * MOE_BLOCK + r), :],
                                     gsem.at[sl])

    def scatter_row(blk, sl, r):
        a = asg_ref[blk * MOE_BLOCK + r]
        dst = jnp.where(a >= 0, (a & 1) * n_tok + (a >> 1), MOE_TOPK * n_tok + sl * MOE_BLOCK + r)
        return pltpu.make_async_copy(ybuf.at[slab_rows(sl * MOE_BLOCK + r), :], y_hbm.at[slab_rows(dst), :],
                                     ssem.at[sl])

    def row_chunks(make, blk, sl, wait):
        n_valid = sched_ref[4, blk]
        for c0 in range(0, MOE_BLOCK, ROW_CHUNK):
            @pl.when(c0 < n_valid)
            def _(c0=c0):
                for r in range(c0, c0 + ROW_CHUNK):
                    if wait:
                        make(blk, sl, r).wait()
                    else:
                        make(blk, sl, r).start(priority=r % 2)

    @pl.when(i == 0)
    def _():
        xbuf[...] = jnp.zeros_like(xbuf)
        ybuf[...] = jnp.zeros_like(ybuf)
        dump = pltpu.make_async_copy(ybuf, y_hbm.at[pl.ds(MOE_TOPK * n_tok * PACK_SLABS, 2 * buf_rows), :], ssem.at[0])
        dump.start()
        dump.wait()

    @pl.when((i == 0) & (n_used > 0))
    def _():
        for cp in weights(sched_ref[0, 0], 0):
            cp.start()
        row_chunks(gather_row, 0, 0, wait=False)

    @pl.when(first)
    def _():
        for cp in weights(sched_ref[0, i], w_slot):
            cp.wait()

    @pl.when(first & (sched_ref[3, i] >= 0))
    def _():
        for cp in weights(sched_ref[3, i], 1 - w_slot):
            cp.start(priority=1)

    @pl.when(first)
    def _():
        wgb[...] = wgf[w_slot].astype(BF16)
        wub[...] = wuf[w_slot].astype(BF16)
        wdb[...] = wdf[w_slot].astype(BF16)

    @pl.when(active)
    def _():
        row_chunks(gather_row, i, slot, wait=True)
        row_chunks(gather_row, i + 1, 1 - slot, wait=False)

    @pl.when(active & (i >= 2))
    def _():
        row_chunks(scatter_row, i - 2, slot, wait=True)

    @pl.when(active)
    def _():
        x = _unpack_rows(xbuf, slot * buf_rows, MOE_BLOCK).astype(BF16)
        a = jnp.dot(x, wgb[...], preferred_element_type=F32)
        b = jnp.dot(x, wub[...], preferred_element_type=F32)
        act = (a * _sigmoid(a) * b).astype(BF16)
        y = jnp.dot(act, wdb[...], preferred_element_type=F32)
        _pack_rows(ybuf, slot * buf_rows, y.astype(BF16).astype(F32))
        row_chunks(scatter_row, i, slot, wait=False)

    @pl.when(i == n_used - 1)
    def _():
        row_chunks(scatter_row, i, slot, wait=True)

    @pl.when((i == n_used - 1) & (i >= 1))
    def _():
        row_chunks(scatter_row, i - 1, 1 - slot, wait=True)


def _moe_experts(xn, sched, slot_asg, n_used, nblk, w_gate, w_up, w_down, layer):
    n_tok = xn.shape[0] // PACK_SLABS
    buf_rows = 2 * MOE_BLOCK * PACK_SLABS
    any_spec = pl.BlockSpec(memory_space=pl.ANY)
    grid_spec = pltpu.PrefetchScalarGridSpec(
        num_scalar_prefetch=3, grid=(nblk,),
        in_specs=[any_spec, any_spec, any_spec, any_spec], out_specs=any_spec,
        scratch_shapes=[pltpu.VMEM((buf_rows, LANE), jnp.uint32), pltpu.VMEM((buf_rows, LANE), jnp.uint32),
                        pltpu.VMEM((2, D_MODEL, MOE_FF), F32), pltpu.VMEM((2, D_MODEL, MOE_FF), F32),
                        pltpu.VMEM((2, MOE_FF, D_MODEL), F32),
                        pltpu.VMEM((D_MODEL, MOE_FF), BF16), pltpu.VMEM((D_MODEL, MOE_FF), BF16),
                        pltpu.VMEM((MOE_FF, D_MODEL), BF16),
                        pltpu.SemaphoreType.DMA((2,)), pltpu.SemaphoreType.DMA((2,)), pltpu.SemaphoreType.DMA((2,))])
    return pl.pallas_call(
        functools.partial(_moe_kernel, n_tok=n_tok, layer=layer), grid_spec=grid_spec,
        out_shape=jax.ShapeDtypeStruct(((MOE_TOPK * n_tok + 2 * MOE_BLOCK) * PACK_SLABS, LANE), jnp.uint32),
        compiler_params=_cparams(("arbitrary",), 56), name="moe_experts",
    )(sched, slot_asg, n_used, xn, w_gate, w_up, w_down)


def _ple_kernel(h1_ref, y0_ref, y1_ref, rg_ref, p_ref, gp_ref, wg_ref, wp_ref, gfin_ref, o_ref, *, final):
    rg = rg_ref[...]
    tm = rg.shape[0]
    ffn = rg[:, 0:1] * _unpack_rows(y0_ref, 0, tm) + rg[:, 1:2] * _unpack_rows(y1_ref, 0, tm)
    h2 = h1_ref[...] + ffn
    hn = _rms(h2, gp_ref[...]).astype(BF16)
    gate = _sigmoid(jnp.dot(hn, wg_ref[...], preferred_element_type=F32))
    pe = jnp.dot(p_ref[...].astype(BF16), wp_ref[...], preferred_element_type=F32)
    h3 = h2 + pe * gate
    o_ref[...] = _rms(h3, gfin_ref[...]) if final else h3


def _ple(h1, y2, rg, p, gp, wg, wp, gfin, final, tm=256):
    rows = h1.shape[0]
    n_t = rows // tm

    def rspec(n):
        return pl.BlockSpec((tm, n), lambda i: (i, 0))

    def yspec(k):
        return pl.BlockSpec((tm * PACK_SLABS, LANE), lambda i: (k * n_t + i, 0))

    return pl.pallas_call(
        functools.partial(_ple_kernel, final=final), grid=(n_t,),
        in_specs=[rspec(D_MODEL), yspec(0), yspec(1), rspec(ROUTE_W), rspec(PLE_DIM), _full(gp), _full(wg),
                  _full(wp), _full(gfin)],
        out_specs=rspec(D_MODEL), out_shape=jax.ShapeDtypeStruct((rows, D_MODEL), F32),
        compiler_params=_cparams(("parallel",), 48), name="moe_combine_ple",
    )(h1, y2, y2, rg, p, gp, wg, wp, gfin)


def _rope_tables(positions, rot_dim, lead, period, reps):
    half = rot_dim // 2
    inv_freq = ROPE_THETA ** (-jnp.arange(0, rot_dim, 2, dtype=F32) / rot_dim)
    ang = positions.astype(F32).reshape(-1, 1) * inv_freq
    cos, sin = jnp.cos(ang), jnp.sin(ang)
    n = cos.shape[0]
    tail = period - lead - rot_dim
    c = jnp.concatenate([jnp.ones((n, lead), F32), cos, cos, jnp.ones((n, tail), F32)], axis=1)
    s1 = jnp.concatenate([jnp.zeros((n, lead), F32), -sin, jnp.zeros((n, half + tail), F32)], axis=1)
    s2 = jnp.concatenate([jnp.zeros((n, lead + half), F32), sin, jnp.zeros((n, tail), F32)], axis=1)
    return tuple(jnp.tile(t, (1, reps)) for t in (c, s1, s2))


def _split_cols(w, sizes):
    out, o = [], 0
    for n in sizes:
        out.append(w[:, o:o + n])
        o += n
    return out


def _in_weights(w_in):
    cb, cc, ch, su, dq, dk, dv, cq, ckv, kr = _split_cols(
        w_in, (CONV_DIM, CONV_DIM, CONV_DIM, SSM_DIM, 3 * DIL_GW, 3 * DIL_GW, 3 * DIL_GW, MLA_Q_RANK, MLA_KV_RANK,
               MLA_ROPE))
    ws = [cb, cc, ch, jnp.pad(su, ((0, 0), (0, SSM_PAD - SSM_DIM)))]
    kinds = ["rows", "rows", "rows", "time_major"]
    for g in range(len(DIL_PATTERNS)):
        ws += [jnp.pad(m[:, g * DIL_GW:(g + 1) * DIL_GW], ((0, 0), (0, DIL_PLANES * LANE - DIL_GW)))
               for m in (dq, dk, dv)]
        kinds += ["planes_rope_q", "planes_rope_k", "planes"]
    ws += [cq, ckv, jnp.pad(kr, ((0, 0), (MLA_NOPE, MLA_HW - MLA_NOPE - MLA_ROPE)))]
    kinds += ["mla_cq", "mla_ckv", "mla_kr"]
    return [w.astype(BF16) for w in ws], kinds


def _mla_weights(w_uq, w_ukv):
    q = w_uq.reshape(MLA_Q_RANK, MLA_HEADS, MLA_QK)
    wq = jnp.pad(q, ((0, 0), (0, 0), (0, MLA_HW - MLA_QK))).reshape(MLA_Q_RANK, MLA_HEADS * MLA_HW)
    kv = w_ukv.reshape(MLA_KV_RANK, MLA_HEADS, MLA_NOPE + MLA_V)
    wk = jnp.pad(kv[:, :, :MLA_NOPE], ((0, 0), (0, 0), (0, MLA_HW - MLA_NOPE))).reshape(MLA_KV_RANK, -1)
    wv = jnp.pad(kv[:, :, MLA_NOPE:].reshape(MLA_KV_RANK, MLA_HEADS * MLA_V),
                 ((0, 0), (0, MLA_VW - MLA_HEADS * MLA_V)))
    return wq.astype(BF16), wk.astype(BF16), wv.astype(BF16)


def _out_weights(w_out):
    o_ssm, o_dil, o_mla = CONV_DIM, CONV_DIM + SSM_DIM, CONV_DIM + SSM_DIM + 3 * DIL_GW
    ws = [w_out[:o_ssm], jnp.pad(w_out[o_ssm:o_dil], ((0, SSM_PAD - SSM_DIM), (0, 0)))]
    ws += [jnp.concatenate([jnp.pad(w_out[o_dil + g * DIL_GW:o_dil + (g + 1) * DIL_GW],
                                    ((0, DIL_PLANES * LANE - DIL_GW), (0, 0))) for g in range(3)], axis=0)]
    ws += [jnp.pad(w_out[o_mla:], ((0, MLA_VW - MLA_HEADS * MLA_V), (0, 0)))]
    return [w.astype(BF16) for w in ws]


def _router_weights(w_group, b_group, w_router, b_router):
    w = jnp.pad(jnp.concatenate([w_group, w_router], axis=1), ((0, 0), (0, ROUTE_W - MOE_GROUPS - N_EXPERTS)))
    hi = w.astype(BF16)
    lo = (w - hi.astype(F32)).astype(BF16)
    b = jnp.pad(jnp.concatenate([b_group, b_router]), (0, ROUTE_W - MOE_GROUPS - N_EXPERTS)).reshape(1, ROUTE_W)
    return jnp.concatenate([hi, lo], axis=1), b


def kernel(x, p, positions, norm_mix_g, w_in, conv_w, ssm_lam_re, ssm_lam_im, ssm_log_dt, ssm_b_re, ssm_b_im, ssm_c_re, ssm_c_im, ssm_d, ssm_glu_w, ssm_glu_b, mla_q_norm_g, mla_w_uq, mla_kv_norm_g, mla_w_ukv, w_out, norm_ffn_g, w_group, b_group, w_router, b_router, moe_w_gate, moe_w_up, moe_w_down, norm_ple_g, ple_w_proj, ple_w_gate, final_norm_g):
    batch, seq, _ = x.shape
    n_tok = batch * seq
    depth = w_in.shape[0]
    dil_tabs = _rope_tables(positions, DIL_ROT, 0, DIL_HEAD_DIM, LANE // DIL_HEAD_DIM)
    mla_tabs = _rope_tables(positions, MLA_ROPE, MLA_NOPE, MLA_HW, 1)
    h = x.reshape(n_tok, D_MODEL)
    for i in range(depth):
        mla_prm = (mla_q_norm_g[i].reshape(1, -1), mla_kv_norm_g[i].reshape(1, -1),
                   *_mla_weights(mla_w_uq[i], mla_w_ukv[i]))
        z = _in_proj(h, norm_mix_g[i].reshape(1, -1), dil_tabs, mla_tabs, mla_prm, *_in_weights(w_in[i]), batch, seq)
        cb, cc, ch, su = z[:4]
        q, k, v = z[13:]
        y_conv = _conv_mixer(cb, cc, ch, conv_w[i], batch, seq)
        prm = _ssm_params(ssm_lam_re[i], ssm_lam_im[i], ssm_log_dt[i], ssm_b_re[i], ssm_b_im[i], ssm_c_re[i],
                          ssm_c_im[i], ssm_d[i], ssm_glu_w[i], ssm_glu_b[i])
        y_ssm = _ssm_mixer(su, prm, batch, seq)
        outs, lses = [], []
        for g, (window, dil) in enumerate(DIL_PATTERNS):
            assert window // dil == DIL_BLK and (seq // dil) % DIL_BLK == 0
            o, l = _dil_group(z[4 + 3 * g], z[5 + 3 * g], z[6 + 3 * g], dil, batch, seq)
            outs.append(o)
            lses.append(l)
        y_mla = _mla_attn(q, k, v, batch, seq)
        wr, br = _router_weights(w_group[i], b_group[i], w_router[i], b_router[i])
        h1, xn, route_i, route_g = _out_proj(
            (y_conv, y_ssm, outs, lses, y_mla), h, _out_weights(w_out[i]),
            norm_ffn_g[i].reshape(1, -1), wr, br, batch, seq)
        sched, slot_asg, n_used, nblk = _moe_plan(route_i, n_tok)
        y2 = _moe_experts(xn, sched, slot_asg, n_used, nblk, moe_w_gate, moe_w_up, moe_w_down, i)
        h = _ple(h1, y2, route_g, p[i].reshape(n_tok, PLE_DIM), norm_ple_g[i].reshape(1, -1),
                 ple_w_gate[i].astype(BF16), ple_w_proj[i].astype(BF16), final_norm_g.reshape(1, -1),
                 final=(i == depth - 1))
    return h.reshape(batch, seq, D_MODEL)
```

```python
import functools
import math

import jax
import jax.numpy as jnp
from jax import lax
from jax.experimental import pallas as pl
from jax.experimental.pallas import tpu as pltpu

D_MODEL = 2048
PLE_DIM = 256
ROPE_THETA = 500000.0
NORM_EPS = 1e-6

CONV_DIM = 448
CONV_K = 3

SSM_DIM = 448
SSM_GROUP = 16
SSM_GROUPS = SSM_DIM // SSM_GROUP
SSM_STATE = 64
SSM_N = SSM_GROUPS * SSM_STATE
SSM_PAD = 512

DIL_HEAD_DIM = 64
DIL_ROT = DIL_HEAD_DIM // 4
DIL_PATTERNS = ((128, 1), (512, 4), (2048, 16))
DIL_HPG = 3
DIL_GW = DIL_HPG * DIL_HEAD_DIM
DIL_PLANES = 2
DIL_BLK = 128
DIL_UNROLL = 4

MLA_HEADS = 9
MLA_Q_RANK = 384
MLA_KV_RANK = 256
MLA_NOPE = 64
MLA_ROPE = 32
MLA_V = 64
MLA_QK = MLA_NOPE + MLA_ROPE
MLA_HW = 128
MLA_VW = 640

MOE_GROUPS = 8
MOE_EPG = 8
N_EXPERTS = MOE_GROUPS * MOE_EPG
MOE_TOPK = 2
MOE_FF = 512
MOE_BLOCK = 128
ROUTE_W = 128

LANE = 128
PACK_SLABS = D_MODEL // 2 // LANE
ROW_CHUNK = 32
NEG_BIG = -1e30

BF16 = jnp.bfloat16
F32 = jnp.float32


def _cparams(sem, vmem_mb):
    return pltpu.CompilerParams(dimension_semantics=sem, vmem_limit_bytes=vmem_mb * 1024 * 1024)


def _rms(x, g):
    ms = jnp.mean(x * x, axis=-1, keepdims=True)
    return (x * lax.rsqrt(ms + NORM_EPS)) * g


def _sigmoid(x):
    return 1.0 / (1.0 + jnp.exp(-x))


def _full(a):
    return pl.BlockSpec(a.shape, lambda *_: (0,) * a.ndim)


_HI16 = 0xFFFF0000


def _pack_rows(ref, base, x):
    n = x.shape[0]
    bits = lax.bitcast_convert_type(x, jnp.uint32)
    for c in range(PACK_SLABS):
        lo = bits[:, c * LANE:(c + 1) * LANE] >> 16
        hi = bits[:, D_MODEL // 2 + c * LANE:D_MODEL // 2 + (c + 1) * LANE] & jnp.uint32(_HI16)
        ref[pl.ds(base + c, n, stride=PACK_SLABS), :] = lo | hi


def _unpack_rows(ref, base, n):
    lo, hi = [], []
    for c in range(PACK_SLABS):
        w = ref[pl.ds(base + c, n, stride=PACK_SLABS), :]
        lo.append(lax.bitcast_convert_type(w << 16, F32))
        hi.append(lax.bitcast_convert_type(w & jnp.uint32(_HI16), F32))
    return jnp.concatenate(lo + hi, axis=1)


def _in_proj_kernel(x_ref, g_ref, dc_ref, ds1_ref, ds2_ref, mc_ref, ms1_ref, ms2_ref, gq_ref, gkv_ref,
                    wq_ref, wk_ref, wv_ref, *refs, kinds):
    n_out = len(kinds)
    w_refs, o_refs = refs[:n_out], refs[n_out:]
    xn = _rms(x_ref[...], g_ref[...]).astype(BF16)
    tabs = [jnp.concatenate([t[...]] * DIL_PLANES, axis=1) for t in (dc_ref, ds1_ref, ds2_ref)]
    latent = {}
    outs = iter(o_refs)
    for kind, w_ref in zip(kinds, w_refs):
        res = jnp.dot(xn, w_ref[...], preferred_element_type=F32)
        if kind.startswith("mla"):
            latent[kind] = res
            continue
        o_ref = next(outs)
        if kind.startswith("planes"):
            if kind != "planes":
                res = _rope_lanes(res, *tabs, DIL_ROT // 2, DIL_PLANES * LANE)
            if kind == "planes_rope_q":
                res = res * (1.0 / math.sqrt(DIL_HEAD_DIM))
            for j in range(o_ref.shape[0]):
                o_ref[j] = res[:, j * LANE:(j + 1) * LANE]
        else:
            o_ref[...] = res
    q_ref, k_ref, v_ref = outs
    rope = functools.partial(_rope_lanes, c=mc_ref[...], s1=ms1_ref[...], s2=ms2_ref[...], half=MLA_ROPE // 2,
                             width=MLA_HW)
    qn = _rms(latent["mla_cq"], gq_ref[...]).astype(BF16)
    kvn = _rms(latent["mla_ckv"], gkv_ref[...]).astype(BF16)
    q = jnp.dot(qn, wq_ref[...], preferred_element_type=F32)
    kn = jnp.dot(kvn, wk_ref[...], preferred_element_type=F32)
    lane = lax.broadcasted_iota(jnp.int32, (1, MLA_HEADS * MLA_HW), 1)
    ones_col = (lane % MLA_HW == MLA_V).astype(F32)
    v_ref[...] = (jnp.dot(kvn, wv_ref[...], preferred_element_type=F32) + ones_col).astype(BF16)
    kr = rope(latent["mla_kr"])
    scale = math.log2(math.e) / math.sqrt(MLA_QK)
    for h in range(MLA_HEADS):
        sl = slice(h * MLA_HW, (h + 1) * MLA_HW)
        q_ref[:, sl] = (rope(q[:, sl]) * scale).astype(BF16)
        k_ref[:, sl] = (kn[:, sl] + kr).astype(BF16)


def _in_proj(h, g, dil_tabs, mla_tabs, mla_prm, weights, kinds, batch, seq, tm=256):
    n_s = seq // tm

    def rspec(n):
        return pl.BlockSpec((tm, n), lambda b, i: (b * n_s + i, 0))

    in_specs = [rspec(D_MODEL), _full(g)] + [rspec(LANE)] * 6 + [_full(a) for a in mla_prm]
    in_specs += [_full(w) for w in weights]
    out_shape, out_specs = [], []
    for kind, w in zip(kinds, weights):
        n = w.shape[1]
        if kind.startswith("mla"):
            continue
        if kind == "time_major":
            out_shape.append(jax.ShapeDtypeStruct((seq, batch * n), F32))
            out_specs.append(pl.BlockSpec((tm, n), lambda b, i: (i, b)))
        elif kind.startswith("planes"):
            out_shape.append(jax.ShapeDtypeStruct((n // LANE, batch * seq, LANE), F32))
            out_specs.append(pl.BlockSpec((n // LANE, tm, LANE), lambda b, i: (0, b * n_s + i, 0)))
        else:
            out_shape.append(jax.ShapeDtypeStruct((batch * seq, n), F32))
            out_specs.append(rspec(n))
    for n in (MLA_HEADS * MLA_HW,) * 3:
        out_shape.append(jax.ShapeDtypeStruct((batch * seq, n), BF16))
        out_specs.append(rspec(n))
    return pl.pallas_call(
        functools.partial(_in_proj_kernel, kinds=tuple(kinds)),
        grid=(batch, n_s), in_specs=in_specs, out_specs=out_specs, out_shape=out_shape,
        compiler_params=_cparams(("parallel", "parallel"), 56), name="in_proj",
    )(h, g, *dil_tabs, *mla_tabs, *mla_prm, *weights)


def _conv_kernel(cb_ref, cc_ref, ch_ref, w_ref, o_ref, u_ref, *, rows):
    seq = cb_ref.shape[0]
    u_ref[0:8, :] = jnp.zeros((8, CONV_DIM), F32)
    for r0 in range(0, seq, rows):
        u_ref[8 + r0:8 + r0 + rows, :] = cc_ref[r0:r0 + rows, :] * ch_ref[r0:r0 + rows, :]
    w0, w1, w2 = w_ref[0:1, :], w_ref[1:2, :], w_ref[2:3, :]
    for r0 in range(0, seq, rows):
        acc = (w2 * u_ref[8 + r0:8 + r0 + rows, :] + w1 * u_ref[7 + r0:7 + r0 + rows, :]
               + w0 * u_ref[6 + r0:6 + r0 + rows, :])
        o_ref[r0:r0 + rows, :] = cb_ref[r0:r0 + rows, :] * acc


def _conv_mixer(cb, cc, ch, conv_w, batch, seq):
    spec = pl.BlockSpec((seq, CONV_DIM), lambda b: (b, 0))
    return pl.pallas_call(
        functools.partial(_conv_kernel, rows=256),
        grid=(batch,), in_specs=[spec, spec, spec, _full(conv_w)], out_specs=spec,
        out_shape=jax.ShapeDtypeStruct((batch * seq, CONV_DIM), F32),
        scratch_shapes=[pltpu.VMEM((seq + 8, CONV_DIM), F32)],
        compiler_params=_cparams(("parallel",), 48), name="conv_mixer",
    )(cb, cc, ch, conv_w)


SSM_NP = SSM_N // LANE
SSM_SCAN_PLANES = 4
_SSM_SPLIT = ((0, 256, 0, 1024), (256, 256, 1024, 768))


def _gelu_tanh(x):
    return 0.5 * x * (1.0 + jnp.tanh(math.sqrt(2.0 / math.pi) * (x + 0.044715 * (x * x * x))))


def _ssm_kernel(u_ref, bf0_ref, bf1_ref, are_ref, aim_ref, cc0_ref, cc1_ref, d_ref, gw_ref, gb_ref, o_ref,
                xs_ref, st_ref, *, batch, tc):
    @pl.when(pl.program_id(0) == 0)
    def _():
        st_ref[...] = jnp.zeros_like(st_ref)

    def batch_rows(b):
        return pl.ds(b, tc, stride=batch)

    for b in range(batch):
        for (l0, lw, s0, sw), bf_ref in zip(_SSM_SPLIT, (bf0_ref, bf1_ref)):
            ub = u_ref[:, b * SSM_PAD + l0:b * SSM_PAD + l0 + lw].astype(BF16)
            r = jnp.dot(ub, bf_ref[...], preferred_element_type=F32)
            for q in range(sw // LANE):
                xs_ref[s0 // LANE + q, batch_rows(b), :] = r[:, q * LANE:(q + 1) * LANE]
                xs_ref[SSM_NP + s0 // LANE + q, batch_rows(b), :] = r[:, sw + q * LANE:sw + (q + 1) * LANE]

    per_tile = 8 // batch
    for p0 in range(0, SSM_NP, SSM_SCAN_PLANES):
        ps = list(range(p0, min(p0 + SSM_SCAN_PLANES, SSM_NP)))
        ar = [jnp.broadcast_to(are_ref[:, p * LANE:(p + 1) * LANE], (batch, LANE)) for p in ps]
        ai = [jnp.broadcast_to(aim_ref[:, p * LANE:(p + 1) * LANE], (batch, LANE)) for p in ps]

        def body(k, carry, ps=ps, ar=ar, ai=ai):
            rows = pl.ds(pl.multiple_of(k * 8, 8), 8)
            new = []
            for idx, p in enumerate(ps):
                sr, si = carry[idx]
                tile_r, tile_i = xs_ref[p, rows, :], xs_ref[SSM_NP + p, rows, :]
                out_r, out_i = [], []
                for j in range(per_tile):
                    step = slice(j * batch, (j + 1) * batch)
                    sr, si = (ar[idx] * sr - ai[idx] * si + tile_r[step], ar[idx] * si + ai[idx] * sr + tile_i[step])
                    out_r.append(sr)
                    out_i.append(si)
                xs_ref[p, rows, :] = jnp.concatenate(out_r, axis=0)
                xs_ref[SSM_NP + p, rows, :] = jnp.concatenate(out_i, axis=0)
                new.append((sr, si))
            return tuple(new)

        init = tuple((st_ref[p, 0:batch, :], st_ref[SSM_NP + p, 0:batch, :]) for p in ps)
        for p, (sr, si) in zip(ps, lax.fori_loop(0, tc // per_tile, body, init)):
            st_ref[p, 0:batch, :] = sr
            st_ref[SSM_NP + p, 0:batch, :] = si

    for b in range(batch):
        ys = []
        for (l0, lw, s0, sw), cc_ref in zip(_SSM_SPLIT, (cc0_ref, cc1_ref)):
            planes = list(range(s0 // LANE, (s0 + sw) // LANE))
            xb = jnp.concatenate([xs_ref[p, batch_rows(b), :] for p in planes]
                                 + [xs_ref[SSM_NP + p, batch_rows(b), :] for p in planes], axis=1)
            ys.append(jnp.dot(xb.astype(BF16), cc_ref[...], preferred_element_type=F32))
        y = jnp.concatenate(ys, axis=1) + d_ref[...] * u_ref[:, b * SSM_PAD:(b + 1) * SSM_PAD]
        g = _gelu_tanh(y)
        o_ref[:, b * SSM_PAD:(b + 1) * SSM_PAD] = g * _sigmoid(
            jnp.dot(g.astype(BF16), gw_ref[...], preferred_element_type=F32) + gb_ref[...])


def _ssm_mixer(su, prm, batch, seq, tc=256):
    spec = pl.BlockSpec((tc, batch * SSM_PAD), lambda c: (c, 0))
    args = (prm["bf0"], prm["bf1"], prm["a_re"], prm["a_im"], prm["cc0"], prm["cc1"], prm["d"], prm["gw"], prm["gb"])
    return pl.pallas_call(
        functools.partial(_ssm_kernel, batch=batch, tc=tc),
        grid=(seq // tc,), in_specs=[spec] + [_full(a) for a in args], out_specs=spec,
        out_shape=jax.ShapeDtypeStruct((seq, batch * SSM_PAD), F32),
        scratch_shapes=[pltpu.VMEM((2 * SSM_NP, tc * batch, LANE), F32), pltpu.VMEM((2 * SSM_NP, 8, LANE), F32)],
        compiler_params=_cparams(("arbitrary",), 48), name="ssm_mixer",
    )(su, *args)


def _ssm_params(lam_re, lam_im, log_dt, b_re, b_im, c_re, c_im, d_skip, glu_w, glu_b):
    dt = jnp.exp(log_dt)[:, None]
    mag = jnp.exp(lam_re * dt)
    a_re = mag * jnp.cos(lam_im * dt)
    a_im = mag * jnp.sin(lam_im * dt)
    nr, ni = a_re - 1.0, a_im
    den = lam_re * lam_re + lam_im * lam_im
    f_re = (nr * lam_re + ni * lam_im) / den
    f_im = (ni * lam_re - nr * lam_im) / den
    bfr = f_re[:, :, None] * b_re - f_im[:, :, None] * b_im
    bfi = f_re[:, :, None] * b_im + f_im[:, :, None] * b_re
    eye = jnp.eye(SSM_GROUPS, dtype=F32)

    def in_blockdiag(m):
        return jnp.einsum("gpc,gh->gchp", m, eye).reshape(SSM_DIM, SSM_N)

    def out_blockdiag(m):
        return jnp.einsum("gcp,gh->gphc", m, eye).reshape(SSM_N, SSM_DIM)

    pad = SSM_PAD - SSM_DIM
    b_re_d, b_im_d = (jnp.pad(in_blockdiag(m), ((0, pad), (0, 0))) for m in (bfr, bfi))
    c_re_d, c_im_d = (jnp.pad(out_blockdiag(m), ((0, 0), (0, pad))) for m in (c_re, -c_im))
    prm = {}
    for c, (l0, lw, s0, sw) in enumerate(_SSM_SPLIT):
        prm[f"bf{c}"] = jnp.concatenate([b_re_d[l0:l0 + lw, s0:s0 + sw], b_im_d[l0:l0 + lw, s0:s0 + sw]],
                                        axis=1).astype(BF16)
        prm[f"cc{c}"] = jnp.concatenate([c_re_d[s0:s0 + sw, l0:l0 + lw], c_im_d[s0:s0 + sw, l0:l0 + lw]],
                                        axis=0).astype(BF16)
    return {
        **prm,
        "a_re": a_re.reshape(1, SSM_N), "a_im": a_im.reshape(1, SSM_N),
        "d": jnp.pad(d_skip, (0, pad)).reshape(1, SSM_PAD),
        "gw": jnp.pad(glu_w, ((0, pad), (0, pad))).astype(BF16),
        "gb": jnp.pad(glu_b, (0, pad)).reshape(1, SSM_PAD),
    }


def _rope_lanes(x, c, s1, s2, half, width):
    return x * c + pltpu.roll(x, width - half, 1) * s1 + pltpu.roll(x, half, 1) * s2


def _dil_kernel(q_ref, k_ref, v_ref, o_ref, l_ref, *, dil):
    seq = q_ref.shape[1]
    nb = (seq // dil) // DIL_BLK
    width = DIL_PLANES * LANE
    lane = lax.broadcasted_iota(jnp.int32, (1, width), 1)
    cmask = [(lane // DIL_HEAD_DIM == c).astype(F32) for c in range(DIL_HPG)]
    qi = lax.broadcasted_iota(jnp.int32, (DIL_HPG * DIL_BLK, 2 * DIL_BLK), 0) % DIL_BLK
    ki = lax.broadcasted_iota(jnp.int32, (DIL_HPG * DIL_BLK, 2 * DIL_BLK), 1)
    band = (ki >= qi) & (ki <= qi + DIL_BLK)

    def planes(ref, rows):
        return jnp.concatenate([ref[j, rows, :] for j in range(DIL_PLANES)], axis=1)

    def rows_of(m, blk):
        if dil == 1:
            return pl.ds(pl.multiple_of(blk * DIL_BLK, DIL_BLK), DIL_BLK)
        return pl.ds(blk * DIL_BLK * dil + m, DIL_BLK, stride=dil)

    def blocks(it, carry):
        ids = [it * DIL_UNROLL + u for u in range(DIL_UNROLL)]
        mis = [(idx // nb, idx % nb) for idx in ids]
        rows_q = [rows_of(m, i) for m, i in mis]
        rows_p = [rows_of(m, jnp.maximum(i - 1, 0)) for m, i in mis]
        scores, vcats = [], []
        for rq, rp in zip(rows_q, rows_p):
            q = planes(q_ref, rq)
            kcat = jnp.concatenate([planes(k_ref, rp), planes(k_ref, rq)], axis=0).astype(BF16)
            vcats.append(jnp.concatenate([planes(v_ref, rp), planes(v_ref, rq)], axis=0).astype(BF16))
            qs = jnp.concatenate([q * cm for cm in cmask], axis=0).astype(BF16)
            scores.append(lax.dot_general(qs, kcat, (((1,), (1,)), ((), ())), preferred_element_type=F32))
        soft = []
        for s, (m, i) in zip(scores, mis):
            s = jnp.where(band & ((ki >= DIL_BLK) | (i > 0)), s, NEG_BIG)
            mx = jnp.max(s, axis=-1, keepdims=True)
            p = jnp.exp(s - mx)
            den = jnp.sum(p, axis=-1, keepdims=True)
            soft.append((p.astype(BF16), den, mx + jnp.log(den)))
        for (p, den, lse), vcat, rq in zip(soft, vcats, rows_q):
            o = jnp.dot(p, vcat, preferred_element_type=F32) / den
            out = jnp.zeros((DIL_BLK, width), F32)
            lout = jnp.zeros((DIL_BLK, width), F32)
            for c, cm in enumerate(cmask):
                out = out + cm * o[c * DIL_BLK:(c + 1) * DIL_BLK, :]
                lout = lout + cm * lse[c * DIL_BLK:(c + 1) * DIL_BLK, :]
            for j in range(DIL_PLANES):
                o_ref[j, rq, :] = out[:, j * LANE:(j + 1) * LANE]
                l_ref[j, rq, :] = lout[:, j * LANE:(j + 1) * LANE]
        return carry

    lax.fori_loop(0, dil * nb // DIL_UNROLL, blocks, 0)


def _dil_group(q, k, v, dil, batch, seq):
    spec = pl.BlockSpec((DIL_PLANES, seq, LANE), lambda b: (0, b, 0))
    sds = jax.ShapeDtypeStruct((DIL_PLANES, batch * seq, LANE), F32)
    return pl.pallas_call(
        functools.partial(_dil_kernel, dil=dil), grid=(batch,), in_specs=[spec] * 3,
        out_specs=[spec, spec], out_shape=[sds, sds],
        compiler_params=_cparams(("parallel",), 48), name=f"dil_attn_d{dil}",
    )(q, k, v)


def _mla_attn_kernel(q_ref, k_ref, v_ref, o_ref, *, tq, group):
    qi = pl.program_id(1)
    row = lax.broadcasted_iota(jnp.int32, (tq, tq), 0)
    col = lax.broadcasted_iota(jnp.int32, (tq, tq), 1)
    causal = col <= row

    def heads_out(heads):
        qs = [q_ref[:, h * MLA_HW:(h + 1) * MLA_HW] for h in heads]

        def step(kj, carry, masked):
            rows = pl.ds(pl.multiple_of(kj * tq, tq), tq)
            scores = [lax.dot_general(q, k_ref[rows, h * MLA_HW:(h + 1) * MLA_HW], (((1,), (1,)), ((), ())),
                                      preferred_element_type=F32) for h, q in zip(heads, qs)]
            stats = []
            for s, (m, acc) in zip(scores, carry):
                if masked:
                    s = jnp.where(causal, s, NEG_BIG)
                mn = jnp.maximum(m, jnp.max(s, axis=-1, keepdims=True))
                stats.append((mn, jnp.exp2(m - mn), jnp.exp2(s - mn).astype(BF16)))
            out = []
            for h, (mn, alpha, p), (_, acc) in zip(heads, stats, carry):
                v = v_ref[rows, h * MLA_HW:(h + 1) * MLA_HW]
                out.append((mn, alpha * acc + jnp.dot(p, v, preferred_element_type=F32)))
            return tuple(out)

        init = tuple((jnp.full((tq, 1), NEG_BIG, F32), jnp.zeros((tq, MLA_HW), F32)) for _ in heads)
        carry = lax.fori_loop(0, qi, functools.partial(step, masked=False), init)
        return [acc / acc[:, MLA_V:MLA_V + 1] for _, acc in step(qi, carry, True)]

    for h0 in range(0, MLA_HEADS, group):
        heads = list(range(h0, min(h0 + group, MLA_HEADS)))
        for h, o in zip(heads, heads_out(heads)):
            lo = h * MLA_V
            if h % 2:
                o_ref[:, lo:lo + MLA_V] = pltpu.roll(o, MLA_V, 1)[:, MLA_V:2 * MLA_V]
            else:
                o_ref[:, lo:lo + MLA_V] = o[:, 0:MLA_V]
    if MLA_HEADS % 2:
        o_ref[:, MLA_HEADS * MLA_V:] = jnp.zeros((tq, MLA_VW - MLA_HEADS * MLA_V), F32)


def _mla_attn(q, k, v, batch, seq, tq=256, group=5):
    hw = MLA_HEADS * MLA_HW
    n_q = seq // tq
    return pl.pallas_call(
        functools.partial(_mla_attn_kernel, tq=tq, group=group), grid=(batch, n_q),
        in_specs=[pl.BlockSpec((tq, hw), lambda b, i: (b * n_q + i, 0)),
                  pl.BlockSpec((seq, hw), lambda b, i: (b, 0)),
                  pl.BlockSpec((seq, hw), lambda b, i: (b, 0))],
        out_specs=pl.BlockSpec((tq, MLA_VW), lambda b, i: (b * n_q + i, 0)),
        out_shape=jax.ShapeDtypeStruct((batch * seq, MLA_VW), F32),
        compiler_params=_cparams(("parallel", "arbitrary"), 48), name="mla_attn",
    )(q, k, v)


def _out_proj_kernel(yc_ref, ys_ref, od0_ref, od1_ref, od2_ref, ld0_ref, ld1_ref, ld2_ref, ym_ref, h_ref,
                     wc_ref, ws_ref, wd_ref, wm_ref,
                     gf_ref, wr_ref, br_ref,
                     h1_ref, xn_ref, ri_ref, rg_ref):
    def mm(y, w_ref):
        return jnp.dot(y.astype(BF16), w_ref[...], preferred_element_type=F32)

    y_dil = [None] * (3 * DIL_PLANES)
    for j in range(DIL_PLANES):
        la, lb, lc = ld0_ref[j], ld1_ref[j], ld2_ref[j]
        mx = jnp.maximum(jnp.maximum(la, lb), lc)
        ea, eb, ec = jnp.exp(la - mx), jnp.exp(lb - mx), jnp.exp(lc - mx)
        inv = 1.0 / (ea + eb + ec)
        for g, (o_ref, e) in enumerate(((od0_ref, ea), (od1_ref, eb), (od2_ref, ec))):
            y_dil[g * DIL_PLANES + j] = o_ref[j] * (e * inv)
    mix = (mm(yc_ref[...], wc_ref) + mm(ys_ref[...], ws_ref) + mm(jnp.concatenate(y_dil, axis=1), wd_ref)
           + mm(ym_ref[...], wm_ref))
    h1 = h_ref[...] + mix
    h1_ref[...] = h1
    hn = _rms(h1, gf_ref[...])
    hi = hn.astype(BF16)
    hi32 = hi.astype(F32)
    lo = (hn - hi32).astype(BF16)
    _pack_rows(xn_ref, 0, hi32)
    r_hi = jnp.dot(hi, wr_ref[...], preferred_element_type=F32)
    r_lo = jnp.dot(lo, wr_ref[:, 0:ROUTE_W], preferred_element_type=F32)
    logits = r_hi[:, 0:ROUTE_W] + r_hi[:, ROUTE_W:2 * ROUTE_W] + r_lo + br_ref[...]
    lane = lax.broadcasted_iota(jnp.int32, logits.shape, 1)
    gl = jnp.where(lane < MOE_GROUPS, logits, NEG_BIG)
    gmax = jnp.max(gl, axis=-1, keepdims=True)
    g_top_p = 1.0 / jnp.sum(jnp.exp(gl - gmax), axis=-1, keepdims=True)
    g_idx = jnp.min(jnp.where(gl == gmax, lane, ROUTE_W), axis=-1, keepdims=True)
    in_group = (lane >= MOE_GROUPS) & (lane < MOE_GROUPS + N_EXPERTS) & (((lane - MOE_GROUPS) >> 3) == g_idx)
    el = jnp.where(in_group, logits, NEG_BIG)
    v1 = jnp.max(el, axis=-1, keepdims=True)
    i1 = jnp.min(jnp.where(el == v1, lane, ROUTE_W), axis=-1, keepdims=True)
    el2 = jnp.where(lane == i1, NEG_BIG, el)
    v2 = jnp.max(el2, axis=-1, keepdims=True)
    i2 = jnp.min(jnp.where(el2 == v2, lane, ROUTE_W), axis=-1, keepdims=True)
    e2 = jnp.exp(v2 - v1)
    w1 = g_top_p / (1.0 + e2)
    w2 = g_top_p * e2 / (1.0 + e2)
    ri_ref[...] = jnp.where(lane == 0, i1 - MOE_GROUPS, jnp.where(lane == 1, i2 - MOE_GROUPS, 0))
    rg_ref[...] = jnp.where(lane == 0, w1, jnp.where(lane == 1, w2, 0.0))


def _out_proj(ys, h, ws, gf, wr, br, batch, seq, tm=256):
    rows = batch * seq
    n_s = seq // tm
    yc, ys_tm, dil_o, dil_l, ym = ys

    def rspec(n):
        return pl.BlockSpec((tm, n), lambda b, i: (b * n_s + i, 0))

    pspec = pl.BlockSpec((DIL_PLANES, tm, LANE), lambda b, i: (0, b * n_s + i, 0))
    in_specs = [rspec(CONV_DIM), pl.BlockSpec((tm, SSM_PAD), lambda b, i: (i, b))] + [pspec] * 6
    in_specs += [rspec(MLA_VW), rspec(D_MODEL)]
    in_specs += [_full(w) for w in ws] + [_full(gf), _full(wr), _full(br)]
    return pl.pallas_call(
        _out_proj_kernel, grid=(batch, n_s), in_specs=in_specs,
        out_specs=[rspec(D_MODEL), pl.BlockSpec((tm * PACK_SLABS, LANE), lambda b, i: (b * n_s + i, 0)),
                   rspec(ROUTE_W), rspec(ROUTE_W)],
        out_shape=[jax.ShapeDtypeStruct((rows, D_MODEL), F32),
                   jax.ShapeDtypeStruct((rows * PACK_SLABS, LANE), jnp.uint32),
                   jax.ShapeDtypeStruct((rows, ROUTE_W), jnp.int32), jax.ShapeDtypeStruct((rows, ROUTE_W), F32)],
        compiler_params=_cparams(("parallel", "parallel"), 48), name="out_proj_router",
    )(yc, ys_tm, *dil_o, *dil_l, ym, h, *ws, gf, wr, br)


def _moe_plan(route_i, n_tok):
    n_asg = n_tok * MOE_TOPK
    nblk = (n_asg + N_EXPERTS * (MOE_BLOCK - 1) + MOE_BLOCK - 1) // MOE_BLOCK
    flat_e = route_i[:, :MOE_TOPK].reshape(-1)
    onehot = (flat_e[:, None] == jnp.arange(N_EXPERTS, dtype=jnp.int32)[None, :]).astype(F32)
    oh3 = onehot.reshape(n_asg // MOE_BLOCK, MOE_BLOCK, N_EXPERTS)
    within = jnp.einsum("ij,bje->bie", jnp.tril(jnp.ones((MOE_BLOCK, MOE_BLOCK), F32)), oh3)
    chunk_tot = within[:, -1, :]
    chunk_off = jnp.cumsum(chunk_tot, axis=0) - chunk_tot
    rank = jnp.sum((within + chunk_off[:, None, :]) * oh3, axis=-1).reshape(n_asg).astype(jnp.int32) - 1
    counts = (chunk_off[-1] + chunk_tot[-1]).astype(jnp.int32)
    padded = ((counts + MOE_BLOCK - 1) // MOE_BLOCK) * MOE_BLOCK
    pend = jnp.cumsum(padded)
    first_blk = ((pend - padded) // MOE_BLOCK).astype(F32)
    dest = jnp.dot(onehot, first_blk).astype(jnp.int32) * MOE_BLOCK + rank
    in_blk = (dest[:, None] // MOE_BLOCK == jnp.arange(nblk, dtype=jnp.int32)[None, :]).astype(F32)
    at_off = (dest[:, None] % MOE_BLOCK == jnp.arange(MOE_BLOCK, dtype=jnp.int32)[None, :]).astype(F32)
    asg = jnp.arange(n_asg, dtype=jnp.int32)
    a_hi, a_lo = (asg // MOE_BLOCK).astype(F32)[:, None], (asg % MOE_BLOCK).astype(F32)[:, None]
    inv = lambda v: jnp.einsum("ab,ao->bo", in_blk * v, at_off)
    slot_asg = jnp.where(inv(1.0) > 0.5, inv(a_hi) * MOE_BLOCK + inv(a_lo), -1.0).astype(jnp.int32).reshape(-1)
    block_expert = jnp.minimum(
        jnp.searchsorted(pend, jnp.arange(nblk, dtype=jnp.int32) * MOE_BLOCK, side="right"), N_EXPERTS - 1
    ).astype(jnp.int32)
    n_used = (pend[-1] // MOE_BLOCK).astype(jnp.int32)
    prev = jnp.concatenate([jnp.full((1,), -1, jnp.int32), block_expert[:-1]])
    is_first = (block_expert != prev).astype(jnp.int32)
    w_slot = (jnp.cumsum(is_first) - 1) % 2
    run_end = (pend // MOE_BLOCK)[block_expert]
    nxt = jnp.where(run_end < n_used, block_expert[jnp.minimum(run_end, nblk - 1)], -1)
    blk = jnp.arange(nblk, dtype=jnp.int32)
    run_start = ((pend - padded) // MOE_BLOCK)[block_expert]
    rows_left = jnp.clip(counts[block_expert] - (blk - run_start) * MOE_BLOCK, 0, MOE_BLOCK)
    sched = jnp.stack([block_expert, is_first, w_slot, nxt, jnp.where(blk < n_used, rows_left, 0)])
    sched = jnp.pad(sched.astype(jnp.int32), ((0, 0), (0, 1)))
    return sched, slot_asg, n_used.reshape(1), nblk


def _moe_kernel(sched_ref, asg_ref, nu_ref, x_hbm, wg_hbm, wu_hbm, wd_hbm, y_hbm,
                xbuf, ybuf, wgf, wuf, wdf, wgb, wub, wdb, gsem, ssem, wsem, *, n_tok, layer):
    i = pl.program_id(0)
    n_used = nu_ref[0]
    slot = i % 2
    buf_rows = MOE_BLOCK * PACK_SLABS
    active = i < n_used
    first = active & (sched_ref[1, i] == 1)
    w_slot = sched_ref[2, i]

    def weights(e, ws):
        return [pltpu.make_async_copy(w_hbm.at[layer, e], buf.at[ws], wsem.at[ws])
                for w_hbm, buf in ((wg_hbm, wgf), (wu_hbm, wuf), (wd_hbm, wdf))]

    def slab_rows(row):
        return pl.ds(pl.multiple_of(row * PACK_SLABS, PACK_SLABS), PACK_SLABS)

    def gather_row(blk, sl, r):
        tok = jnp.maximum(asg_ref[blk * MOE_BLOCK + r], 0) >> 1
        return pltpu.make_async_copy(x_hbm.at[slab_rows(tok), :], xbuf.at[slab_rows(sl * MOE_BLOCK + r), :],
                                     gsem.at[sl])

    def scatter_row(blk, sl, r):
        a = asg_ref[blk * MOE_BLOCK + r]
        dst = jnp.where(a >= 0, (a & 1) * n_tok + (a >> 1), MOE_TOPK * n_tok + sl * MOE_BLOCK + r)
        return pltpu.make_async_copy(ybuf.at[slab_rows(sl * MOE_BLOCK + r), :], y_hbm.at[slab_rows(dst), :],
                                     ssem.at[sl])

    def row_chunks(make, blk, sl, wait):
        n_valid = sched_ref[4, blk]
        for c0 in range(0, MOE_BLOCK, ROW_CHUNK):
            @pl.when(c0 < n_valid)
            def _(c0=c0):
                for r in range(c0, c0 + ROW_CHUNK):
                    if wait:
                        make(blk, sl, r).wait()
                    else:
                        make(blk, sl, r).start()

    @pl.when(i == 0)
    def _():
        xbuf[...] = jnp.zeros_like(xbuf)
        ybuf[...] = jnp.zeros_like(ybuf)
        dump = pltpu.make_async_copy(ybuf, y_hbm.at[pl.ds(MOE_TOPK * n_tok * PACK_SLABS, 2 * buf_rows), :], ssem.at[0])
        dump.start()
        dump.wait()

    @pl.when((i == 0) & (n_used > 0))
    def _():
        for cp in weights(sched_ref[0, 0], 0):
            cp.start()
        row_chunks(gather_row, 0, 0, wait=False)

    @pl.when(first)
    def _():
        for cp in weights(sched_ref[0, i], w_slot):
            cp.wait()

    @pl.when(first & (sched_ref[3, i] >= 0))
    def _():
        for cp in weights(sched_ref[3, i], 1 - w_slot):
            cp.start(priority=1)

    @pl.when(first)
    def _():
        wgb[...] = wgf[w_slot].astype(BF16)
        wub[...] = wuf[w_slot].astype(BF16)
        wdb[...] = wdf[w_slot].astype(BF16)

    @pl.when(active)
    def _():
        row_chunks(gather_row, i, slot, wait=True)
        row_chunks(gather_row, i + 1, 1 - slot, wait=False)

    @pl.when(active & (i >= 2))
    def _():
        row_chunks(scatter_row, i - 2, slot, wait=True)

    @pl.when(active)
    def _():
        x = _unpack_rows(xbuf, slot * buf_rows, MOE_BLOCK).astype(BF16)
        a = jnp.dot(x, wgb[...], preferred_element_type=F32)
        b = jnp.dot(x, wub[...], preferred_element_type=F32)
        act = (a * _sigmoid(a) * b).astype(BF16)
        y = jnp.dot(act, wdb[...], preferred_element_type=F32)
        _pack_rows(ybuf, slot * buf_rows, y.astype(BF16).astype(F32))
        row_chunks(scatter_row, i, slot, wait=False)

    @pl.when(i == n_used - 1)
    def _():
        row_chunks(scatter_row, i, slot, wait=True)

    @pl.when((i == n_used - 1) & (i >= 1))
    def _():
        row_chunks(scatter_row, i - 1, 1 - slot, wait=True)


def _moe_experts(xn, sched, slot_asg, n_used, nblk, w_gate, w_up, w_down, layer):
    n_tok = xn.shape[0] // PACK_SLABS
    buf_rows = 2 * MOE_BLOCK * PACK_SLABS
    any_spec = pl.BlockSpec(memory_space=pl.ANY)
    grid_spec = pltpu.PrefetchScalarGridSpec(
        num_scalar_prefetch=3, grid=(nblk,),
        in_specs=[any_spec, any_spec, any_spec, any_spec], out_specs=any_spec,
        scratch_shapes=[pltpu.VMEM((buf_rows, LANE), jnp.uint32), pltpu.VMEM((buf_rows, LANE), jnp.uint32),
                        pltpu.VMEM((2, D_MODEL, MOE_FF), F32), pltpu.VMEM((2, D_MODEL, MOE_FF), F32),
                        pltpu.VMEM((2, MOE_FF, D_MODEL), F32),
                        pltpu.VMEM((D_MODEL, MOE_FF), BF16), pltpu.VMEM((D_MODEL, MOE_FF), BF16),
                        pltpu.VMEM((MOE_FF, D_MODEL), BF16),
                        pltpu.SemaphoreType.DMA((2,)), pltpu.SemaphoreType.DMA((2,)), pltpu.SemaphoreType.DMA((2,))])
    return pl.pallas_call(
        functools.partial(_moe_kernel, n_tok=n_tok, layer=layer), grid_spec=grid_spec,
        out_shape=jax.ShapeDtypeStruct(((MOE_TOPK * n_tok + 2 * MOE_BLOCK) * PACK_SLABS, LANE), jnp.uint32),
        compiler_params=_cparams(("arbitrary",), 56), name="moe_experts",
    )(sched, slot_asg, n_used, xn, w_gate, w_up, w_down)


def _ple_kernel(h1_ref, y0_ref, y1_ref, rg_ref, p_ref, gp_ref, wg_ref, wp_ref, gfin_ref, o_ref, *, final):
    rg = rg_ref[...]
    tm = rg.shape[0]
    ffn = rg[:, 0:1] * _unpack_rows(y0_ref, 0, tm) + rg[:, 1:2] * _unpack_rows(y1_ref, 0, tm)
    h2 = h1_ref[...] + ffn
    hn = _rms(h2, gp_ref[...]).astype(BF16)
    gate = _sigmoid(jnp.dot(hn, wg_ref[...], preferred_element_type=F32))
    pe = jnp.dot(p_ref[...].astype(BF16), wp_ref[...], preferred_element_type=F32)
    h3 = h2 + pe * gate
    o_ref[...] = _rms(h3, gfin_ref[...]) if final else h3


def _ple(h1, y2, rg, p, gp, wg, wp, gfin, final, tm=256):
    rows = h1.shape[0]
    n_t = rows // tm

    def rspec(n):
        return pl.BlockSpec((tm, n), lambda i: (i, 0))

    def yspec(k):
        return pl.BlockSpec((tm * PACK_SLABS, LANE), lambda i: (k * n_t + i, 0))

    return pl.pallas_call(
        functools.partial(_ple_kernel, final=final), grid=(n_t,),
        in_specs=[rspec(D_MODEL), yspec(0), yspec(1), rspec(ROUTE_W), rspec(PLE_DIM), _full(gp), _full(wg),
                  _full(wp), _full(gfin)],
        out_specs=rspec(D_MODEL), out_shape=jax.ShapeDtypeStruct((rows, D_MODEL), F32),
        compiler_params=_cparams(("parallel",), 48), name="moe_combine_ple",
    )(h1, y2, y2, rg, p, gp, wg, wp, gfin)


def _rope_tables(positions, rot_dim, lead, period, reps):
    half = rot_dim // 2
    inv_freq = ROPE_THETA ** (-jnp.arange(0, rot_dim, 2, dtype=F32) / rot_dim)
    ang = positions.astype(F32).reshape(-1, 1) * inv_freq
    cos, sin = jnp.cos(ang), jnp.sin(ang)
    n = cos.shape[0]
    tail = period - lead - rot_dim
    c = jnp.concatenate([jnp.ones((n, lead), F32), cos, cos, jnp.ones((n, tail), F32)], axis=1)
    s1 = jnp.concatenate([jnp.zeros((n, lead), F32), -sin, jnp.zeros((n, half + tail), F32)], axis=1)
    s2 = jnp.concatenate([jnp.zeros((n, lead + half), F32), sin, jnp.zeros((n, tail), F32)], axis=1)
    return tuple(jnp.tile(t, (1, reps)) for t in (c, s1, s2))


def _split_cols(w, sizes):
    out, o = [], 0
    for n in sizes:
        out.append(w[:, o:o + n])
        o += n
    return out


def _in_weights(w_in):
    cb, cc, ch, su, dq, dk, dv, cq, ckv, kr = _split_cols(
        w_in, (CONV_DIM, CONV_DIM, CONV_DIM, SSM_DIM, 3 * DIL_GW, 3 * DIL_GW, 3 * DIL_GW, MLA_Q_RANK, MLA_KV_RANK,
               MLA_ROPE))
    ws = [cb, cc, ch, jnp.pad(su, ((0, 0), (0, SSM_PAD - SSM_DIM)))]
    kinds = ["rows", "rows", "rows", "time_major"]
    for g in range(len(DIL_PATTERNS)):
        ws += [jnp.pad(m[:, g * DIL_GW:(g + 1) * DIL_GW], ((0, 0), (0, DIL_PLANES * LANE - DIL_GW)))
               for m in (dq, dk, dv)]
        kinds += ["planes_rope_q", "planes_rope_k", "planes"]
    ws += [cq, ckv, jnp.pad(kr, ((0, 0), (MLA_NOPE, MLA_HW - MLA_NOPE - MLA_ROPE)))]
    kinds += ["mla_cq", "mla_ckv", "mla_kr"]
    return [w.astype(BF16) for w in ws], kinds


def _mla_weights(w_uq, w_ukv):
    q = w_uq.reshape(MLA_Q_RANK, MLA_HEADS, MLA_QK)
    wq = jnp.pad(q, ((0, 0), (0, 0), (0, MLA_HW - MLA_QK))).reshape(MLA_Q_RANK, MLA_HEADS * MLA_HW)
    kv = w_ukv.reshape(MLA_KV_RANK, MLA_HEADS, MLA_NOPE + MLA_V)
    wk = jnp.pad(kv[:, :, :MLA_NOPE], ((0, 0), (0, 0), (0, MLA_HW - MLA_NOPE))).reshape(MLA_KV_RANK, -1)
    wv = jnp.pad(kv[:, :, MLA_NOPE:], ((0, 0), (0, 0), (0, MLA_HW - MLA_V))).reshape(MLA_KV_RANK, -1)
    return wq.astype(BF16), wk.astype(BF16), wv.astype(BF16)


def _out_weights(w_out):
    o_ssm, o_dil, o_mla = CONV_DIM, CONV_DIM + SSM_DIM, CONV_DIM + SSM_DIM + 3 * DIL_GW
    ws = [w_out[:o_ssm], jnp.pad(w_out[o_ssm:o_dil], ((0, SSM_PAD - SSM_DIM), (0, 0)))]
    ws += [jnp.concatenate([jnp.pad(w_out[o_dil + g * DIL_GW:o_dil + (g + 1) * DIL_GW],
                                    ((0, DIL_PLANES * LANE - DIL_GW), (0, 0))) for g in range(3)], axis=0)]
    ws += [jnp.pad(w_out[o_mla:], ((0, MLA_VW - MLA_HEADS * MLA_V), (0, 0)))]
    return [w.astype(BF16) for w in ws]


def _router_weights(w_group, b_group, w_router, b_router):
    w = jnp.pad(jnp.concatenate([w_group, w_router], axis=1), ((0, 0), (0, ROUTE_W - MOE_GROUPS - N_EXPERTS)))
    hi = w.astype(BF16)
    lo = (w - hi.astype(F32)).astype(BF16)
    b = jnp.pad(jnp.concatenate([b_group, b_router]), (0, ROUTE_W - MOE_GROUPS - N_EXPERTS)).reshape(1, ROUTE_W)
    return jnp.concatenate([hi, lo], axis=1), b


def kernel(x, p, positions, norm_mix_g, w_in, conv_w, ssm_lam_re, ssm_lam_im, ssm_log_dt, ssm_b_re, ssm_b_im, ssm_c_re, ssm_c_im, ssm_d, ssm_glu_w, ssm_glu_b, mla_q_norm_g, mla_w_uq, mla_kv_norm_g, mla_w_ukv, w_out, norm_ffn_g, w_group, b_group, w_router, b_router, moe_w_gate, moe_w_up, moe_w_down, norm_ple_g, ple_w_proj, ple_w_gate, final_norm_g):
    batch, seq, _ = x.shape
    n_tok = batch * seq
    depth = w_in.shape[0]
    dil_tabs = _rope_tables(positions, DIL_ROT, 0, DIL_HEAD_DIM, LANE // DIL_HEAD_DIM)
    mla_tabs = _rope_tables(positions, MLA_ROPE, MLA_NOPE, MLA_HW, 1)
    h = x.reshape(n_tok, D_MODEL)
    for i in range(depth):
        mla_prm = (mla_q_norm_g[i].reshape(1, -1), mla_kv_norm_g[i].reshape(1, -1),
                   *_mla_weights(mla_w_uq[i], mla_w_ukv[i]))
        z = _in_proj(h, norm_mix_g[i].reshape(1, -1), dil_tabs, mla_tabs, mla_prm, *_in_weights(w_in[i]), batch, seq)
        cb, cc, ch, su = z[:4]
        q, k, v = z[13:]
        y_conv = _conv_mixer(cb, cc, ch, conv_w[i], batch, seq)
        prm = _ssm_params(ssm_lam_re[i], ssm_lam_im[i], ssm_log_dt[i], ssm_b_re[i], ssm_b_im[i], ssm_c_re[i],
                          ssm_c_im[i], ssm_d[i], ssm_glu_w[i], ssm_glu_b[i])
        y_ssm = _ssm_mixer(su, prm, batch, seq)
        outs, lses = [], []
        for g, (window, dil) in enumerate(DIL_PATTERNS):
            assert window // dil == DIL_BLK and (seq // dil) % DIL_BLK == 0
            o, l = _dil_group(z[4 + 3 * g], z[5 + 3 * g], z[6 + 3 * g], dil, batch, seq)
            outs.append(o)
            lses.append(l)
        y_mla = _mla_attn(q, k, v, batch, seq)
        wr, br = _router_weights(w_group[i], b_group[i], w_router[i], b_router[i])
        h1, xn, route_i, route_g = _out_proj(
            (y_conv, y_ssm, outs, lses, y_mla), h, _out_weights(w_out[i]),
            norm_ffn_g[i].reshape(1, -1), wr, br, batch, seq)
        sched, slot_asg, n_used, nblk = _moe_plan(route_i, n_tok)
        y2 = _moe_experts(xn, sched, slot_asg, n_used, nblk, moe_w_gate, moe_w_up, moe_w_down, i)
        h = _ple(h1, y2, route_g, p[i].reshape(n_tok, PLE_DIM), norm_ple_g[i].reshape(1, -1),
                 ple_w_gate[i].astype(BF16), ple_w_proj[i].astype(BF16), final_norm_g.reshape(1, -1),
                 final=(i == depth - 1))
    return h.reshape(batch, seq, D_MODEL)
```

```python
import functools
import math

import jax
import jax.numpy as jnp
from jax import lax
from jax.experimental import pallas as pl
from jax.experimental.pallas import tpu as pltpu

D_MODEL = 2048
PLE_DIM = 256
ROPE_THETA = 500000.0
NORM_EPS = 1e-6

CONV_DIM = 448
CONV_K = 3

SSM_DIM = 448
SSM_GROUP = 16
SSM_GROUPS = SSM_DIM // SSM_GROUP
SSM_STATE = 64
SSM_N = SSM_GROUPS * SSM_STATE
SSM_PAD = 512

DIL_HEAD_DIM = 64
DIL_ROT = DIL_HEAD_DIM // 4
DIL_PATTERNS = ((128, 1), (512, 4), (2048, 16))
DIL_HPG = 3
DIL_GW = DIL_HPG * DIL_HEAD_DIM
DIL_PLANES = 2
DIL_BLK = 128
DIL_UNROLL = 4

MLA_HEADS = 9
MLA_Q_RANK = 384
MLA_KV_RANK = 256
MLA_NOPE = 64
MLA_ROPE = 32
MLA_V = 64
MLA_QK = MLA_NOPE + MLA_ROPE
MLA_HW = 128
MLA_VW = 640

MOE_GROUPS = 8
MOE_EPG = 8
N_EXPERTS = MOE_GROUPS * MOE_EPG
MOE_TOPK = 2
MOE_FF = 512
MOE_BLOCK = 128
ROUTE_W = 128

LANE = 128
PACK_SLABS = D_MODEL // 2 // LANE
ROW_CHUNK = 32
SLOT_SHIFT = 15
NEG_BIG = -1e30

BF16 = jnp.bfloat16
F32 = jnp.float32


def _cparams(sem, vmem_mb):
    return pltpu.CompilerParams(dimension_semantics=sem, vmem_limit_bytes=vmem_mb * 1024 * 1024)


def _rms(x, g):
    ms = jnp.mean(x * x, axis=-1, keepdims=True)
    return (x * lax.rsqrt(ms + NORM_EPS)) * g


def _sigmoid(x):
    return 1.0 / (1.0 + jnp.exp(-x))


def _full(a):
    return pl.BlockSpec(a.shape, lambda *_: (0,) * a.ndim)


def _param(a, layer):
    return pl.BlockSpec((None,) + a.shape[1:], lambda *_: (layer,) + (0,) * (a.ndim - 1),
                        pipeline_mode=pl.Buffered(1))


_HI16 = 0xFFFF0000


def _pack_rows(ref, base, x):
    n = x.shape[0]
    bits = lax.bitcast_convert_type(x, jnp.uint32)
    for c in range(PACK_SLABS):
        lo = bits[:, c * LANE:(c + 1) * LANE] >> 16
        hi = bits[:, D_MODEL // 2 + c * LANE:D_MODEL // 2 + (c + 1) * LANE] & jnp.uint32(_HI16)
        ref[pl.ds(base + c, n, stride=PACK_SLABS), :] = lo | hi


def _unpack_rows(ref, base, n):
    lo, hi = [], []
    for c in range(PACK_SLABS):
        w = ref[pl.ds(base + c, n, stride=PACK_SLABS), :]
        lo.append(lax.bitcast_convert_type(w << 16, F32))
        hi.append(lax.bitcast_convert_type(w & jnp.uint32(_HI16), F32))
    return jnp.concatenate(lo + hi, axis=1)


def _in_proj_kernel(x_ref, g_ref, dc_ref, ds1_ref, ds2_ref, mc_ref, ms1_ref, ms2_ref, gq_ref, gkv_ref,
                    wq_ref, wk_ref, wv_ref, *refs, kinds):
    n_out = len(kinds)
    w_refs, o_refs = refs[:n_out], refs[n_out:]
    xn = _rms(x_ref[...], g_ref[...]).astype(BF16)
    tabs = [jnp.concatenate([t[...]] * DIL_PLANES, axis=1) for t in (dc_ref, ds1_ref, ds2_ref)]
    latent = {}
    outs = iter(o_refs)
    for kind, w_ref in zip(kinds, w_refs):
        res = jnp.dot(xn, w_ref[...], preferred_element_type=F32)
        if kind.startswith("mla"):
            latent[kind] = res
            continue
        o_ref = next(outs)
        if kind.startswith("planes"):
            if kind != "planes":
                res = _rope_lanes(res, *tabs, DIL_ROT // 2, DIL_PLANES * LANE)
            if kind == "planes_rope_q":
                res = res * (1.0 / math.sqrt(DIL_HEAD_DIM))
            for j in range(o_ref.shape[0]):
                o_ref[j] = res[:, j * LANE:(j + 1) * LANE]
        else:
            o_ref[...] = res
    q_ref, k_ref, v_ref = outs
    rope = functools.partial(_rope_lanes, c=mc_ref[...], s1=ms1_ref[...], s2=ms2_ref[...], half=MLA_ROPE // 2,
                             width=MLA_HW)
    qn = _rms(latent["mla_cq"], gq_ref[...]).astype(BF16)
    kvn = _rms(latent["mla_ckv"], gkv_ref[...]).astype(BF16)
    q = jnp.dot(qn, wq_ref[...], preferred_element_type=F32)
    kn = jnp.dot(kvn, wk_ref[...], preferred_element_type=F32)
    lane = lax.broadcasted_iota(jnp.int32, (1, MLA_HEADS * MLA_HW), 1)
    ones_col = (lane % MLA_HW == MLA_V).astype(F32)
    v_ref[...] = (jnp.dot(kvn, wv_ref[...], preferred_element_type=F32) + ones_col).astype(BF16)
    kr = rope(latent["mla_kr"])
    scale = math.log2(math.e) / math.sqrt(MLA_QK)
    for h in range(MLA_HEADS):
        sl = slice(h * MLA_HW, (h + 1) * MLA_HW)
        q_ref[:, sl] = (rope(q[:, sl]) * scale).astype(BF16)
        k_ref[:, sl] = (kn[:, sl] + kr).astype(BF16)


def _in_proj(h, g, dil_tabs, mla_tabs, mla_prm, weights, kinds, batch, seq, layer, tm=256):
    n_s = seq // tm

    def rspec(n):
        return pl.BlockSpec((tm, n), lambda b, i: (b * n_s + i, 0))

    in_specs = [rspec(D_MODEL), _param(g, layer)] + [rspec(LANE)] * 6 + [_param(a, layer) for a in mla_prm]
    in_specs += [_param(w, layer) for w in weights]
    out_shape, out_specs = [], []
    for kind, w in zip(kinds, weights):
        n = w.shape[-1]
        if kind.startswith("mla"):
            continue
        if kind == "time_major":
            out_shape.append(jax.ShapeDtypeStruct((seq, batch * n), F32))
            out_specs.append(pl.BlockSpec((tm, n), lambda b, i: (i, b)))
        elif kind.startswith("planes"):
            out_shape.append(jax.ShapeDtypeStruct((n // LANE, batch * seq, LANE), F32))
            out_specs.append(pl.BlockSpec((n // LANE, tm, LANE), lambda b, i: (0, b * n_s + i, 0)))
        else:
            out_shape.append(jax.ShapeDtypeStruct((batch * seq, n), F32))
            out_specs.append(rspec(n))
    for n in (MLA_HEADS * MLA_HW,) * 3:
        out_shape.append(jax.ShapeDtypeStruct((batch * seq, n), BF16))
        out_specs.append(rspec(n))
    return pl.pallas_call(
        functools.partial(_in_proj_kernel, kinds=tuple(kinds)),
        grid=(batch, n_s), in_specs=in_specs, out_specs=out_specs, out_shape=out_shape,
        compiler_params=_cparams(("parallel", "parallel"), 56), name="in_proj",
    )(h, g, *dil_tabs, *mla_tabs, *mla_prm, *weights)


def _conv_kernel(cb_ref, cc_ref, ch_ref, w_ref, o_ref, u_ref, *, rows):
    seq = cb_ref.shape[0]
    u_ref[0:8, :] = jnp.zeros((8, CONV_DIM), F32)
    for r0 in range(0, seq, rows):
        u_ref[8 + r0:8 + r0 + rows, :] = cc_ref[r0:r0 + rows, :] * ch_ref[r0:r0 + rows, :]
    w0, w1, w2 = w_ref[0:1, :], w_ref[1:2, :], w_ref[2:3, :]
    for r0 in range(0, seq, rows):
        acc = (w2 * u_ref[8 + r0:8 + r0 + rows, :] + w1 * u_ref[7 + r0:7 + r0 + rows, :]
               + w0 * u_ref[6 + r0:6 + r0 + rows, :])
        o_ref[r0:r0 + rows, :] = cb_ref[r0:r0 + rows, :] * acc


def _conv_mixer(cb, cc, ch, conv_w, batch, seq, layer):
    spec = pl.BlockSpec((seq, CONV_DIM), lambda b: (b, 0))
    return pl.pallas_call(
        functools.partial(_conv_kernel, rows=256),
        grid=(batch,), in_specs=[spec, spec, spec, _param(conv_w, layer)], out_specs=spec,
        out_shape=jax.ShapeDtypeStruct((batch * seq, CONV_DIM), F32),
        scratch_shapes=[pltpu.VMEM((seq + 8, CONV_DIM), F32)],
        compiler_params=_cparams(("parallel",), 48), name="conv_mixer",
    )(cb, cc, ch, conv_w)


SSM_NP = SSM_N // LANE
SSM_SCAN_PLANES = 4
_SSM_SPLIT = ((0, 256, 0, 1024), (256, 256, 1024, 768))


def _gelu_tanh(x):
    return 0.5 * x * (1.0 + jnp.tanh(math.sqrt(2.0 / math.pi) * (x + 0.044715 * (x * x * x))))


def _ssm_kernel(u_ref, bf0_ref, bf1_ref, are_ref, aim_ref, cc0_ref, cc1_ref, d_ref, gw_ref, gb_ref, o_ref,
                xs_ref, st_ref, *, batch, tc):
    @pl.when(pl.program_id(0) == 0)
    def _():
        st_ref[...] = jnp.zeros_like(st_ref)

    def batch_rows(b):
        return pl.ds(b, tc, stride=batch)

    for b in range(batch):
        for (l0, lw, s0, sw), bf_ref in zip(_SSM_SPLIT, (bf0_ref, bf1_ref)):
            ub = u_ref[:, b * SSM_PAD + l0:b * SSM_PAD + l0 + lw].astype(BF16)
            r = jnp.dot(ub, bf_ref[...], preferred_element_type=F32)
            for q in range(sw // LANE):
                xs_ref[s0 // LANE + q, batch_rows(b), :] = r[:, q * LANE:(q + 1) * LANE]
                xs_ref[SSM_NP + s0 // LANE + q, batch_rows(b), :] = r[:, sw + q * LANE:sw + (q + 1) * LANE]

    per_tile = 8 // batch
    for p0 in range(0, SSM_NP, SSM_SCAN_PLANES):
        ps = list(range(p0, min(p0 + SSM_SCAN_PLANES, SSM_NP)))
        ar = [jnp.broadcast_to(are_ref[:, p * LANE:(p + 1) * LANE], (batch, LANE)) for p in ps]
        ai = [jnp.broadcast_to(aim_ref[:, p * LANE:(p + 1) * LANE], (batch, LANE)) for p in ps]

        def body(k, carry, ps=ps, ar=ar, ai=ai):
            rows = pl.ds(pl.multiple_of(k * 8, 8), 8)
            new = []
            for idx, p in enumerate(ps):
                sr, si = carry[idx]
                tile_r, tile_i = xs_ref[p, rows, :], xs_ref[SSM_NP + p, rows, :]
                out_r, out_i = [], []
                for j in range(per_tile):
                    step = slice(j * batch, (j + 1) * batch)
                    sr, si = (ar[idx] * sr - ai[idx] * si + tile_r[step], ar[idx] * si + ai[idx] * sr + tile_i[step])
                    out_r.append(sr)
                    out_i.append(si)
                xs_ref[p, rows, :] = jnp.concatenate(out_r, axis=0)
                xs_ref[SSM_NP + p, rows, :] = jnp.concatenate(out_i, axis=0)
                new.append((sr, si))
            return tuple(new)

        init = tuple((st_ref[p, 0:batch, :], st_ref[SSM_NP + p, 0:batch, :]) for p in ps)
        for p, (sr, si) in zip(ps, lax.fori_loop(0, tc // per_tile, body, init)):
            st_ref[p, 0:batch, :] = sr
            st_ref[SSM_NP + p, 0:batch, :] = si

    for b in range(batch):
        ys = []
        for (l0, lw, s0, sw), cc_ref in zip(_SSM_SPLIT, (cc0_ref, cc1_ref)):
            planes = list(range(s0 // LANE, (s0 + sw) // LANE))
            xb = jnp.concatenate([xs_ref[p, batch_rows(b), :] for p in planes]
                                 + [xs_ref[SSM_NP + p, batch_rows(b), :] for p in planes], axis=1)
            ys.append(jnp.dot(xb.astype(BF16), cc_ref[...], preferred_element_type=F32))
        y = jnp.concatenate(ys, axis=1) + d_ref[...] * u_ref[:, b * SSM_PAD:(b + 1) * SSM_PAD]
        g = _gelu_tanh(y)
        o_ref[:, b * SSM_PAD:(b + 1) * SSM_PAD] = g * _sigmoid(
            jnp.dot(g.astype(BF16), gw_ref[...], preferred_element_type=F32) + gb_ref[...])


def _ssm_mixer(su, prm, batch, seq, layer, tc=256):
    spec = pl.BlockSpec((tc, batch * SSM_PAD), lambda c: (c, 0))
    args = (prm["bf0"], prm["bf1"], prm["a_re"], prm["a_im"], prm["cc0"], prm["cc1"], prm["d"], prm["gw"], prm["gb"])
    return pl.pallas_call(
        functools.partial(_ssm_kernel, batch=batch, tc=tc),
        grid=(seq // tc,), in_specs=[spec] + [_param(a, layer) for a in args], out_specs=spec,
        out_shape=jax.ShapeDtypeStruct((seq, batch * SSM_PAD), F32),
        scratch_shapes=[pltpu.VMEM((2 * SSM_NP, tc * batch, LANE), F32), pltpu.VMEM((2 * SSM_NP, 8, LANE), F32)],
        compiler_params=_cparams(("arbitrary",), 48), name="ssm_mixer",
    )(su, *args)


def _ssm_params(lam_re, lam_im, log_dt, b_re, b_im, c_re, c_im, d_skip, glu_w, glu_b):
    depth = lam_re.shape[0]
    dt = jnp.exp(log_dt)[..., None]
    mag = jnp.exp(lam_re * dt)
    a_re = mag * jnp.cos(lam_im * dt)
    a_im = mag * jnp.sin(lam_im * dt)
    nr, ni = a_re - 1.0, a_im
    den = lam_re * lam_re + lam_im * lam_im
    f_re = (nr * lam_re + ni * lam_im) / den
    f_im = (ni * lam_re - nr * lam_im) / den
    bfr = f_re[..., None] * b_re - f_im[..., None] * b_im
    bfi = f_re[..., None] * b_im + f_im[..., None] * b_re
    same_group = (jnp.arange(SSM_DIM)[:, None] // SSM_GROUP == jnp.arange(SSM_N)[None, :] // SSM_STATE).astype(F32)

    def in_blockdiag(m):
        cols = jnp.transpose(m, (0, 3, 1, 2)).reshape(depth, SSM_GROUP, SSM_N)
        return jnp.tile(cols, (1, SSM_GROUPS, 1)) * same_group

    def out_blockdiag(m):
        rows = jnp.transpose(m, (0, 1, 3, 2)).reshape(depth, SSM_N, SSM_GROUP)
        return jnp.tile(rows, (1, 1, SSM_GROUPS)) * same_group.T

    pad = SSM_PAD - SSM_DIM
    b_re_d, b_im_d = (_pad_rows(in_blockdiag(m), pad) for m in (bfr, bfi))
    c_re_d, c_im_d = (_pad_last(out_blockdiag(m), 0, pad) for m in (c_re, -c_im))
    prm = {}
    for c, (l0, lw, s0, sw) in enumerate(_SSM_SPLIT):
        prm[f"bf{c}"] = jnp.concatenate([b_re_d[:, l0:l0 + lw, s0:s0 + sw], b_im_d[:, l0:l0 + lw, s0:s0 + sw]],
                                        axis=2).astype(BF16)
        prm[f"cc{c}"] = jnp.concatenate([c_re_d[:, s0:s0 + sw, l0:l0 + lw], c_im_d[:, s0:s0 + sw, l0:l0 + lw]],
                                        axis=1).astype(BF16)
    return {
        **prm,
        "a_re": a_re.reshape(depth, 1, SSM_N), "a_im": a_im.reshape(depth, 1, SSM_N),
        "d": _pad_last(d_skip, 0, pad)[:, None, :],
        "gw": _pad_last(_pad_rows(glu_w, pad), 0, pad).astype(BF16),
        "gb": _pad_last(glu_b, 0, pad)[:, None, :],
    }


def _rope_lanes(x, c, s1, s2, half, width):
    return x * c + pltpu.roll(x, width - half, 1) * s1 + pltpu.roll(x, half, 1) * s2


def _dil_kernel(q_ref, k_ref, v_ref, o_ref, l_ref, *, dil):
    seq = q_ref.shape[1]
    nb = (seq // dil) // DIL_BLK
    width = DIL_PLANES * LANE
    lane = lax.broadcasted_iota(jnp.int32, (1, width), 1)
    cmask = [(lane // DIL_HEAD_DIM == c).astype(F32) for c in range(DIL_HPG)]
    qi = lax.broadcasted_iota(jnp.int32, (DIL_HPG * DIL_BLK, 2 * DIL_BLK), 0) % DIL_BLK
    ki = lax.broadcasted_iota(jnp.int32, (DIL_HPG * DIL_BLK, 2 * DIL_BLK), 1)
    band = (ki >= qi) & (ki <= qi + DIL_BLK)

    def planes(ref, rows):
        return jnp.concatenate([ref[j, rows, :] for j in range(DIL_PLANES)], axis=1)

    def rows_of(m, blk):
        if dil == 1:
            return pl.ds(pl.multiple_of(blk * DIL_BLK, DIL_BLK), DIL_BLK)
        return pl.ds(blk * DIL_BLK * dil + m, DIL_BLK, stride=dil)

    def blocks(it, carry):
        ids = [it * DIL_UNROLL + u for u in range(DIL_UNROLL)]
        mis = [(idx // nb, idx % nb) for idx in ids]
        rows_q = [rows_of(m, i) for m, i in mis]
        rows_p = [rows_of(m, jnp.maximum(i - 1, 0)) for m, i in mis]
        scores, vcats = [], []
        for rq, rp in zip(rows_q, rows_p):
            q = planes(q_ref, rq)
            kcat = jnp.concatenate([planes(k_ref, rp), planes(k_ref, rq)], axis=0).astype(BF16)
            vcats.append(jnp.concatenate([planes(v_ref, rp), planes(v_ref, rq)], axis=0).astype(BF16))
            qs = jnp.concatenate([q * cm for cm in cmask], axis=0).astype(BF16)
            scores.append(lax.dot_general(qs, kcat, (((1,), (1,)), ((), ())), preferred_element_type=F32))
        soft = []
        for s, (m, i) in zip(scores, mis):
            s = jnp.where(band & ((ki >= DIL_BLK) | (i > 0)), s, NEG_BIG)
            mx = jnp.max(s, axis=-1, keepdims=True)
            p = jnp.exp(s - mx)
            den = jnp.sum(p, axis=-1, keepdims=True)
            soft.append((p.astype(BF16), den, mx + jnp.log(den)))
        for (p, den, lse), vcat, rq in zip(soft, vcats, rows_q):
            o = jnp.dot(p, vcat, preferred_element_type=F32) / den
            out = jnp.zeros((DIL_BLK, width), F32)
            lout = jnp.zeros((DIL_BLK, width), F32)
            for c, cm in enumerate(cmask):
                out = out + cm * o[c * DIL_BLK:(c + 1) * DIL_BLK, :]
                lout = lout + cm * lse[c * DIL_BLK:(c + 1) * DIL_BLK, :]
            for j in range(DIL_PLANES):
                o_ref[j, rq, :] = out[:, j * LANE:(j + 1) * LANE]
                l_ref[j, rq, :] = lout[:, j * LANE:(j + 1) * LANE]
        return carry

    lax.fori_loop(0, dil * nb // DIL_UNROLL, blocks, 0)


def _dil_group(q, k, v, dil, batch, seq):
    spec = pl.BlockSpec((DIL_PLANES, seq, LANE), lambda b: (0, b, 0))
    sds = jax.ShapeDtypeStruct((DIL_PLANES, batch * seq, LANE), F32)
    return pl.pallas_call(
        functools.partial(_dil_kernel, dil=dil), grid=(batch,), in_specs=[spec] * 3,
        out_specs=[spec, spec], out_shape=[sds, sds],
        compiler_params=_cparams(("parallel",), 48), name=f"dil_attn_d{dil}",
    )(q, k, v)


def _mla_attn_kernel(q_ref, k_ref, v_ref, o_ref, *, tq, group):
    qi = pl.program_id(1)
    row = lax.broadcasted_iota(jnp.int32, (tq, tq), 0)
    col = lax.broadcasted_iota(jnp.int32, (tq, tq), 1)
    causal = col <= row

    def heads_out(heads):
        qs = [q_ref[:, h * MLA_HW:(h + 1) * MLA_HW] for h in heads]

        def step(kj, carry, masked):
            rows = pl.ds(pl.multiple_of(kj * tq, tq), tq)
            scores = [lax.dot_general(q, k_ref[rows, h * MLA_HW:(h + 1) * MLA_HW], (((1,), (1,)), ((), ())),
                                      preferred_element_type=F32) for h, q in zip(heads, qs)]
            stats = []
            for s, (m, acc) in zip(scores, carry):
                if masked:
                    s = jnp.where(causal, s, NEG_BIG)
                mn = jnp.maximum(m, jnp.max(s, axis=-1, keepdims=True))
                stats.append((mn, jnp.exp2(m - mn), jnp.exp2(s - mn).astype(BF16)))
            out = []
            for h, (mn, alpha, p), (_, acc) in zip(heads, stats, carry):
                v = v_ref[rows, h * MLA_HW:(h + 1) * MLA_HW]
                out.append((mn, alpha * acc + jnp.dot(p, v, preferred_element_type=F32)))
            return tuple(out)

        init = tuple((jnp.full((tq, 1), NEG_BIG, F32), jnp.zeros((tq, MLA_HW), F32)) for _ in heads)
        carry = lax.fori_loop(0, qi, functools.partial(step, masked=False), init)
        return [acc / acc[:, MLA_V:MLA_V + 1] for _, acc in step(qi, carry, True)]

    for h0 in range(0, MLA_HEADS, group):
        heads = list(range(h0, min(h0 + group, MLA_HEADS)))
        for h, o in zip(heads, heads_out(heads)):
            lo = h * MLA_V
            if h % 2:
                o_ref[:, lo:lo + MLA_V] = pltpu.roll(o, MLA_V, 1)[:, MLA_V:2 * MLA_V]
            else:
                o_ref[:, lo:lo + MLA_V] = o[:, 0:MLA_V]
    if MLA_HEADS % 2:
        o_ref[:, MLA_HEADS * MLA_V:] = jnp.zeros((tq, MLA_VW - MLA_HEADS * MLA_V), F32)


def _mla_attn(q, k, v, batch, seq, tq=256, group=5):
    hw = MLA_HEADS * MLA_HW
    n_q = seq // tq
    return pl.pallas_call(
        functools.partial(_mla_attn_kernel, tq=tq, group=group), grid=(batch, n_q),
        in_specs=[pl.BlockSpec((tq, hw), lambda b, i: (b * n_q + i, 0)),
                  pl.BlockSpec((seq, hw), lambda b, i: (b, 0)),
                  pl.BlockSpec((seq, hw), lambda b, i: (b, 0))],
        out_specs=pl.BlockSpec((tq, MLA_VW), lambda b, i: (b * n_q + i, 0)),
        out_shape=jax.ShapeDtypeStruct((batch * seq, MLA_VW), F32),
        compiler_params=_cparams(("parallel", "arbitrary"), 48), name="mla_attn",
    )(q, k, v)


def _out_proj_kernel(yc_ref, ys_ref, od0_ref, od1_ref, od2_ref, ld0_ref, ld1_ref, ld2_ref, ym_ref, h_ref,
                     wc_ref, ws_ref, wd_ref, wm_ref,
                     gf_ref, wr_ref, br_ref,
                     h1_ref, xn_ref, ri_ref, rg_ref):
    def mm(y, w_ref):
        return jnp.dot(y.astype(BF16), w_ref[...], preferred_element_type=F32)

    y_dil = [None] * (3 * DIL_PLANES)
    for j in range(DIL_PLANES):
        la, lb, lc = ld0_ref[j], ld1_ref[j], ld2_ref[j]
        mx = jnp.maximum(jnp.maximum(la, lb), lc)
        ea, eb, ec = jnp.exp(la - mx), jnp.exp(lb - mx), jnp.exp(lc - mx)
        inv = 1.0 / (ea + eb + ec)
        for g, (o_ref, e) in enumerate(((od0_ref, ea), (od1_ref, eb), (od2_ref, ec))):
            y_dil[g * DIL_PLANES + j] = o_ref[j] * (e * inv)
    mix = (mm(yc_ref[...], wc_ref) + mm(ys_ref[...], ws_ref) + mm(jnp.concatenate(y_dil, axis=1), wd_ref)
           + mm(ym_ref[...], wm_ref))
    h1 = h_ref[...] + mix
    h1_ref[...] = h1
    hn = _rms(h1, gf_ref[...])
    hi = hn.astype(BF16)
    hi32 = hi.astype(F32)
    lo = (hn - hi32).astype(BF16)
    _pack_rows(xn_ref, 0, hi32)
    r_hi = jnp.dot(hi, wr_ref[...], preferred_element_type=F32)
    r_lo = jnp.dot(lo, wr_ref[:, 0:ROUTE_W], preferred_element_type=F32)
    logits = r_hi[:, 0:ROUTE_W] + r_hi[:, ROUTE_W:2 * ROUTE_W] + r_lo + br_ref[...]
    lane = lax.broadcasted_iota(jnp.int32, logits.shape, 1)
    gl = jnp.where(lane < MOE_GROUPS, logits, NEG_BIG)
    gmax = jnp.max(gl, axis=-1, keepdims=True)
    g_top_p = 1.0 / jnp.sum(jnp.exp(gl - gmax), axis=-1, keepdims=True)
    g_idx = jnp.min(jnp.where(gl == gmax, lane, ROUTE_W), axis=-1, keepdims=True)
    in_group = (lane >= MOE_GROUPS) & (lane < MOE_GROUPS + N_EXPERTS) & (((lane - MOE_GROUPS) >> 3) == g_idx)
    el = jnp.where(in_group, logits, NEG_BIG)
    v1 = jnp.max(el, axis=-1, keepdims=True)
    i1 = jnp.min(jnp.where(el == v1, lane, ROUTE_W), axis=-1, keepdims=True)
    el2 = jnp.where(lane == i1, NEG_BIG, el)
    v2 = jnp.max(el2, axis=-1, keepdims=True)
    i2 = jnp.min(jnp.where(el2 == v2, lane, ROUTE_W), axis=-1, keepdims=True)
    e2 = jnp.exp(v2 - v1)
    w1 = g_top_p / (1.0 + e2)
    w2 = g_top_p * e2 / (1.0 + e2)
    ri_ref[...] = jnp.where(lane == 0, i1 - MOE_GROUPS, jnp.where(lane == 1, i2 - MOE_GROUPS, 0))
    rg_ref[...] = jnp.where(lane == 0, w1, jnp.where(lane == 1, w2, 0.0))


def _out_proj(ys, h, ws, gf, wr, br, batch, seq, layer, tm=256):
    rows = batch * seq
    n_s = seq // tm
    yc, ys_tm, dil_o, dil_l, ym = ys

    def rspec(n):
        return pl.BlockSpec((tm, n), lambda b, i: (b * n_s + i, 0))

    pspec = pl.BlockSpec((DIL_PLANES, tm, LANE), lambda b, i: (0, b * n_s + i, 0))
    in_specs = [rspec(CONV_DIM), pl.BlockSpec((tm, SSM_PAD), lambda b, i: (i, b))] + [pspec] * 6
    in_specs += [rspec(MLA_VW), rspec(D_MODEL)]
    in_specs += [_param(a, layer) for a in (*ws, gf, wr, br)]
    return pl.pallas_call(
        _out_proj_kernel, grid=(batch, n_s), in_specs=in_specs,
        out_specs=[rspec(D_MODEL), pl.BlockSpec((tm * PACK_SLABS, LANE), lambda b, i: (b * n_s + i, 0)),
                   rspec(ROUTE_W), rspec(ROUTE_W)],
        out_shape=[jax.ShapeDtypeStruct((rows, D_MODEL), F32),
                   jax.ShapeDtypeStruct((rows * PACK_SLABS, LANE), jnp.uint32),
                   jax.ShapeDtypeStruct((rows, ROUTE_W), jnp.int32), jax.ShapeDtypeStruct((rows, ROUTE_W), F32)],
        compiler_params=_cparams(("parallel", "parallel"), 48), name="out_proj_router",
    )(yc, ys_tm, *dil_o, *dil_l, ym, h, *ws, gf, wr, br)


def _moe_plan(route_i, n_tok):
    n_asg = n_tok * MOE_TOPK
    nblk = (n_asg + N_EXPERTS * (MOE_BLOCK - 1) + MOE_BLOCK - 1) // MOE_BLOCK
    flat_e = route_i[:, :MOE_TOPK].reshape(-1)
    onehot = (flat_e[:, None] == jnp.arange(N_EXPERTS, dtype=jnp.int32)[None, :]).astype(F32)
    oh3 = onehot.reshape(n_asg // MOE_BLOCK, MOE_BLOCK, N_EXPERTS)
    within = jnp.einsum("ij,bje->bie", jnp.tril(jnp.ones((MOE_BLOCK, MOE_BLOCK), F32)), oh3)
    chunk_tot = within[:, -1, :]
    chunk_off = jnp.cumsum(chunk_tot, axis=0) - chunk_tot
    rank = jnp.sum((within + chunk_off[:, None, :]) * oh3, axis=-1).reshape(n_asg).astype(jnp.int32) - 1
    counts = (chunk_off[-1] + chunk_tot[-1]).astype(jnp.int32)
    padded = ((counts + MOE_BLOCK - 1) // MOE_BLOCK) * MOE_BLOCK
    pend = jnp.cumsum(padded)
    first_blk = ((pend - padded) // MOE_BLOCK).astype(F32)
    dest = jnp.dot(onehot, first_blk).astype(jnp.int32) * MOE_BLOCK + rank
    in_blk = (dest[:, None] // MOE_BLOCK == jnp.arange(nblk, dtype=jnp.int32)[None, :]).astype(F32)
    at_off = (dest[:, None] % MOE_BLOCK == jnp.arange(MOE_BLOCK, dtype=jnp.int32)[None, :]).astype(F32)
    asg = jnp.arange(n_asg, dtype=jnp.int32)
    a_hi, a_lo = (asg // MOE_BLOCK).astype(F32)[:, None], (asg % MOE_BLOCK).astype(F32)[:, None]
    inv = lambda v: jnp.einsum("ab,ao->bo", in_blk * v, at_off)
    slot_a = jnp.where(inv(1.0) > 0.5, inv(a_hi) * MOE_BLOCK + inv(a_lo), -1.0).astype(jnp.int32).reshape(-1)
    slot = jnp.arange(nblk * MOE_BLOCK, dtype=jnp.int32)
    dump = MOE_TOPK * n_tok + ((slot // MOE_BLOCK) % 2) * MOE_BLOCK + slot % MOE_BLOCK
    assert MOE_TOPK * n_tok + 2 * MOE_BLOCK <= 1 << SLOT_SHIFT
    slot_asg = jnp.where(slot_a >= 0, ((slot_a >> 1) << SLOT_SHIFT) | ((slot_a & 1) * n_tok + (slot_a >> 1)), dump)
    block_expert = jnp.minimum(
        jnp.searchsorted(pend, jnp.arange(nblk, dtype=jnp.int32) * MOE_BLOCK, side="right"), N_EXPERTS - 1
    ).astype(jnp.int32)
    n_used = (pend[-1] // MOE_BLOCK).astype(jnp.int32)
    prev = jnp.concatenate([jnp.full((1,), -1, jnp.int32), block_expert[:-1]])
    is_first = (block_expert != prev).astype(jnp.int32)
    w_slot = (jnp.cumsum(is_first) - 1) % 2
    run_end = (pend // MOE_BLOCK)[block_expert]
    nxt = jnp.where(run_end < n_used, block_expert[jnp.minimum(run_end, nblk - 1)], -1)
    blk = jnp.arange(nblk, dtype=jnp.int32)
    run_start = ((pend - padded) // MOE_BLOCK)[block_expert]
    rows_left = jnp.clip(counts[block_expert] - (blk - run_start) * MOE_BLOCK, 0, MOE_BLOCK)
    sched = jnp.stack([block_expert, is_first, w_slot, nxt, jnp.where(blk < n_used, rows_left, 0)])
    sched = jnp.pad(sched.astype(jnp.int32), ((0, 0), (0, 1)))
    return sched, slot_asg, n_used.reshape(1), nblk


def _moe_kernel(sched_ref, asg_ref, nu_ref, x_hbm, wg_hbm, wu_hbm, wd_hbm, y_hbm,
                xbuf, ybuf, wgf, wuf, wdf, wgb, wub, wdb, gsem, ssem, wsem, *, n_tok, layer):
    i = pl.program_id(0)
    n_used = nu_ref[0]
    slot = i % 2
    buf_rows = MOE_BLOCK * PACK_SLABS
    active = i < n_used
    first = active & (sched_ref[1, i] == 1)
    w_slot = sched_ref[2, i]

    def weights(e, ws):
        return [pltpu.make_async_copy(w_hbm.at[layer, e], buf.at[ws], wsem.at[ws])
                for w_hbm, buf in ((wg_hbm, wgf), (wu_hbm, wuf), (wd_hbm, wdf))]

    def slab_rows(row):
        return pl.ds(pl.multiple_of(row * PACK_SLABS, PACK_SLABS), PACK_SLABS)

    def gather_row(blk, sl, r):
        tok = asg_ref[blk * MOE_BLOCK + r] >> SLOT_SHIFT
        return pltpu.make_async_copy(x_hbm.at[slab_rows(tok), :], xbuf.at[slab_rows(sl * MOE_BLOCK + r), :],
                                     gsem.at[sl])

    def scatter_row(blk, sl, r):
        dst = asg_ref[blk * MOE_BLOCK + r] & ((1 << SLOT_SHIFT) - 1)
        return pltpu.make_async_copy(ybuf.at[slab_rows(sl * MOE_BLOCK + r), :], y_hbm.at[slab_rows(dst), :],
                                     ssem.at[sl])

    def row_chunks(make, blk, sl, wait):
        n_valid = sched_ref[4, blk]
        for c0 in range(0, MOE_BLOCK, ROW_CHUNK):
            @pl.when(c0 < n_valid)
            def _(c0=c0):
                for r in range(c0, c0 + ROW_CHUNK):
                    if wait:
                        make(blk, sl, r).wait()
                    else:
                        make(blk, sl, r).start()

    @pl.when(i == 0)
    def _():
        xbuf[...] = jnp.zeros_like(xbuf)
        ybuf[...] = jnp.zeros_like(ybuf)
        dump = pltpu.make_async_copy(ybuf, y_hbm.at[pl.ds(MOE_TOPK * n_tok * PACK_SLABS, 2 * buf_rows), :], ssem.at[0])
        dump.start()
        dump.wait()

    @pl.when((i == 0) & (n_used > 0))
    def _():
        for cp in weights(sched_ref[0, 0], 0):
            cp.start()
        row_chunks(gather_row, 0, 0, wait=False)

    @pl.when(first)
    def _():
        for cp in weights(sched_ref[0, i], w_slot):
            cp.wait()

    @pl.when(first & (sched_ref[3, i] >= 0))
    def _():
        for cp in weights(sched_ref[3, i], 1 - w_slot):
            cp.start(priority=1)

    @pl.when(first)
    def _():
        wgb[...] = wgf[w_slot].astype(BF16)
        wub[...] = wuf[w_slot].astype(BF16)
        wdb[...] = wdf[w_slot].astype(BF16)

    @pl.when(active)
    def _():
        row_chunks(gather_row, i, slot, wait=True)
        row_chunks(gather_row, i + 1, 1 - slot, wait=False)

    @pl.when(active & (i >= 2))
    def _():
        row_chunks(scatter_row, i - 2, slot, wait=True)

    @pl.when(active)
    def _():
        x = _unpack_rows(xbuf, slot * buf_rows, MOE_BLOCK).astype(BF16)
        a = jnp.dot(x, wgb[...], preferred_element_type=F32)
        b = jnp.dot(x, wub[...], preferred_element_type=F32)
        act = (a * _sigmoid(a) * b).astype(BF16)
        y = jnp.dot(act, wdb[...], preferred_element_type=F32)
        _pack_rows(ybuf, slot * buf_rows, y.astype(BF16).astype(F32))
        row_chunks(scatter_row, i, slot, wait=False)

    @pl.when(i == n_used - 1)
    def _():
        row_chunks(scatter_row, i, slot, wait=True)

    @pl.when((i == n_used - 1) & (i >= 1))
    def _():
        row_chunks(scatter_row, i - 1, 1 - slot, wait=True)


def _moe_experts(xn, sched, slot_asg, n_used, nblk, w_gate, w_up, w_down, layer):
    n_tok = xn.shape[0] // PACK_SLABS
    buf_rows = 2 * MOE_BLOCK * PACK_SLABS
    any_spec = pl.BlockSpec(memory_space=pl.ANY)
    grid_spec = pltpu.PrefetchScalarGridSpec(
        num_scalar_prefetch=3, grid=(nblk,),
        in_specs=[any_spec, any_spec, any_spec, any_spec], out_specs=any_spec,
        scratch_shapes=[pltpu.VMEM((buf_rows, LANE), jnp.uint32), pltpu.VMEM((buf_rows, LANE), jnp.uint32),
                        pltpu.VMEM((2, D_MODEL, MOE_FF), F32), pltpu.VMEM((2, D_MODEL, MOE_FF), F32),
                        pltpu.VMEM((2, MOE_FF, D_MODEL), F32),
                        pltpu.VMEM((D_MODEL, MOE_FF), BF16), pltpu.VMEM((D_MODEL, MOE_FF), BF16),
                        pltpu.VMEM((MOE_FF, D_MODEL), BF16),
                        pltpu.SemaphoreType.DMA((2,)), pltpu.SemaphoreType.DMA((2,)), pltpu.SemaphoreType.DMA((2,))])
    return pl.pallas_call(
        functools.partial(_moe_kernel, n_tok=n_tok, layer=layer), grid_spec=grid_spec,
        out_shape=jax.ShapeDtypeStruct(((MOE_TOPK * n_tok + 2 * MOE_BLOCK) * PACK_SLABS, LANE), jnp.uint32),
        compiler_params=_cparams(("arbitrary",), 56), name="moe_experts",
    )(sched, slot_asg, n_used, xn, w_gate, w_up, w_down)


def _ple_kernel(h1_ref, y0_ref, y1_ref, rg_ref, p_ref, gp_ref, wg_ref, wp_ref, gfin_ref, o_ref, *, final):
    rg = rg_ref[...]
    tm = rg.shape[0]
    ffn = rg[:, 0:1] * _unpack_rows(y0_ref, 0, tm) + rg[:, 1:2] * _unpack_rows(y1_ref, 0, tm)
    h2 = h1_ref[...] + ffn
    hn = _rms(h2, gp_ref[...]).astype(BF16)
    gate = _sigmoid(jnp.dot(hn, wg_ref[...], preferred_element_type=F32))
    pe = jnp.dot(p_ref[...].astype(BF16), wp_ref[...], preferred_element_type=F32)
    h3 = h2 + pe * gate
    o_ref[...] = _rms(h3, gfin_ref[...]) if final else h3


def _ple(h1, y2, rg, p, gp, wg, wp, gfin, final, layer, tm=256):
    rows = h1.shape[0]
    n_t = rows // tm

    def rspec(n):
        return pl.BlockSpec((tm, n), lambda i: (i, 0))

    def yspec(k):
        return pl.BlockSpec((tm * PACK_SLABS, LANE), lambda i: (k * n_t + i, 0))

    return pl.pallas_call(
        functools.partial(_ple_kernel, final=final), grid=(n_t,),
        in_specs=[rspec(D_MODEL), yspec(0), yspec(1), rspec(ROUTE_W),
                  pl.BlockSpec((None, tm, PLE_DIM), lambda i: (layer, i, 0)), _param(gp, layer), _param(wg, layer),
                  _param(wp, layer), _full(gfin)],
        out_specs=rspec(D_MODEL), out_shape=jax.ShapeDtypeStruct((rows, D_MODEL), F32),
        compiler_params=_cparams(("parallel",), 48), name="moe_combine_ple",
    )(h1, y2, y2, rg, p, gp, wg, wp, gfin)


def _rope_tables(positions, rot_dim, lead, period, reps):
    half = rot_dim // 2
    inv_freq = ROPE_THETA ** (-jnp.arange(0, rot_dim, 2, dtype=F32) / rot_dim)
    ang = positions.astype(F32).reshape(-1, 1) * inv_freq
    cos, sin = jnp.cos(ang), jnp.sin(ang)
    n = cos.shape[0]
    tail = period - lead - rot_dim
    c = jnp.concatenate([jnp.ones((n, lead), F32), cos, cos, jnp.ones((n, tail), F32)], axis=1)
    s1 = jnp.concatenate([jnp.zeros((n, lead), F32), -sin, jnp.zeros((n, half + tail), F32)], axis=1)
    s2 = jnp.concatenate([jnp.zeros((n, lead + half), F32), sin, jnp.zeros((n, tail), F32)], axis=1)
    return tuple(jnp.tile(t, (1, reps)) for t in (c, s1, s2))


def _pad_last(a, lo, hi):
    return jnp.pad(a, [(0, 0)] * (a.ndim - 1) + [(lo, hi)])


def _pad_rows(a, hi):
    return jnp.pad(a, [(0, 0)] * (a.ndim - 2) + [(0, hi), (0, 0)])


def _split_cols(w, sizes):
    out, o = [], 0
    for n in sizes:
        out.append(w[..., o:o + n])
        o += n
    return out


def _in_weights(w_in):
    cb, cc, ch, su, dq, dk, dv, cq, ckv, kr = _split_cols(
        w_in, (CONV_DIM, CONV_DIM, CONV_DIM, SSM_DIM, 3 * DIL_GW, 3 * DIL_GW, 3 * DIL_GW, MLA_Q_RANK, MLA_KV_RANK,
               MLA_ROPE))
    ws = [cb, cc, ch, _pad_last(su, 0, SSM_PAD - SSM_DIM)]
    kinds = ["rows", "rows", "rows", "time_major"]
    for g in range(len(DIL_PATTERNS)):
        ws += [_pad_last(m[..., g * DIL_GW:(g + 1) * DIL_GW], 0, DIL_PLANES * LANE - DIL_GW) for m in (dq, dk, dv)]
        kinds += ["planes_rope_q", "planes_rope_k", "planes"]
    ws += [cq, ckv, _pad_last(kr, MLA_NOPE, MLA_HW - MLA_NOPE - MLA_ROPE)]
    kinds += ["mla_cq", "mla_ckv", "mla_kr"]
    return [w.astype(BF16) for w in ws], kinds


def _mla_weights(w_uq, w_ukv):
    depth = w_uq.shape[0]
    q = w_uq.reshape(depth, MLA_Q_RANK, MLA_HEADS, MLA_QK)
    wq = _pad_last(q, 0, MLA_HW - MLA_QK).reshape(depth, MLA_Q_RANK, MLA_HEADS * MLA_HW)
    kv = w_ukv.reshape(depth, MLA_KV_RANK, MLA_HEADS, MLA_NOPE + MLA_V)
    wk = _pad_last(kv[..., :MLA_NOPE], 0, MLA_HW - MLA_NOPE).reshape(depth, MLA_KV_RANK, -1)
    wv = _pad_last(kv[..., MLA_NOPE:], 0, MLA_HW - MLA_V).reshape(depth, MLA_KV_RANK, -1)
    return wq.astype(BF16), wk.astype(BF16), wv.astype(BF16)


def _out_weights(w_out):
    o_ssm, o_dil, o_mla = CONV_DIM, CONV_DIM + SSM_DIM, CONV_DIM + SSM_DIM + 3 * DIL_GW
    ws = [w_out[:, :o_ssm], _pad_rows(w_out[:, o_ssm:o_dil], SSM_PAD - SSM_DIM)]
    ws += [jnp.concatenate([_pad_rows(w_out[:, o_dil + g * DIL_GW:o_dil + (g + 1) * DIL_GW],
                                      DIL_PLANES * LANE - DIL_GW) for g in range(3)], axis=1)]
    ws += [_pad_rows(w_out[:, o_mla:], MLA_VW - MLA_HEADS * MLA_V)]
    return [w.astype(BF16) for w in ws]


def _router_weights(w_group, b_group, w_router, b_router):
    w = _pad_last(jnp.concatenate([w_group, w_router], axis=-1), 0, ROUTE_W - MOE_GROUPS - N_EXPERTS)
    hi = w.astype(BF16)
    lo = (w - hi.astype(F32)).astype(BF16)
    b = _pad_last(jnp.concatenate([b_group, b_router], axis=-1), 0, ROUTE_W - MOE_GROUPS - N_EXPERTS)
    return jnp.concatenate([hi, lo], axis=-1), b[:, None, :]


def kernel(x, p, positions, norm_mix_g, w_in, conv_w, ssm_lam_re, ssm_lam_im, ssm_log_dt, ssm_b_re, ssm_b_im, ssm_c_re, ssm_c_im, ssm_d, ssm_glu_w, ssm_glu_b, mla_q_norm_g, mla_w_uq, mla_kv_norm_g, mla_w_ukv, w_out, norm_ffn_g, w_group, b_group, w_router, b_router, moe_w_gate, moe_w_up, moe_w_down, norm_ple_g, ple_w_proj, ple_w_gate, final_norm_g):
    batch, seq, _ = x.shape
    n_tok = batch * seq
    depth = w_in.shape[0]
    dil_tabs = _rope_tables(positions, DIL_ROT, 0, DIL_HEAD_DIM, LANE // DIL_HEAD_DIM)
    mla_tabs = _rope_tables(positions, MLA_ROPE, MLA_NOPE, MLA_HW, 1)
    h = x.reshape(n_tok, D_MODEL)
    gains = lambda g: g[:, None, :]
    in_ws, in_kinds = _in_weights(w_in)
    mla_prm = (gains(mla_q_norm_g), gains(mla_kv_norm_g), *_mla_weights(mla_w_uq, mla_w_ukv))
    ssm_prm = _ssm_params(ssm_lam_re, ssm_lam_im, ssm_log_dt, ssm_b_re, ssm_b_im, ssm_c_re, ssm_c_im, ssm_d,
                          ssm_glu_w, ssm_glu_b)
    out_ws = _out_weights(w_out)
    wr, br = _router_weights(w_group, b_group, w_router, b_router)
    ple_wg, ple_wp = ple_w_gate.astype(BF16), ple_w_proj.astype(BF16)
    p_rows = p.reshape(depth, n_tok, PLE_DIM)
    for i in range(depth):
        z = _in_proj(h, gains(norm_mix_g), dil_tabs, mla_tabs, mla_prm, in_ws, in_kinds, batch, seq, i)
        cb, cc, ch, su = z[:4]
        q, k, v = z[13:]
        y_conv = _conv_mixer(cb, cc, ch, conv_w, batch, seq, i)
        y_ssm = _ssm_mixer(su, ssm_prm, batch, seq, i)
        outs, lses = [], []
        for g, (window, dil) in enumerate(DIL_PATTERNS):
            assert window // dil == DIL_BLK and (seq // dil) % DIL_BLK == 0
            o, l = _dil_group(z[4 + 3 * g], z[5 + 3 * g], z[6 + 3 * g], dil, batch, seq)
            outs.append(o)
            lses.append(l)
        y_mla = _mla_attn(q, k, v, batch, seq)
        h1, xn, route_i, route_g = _out_proj(
            (y_conv, y_ssm, outs, lses, y_mla), h, out_ws, gains(norm_ffn_g), wr, br, batch, seq, i)
        sched, slot_asg, n_used, nblk = _moe_plan(route_i, n_tok)
        y2 = _moe_experts(xn, sched, slot_asg, n_used, nblk, moe_w_gate, moe_w_up, moe_w_down, i)
        h = _ple(h1, y2, route_g, p_rows, gains(norm_ple_g), ple_wg, ple_wp, final_norm_g.reshape(1, -1),
                 final=(i == depth - 1), layer=i)
    return h.reshape(batch, seq, D_MODEL)
```

```python
import functools
import math

import jax
import jax.numpy as jnp
from jax import lax
from jax.experimental import pallas as pl
from jax.experimental.pallas import tpu as pltpu

D_MODEL = 2048
PLE_DIM = 256
ROPE_THETA = 500000.0
NORM_EPS = 1e-6

CONV_DIM = 448
CONV_K = 3

SSM_DIM = 448
SSM_GROUP = 16
SSM_GROUPS = SSM_DIM // SSM_GROUP
SSM_STATE = 64
SSM_N = SSM_GROUPS * SSM_STATE
SSM_PAD = 512

DIL_HEAD_DIM = 64
DIL_ROT = DIL_HEAD_DIM // 4
DIL_PATTERNS = ((128, 1), (512, 4), (2048, 16))
DIL_HPG = 3
DIL_GW = DIL_HPG * DIL_HEAD_DIM
DIL_PLANES = 2
DIL_BLK = 128
DIL_UNROLL = 4

MLA_HEADS = 9
MLA_Q_RANK = 384
MLA_KV_RANK = 256
MLA_NOPE = 64
MLA_ROPE = 32
MLA_V = 64
MLA_QK = MLA_NOPE + MLA_ROPE
MLA_HW = 128
MLA_VW = 640

MOE_GROUPS = 8
MOE_EPG = 8
N_EXPERTS = MOE_GROUPS * MOE_EPG
MOE_TOPK = 2
MOE_FF = 512
MOE_BLOCK = 128
ROUTE_W = 128

LANE = 128
PACK_SLABS = D_MODEL // 2 // LANE
ROW_CHUNK = 32
SLOT_SHIFT = 15
NEG_BIG = -1e30

BF16 = jnp.bfloat16
F32 = jnp.float32


def _cparams(sem, vmem_mb):
    return pltpu.CompilerParams(dimension_semantics=sem, vmem_limit_bytes=vmem_mb * 1024 * 1024)


def _rms(x, g):
    ms = jnp.mean(x * x, axis=-1, keepdims=True)
    return (x * lax.rsqrt(ms + NORM_EPS)) * g


def _sigmoid(x):
    return 1.0 / (1.0 + jnp.exp(-x))


def _full(a):
    return pl.BlockSpec(a.shape, lambda *_: (0,) * a.ndim)


def _param(a, layer):
    return pl.BlockSpec((None,) + a.shape[1:], lambda *_: (layer,) + (0,) * (a.ndim - 1),
                        pipeline_mode=pl.Buffered(1))


_HI16 = 0xFFFF0000


def _pack_rows(ref, base, x):
    n = x.shape[0]
    bits = lax.bitcast_convert_type(x, jnp.uint32)
    for c in range(PACK_SLABS):
        lo = bits[:, c * LANE:(c + 1) * LANE] >> 16
        hi = bits[:, D_MODEL // 2 + c * LANE:D_MODEL // 2 + (c + 1) * LANE] & jnp.uint32(_HI16)
        ref[pl.ds(base + c, n, stride=PACK_SLABS), :] = lo | hi


def _unpack_rows(ref, base, n):
    lo, hi = [], []
    for c in range(PACK_SLABS):
        w = ref[pl.ds(base + c, n, stride=PACK_SLABS), :]
        lo.append(lax.bitcast_convert_type(w << 16, F32))
        hi.append(lax.bitcast_convert_type(w & jnp.uint32(_HI16), F32))
    return jnp.concatenate(lo + hi, axis=1)


def _in_proj_kernel(x_ref, g_ref, dc_ref, ds1_ref, ds2_ref, mc_ref, ms1_ref, ms2_ref, gq_ref, gkv_ref,
                    wq_ref, wk_ref, wv_ref, *refs, kinds):
    n_out = len(kinds)
    w_refs, o_refs = refs[:n_out], refs[n_out:]
    xn = _rms(x_ref[...], g_ref[...]).astype(BF16)
    tabs = [jnp.concatenate([t[...]] * DIL_PLANES, axis=1) for t in (dc_ref, ds1_ref, ds2_ref)]
    latent = {}
    outs = iter(o_refs)
    for kind, w_ref in zip(kinds, w_refs):
        res = jnp.dot(xn, w_ref[...], preferred_element_type=F32)
        if kind.startswith("mla"):
            latent[kind] = res
            continue
        o_ref = next(outs)
        if kind.startswith("planes"):
            if kind != "planes":
                res = _rope_lanes(res, *tabs, DIL_ROT // 2, DIL_PLANES * LANE)
            if kind == "planes_rope_q":
                res = res * (1.0 / math.sqrt(DIL_HEAD_DIM))
            for j in range(o_ref.shape[0]):
                o_ref[j] = res[:, j * LANE:(j + 1) * LANE]
        else:
            o_ref[...] = res
    q_ref, k_ref, v_ref = outs
    rope = functools.partial(_rope_lanes, c=mc_ref[...], s1=ms1_ref[...], s2=ms2_ref[...], half=MLA_ROPE // 2,
                             width=MLA_HW)
    qn = _rms(latent["mla_cq"], gq_ref[...]).astype(BF16)
    kvn = _rms(latent["mla_ckv"], gkv_ref[...]).astype(BF16)
    q = jnp.dot(qn, wq_ref[...], preferred_element_type=F32)
    kn = jnp.dot(kvn, wk_ref[...], preferred_element_type=F32)
    lane = lax.broadcasted_iota(jnp.int32, (1, MLA_HEADS * MLA_HW), 1)
    ones_col = (lane % MLA_HW == MLA_V).astype(F32)
    v_ref[...] = (jnp.dot(kvn, wv_ref[...], preferred_element_type=F32) + ones_col).astype(BF16)
    kr = rope(latent["mla_kr"])
    scale = math.log2(math.e) / math.sqrt(MLA_QK)
    for h in range(MLA_HEADS):
        sl = slice(h * MLA_HW, (h + 1) * MLA_HW)
        q_ref[:, sl] = (rope(q[:, sl]) * scale).astype(BF16)
        k_ref[:, sl] = (kn[:, sl] + kr).astype(BF16)


def _in_proj(h, g, dil_tabs, mla_tabs, mla_prm, weights, kinds, batch, seq, layer, tm=256):
    n_s = seq // tm

    def rspec(n):
        return pl.BlockSpec((tm, n), lambda b, i: (b * n_s + i, 0))

    in_specs = [rspec(D_MODEL), _param(g, layer)] + [rspec(LANE)] * 6 + [_param(a, layer) for a in mla_prm]
    in_specs += [_param(w, layer) for w in weights]
    out_shape, out_specs = [], []
    for kind, w in zip(kinds, weights):
        n = w.shape[-1]
        if kind.startswith("mla"):
            continue
        if kind == "time_major":
            out_shape.append(jax.ShapeDtypeStruct((seq, batch * n), F32))
            out_specs.append(pl.BlockSpec((tm, n), lambda b, i: (i, b)))
        elif kind.startswith("planes"):
            out_shape.append(jax.ShapeDtypeStruct((n // LANE, batch * seq, LANE), F32))
            out_specs.append(pl.BlockSpec((n // LANE, tm, LANE), lambda b, i: (0, b * n_s + i, 0)))
        else:
            out_shape.append(jax.ShapeDtypeStruct((batch * seq, n), F32))
            out_specs.append(rspec(n))
    for n in (MLA_HEADS * MLA_HW,) * 3:
        out_shape.append(jax.ShapeDtypeStruct((batch * seq, n), BF16))
        out_specs.append(rspec(n))
    return pl.pallas_call(
        functools.partial(_in_proj_kernel, kinds=tuple(kinds)),
        grid=(batch, n_s), in_specs=in_specs, out_specs=out_specs, out_shape=out_shape,
        compiler_params=_cparams(("parallel", "parallel"), 56), name="in_proj",
    )(h, g, *dil_tabs, *mla_tabs, *mla_prm, *weights)


def _conv_kernel(cb_ref, cc_ref, ch_ref, w_ref, o_ref, u_ref, *, rows):
    seq = cb_ref.shape[0]
    u_ref[0:8, :] = jnp.zeros((8, CONV_DIM), F32)
    for r0 in range(0, seq, rows):
        u_ref[8 + r0:8 + r0 + rows, :] = cc_ref[r0:r0 + rows, :] * ch_ref[r0:r0 + rows, :]
    w0, w1, w2 = w_ref[0:1, :], w_ref[1:2, :], w_ref[2:3, :]
    for r0 in range(0, seq, rows):
        acc = (w2 * u_ref[8 + r0:8 + r0 + rows, :] + w1 * u_ref[7 + r0:7 + r0 + rows, :]
               + w0 * u_ref[6 + r0:6 + r0 + rows, :])
        o_ref[r0:r0 + rows, :] = cb_ref[r0:r0 + rows, :] * acc


def _conv_mixer(cb, cc, ch, conv_w, batch, seq, layer):
    spec = pl.BlockSpec((seq, CONV_DIM), lambda b: (b, 0))
    return pl.pallas_call(
        functools.partial(_conv_kernel, rows=256),
        grid=(batch,), in_specs=[spec, spec, spec, _param(conv_w, layer)], out_specs=spec,
        out_shape=jax.ShapeDtypeStruct((batch * seq, CONV_DIM), F32),
        scratch_shapes=[pltpu.VMEM((seq + 8, CONV_DIM), F32)],
        compiler_params=_cparams(("parallel",), 48), name="conv_mixer",
    )(cb, cc, ch, conv_w)


SSM_NP = SSM_N // LANE
SSM_SCAN_PLANES = 4
_SSM_SPLIT = ((0, 256, 0, 1024), (256, 256, 1024, 768))


def _gelu_tanh(x):
    return 0.5 * x * (1.0 + jnp.tanh(math.sqrt(2.0 / math.pi) * (x + 0.044715 * (x * x * x))))


def _ssm_kernel(u_ref, bf0_ref, bf1_ref, are_ref, aim_ref, cc0_ref, cc1_ref, d_ref, gw_ref, gb_ref, o_ref,
                xs_ref, st_ref, *, batch, tc):
    per_tile = 8 // batch
    npk = SSM_NP // per_tile

    @pl.when(pl.program_id(0) == 0)
    def _():
        st_ref[...] = jnp.zeros_like(st_ref)

    def plane_rows(p, b):
        return p % npk, pl.ds((p // npk) * batch + b, tc, stride=8)

    for b in range(batch):
        for (l0, lw, s0, sw), bf_ref in zip(_SSM_SPLIT, (bf0_ref, bf1_ref)):
            ub = u_ref[:, b * SSM_PAD + l0:b * SSM_PAD + l0 + lw].astype(BF16)
            r = jnp.dot(ub, bf_ref[...], preferred_element_type=F32)
            for q in range(sw // LANE):
                pk, rows = plane_rows(s0 // LANE + q, b)
                xs_ref[pk, rows, :] = r[:, q * LANE:(q + 1) * LANE]
                xs_ref[npk + pk, rows, :] = r[:, sw + q * LANE:sw + (q + 1) * LANE]

    def decay(a_ref, pk):
        return jnp.concatenate([jnp.broadcast_to(a_ref[:, (pk + j * npk) * LANE:(pk + j * npk + 1) * LANE],
                                                 (batch, LANE)) for j in range(per_tile)], axis=0)

    for p0 in range(0, npk, SSM_SCAN_PLANES):
        pks = list(range(p0, min(p0 + SSM_SCAN_PLANES, npk)))
        ar = [decay(are_ref, pk) for pk in pks]
        ai = [decay(aim_ref, pk) for pk in pks]

        def body(t, carry, pks=pks, ar=ar, ai=ai):
            rows = pl.ds(pl.multiple_of(t * 8, 8), 8)
            new = []
            for idx, pk in enumerate(pks):
                sr, si = carry[idx]
                sr, si = (ar[idx] * sr - ai[idx] * si + xs_ref[pk, rows, :],
                          ar[idx] * si + ai[idx] * sr + xs_ref[npk + pk, rows, :])
                xs_ref[pk, rows, :] = sr
                xs_ref[npk + pk, rows, :] = si
                new.append((sr, si))
            return tuple(new)

        init = tuple((st_ref[pk], st_ref[npk + pk]) for pk in pks)
        for pk, (sr, si) in zip(pks, lax.fori_loop(0, tc, body, init, unroll=2)):
            st_ref[pk] = sr
            st_ref[npk + pk] = si

    for b in range(batch):
        ys = []
        for (l0, lw, s0, sw), cc_ref in zip(_SSM_SPLIT, (cc0_ref, cc1_ref)):
            planes = [plane_rows(p, b) for p in range(s0 // LANE, (s0 + sw) // LANE)]
            xb = jnp.concatenate([xs_ref[pk, rows, :] for pk, rows in planes]
                                 + [xs_ref[npk + pk, rows, :] for pk, rows in planes], axis=1)
            ys.append(jnp.dot(xb.astype(BF16), cc_ref[...], preferred_element_type=F32))
        y = jnp.concatenate(ys, axis=1) + d_ref[...] * u_ref[:, b * SSM_PAD:(b + 1) * SSM_PAD]
        g = _gelu_tanh(y)
        o_ref[:, b * SSM_PAD:(b + 1) * SSM_PAD] = g * _sigmoid(
            jnp.dot(g.astype(BF16), gw_ref[...], preferred_element_type=F32) + gb_ref[...])


def _ssm_mixer(su, prm, batch, seq, layer, tc=256):
    spec = pl.BlockSpec((tc, batch * SSM_PAD), lambda c: (c, 0))
    args = (prm["bf0"], prm["bf1"], prm["a_re"], prm["a_im"], prm["cc0"], prm["cc1"], prm["d"], prm["gw"], prm["gb"])
    return pl.pallas_call(
        functools.partial(_ssm_kernel, batch=batch, tc=tc),
        grid=(seq // tc,), in_specs=[spec] + [_param(a, layer) for a in args], out_specs=spec,
        out_shape=jax.ShapeDtypeStruct((seq, batch * SSM_PAD), F32),
        scratch_shapes=[pltpu.VMEM((2 * SSM_NP * batch // 8, tc * 8, LANE), F32),
                        pltpu.VMEM((2 * SSM_NP * batch // 8, 8, LANE), F32)],
        compiler_params=_cparams(("arbitrary",), 48), name="ssm_mixer",
    )(su, *args)


def _ssm_params(lam_re, lam_im, log_dt, b_re, b_im, c_re, c_im, d_skip, glu_w, glu_b):
    depth = lam_re.shape[0]
    dt = jnp.exp(log_dt)[..., None]
    mag = jnp.exp(lam_re * dt)
    a_re = mag * jnp.cos(lam_im * dt)
    a_im = mag * jnp.sin(lam_im * dt)
    nr, ni = a_re - 1.0, a_im
    den = lam_re * lam_re + lam_im * lam_im
    f_re = (nr * lam_re + ni * lam_im) / den
    f_im = (ni * lam_re - nr * lam_im) / den
    bfr = f_re[..., None] * b_re - f_im[..., None] * b_im
    bfi = f_re[..., None] * b_im + f_im[..., None] * b_re
    same_group = (jnp.arange(SSM_DIM)[:, None] // SSM_GROUP == jnp.arange(SSM_N)[None, :] // SSM_STATE).astype(F32)

    def in_blockdiag(m):
        cols = jnp.transpose(m, (0, 3, 1, 2)).reshape(depth, SSM_GROUP, SSM_N)
        return jnp.tile(cols, (1, SSM_GROUPS, 1)) * same_group

    def out_blockdiag(m):
        rows = jnp.transpose(m, (0, 1, 3, 2)).reshape(depth, SSM_N, SSM_GROUP)
        return jnp.tile(rows, (1, 1, SSM_GROUPS)) * same_group.T

    pad = SSM_PAD - SSM_DIM
    b_re_d, b_im_d = (_pad_rows(in_blockdiag(m), pad) for m in (bfr, bfi))
    c_re_d, c_im_d = (_pad_last(out_blockdiag(m), 0, pad) for m in (c_re, -c_im))
    prm = {}
    for c, (l0, lw, s0, sw) in enumerate(_SSM_SPLIT):
        prm[f"bf{c}"] = jnp.concatenate([b_re_d[:, l0:l0 + lw, s0:s0 + sw], b_im_d[:, l0:l0 + lw, s0:s0 + sw]],
                                        axis=2).astype(BF16)
        prm[f"cc{c}"] = jnp.concatenate([c_re_d[:, s0:s0 + sw, l0:l0 + lw], c_im_d[:, s0:s0 + sw, l0:l0 + lw]],
                                        axis=1).astype(BF16)
    return {
        **prm,
        "a_re": a_re.reshape(depth, 1, SSM_N), "a_im": a_im.reshape(depth, 1, SSM_N),
        "d": _pad_last(d_skip, 0, pad)[:, None, :],
        "gw": _pad_last(_pad_rows(glu_w, pad), 0, pad).astype(BF16),
        "gb": _pad_last(glu_b, 0, pad)[:, None, :],
    }


def _rope_lanes(x, c, s1, s2, half, width):
    return x * c + pltpu.roll(x, width - half, 1) * s1 + pltpu.roll(x, half, 1) * s2


def _dil_kernel(q_ref, k_ref, v_ref, o_ref, l_ref, *, dil):
    seq = q_ref.shape[1]
    nb = (seq // dil) // DIL_BLK
    width = DIL_PLANES * LANE
    lane = lax.broadcasted_iota(jnp.int32, (1, width), 1)
    cmask = [(lane // DIL_HEAD_DIM == c).astype(F32) for c in range(DIL_HPG)]
    qi = lax.broadcasted_iota(jnp.int32, (DIL_HPG * DIL_BLK, 2 * DIL_BLK), 0) % DIL_BLK
    ki = lax.broadcasted_iota(jnp.int32, (DIL_HPG * DIL_BLK, 2 * DIL_BLK), 1)
    band = (ki >= qi) & (ki <= qi + DIL_BLK)

    def planes(ref, rows):
        return jnp.concatenate([ref[j, rows, :] for j in range(DIL_PLANES)], axis=1)

    def rows_of(m, blk):
        if dil == 1:
            return pl.ds(pl.multiple_of(blk * DIL_BLK, DIL_BLK), DIL_BLK)
        return pl.ds(blk * DIL_BLK * dil + m, DIL_BLK, stride=dil)

    def blocks(it, carry):
        ids = [it * DIL_UNROLL + u for u in range(DIL_UNROLL)]
        mis = [(idx // nb, idx % nb) for idx in ids]
        rows_q = [rows_of(m, i) for m, i in mis]
        rows_p = [rows_of(m, jnp.maximum(i - 1, 0)) for m, i in mis]
        scores, vcats = [], []
        for rq, rp in zip(rows_q, rows_p):
            q = planes(q_ref, rq)
            kcat = jnp.concatenate([planes(k_ref, rp), planes(k_ref, rq)], axis=0).astype(BF16)
            vcats.append(jnp.concatenate([planes(v_ref, rp), planes(v_ref, rq)], axis=0).astype(BF16))
            qs = jnp.concatenate([q * cm for cm in cmask], axis=0).astype(BF16)
            scores.append(lax.dot_general(qs, kcat, (((1,), (1,)), ((), ())), preferred_element_type=F32))
        soft = []
        for s, (m, i) in zip(scores, mis):
            s = jnp.where(band & ((ki >= DIL_BLK) | (i > 0)), s, NEG_BIG)
            mx = jnp.max(s, axis=-1, keepdims=True)
            p = jnp.exp(s - mx)
            den = jnp.sum(p, axis=-1, keepdims=True)
            soft.append((p.astype(BF16), den, mx + jnp.log(den)))
        for (p, den, lse), vcat, rq in zip(soft, vcats, rows_q):
            o = jnp.dot(p, vcat, preferred_element_type=F32) / den
            out = jnp.zeros((DIL_BLK, width), F32)
            lout = jnp.zeros((DIL_BLK, width), F32)
            for c, cm in enumerate(cmask):
                out = out + cm * o[c * DIL_BLK:(c + 1) * DIL_BLK, :]
                lout = lout + cm * lse[c * DIL_BLK:(c + 1) * DIL_BLK, :]
            for j in range(DIL_PLANES):
                o_ref[j, rq, :] = out[:, j * LANE:(j + 1) * LANE]
                l_ref[j, rq, :] = lout[:, j * LANE:(j + 1) * LANE]
        return carry

    lax.fori_loop(0, dil * nb // DIL_UNROLL, blocks, 0)


def _dil_group(q, k, v, dil, batch, seq):
    spec = pl.BlockSpec((DIL_PLANES, seq, LANE), lambda b: (0, b, 0))
    sds = jax.ShapeDtypeStruct((DIL_PLANES, batch * seq, LANE), F32)
    return pl.pallas_call(
        functools.partial(_dil_kernel, dil=dil), grid=(batch,), in_specs=[spec] * 3,
        out_specs=[spec, spec], out_shape=[sds, sds],
        compiler_params=_cparams(("parallel",), 48), name=f"dil_attn_d{dil}",
    )(q, k, v)


def _mla_attn_kernel(q_ref, k_ref, v_ref, o_ref, *, tq, group):
    qi = pl.program_id(1)
    row = lax.broadcasted_iota(jnp.int32, (tq, tq), 0)
    col = lax.broadcasted_iota(jnp.int32, (tq, tq), 1)
    causal = col <= row

    def heads_out(heads):
        qs = [q_ref[:, h * MLA_HW:(h + 1) * MLA_HW] for h in heads]

        def step(kj, carry, masked):
            rows = pl.ds(pl.multiple_of(kj * tq, tq), tq)
            scores = [lax.dot_general(q, k_ref[rows, h * MLA_HW:(h + 1) * MLA_HW], (((1,), (1,)), ((), ())),
                                      preferred_element_type=F32) for h, q in zip(heads, qs)]
            stats = []
            for s, (m, acc) in zip(scores, carry):
                if masked:
                    s = jnp.where(causal, s, NEG_BIG)
                mn = jnp.maximum(m, jnp.max(s, axis=-1, keepdims=True))
                stats.append((mn, jnp.exp2(m - mn), jnp.exp2(s - mn).astype(BF16)))
            out = []
            for h, (mn, alpha, p), (_, acc) in zip(heads, stats, carry):
                v = v_ref[rows, h * MLA_HW:(h + 1) * MLA_HW]
                out.append((mn, alpha * acc + jnp.dot(p, v, preferred_element_type=F32)))
            return tuple(out)

        init = tuple((jnp.full((tq, 1), NEG_BIG, F32), jnp.zeros((tq, MLA_HW), F32)) for _ in heads)
        carry = lax.fori_loop(0, qi, functools.partial(step, masked=False), init)
        return [acc / acc[:, MLA_V:MLA_V + 1] for _, acc in step(qi, carry, True)]

    for h0 in range(0, MLA_HEADS, group):
        heads = list(range(h0, min(h0 + group, MLA_HEADS)))
        for h, o in zip(heads, heads_out(heads)):
            lo = h * MLA_V
            if h % 2:
                o_ref[:, lo:lo + MLA_V] = pltpu.roll(o, MLA_V, 1)[:, MLA_V:2 * MLA_V]
            else:
                o_ref[:, lo:lo + MLA_V] = o[:, 0:MLA_V]
    if MLA_HEADS % 2:
        o_ref[:, MLA_HEADS * MLA_V:] = jnp.zeros((tq, MLA_VW - MLA_HEADS * MLA_V), F32)


def _mla_attn(q, k, v, batch, seq, tq=256, group=5):
    hw = MLA_HEADS * MLA_HW
    n_q = seq // tq
    return pl.pallas_call(
        functools.partial(_mla_attn_kernel, tq=tq, group=group), grid=(batch, n_q),
        in_specs=[pl.BlockSpec((tq, hw), lambda b, i: (b * n_q + i, 0)),
                  pl.BlockSpec((seq, hw), lambda b, i: (b, 0)),
                  pl.BlockSpec((seq, hw), lambda b, i: (b, 0))],
        out_specs=pl.BlockSpec((tq, MLA_VW), lambda b, i: (b * n_q + i, 0)),
        out_shape=jax.ShapeDtypeStruct((batch * seq, MLA_VW), F32),
        compiler_params=_cparams(("parallel", "arbitrary"), 48), name="mla_attn",
    )(q, k, v)


def _out_proj_kernel(yc_ref, ys_ref, od0_ref, od1_ref, od2_ref, ld0_ref, ld1_ref, ld2_ref, ym_ref, h_ref,
                     wc_ref, ws_ref, wd_ref, wm_ref,
                     gf_ref, wr_ref, br_ref,
                     h1_ref, xn_ref, ri_ref, rg_ref):
    def mm(y, w_ref):
        return jnp.dot(y.astype(BF16), w_ref[...], preferred_element_type=F32)

    y_dil = [None] * (3 * DIL_PLANES)
    for j in range(DIL_PLANES):
        la, lb, lc = ld0_ref[j], ld1_ref[j], ld2_ref[j]
        mx = jnp.maximum(jnp.maximum(la, lb), lc)
        ea, eb, ec = jnp.exp(la - mx), jnp.exp(lb - mx), jnp.exp(lc - mx)
        inv = 1.0 / (ea + eb + ec)
        for g, (o_ref, e) in enumerate(((od0_ref, ea), (od1_ref, eb), (od2_ref, ec))):
            y_dil[g * DIL_PLANES + j] = o_ref[j] * (e * inv)
    mix = (mm(yc_ref[...], wc_ref) + mm(ys_ref[...], ws_ref) + mm(jnp.concatenate(y_dil, axis=1), wd_ref)
           + mm(ym_ref[...], wm_ref))
    h1 = h_ref[...] + mix
    h1_ref[...] = h1
    hn = _rms(h1, gf_ref[...])
    hi = hn.astype(BF16)
    hi32 = hi.astype(F32)
    lo = (hn - hi32).astype(BF16)
    _pack_rows(xn_ref, 0, hi32)
    r_hi = jnp.dot(hi, wr_ref[...], preferred_element_type=F32)
    r_lo = jnp.dot(lo, wr_ref[:, 0:ROUTE_W], preferred_element_type=F32)
    logits = r_hi[:, 0:ROUTE_W] + r_hi[:, ROUTE_W:2 * ROUTE_W] + r_lo + br_ref[...]
    lane = lax.broadcasted_iota(jnp.int32, logits.shape, 1)
    gl = jnp.where(lane < MOE_GROUPS, logits, NEG_BIG)
    gmax = jnp.max(gl, axis=-1, keepdims=True)
    g_top_p = 1.0 / jnp.sum(jnp.exp(gl - gmax), axis=-1, keepdims=True)
    g_idx = jnp.min(jnp.where(gl == gmax, lane, ROUTE_W), axis=-1, keepdims=True)
    in_group = (lane >= MOE_GROUPS) & (lane < MOE_GROUPS + N_EXPERTS) & (((lane - MOE_GROUPS) >> 3) == g_idx)
    el = jnp.where(in_group, logits, NEG_BIG)
    v1 = jnp.max(el, axis=-1, keepdims=True)
    i1 = jnp.min(jnp.where(el == v1, lane, ROUTE_W), axis=-1, keepdims=True)
    el2 = jnp.where(lane == i1, NEG_BIG, el)
    v2 = jnp.max(el2, axis=-1, keepdims=True)
    i2 = jnp.min(jnp.where(el2 == v2, lane, ROUTE_W), axis=-1, keepdims=True)
    e2 = jnp.exp(v2 - v1)
    w1 = g_top_p / (1.0 + e2)
    w2 = g_top_p * e2 / (1.0 + e2)
    ri_ref[...] = jnp.where(lane == 0, i1 - MOE_GROUPS, jnp.where(lane == 1, i2 - MOE_GROUPS, 0))
    rg_ref[...] = jnp.where(lane == 0, w1, jnp.where(lane == 1, w2, 0.0))


def _out_proj(ys, h, ws, gf, wr, br, batch, seq, layer, tm=256):
    rows = batch * seq
    n_s = seq // tm
    yc, ys_tm, dil_o, dil_l, ym = ys

    def rspec(n):
        return pl.BlockSpec((tm, n), lambda b, i: (b * n_s + i, 0))

    pspec = pl.BlockSpec((DIL_PLANES, tm, LANE), lambda b, i: (0, b * n_s + i, 0))
    in_specs = [rspec(CONV_DIM), pl.BlockSpec((tm, SSM_PAD), lambda b, i: (i, b))] + [pspec] * 6
    in_specs += [rspec(MLA_VW), rspec(D_MODEL)]
    in_specs += [_param(a, layer) for a in (*ws, gf, wr, br)]
    return pl.pallas_call(
        _out_proj_kernel, grid=(batch, n_s), in_specs=in_specs,
        out_specs=[rspec(D_MODEL), pl.BlockSpec((tm * PACK_SLABS, LANE), lambda b, i: (b * n_s + i, 0)),
                   rspec(ROUTE_W), rspec(ROUTE_W)],
        out_shape=[jax.ShapeDtypeStruct((rows, D_MODEL), F32),
                   jax.ShapeDtypeStruct((rows * PACK_SLABS, LANE), jnp.uint32),
                   jax.ShapeDtypeStruct((rows, ROUTE_W), jnp.int32), jax.ShapeDtypeStruct((rows, ROUTE_W), F32)],
        compiler_params=_cparams(("parallel", "parallel"), 48), name="out_proj_router",
    )(yc, ys_tm, *dil_o, *dil_l, ym, h, *ws, gf, wr, br)


def _moe_plan(route_i, n_tok):
    n_asg = n_tok * MOE_TOPK
    nblk = (n_asg + N_EXPERTS * (MOE_BLOCK - 1) + MOE_BLOCK - 1) // MOE_BLOCK
    flat_e = route_i[:, :MOE_TOPK].reshape(-1)
    onehot = (flat_e[:, None] == jnp.arange(N_EXPERTS, dtype=jnp.int32)[None, :]).astype(F32)
    oh3 = onehot.reshape(n_asg // MOE_BLOCK, MOE_BLOCK, N_EXPERTS)
    within = jnp.einsum("ij,bje->bie", jnp.tril(jnp.ones((MOE_BLOCK, MOE_BLOCK), F32)), oh3)
    chunk_tot = within[:, -1, :]
    n_chunk = chunk_tot.shape[0]
    chunk_off = jnp.dot(jnp.tril(jnp.ones((n_chunk, n_chunk), F32), -1), chunk_tot)
    rank = jnp.sum((within + chunk_off[:, None, :]) * oh3, axis=-1).reshape(n_asg).astype(jnp.int32) - 1
    counts = (chunk_off[-1] + chunk_tot[-1]).astype(jnp.int32)
    blocks_of = (counts + MOE_BLOCK - 1) // MOE_BLOCK
    upto = jnp.arange(N_EXPERTS)[:, None] <= jnp.arange(N_EXPERTS)[None, :]
    blk_end = jnp.sum(jnp.where(upto, blocks_of[:, None], 0), axis=0)
    blk_start = blk_end - blocks_of
    dest = jnp.dot(onehot, blk_start.astype(F32)).astype(jnp.int32) * MOE_BLOCK + rank
    in_blk = (dest[:, None] // MOE_BLOCK == jnp.arange(nblk, dtype=jnp.int32)[None, :]).astype(F32)
    at_off = (dest[:, None] % MOE_BLOCK == jnp.arange(MOE_BLOCK, dtype=jnp.int32)[None, :]).astype(F32)
    asg = jnp.arange(n_asg, dtype=jnp.int32)
    a_hi, a_lo = (asg // MOE_BLOCK).astype(F32)[:, None], (asg % MOE_BLOCK).astype(F32)[:, None]
    inv = lambda v: jnp.einsum("ab,ao->bo", in_blk * v, at_off)
    slot_a = jnp.where(inv(1.0) > 0.5, inv(a_hi) * MOE_BLOCK + inv(a_lo), -1.0).astype(jnp.int32).reshape(-1)
    slot = jnp.arange(nblk * MOE_BLOCK, dtype=jnp.int32)
    dump = MOE_TOPK * n_tok + ((slot // MOE_BLOCK) % 2) * MOE_BLOCK + slot % MOE_BLOCK
    assert MOE_TOPK * n_tok + 2 * MOE_BLOCK <= 1 << SLOT_SHIFT
    slot_asg = jnp.where(slot_a >= 0, ((slot_a >> 1) << SLOT_SHIFT) | ((slot_a & 1) * n_tok + (slot_a >> 1)), dump)
    n_used = blk_end[-1].astype(jnp.int32)
    blk = jnp.arange(nblk, dtype=jnp.int32)
    block_expert = jnp.minimum(jnp.sum(blk_end[None, :] <= blk[:, None], axis=1), N_EXPERTS - 1).astype(jnp.int32)
    mine = block_expert[:, None] == jnp.arange(N_EXPERTS, dtype=jnp.int32)[None, :]
    lookup = lambda table: jnp.sum(jnp.where(mine, table[None, :], 0), axis=1)
    run_start, run_end = lookup(blk_start), lookup(blk_end)
    is_first = ((blk == run_start) & (blk < n_used)).astype(jnp.int32)
    used = (blocks_of > 0).astype(jnp.int32)
    nth_used = jnp.sum(jnp.where(upto, used[:, None], 0), axis=0) - used
    w_slot = lookup(nth_used) % 2
    starts_at = (blk_start[None, :] == run_end[:, None]) & (blocks_of[None, :] > 0)
    nxt = jnp.where(run_end < n_used, jnp.sum(jnp.where(starts_at, jnp.arange(N_EXPERTS)[None, :], 0), axis=1), -1)
    rows_left = jnp.clip(lookup(counts) - (blk - run_start) * MOE_BLOCK, 0, MOE_BLOCK)
    sched = jnp.stack([block_expert, is_first, w_slot, nxt, jnp.where(blk < n_used, rows_left, 0)])
    sched = jnp.pad(sched.astype(jnp.int32), ((0, 0), (0, 1)))
    return sched, slot_asg, n_used.reshape(1), nblk


def _moe_kernel(sched_ref, asg_ref, nu_ref, x_hbm, wg_hbm, wu_hbm, wd_hbm, y_hbm,
                xbuf, ybuf, wgf, wuf, wdf, wgb, wub, wdb, gsem, ssem, wsem, *, n_tok, layer):
    i = pl.program_id(0)
    n_used = nu_ref[0]
    slot = i % 2
    buf_rows = MOE_BLOCK * PACK_SLABS
    active = i < n_used
    first = active & (sched_ref[1, i] == 1)
    w_slot = sched_ref[2, i]

    def weights(e, ws):
        return [pltpu.make_async_copy(w_hbm.at[layer, e], buf.at[ws], wsem.at[ws])
                for w_hbm, buf in ((wg_hbm, wgf), (wu_hbm, wuf), (wd_hbm, wdf))]

    def slab_rows(row):
        return pl.ds(pl.multiple_of(row * PACK_SLABS, PACK_SLABS), PACK_SLABS)

    def gather_row(blk, sl, r):
        tok = asg_ref[blk * MOE_BLOCK + r] >> SLOT_SHIFT
        return pltpu.make_async_copy(x_hbm.at[slab_rows(tok), :], xbuf.at[slab_rows(sl * MOE_BLOCK + r), :],
                                     gsem.at[sl])

    def scatter_row(blk, sl, r):
        dst = asg_ref[blk * MOE_BLOCK + r] & ((1 << SLOT_SHIFT) - 1)
        return pltpu.make_async_copy(ybuf.at[slab_rows(sl * MOE_BLOCK + r), :], y_hbm.at[slab_rows(dst), :],
                                     ssem.at[sl])

    def row_chunks(make, blk, sl, wait):
        n_valid = sched_ref[4, blk]
        for c0 in range(0, MOE_BLOCK, ROW_CHUNK):
            @pl.when(c0 < n_valid)
            def _(c0=c0):
                for r in range(c0, c0 + ROW_CHUNK):
                    if wait:
                        make(blk, sl, r).wait()
                    else:
                        make(blk, sl, r).start()

    @pl.when(i == 0)
    def _():
        xbuf[...] = jnp.zeros_like(xbuf)
        ybuf[...] = jnp.zeros_like(ybuf)
        dump = pltpu.make_async_copy(ybuf, y_hbm.at[pl.ds(MOE_TOPK * n_tok * PACK_SLABS, 2 * buf_rows), :], ssem.at[0])
        dump.start()
        dump.wait()

    @pl.when((i == 0) & (n_used > 0))
    def _():
        for cp in weights(sched_ref[0, 0], 0):
            cp.start()
        row_chunks(gather_row, 0, 0, wait=False)

    @pl.when(first)
    def _():
        for cp in weights(sched_ref[0, i], w_slot):
            cp.wait()

    @pl.when(first & (sched_ref[3, i] >= 0))
    def _():
        for cp, queue in zip(weights(sched_ref[3, i], 1 - w_slot), (0, 1, 1)):
            cp.start(priority=queue)

    @pl.when(first)
    def _():
        wgb[...] = wgf[w_slot].astype(BF16)
        wub[...] = wuf[w_slot].astype(BF16)
        wdb[...] = wdf[w_slot].astype(BF16)

    @pl.when(active)
    def _():
        row_chunks(gather_row, i, slot, wait=True)
        row_chunks(gather_row, i + 1, 1 - slot, wait=False)

    @pl.when(active & (i >= 2))
    def _():
        row_chunks(scatter_row, i - 2, slot, wait=True)

    @pl.when(active)
    def _():
        x = _unpack_rows(xbuf, slot * buf_rows, MOE_BLOCK).astype(BF16)
        a = jnp.dot(x, wgb[...], preferred_element_type=F32)
        b = jnp.dot(x, wub[...], preferred_element_type=F32)
        act = (a * _sigmoid(a) * b).astype(BF16)
        y = jnp.dot(act, wdb[...], preferred_element_type=F32)
        _pack_rows(ybuf, slot * buf_rows, y.astype(BF16).astype(F32))
        row_chunks(scatter_row, i, slot, wait=False)

    @pl.when(i == n_used - 1)
    def _():
        row_chunks(scatter_row, i, slot, wait=True)

    @pl.when((i == n_used - 1) & (i >= 1))
    def _():
        row_chunks(scatter_row, i - 1, 1 - slot, wait=True)


def _moe_experts(xn, sched, slot_asg, n_used, nblk, w_gate, w_up, w_down, layer):
    n_tok = xn.shape[0] // PACK_SLABS
    buf_rows = 2 * MOE_BLOCK * PACK_SLABS
    any_spec = pl.BlockSpec(memory_space=pl.ANY)
    grid_spec = pltpu.PrefetchScalarGridSpec(
        num_scalar_prefetch=3, grid=(nblk,),
        in_specs=[any_spec, any_spec, any_spec, any_spec], out_specs=any_spec,
        scratch_shapes=[pltpu.VMEM((buf_rows, LANE), jnp.uint32), pltpu.VMEM((buf_rows, LANE), jnp.uint32),
                        pltpu.VMEM((2, D_MODEL, MOE_FF), F32), pltpu.VMEM((2, D_MODEL, MOE_FF), F32),
                        pltpu.VMEM((2, MOE_FF, D_MODEL), F32),
                        pltpu.VMEM((D_MODEL, MOE_FF), BF16), pltpu.VMEM((D_MODEL, MOE_FF), BF16),
                        pltpu.VMEM((MOE_FF, D_MODEL), BF16),
                        pltpu.SemaphoreType.DMA((2,)), pltpu.SemaphoreType.DMA((2,)), pltpu.SemaphoreType.DMA((2,))])
    return pl.pallas_call(
        functools.partial(_moe_kernel, n_tok=n_tok, layer=layer), grid_spec=grid_spec,
        out_shape=jax.ShapeDtypeStruct(((MOE_TOPK * n_tok + 2 * MOE_BLOCK) * PACK_SLABS, LANE), jnp.uint32),
        compiler_params=_cparams(("arbitrary",), 56), name="moe_experts",
    )(sched, slot_asg, n_used, xn, w_gate, w_up, w_down)


def _ple_kernel(h1_ref, y0_ref, y1_ref, rg_ref, p_ref, gp_ref, wg_ref, wp_ref, gfin_ref, o_ref, *, final):
    rg = rg_ref[...]
    tm = rg.shape[0]
    ffn = rg[:, 0:1] * _unpack_rows(y0_ref, 0, tm) + rg[:, 1:2] * _unpack_rows(y1_ref, 0, tm)
    h2 = h1_ref[...] + ffn
    hn = _rms(h2, gp_ref[...]).astype(BF16)
    gate = _sigmoid(jnp.dot(hn, wg_ref[...], preferred_element_type=F32))
    pe = jnp.dot(p_ref[...].astype(BF16), wp_ref[...], preferred_element_type=F32)
    h3 = h2 + pe * gate
    o_ref[...] = _rms(h3, gfin_ref[...]) if final else h3


def _ple(h1, y2, rg, p, gp, wg, wp, gfin, final, layer, tm=256):
    rows = h1.shape[0]
    n_t = rows // tm

    def rspec(n):
        return pl.BlockSpec((tm, n), lambda i: (i, 0))

    def yspec(k):
        return pl.BlockSpec((tm * PACK_SLABS, LANE), lambda i: (k * n_t + i, 0))

    return pl.pallas_call(
        functools.partial(_ple_kernel, final=final), grid=(n_t,),
        in_specs=[rspec(D_MODEL), yspec(0), yspec(1), rspec(ROUTE_W),
                  pl.BlockSpec((None, tm, PLE_DIM), lambda i: (layer, i, 0)), _param(gp, layer), _param(wg, layer),
                  _param(wp, layer), _full(gfin)],
        out_specs=rspec(D_MODEL), out_shape=jax.ShapeDtypeStruct((rows, D_MODEL), F32),
        compiler_params=_cparams(("parallel",), 48), name="moe_combine_ple",
    )(h1, y2, y2, rg, p, gp, wg, wp, gfin)


def _rope_tables(positions, rot_dim, lead, period, reps):
    half = rot_dim // 2
    inv_freq = ROPE_THETA ** (-jnp.arange(0, rot_dim, 2, dtype=F32) / rot_dim)
    ang = positions.astype(F32).reshape(-1, 1) * inv_freq
    cos, sin = jnp.cos(ang), jnp.sin(ang)
    n = cos.shape[0]
    tail = period - lead - rot_dim
    c = jnp.concatenate([jnp.ones((n, lead), F32), cos, cos, jnp.ones((n, tail), F32)], axis=1)
    s1 = jnp.concatenate([jnp.zeros((n, lead), F32), -sin, jnp.zeros((n, half + tail), F32)], axis=1)
    s2 = jnp.concatenate([jnp.zeros((n, lead + half), F32), sin, jnp.zeros((n, tail), F32)], axis=1)
    return tuple(jnp.tile(t, (1, reps)) for t in (c, s1, s2))


def _pad_last(a, lo, hi):
    return jnp.pad(a, [(0, 0)] * (a.ndim - 1) + [(lo, hi)])


def _pad_rows(a, hi):
    return jnp.pad(a, [(0, 0)] * (a.ndim - 2) + [(0, hi), (0, 0)])


def _split_cols(w, sizes):
    out, o = [], 0
    for n in sizes:
        out.append(w[..., o:o + n])
        o += n
    return out


def _in_weights(w_in):
    cb, cc, ch, su, dq, dk, dv, cq, ckv, kr = _split_cols(
        w_in, (CONV_DIM, CONV_DIM, CONV_DIM, SSM_DIM, 3 * DIL_GW, 3 * DIL_GW, 3 * DIL_GW, MLA_Q_RANK, MLA_KV_RANK,
               MLA_ROPE))
    ws = [cb, cc, ch, _pad_last(su, 0, SSM_PAD - SSM_DIM)]
    kinds = ["rows", "rows", "rows", "time_major"]
    for g in range(len(DIL_PATTERNS)):
        ws += [_pad_last(m[..., g * DIL_GW:(g + 1) * DIL_GW], 0, DIL_PLANES * LANE - DIL_GW) for m in (dq, dk, dv)]
        kinds += ["planes_rope_q", "planes_rope_k", "planes"]
    ws += [cq, ckv, _pad_last(kr, MLA_NOPE, MLA_HW - MLA_NOPE - MLA_ROPE)]
    kinds += ["mla_cq", "mla_ckv", "mla_kr"]
    return [w.astype(BF16) for w in ws], kinds


def _mla_weights(w_uq, w_ukv):
    depth = w_uq.shape[0]
    q = w_uq.reshape(depth, MLA_Q_RANK, MLA_HEADS, MLA_QK)
    wq = _pad_last(q, 0, MLA_HW - MLA_QK).reshape(depth, MLA_Q_RANK, MLA_HEADS * MLA_HW)
    kv = w_ukv.reshape(depth, MLA_KV_RANK, MLA_HEADS, MLA_NOPE + MLA_V)
    wk = _pad_last(kv[..., :MLA_NOPE], 0, MLA_HW - MLA_NOPE).reshape(depth, MLA_KV_RANK, -1)
    wv = _pad_last(kv[..., MLA_NOPE:], 0, MLA_HW - MLA_V).reshape(depth, MLA_KV_RANK, -1)
    return wq.astype(BF16), wk.astype(BF16), wv.astype(BF16)


def _out_weights(w_out):
    o_ssm, o_dil, o_mla = CONV_DIM, CONV_DIM + SSM_DIM, CONV_DIM + SSM_DIM + 3 * DIL_GW
    ws = [w_out[:, :o_ssm], _pad_rows(w_out[:, o_ssm:o_dil], SSM_PAD - SSM_DIM)]
    ws += [jnp.concatenate([_pad_rows(w_out[:, o_dil + g * DIL_GW:o_dil + (g + 1) * DIL_GW],
                                      DIL_PLANES * LANE - DIL_GW) for g in range(3)], axis=1)]
    ws += [_pad_rows(w_out[:, o_mla:], MLA_VW - MLA_HEADS * MLA_V)]
    return [w.astype(BF16) for w in ws]


def _router_weights(w_group, b_group, w_router, b_router):
    w = _pad_last(jnp.concatenate([w_group, w_router], axis=-1), 0, ROUTE_W - MOE_GROUPS - N_EXPERTS)
    hi = w.astype(BF16)
    lo = (w - hi.astype(F32)).astype(BF16)
    b = _pad_last(jnp.concatenate([b_group, b_router], axis=-1), 0, ROUTE_W - MOE_GROUPS - N_EXPERTS)
    return jnp.concatenate([hi, lo], axis=-1), b[:, None, :]


def kernel(x, p, positions, norm_mix_g, w_in, conv_w, ssm_lam_re, ssm_lam_im, ssm_log_dt, ssm_b_re, ssm_b_im, ssm_c_re, ssm_c_im, ssm_d, ssm_glu_w, ssm_glu_b, mla_q_norm_g, mla_w_uq, mla_kv_norm_g, mla_w_ukv, w_out, norm_ffn_g, w_group, b_group, w_router, b_router, moe_w_gate, moe_w_up, moe_w_down, norm_ple_g, ple_w_proj, ple_w_gate, final_norm_g):
    batch, seq, _ = x.shape
    n_tok = batch * seq
    depth = w_in.shape[0]
    dil_tabs = _rope_tables(positions, DIL_ROT, 0, DIL_HEAD_DIM, LANE // DIL_HEAD_DIM)
    mla_tabs = _rope_tables(positions, MLA_ROPE, MLA_NOPE, MLA_HW, 1)
    h = x.reshape(n_tok, D_MODEL)
    gains = lambda g: g[:, None, :]
    in_ws, in_kinds = _in_weights(w_in)
    mla_prm = (gains(mla_q_norm_g), gains(mla_kv_norm_g), *_mla_weights(mla_w_uq, mla_w_ukv))
    ssm_prm = _ssm_params(ssm_lam_re, ssm_lam_im, ssm_log_dt, ssm_b_re, ssm_b_im, ssm_c_re, ssm_c_im, ssm_d,
                          ssm_glu_w, ssm_glu_b)
    out_ws = _out_weights(w_out)
    wr, br = _router_weights(w_group, b_group, w_router, b_router)
    ple_wg, ple_wp = ple_w_gate.astype(BF16), ple_w_proj.astype(BF16)
    p_rows = p.reshape(depth, n_tok, PLE_DIM)
    for i in range(depth):
        z = _in_proj(h, gains(norm_mix_g), dil_tabs, mla_tabs, mla_prm, in_ws, in_kinds, batch, seq, i)
        cb, cc, ch, su = z[:4]
        q, k, v = z[13:]
        y_conv = _conv_mixer(cb, cc, ch, conv_w, batch, seq, i)
        y_ssm = _ssm_mixer(su, ssm_prm, batch, seq, i)
        outs, lses = [], []
        for g, (window, dil) in enumerate(DIL_PATTERNS):
            assert window // dil == DIL_BLK and (seq // dil) % DIL_BLK == 0
            o, l = _dil_group(z[4 + 3 * g], z[5 + 3 * g], z[6 + 3 * g], dil, batch, seq)
            outs.append(o)
            lses.append(l)
        y_mla = _mla_attn(q, k, v, batch, seq)
        h1, xn, route_i, route_g = _out_proj(
            (y_conv, y_ssm, outs, lses, y_mla), h, out_ws, gains(norm_ffn_g), wr, br, batch, seq, i)
        sched, slot_asg, n_used, nblk = _moe_plan(route_i, n_tok)
        y2 = _moe_experts(xn, sched, slot_asg, n_used, nblk, moe_w_gate, moe_w_up, moe_w_down, i)
        h = _ple(h1, y2, route_g, p_rows, gains(norm_ple_g), ple_wg, ple_wp, final_norm_g.reshape(1, -1),
                 final=(i == depth - 1), layer=i)
    return h.reshape(batch, seq, D_MODEL)
```

```python
import functools
import math

import jax
import jax.numpy as jnp
from jax import lax
from jax.experimental import pallas as pl
from jax.experimental.pallas import tpu as pltpu

D_MODEL = 2048
PLE_DIM = 256
ROPE_THETA = 500000.0
NORM_EPS = 1e-6

CONV_DIM = 448
CONV_K = 3

SSM_DIM = 448
SSM_GROUP = 16
SSM_GROUPS = SSM_DIM // SSM_GROUP
SSM_STATE = 64
SSM_N = SSM_GROUPS * SSM_STATE
SSM_PAD = 512

DIL_HEAD_DIM = 64
DIL_ROT = DIL_HEAD_DIM // 4
DIL_PATTERNS = ((128, 1), (512, 4), (2048, 16))
DIL_HPG = 3
DIL_GW = DIL_HPG * DIL_HEAD_DIM
DIL_PLANES = 2
DIL_BLK = 128
DIL_UNROLL = 4

MLA_HEADS = 9
MLA_Q_RANK = 384
MLA_KV_RANK = 256
MLA_NOPE = 64
MLA_ROPE = 32
MLA_V = 64
MLA_QK = MLA_NOPE + MLA_ROPE
MLA_HW = 128
MLA_VW = 640

MOE_GROUPS = 8
MOE_EPG = 8
N_EXPERTS = MOE_GROUPS * MOE_EPG
MOE_TOPK = 2
MOE_FF = 512
MOE_BLOCK = 128
ROUTE_W = 128

LANE = 128
PACK_SLABS = D_MODEL // 2 // LANE
ROW_CHUNK = 32
SLOT_SHIFT = 15
NEG_BIG = -1e30

BF16 = jnp.bfloat16
F32 = jnp.float32


def _cparams(sem, vmem_mb):
    return pltpu.CompilerParams(dimension_semantics=sem, vmem_limit_bytes=vmem_mb * 1024 * 1024)


def _rms(x, g):
    ms = jnp.mean(x * x, axis=-1, keepdims=True)
    return (x * lax.rsqrt(ms + NORM_EPS)) * g


def _sigmoid(x):
    return 1.0 / (1.0 + jnp.exp(-x))


def _full(a):
    return pl.BlockSpec(a.shape, lambda *_: (0,) * a.ndim)


def _param(a, layer):
    return pl.BlockSpec((None,) + a.shape[1:], lambda *_: (layer,) + (0,) * (a.ndim - 1),
                        pipeline_mode=pl.Buffered(1))


_HI16 = 0xFFFF0000


def _pack_rows(ref, base, x):
    n = x.shape[0]
    bits = lax.bitcast_convert_type(x, jnp.uint32)
    for c in range(PACK_SLABS):
        lo = bits[:, c * LANE:(c + 1) * LANE] >> 16
        hi = bits[:, D_MODEL // 2 + c * LANE:D_MODEL // 2 + (c + 1) * LANE] & jnp.uint32(_HI16)
        ref[pl.ds(base + c, n, stride=PACK_SLABS), :] = lo | hi


def _unpack_rows(ref, base, n):
    lo, hi = [], []
    for c in range(PACK_SLABS):
        w = ref[pl.ds(base + c, n, stride=PACK_SLABS), :]
        lo.append(lax.bitcast_convert_type(w << 16, F32))
        hi.append(lax.bitcast_convert_type(w & jnp.uint32(_HI16), F32))
    return jnp.concatenate(lo + hi, axis=1)


def _in_proj_kernel(x_ref, g_ref, dc_ref, ds1_ref, ds2_ref, mc_ref, ms1_ref, ms2_ref, gq_ref, gkv_ref,
                    wq_ref, wk_ref, wv_ref, *refs, kinds):
    n_out = len(kinds)
    w_refs, o_refs = refs[:n_out], refs[n_out:]
    xn = _rms(x_ref[...], g_ref[...]).astype(BF16)
    tabs = [jnp.concatenate([t[...]] * DIL_PLANES, axis=1) for t in (dc_ref, ds1_ref, ds2_ref)]
    latent = {}
    outs = iter(o_refs)
    for kind, w_ref in zip(kinds, w_refs):
        res = jnp.dot(xn, w_ref[...], preferred_element_type=F32)
        if kind.startswith("mla"):
            latent[kind] = res
            continue
        o_ref = next(outs)
        if kind.startswith("planes"):
            if kind != "planes":
                res = _rope_lanes(res, *tabs, DIL_ROT // 2, DIL_PLANES * LANE)
            if kind == "planes_rope_q":
                res = res * (1.0 / math.sqrt(DIL_HEAD_DIM))
            for j in range(o_ref.shape[0]):
                o_ref[j] = res[:, j * LANE:(j + 1) * LANE]
        else:
            o_ref[...] = res
    q_ref, k_ref, v_ref = outs
    rope = functools.partial(_rope_lanes, c=mc_ref[...], s1=ms1_ref[...], s2=ms2_ref[...], half=MLA_ROPE // 2,
                             width=MLA_HW)
    qn = _rms(latent["mla_cq"], gq_ref[...]).astype(BF16)
    kvn = _rms(latent["mla_ckv"], gkv_ref[...]).astype(BF16)
    q = jnp.dot(qn, wq_ref[...], preferred_element_type=F32)
    kn = jnp.dot(kvn, wk_ref[...], preferred_element_type=F32)
    lane = lax.broadcasted_iota(jnp.int32, (1, MLA_HEADS * MLA_HW), 1)
    ones_col = (lane % MLA_HW == MLA_V).astype(F32)
    v_ref[...] = (jnp.dot(kvn, wv_ref[...], preferred_element_type=F32) + ones_col).astype(BF16)
    kr = rope(latent["mla_kr"])
    scale = math.log2(math.e) / math.sqrt(MLA_QK)
    for h in range(MLA_HEADS):
        sl = slice(h * MLA_HW, (h + 1) * MLA_HW)
        q_ref[:, sl] = (rope(q[:, sl]) * scale).astype(BF16)
        k_ref[:, sl] = (kn[:, sl] + kr).astype(BF16)


def _in_proj(h, g, dil_tabs, mla_tabs, mla_prm, weights, kinds, batch, seq, layer, tm=256):
    n_s = seq // tm

    def rspec(n):
        return pl.BlockSpec((tm, n), lambda b, i: (b * n_s + i, 0))

    in_specs = [rspec(D_MODEL), _param(g, layer)] + [rspec(LANE)] * 6 + [_param(a, layer) for a in mla_prm]
    in_specs += [_param(w, layer) for w in weights]
    out_shape, out_specs = [], []
    for kind, w in zip(kinds, weights):
        n = w.shape[-1]
        if kind.startswith("mla"):
            continue
        if kind == "time_major":
            out_shape.append(jax.ShapeDtypeStruct((seq, batch * n), F32))
            out_specs.append(pl.BlockSpec((tm, n), lambda b, i: (i, b)))
        elif kind.startswith("planes"):
            out_shape.append(jax.ShapeDtypeStruct((n // LANE, batch * seq, LANE), F32))
            out_specs.append(pl.BlockSpec((n // LANE, tm, LANE), lambda b, i: (0, b * n_s + i, 0)))
        else:
            out_shape.append(jax.ShapeDtypeStruct((batch * seq, n), F32))
            out_specs.append(rspec(n))
    for n in (MLA_HEADS * MLA_HW,) * 3:
        out_shape.append(jax.ShapeDtypeStruct((batch * seq, n), BF16))
        out_specs.append(rspec(n))
    return pl.pallas_call(
        functools.partial(_in_proj_kernel, kinds=tuple(kinds)),
        grid=(batch, n_s), in_specs=in_specs, out_specs=out_specs, out_shape=out_shape,
        compiler_params=_cparams(("parallel", "parallel"), 56), name="in_proj",
    )(h, g, *dil_tabs, *mla_tabs, *mla_prm, *weights)


def _conv_kernel(cb_ref, cc_ref, ch_ref, w_ref, o_ref, u_ref, *, rows):
    seq = cb_ref.shape[0]
    u_ref[0:8, :] = jnp.zeros((8, CONV_DIM), F32)
    for r0 in range(0, seq, rows):
        u_ref[8 + r0:8 + r0 + rows, :] = cc_ref[r0:r0 + rows, :] * ch_ref[r0:r0 + rows, :]
    w0, w1, w2 = w_ref[0:1, :], w_ref[1:2, :], w_ref[2:3, :]
    for r0 in range(0, seq, rows):
        acc = (w2 * u_ref[8 + r0:8 + r0 + rows, :] + w1 * u_ref[7 + r0:7 + r0 + rows, :]
               + w0 * u_ref[6 + r0:6 + r0 + rows, :])
        o_ref[r0:r0 + rows, :] = cb_ref[r0:r0 + rows, :] * acc


def _conv_mixer(cb, cc, ch, conv_w, batch, seq, layer):
    spec = pl.BlockSpec((seq, CONV_DIM), lambda b: (b, 0))
    return pl.pallas_call(
        functools.partial(_conv_kernel, rows=256),
        grid=(batch,), in_specs=[spec, spec, spec, _param(conv_w, layer)], out_specs=spec,
        out_shape=jax.ShapeDtypeStruct((batch * seq, CONV_DIM), F32),
        scratch_shapes=[pltpu.VMEM((seq + 8, CONV_DIM), F32)],
        compiler_params=_cparams(("parallel",), 48), name="conv_mixer",
    )(cb, cc, ch, conv_w)


SSM_NP = SSM_N // LANE
SSM_SCAN_PLANES = 4
_SSM_SPLIT = ((0, 256, 0, 1024), (256, 256, 1024, 768))


def _gelu_tanh(x):
    return 0.5 * x * (1.0 + jnp.tanh(math.sqrt(2.0 / math.pi) * (x + 0.044715 * (x * x * x))))


def _ssm_kernel(u_ref, bf0_ref, bf1_ref, are_ref, aim_ref, cc0_ref, cc1_ref, d_ref, gw_ref, gb_ref, o_ref,
                xs_ref, st_ref, *, batch, tc):
    per_tile = 8 // batch
    npk = SSM_NP // per_tile

    @pl.when(pl.program_id(0) == 0)
    def _():
        st_ref[...] = jnp.zeros_like(st_ref)

    def plane_rows(p, b):
        return p % npk, pl.ds((p // npk) * batch + b, tc, stride=8)

    for b in range(batch):
        for (l0, lw, s0, sw), bf_ref in zip(_SSM_SPLIT, (bf0_ref, bf1_ref)):
            ub = u_ref[:, b * SSM_PAD + l0:b * SSM_PAD + l0 + lw].astype(BF16)
            r = jnp.dot(ub, bf_ref[...], preferred_element_type=F32)
            for q in range(sw // LANE):
                pk, rows = plane_rows(s0 // LANE + q, b)
                xs_ref[pk, rows, :] = r[:, q * LANE:(q + 1) * LANE]
                xs_ref[npk + pk, rows, :] = r[:, sw + q * LANE:sw + (q + 1) * LANE]

    def decay(a_ref, pk):
        return jnp.concatenate([jnp.broadcast_to(a_ref[:, (pk + j * npk) * LANE:(pk + j * npk + 1) * LANE],
                                                 (batch, LANE)) for j in range(per_tile)], axis=0)

    for p0 in range(0, npk, SSM_SCAN_PLANES):
        pks = list(range(p0, min(p0 + SSM_SCAN_PLANES, npk)))
        ar = [decay(are_ref, pk) for pk in pks]
        ai = [decay(aim_ref, pk) for pk in pks]

        def body(t, carry, pks=pks, ar=ar, ai=ai):
            rows = pl.ds(pl.multiple_of(t * 8, 8), 8)
            new = []
            for idx, pk in enumerate(pks):
                sr, si = carry[idx]
                sr, si = (ar[idx] * sr - ai[idx] * si + xs_ref[pk, rows, :],
                          ar[idx] * si + ai[idx] * sr + xs_ref[npk + pk, rows, :])
                xs_ref[pk, rows, :] = sr
                xs_ref[npk + pk, rows, :] = si
                new.append((sr, si))
            return tuple(new)

        init = tuple((st_ref[pk], st_ref[npk + pk]) for pk in pks)
        for pk, (sr, si) in zip(pks, lax.fori_loop(0, tc, body, init, unroll=2)):
            st_ref[pk] = sr
            st_ref[npk + pk] = si

    for b in range(batch):
        ys = []
        for (l0, lw, s0, sw), cc_ref in zip(_SSM_SPLIT, (cc0_ref, cc1_ref)):
            planes = [plane_rows(p, b) for p in range(s0 // LANE, (s0 + sw) // LANE)]
            xb = jnp.concatenate([xs_ref[pk, rows, :] for pk, rows in planes]
                                 + [xs_ref[npk + pk, rows, :] for pk, rows in planes], axis=1)
            ys.append(jnp.dot(xb.astype(BF16), cc_ref[...], preferred_element_type=F32))
        y = jnp.concatenate(ys, axis=1) + d_ref[...] * u_ref[:, b * SSM_PAD:(b + 1) * SSM_PAD]
        g = _gelu_tanh(y)
        o_ref[:, b * SSM_PAD:(b + 1) * SSM_PAD] = g * _sigmoid(
            jnp.dot(g.astype(BF16), gw_ref[...], preferred_element_type=F32) + gb_ref[...])


def _ssm_mixer(su, prm, batch, seq, layer, tc=256):
    spec = pl.BlockSpec((tc, batch * SSM_PAD), lambda c: (c, 0))
    args = (prm["bf0"], prm["bf1"], prm["a_re"], prm["a_im"], prm["cc0"], prm["cc1"], prm["d"], prm["gw"], prm["gb"])
    return pl.pallas_call(
        functools.partial(_ssm_kernel, batch=batch, tc=tc),
        grid=(seq // tc,), in_specs=[spec] + [_param(a, layer) for a in args], out_specs=spec,
        out_shape=jax.ShapeDtypeStruct((seq, batch * SSM_PAD), F32),
        scratch_shapes=[pltpu.VMEM((2 * SSM_NP * batch // 8, tc * 8, LANE), F32),
                        pltpu.VMEM((2 * SSM_NP * batch // 8, 8, LANE), F32)],
        compiler_params=_cparams(("arbitrary",), 48), name="ssm_mixer",
    )(su, *args)


def _ssm_params(lam_re, lam_im, log_dt, b_re, b_im, c_re, c_im, d_skip, glu_w, glu_b):
    depth = lam_re.shape[0]
    dt = jnp.exp(log_dt)[..., None]
    mag = jnp.exp(lam_re * dt)
    a_re = mag * jnp.cos(lam_im * dt)
    a_im = mag * jnp.sin(lam_im * dt)
    nr, ni = a_re - 1.0, a_im
    den = lam_re * lam_re + lam_im * lam_im
    f_re = (nr * lam_re + ni * lam_im) / den
    f_im = (ni * lam_re - nr * lam_im) / den
    bfr = f_re[..., None] * b_re - f_im[..., None] * b_im
    bfi = f_re[..., None] * b_im + f_im[..., None] * b_re
    same_group = (jnp.arange(SSM_DIM)[:, None] // SSM_GROUP == jnp.arange(SSM_N)[None, :] // SSM_STATE).astype(F32)

    def in_blockdiag(m):
        cols = jnp.transpose(m, (0, 3, 1, 2)).reshape(depth, SSM_GROUP, SSM_N)
        return jnp.tile(cols, (1, SSM_GROUPS, 1)) * same_group

    def out_blockdiag(m):
        rows = jnp.transpose(m, (0, 1, 3, 2)).reshape(depth, SSM_N, SSM_GROUP)
        return jnp.tile(rows, (1, 1, SSM_GROUPS)) * same_group.T

    pad = SSM_PAD - SSM_DIM
    b_re_d, b_im_d = (_pad_rows(in_blockdiag(m), pad) for m in (bfr, bfi))
    c_re_d, c_im_d = (_pad_last(out_blockdiag(m), 0, pad) for m in (c_re, -c_im))
    prm = {}
    for c, (l0, lw, s0, sw) in enumerate(_SSM_SPLIT):
        prm[f"bf{c}"] = jnp.concatenate([b_re_d[:, l0:l0 + lw, s0:s0 + sw], b_im_d[:, l0:l0 + lw, s0:s0 + sw]],
                                        axis=2).astype(BF16)
        prm[f"cc{c}"] = jnp.concatenate([c_re_d[:, s0:s0 + sw, l0:l0 + lw], c_im_d[:, s0:s0 + sw, l0:l0 + lw]],
                                        axis=1).astype(BF16)
    return {
        **prm,
        "a_re": a_re.reshape(depth, 1, SSM_N), "a_im": a_im.reshape(depth, 1, SSM_N),
        "d": _pad_last(d_skip, 0, pad)[:, None, :],
        "gw": _pad_last(_pad_rows(glu_w, pad), 0, pad).astype(BF16),
        "gb": _pad_last(glu_b, 0, pad)[:, None, :],
    }


def _rope_lanes(x, c, s1, s2, half, width):
    return x * c + pltpu.roll(x, width - half, 1) * s1 + pltpu.roll(x, half, 1) * s2


def _dil_kernel(q_ref, k_ref, v_ref, o_ref, l_ref, *, dil):
    seq = q_ref.shape[1]
    nb = (seq // dil) // DIL_BLK
    width = DIL_PLANES * LANE
    lane = lax.broadcasted_iota(jnp.int32, (1, width), 1)
    cmask = [(lane // DIL_HEAD_DIM == c).astype(F32) for c in range(DIL_HPG)]
    qi = lax.broadcasted_iota(jnp.int32, (DIL_HPG * DIL_BLK, 2 * DIL_BLK), 0) % DIL_BLK
    ki = lax.broadcasted_iota(jnp.int32, (DIL_HPG * DIL_BLK, 2 * DIL_BLK), 1)
    band = (ki >= qi) & (ki <= qi + DIL_BLK)

    def planes(ref, rows):
        return jnp.concatenate([ref[j, rows, :] for j in range(DIL_PLANES)], axis=1)

    def rows_of(m, blk):
        if dil == 1:
            return pl.ds(pl.multiple_of(blk * DIL_BLK, DIL_BLK), DIL_BLK)
        return pl.ds(blk * DIL_BLK * dil + m, DIL_BLK, stride=dil)

    def blocks(it, carry):
        ids = [it * DIL_UNROLL + u for u in range(DIL_UNROLL)]
        mis = [(idx // nb, idx % nb) for idx in ids]
        rows_q = [rows_of(m, i) for m, i in mis]
        rows_p = [rows_of(m, jnp.maximum(i - 1, 0)) for m, i in mis]
        scores, vcats = [], []
        for rq, rp in zip(rows_q, rows_p):
            q = planes(q_ref, rq)
            kcat = jnp.concatenate([planes(k_ref, rp), planes(k_ref, rq)], axis=0).astype(BF16)
            vcats.append(jnp.concatenate([planes(v_ref, rp), planes(v_ref, rq)], axis=0).astype(BF16))
            qs = jnp.concatenate([q * cm for cm in cmask], axis=0).astype(BF16)
            scores.append(lax.dot_general(qs, kcat, (((1,), (1,)), ((), ())), preferred_element_type=F32))
        soft = []
        for s, (m, i) in zip(scores, mis):
            s = jnp.where(band & ((ki >= DIL_BLK) | (i > 0)), s, NEG_BIG)
            mx = jnp.max(s, axis=-1, keepdims=True)
            p = jnp.exp(s - mx)
            den = jnp.sum(p, axis=-1, keepdims=True)
            soft.append((p.astype(BF16), den, mx + jnp.log(den)))
        for (p, den, lse), vcat, rq in zip(soft, vcats, rows_q):
            o = jnp.dot(p, vcat, preferred_element_type=F32) / den
            out = jnp.zeros((DIL_BLK, width), F32)
            lout = jnp.zeros((DIL_BLK, width), F32)
            for c, cm in enumerate(cmask):
                out = out + cm * o[c * DIL_BLK:(c + 1) * DIL_BLK, :]
                lout = lout + cm * lse[c * DIL_BLK:(c + 1) * DIL_BLK, :]
            for j in range(DIL_PLANES):
                o_ref[j, rq, :] = out[:, j * LANE:(j + 1) * LANE]
                l_ref[j, rq, :] = lout[:, j * LANE:(j + 1) * LANE]
        return carry

    lax.fori_loop(0, dil * nb // DIL_UNROLL, blocks, 0)


def _dil_group(q, k, v, dil, batch, seq):
    spec = pl.BlockSpec((DIL_PLANES, seq, LANE), lambda b: (0, b, 0))
    sds = jax.ShapeDtypeStruct((DIL_PLANES, batch * seq, LANE), F32)
    return pl.pallas_call(
        functools.partial(_dil_kernel, dil=dil), grid=(batch,), in_specs=[spec] * 3,
        out_specs=[spec, spec], out_shape=[sds, sds],
        compiler_params=_cparams(("parallel",), 48), name=f"dil_attn_d{dil}",
    )(q, k, v)


def _mla_attn_kernel(q_ref, k_ref, v_ref, o_ref, *, tq, group):
    qi = pl.program_id(1)
    row = lax.broadcasted_iota(jnp.int32, (tq, tq), 0)
    col = lax.broadcasted_iota(jnp.int32, (tq, tq), 1)
    causal = col <= row

    def heads_out(heads):
        qs = [q_ref[:, h * MLA_HW:(h + 1) * MLA_HW] for h in heads]

        def step(kj, carry, masked):
            rows = pl.ds(pl.multiple_of(kj * tq, tq), tq)
            scores = [lax.dot_general(q, k_ref[rows, h * MLA_HW:(h + 1) * MLA_HW], (((1,), (1,)), ((), ())),
                                      preferred_element_type=F32) for h, q in zip(heads, qs)]
            stats = []
            for s, (m, acc) in zip(scores, carry):
                if masked:
                    s = jnp.where(causal, s, NEG_BIG)
                mn = jnp.maximum(m, jnp.max(s, axis=-1, keepdims=True))
                stats.append((mn, jnp.exp2(m - mn), jnp.exp2(s - mn).astype(BF16)))
            out = []
            for h, (mn, alpha, p), (_, acc) in zip(heads, stats, carry):
                v = v_ref[rows, h * MLA_HW:(h + 1) * MLA_HW]
                out.append((mn, alpha * acc + jnp.dot(p, v, preferred_element_type=F32)))
            return tuple(out)

        init = tuple((jnp.full((tq, 1), NEG_BIG, F32), jnp.zeros((tq, MLA_HW), F32)) for _ in heads)
        carry = lax.fori_loop(0, qi, functools.partial(step, masked=False), init)
        return [acc / acc[:, MLA_V:MLA_V + 1] for _, acc in step(qi, carry, True)]

    for h0 in range(0, MLA_HEADS, group):
        heads = list(range(h0, min(h0 + group, MLA_HEADS)))
        for h, o in zip(heads, heads_out(heads)):
            lo = h * MLA_V
            if h % 2:
                o_ref[:, lo:lo + MLA_V] = pltpu.roll(o, MLA_V, 1)[:, MLA_V:2 * MLA_V]
            else:
                o_ref[:, lo:lo + MLA_V] = o[:, 0:MLA_V]
    if MLA_HEADS % 2:
        o_ref[:, MLA_HEADS * MLA_V:] = jnp.zeros((tq, MLA_VW - MLA_HEADS * MLA_V), F32)


def _mla_attn(q, k, v, batch, seq, tq=256, group=9):
    hw = MLA_HEADS * MLA_HW
    n_q = seq // tq
    return pl.pallas_call(
        functools.partial(_mla_attn_kernel, tq=tq, group=group), grid=(batch, n_q),
        in_specs=[pl.BlockSpec((tq, hw), lambda b, i: (b * n_q + i, 0)),
                  pl.BlockSpec((seq, hw), lambda b, i: (b, 0)),
                  pl.BlockSpec((seq, hw), lambda b, i: (b, 0))],
        out_specs=pl.BlockSpec((tq, MLA_VW), lambda b, i: (b * n_q + i, 0)),
        out_shape=jax.ShapeDtypeStruct((batch * seq, MLA_VW), F32),
        compiler_params=_cparams(("parallel", "arbitrary"), 48), name="mla_attn",
    )(q, k, v)


def _out_proj_kernel(yc_ref, ys_ref, od0_ref, od1_ref, od2_ref, ld0_ref, ld1_ref, ld2_ref, ym_ref, h_ref,
                     wc_ref, ws_ref, wd_ref, wm_ref,
                     gf_ref, wr_ref, br_ref,
                     h1_ref, xn_ref, ri_ref, rg_ref):
    def mm(y, w_ref):
        return jnp.dot(y.astype(BF16), w_ref[...], preferred_element_type=F32)

    y_dil = [None] * (3 * DIL_PLANES)
    for j in range(DIL_PLANES):
        la, lb, lc = ld0_ref[j], ld1_ref[j], ld2_ref[j]
        mx = jnp.maximum(jnp.maximum(la, lb), lc)
        ea, eb, ec = jnp.exp(la - mx), jnp.exp(lb - mx), jnp.exp(lc - mx)
        inv = 1.0 / (ea + eb + ec)
        for g, (o_ref, e) in enumerate(((od0_ref, ea), (od1_ref, eb), (od2_ref, ec))):
            y_dil[g * DIL_PLANES + j] = o_ref[j] * (e * inv)
    mix = (mm(yc_ref[...], wc_ref) + mm(ys_ref[...], ws_ref) + mm(jnp.concatenate(y_dil, axis=1), wd_ref)
           + mm(ym_ref[...], wm_ref))
    h1 = h_ref[...] + mix
    h1_ref[...] = h1
    hn = _rms(h1, gf_ref[...])
    hi = hn.astype(BF16)
    hi32 = hi.astype(F32)
    lo = (hn - hi32).astype(BF16)
    _pack_rows(xn_ref, 0, hi32)
    r_hi = jnp.dot(hi, wr_ref[...], preferred_element_type=F32)
    r_lo = jnp.dot(lo, wr_ref[:, 0:ROUTE_W], preferred_element_type=F32)
    logits = r_hi[:, 0:ROUTE_W] + r_hi[:, ROUTE_W:2 * ROUTE_W] + r_lo + br_ref[...]
    lane = lax.broadcasted_iota(jnp.int32, logits.shape, 1)
    gl = jnp.where(lane < MOE_GROUPS, logits, NEG_BIG)
    gmax = jnp.max(gl, axis=-1, keepdims=True)
    g_top_p = 1.0 / jnp.sum(jnp.exp(gl - gmax), axis=-1, keepdims=True)
    g_idx = jnp.min(jnp.where(gl == gmax, lane, ROUTE_W), axis=-1, keepdims=True)
    in_group = (lane >= MOE_GROUPS) & (lane < MOE_GROUPS + N_EXPERTS) & (((lane - MOE_GROUPS) >> 3) == g_idx)
    el = jnp.where(in_group, logits, NEG_BIG)
    v1 = jnp.max(el, axis=-1, keepdims=True)
    i1 = jnp.min(jnp.where(el == v1, lane, ROUTE_W), axis=-1, keepdims=True)
    el2 = jnp.where(lane == i1, NEG_BIG, el)
    v2 = jnp.max(el2, axis=-1, keepdims=True)
    i2 = jnp.min(jnp.where(el2 == v2, lane, ROUTE_W), axis=-1, keepdims=True)
    e2 = jnp.exp(v2 - v1)
    w1 = g_top_p / (1.0 + e2)
    w2 = g_top_p * e2 / (1.0 + e2)
    ri_ref[...] = jnp.where(lane == 0, i1 - MOE_GROUPS, jnp.where(lane == 1, i2 - MOE_GROUPS, 0))
    rg_ref[...] = jnp.where(lane == 0, w1, jnp.where(lane == 1, w2, 0.0))


def _out_proj(ys, h, ws, gf, wr, br, batch, seq, layer, tm=256):
    rows = batch * seq
    n_s = seq // tm
    yc, ys_tm, dil_o, dil_l, ym = ys

    def rspec(n):
        return pl.BlockSpec((tm, n), lambda b, i: (b * n_s + i, 0))

    pspec = pl.BlockSpec((DIL_PLANES, tm, LANE), lambda b, i: (0, b * n_s + i, 0))
    in_specs = [rspec(CONV_DIM), pl.BlockSpec((tm, SSM_PAD), lambda b, i: (i, b))] + [pspec] * 6
    in_specs += [rspec(MLA_VW), rspec(D_MODEL)]
    in_specs += [_param(a, layer) for a in (*ws, gf, wr, br)]
    return pl.pallas_call(
        _out_proj_kernel, grid=(batch, n_s), in_specs=in_specs,
        out_specs=[rspec(D_MODEL), pl.BlockSpec((tm * PACK_SLABS, LANE), lambda b, i: (b * n_s + i, 0)),
                   rspec(ROUTE_W), rspec(ROUTE_W)],
        out_shape=[jax.ShapeDtypeStruct((rows, D_MODEL), F32),
                   jax.ShapeDtypeStruct((rows * PACK_SLABS, LANE), jnp.uint32),
                   jax.ShapeDtypeStruct((rows, ROUTE_W), jnp.int32), jax.ShapeDtypeStruct((rows, ROUTE_W), F32)],
        compiler_params=_cparams(("parallel", "parallel"), 48), name="out_proj_router",
    )(yc, ys_tm, *dil_o, *dil_l, ym, h, *ws, gf, wr, br)


def _moe_plan(route_i, n_tok):
    n_asg = n_tok * MOE_TOPK
    nblk = (n_asg + N_EXPERTS * (MOE_BLOCK - 1) + MOE_BLOCK - 1) // MOE_BLOCK
    flat_e = route_i[:, :MOE_TOPK].reshape(-1)
    onehot = (flat_e[:, None] == jnp.arange(N_EXPERTS, dtype=jnp.int32)[None, :]).astype(F32)
    oh3 = onehot.reshape(n_asg // MOE_BLOCK, MOE_BLOCK, N_EXPERTS)
    within = jnp.einsum("ij,bje->bie", jnp.tril(jnp.ones((MOE_BLOCK, MOE_BLOCK), F32)), oh3)
    chunk_tot = within[:, -1, :]
    n_chunk = chunk_tot.shape[0]
    chunk_off = jnp.dot(jnp.tril(jnp.ones((n_chunk, n_chunk), F32), -1), chunk_tot)
    rank = jnp.sum((within + chunk_off[:, None, :]) * oh3, axis=-1).reshape(n_asg).astype(jnp.int32) - 1
    counts = (chunk_off[-1] + chunk_tot[-1]).astype(jnp.int32)
    blocks_of = (counts + MOE_BLOCK - 1) // MOE_BLOCK
    upto = jnp.arange(N_EXPERTS)[:, None] <= jnp.arange(N_EXPERTS)[None, :]
    blk_end = jnp.sum(jnp.where(upto, blocks_of[:, None], 0), axis=0)
    blk_start = blk_end - blocks_of
    dest = jnp.dot(onehot, blk_start.astype(F32)).astype(jnp.int32) * MOE_BLOCK + rank
    in_blk = (dest[:, None] // MOE_BLOCK == jnp.arange(nblk, dtype=jnp.int32)[None, :]).astype(F32)
    at_off = (dest[:, None] % MOE_BLOCK == jnp.arange(MOE_BLOCK, dtype=jnp.int32)[None, :]).astype(F32)
    asg = jnp.arange(n_asg, dtype=jnp.int32)
    a_hi, a_lo = (asg // MOE_BLOCK).astype(F32)[:, None], (asg % MOE_BLOCK).astype(F32)[:, None]
    inv = lambda v: jnp.einsum("ab,ao->bo", in_blk * v, at_off)
    slot_a = jnp.where(inv(1.0) > 0.5, inv(a_hi) * MOE_BLOCK + inv(a_lo), -1.0).astype(jnp.int32).reshape(-1)
    slot = jnp.arange(nblk * MOE_BLOCK, dtype=jnp.int32)
    dump = MOE_TOPK * n_tok + ((slot // MOE_BLOCK) % 2) * MOE_BLOCK + slot % MOE_BLOCK
    assert MOE_TOPK * n_tok + 2 * MOE_BLOCK <= 1 << SLOT_SHIFT
    slot_asg = jnp.where(slot_a >= 0, ((slot_a >> 1) << SLOT_SHIFT) | ((slot_a & 1) * n_tok + (slot_a >> 1)), dump)
    n_used = blk_end[-1].astype(jnp.int32)
    blk = jnp.arange(nblk, dtype=jnp.int32)
    block_expert = jnp.minimum(jnp.sum(blk_end[None, :] <= blk[:, None], axis=1), N_EXPERTS - 1).astype(jnp.int32)
    mine = block_expert[:, None] == jnp.arange(N_EXPERTS, dtype=jnp.int32)[None, :]
    lookup = lambda table: jnp.sum(jnp.where(mine, table[None, :], 0), axis=1)
    run_start, run_end = lookup(blk_start), lookup(blk_end)
    is_first = ((blk == run_start) & (blk < n_used)).astype(jnp.int32)
    used = (blocks_of > 0).astype(jnp.int32)
    nth_used = jnp.sum(jnp.where(upto, used[:, None], 0), axis=0) - used
    w_slot = lookup(nth_used) % 2
    starts_at = (blk_start[None, :] == run_end[:, None]) & (blocks_of[None, :] > 0)
    nxt = jnp.where(run_end < n_used, jnp.sum(jnp.where(starts_at, jnp.arange(N_EXPERTS)[None, :], 0), axis=1), -1)
    rows_left = jnp.clip(lookup(counts) - (blk - run_start) * MOE_BLOCK, 0, MOE_BLOCK)
    sched = jnp.stack([block_expert, is_first, w_slot, nxt, jnp.where(blk < n_used, rows_left, 0)])
    sched = jnp.pad(sched.astype(jnp.int32), ((0, 0), (0, 1)))
    return sched, slot_asg, n_used.reshape(1), nblk


def _moe_kernel(sched_ref, asg_ref, nu_ref, x_hbm, wg_hbm, wu_hbm, wd_hbm, y_hbm,
                xbuf, ybuf, wgf, wuf, wdf, wgb, wub, wdb, gsem, ssem, wsem, *, n_tok, layer):
    i = pl.program_id(0)
    n_used = nu_ref[0]
    slot = i % 2
    buf_rows = MOE_BLOCK * PACK_SLABS
    active = i < n_used
    first = active & (sched_ref[1, i] == 1)
    w_slot = sched_ref[2, i]

    def weights(e, ws):
        return [pltpu.make_async_copy(w_hbm.at[layer, e], buf.at[ws], wsem.at[ws])
                for w_hbm, buf in ((wg_hbm, wgf), (wu_hbm, wuf), (wd_hbm, wdf))]

    def slab_rows(row):
        return pl.ds(pl.multiple_of(row * PACK_SLABS, PACK_SLABS), PACK_SLABS)

    def gather_row(blk, sl, r):
        tok = asg_ref[blk * MOE_BLOCK + r] >> SLOT_SHIFT
        return pltpu.make_async_copy(x_hbm.at[slab_rows(tok), :], xbuf.at[slab_rows(sl * MOE_BLOCK + r), :],
                                     gsem.at[sl])

    def scatter_row(blk, sl, r):
        dst = asg_ref[blk * MOE_BLOCK + r] & ((1 << SLOT_SHIFT) - 1)
        return pltpu.make_async_copy(ybuf.at[slab_rows(sl * MOE_BLOCK + r), :], y_hbm.at[slab_rows(dst), :],
                                     ssem.at[sl])

    def row_chunks(make, blk, sl, wait):
        n_valid = sched_ref[4, blk]
        if wait:
            buf, sem = (xbuf, gsem) if make is gather_row else (ybuf, ssem)
            shift = ROW_CHUNK.bit_length() - 1
            n_slabs = (((n_valid + (ROW_CHUNK - 1)) >> shift) << shift) * PACK_SLABS

            @pl.when(n_valid > 0)
            def _():
                pltpu.make_async_copy(buf.at[pl.ds(0, n_slabs), :], buf.at[pl.ds(0, n_slabs), :], sem.at[sl]).wait()
            return
        for c0 in range(0, MOE_BLOCK, ROW_CHUNK):
            @pl.when(c0 < n_valid)
            def _(c0=c0):
                for r in range(c0, c0 + ROW_CHUNK):
                    make(blk, sl, r).start()

    @pl.when(i == 0)
    def _():
        xbuf[...] = jnp.zeros_like(xbuf)
        ybuf[...] = jnp.zeros_like(ybuf)
        dump = pltpu.make_async_copy(ybuf, y_hbm.at[pl.ds(MOE_TOPK * n_tok * PACK_SLABS, 2 * buf_rows), :], ssem.at[0])
        dump.start()
        dump.wait()

    @pl.when((i == 0) & (n_used > 0))
    def _():
        for cp in weights(sched_ref[0, 0], 0):
            cp.start()
        row_chunks(gather_row, 0, 0, wait=False)

    @pl.when(first)
    def _():
        for cp in weights(sched_ref[0, i], w_slot):
            cp.wait()

    @pl.when(first & (sched_ref[3, i] >= 0))
    def _():
        for cp, queue in zip(weights(sched_ref[3, i], 1 - w_slot), (0, 1, 1)):
            cp.start(priority=queue)

    @pl.when(first)
    def _():
        wgb[...] = wgf[w_slot].astype(BF16)
        wub[...] = wuf[w_slot].astype(BF16)
        wdb[...] = wdf[w_slot].astype(BF16)

    @pl.when(active)
    def _():
        row_chunks(gather_row, i, slot, wait=True)
        row_chunks(gather_row, i + 1, 1 - slot, wait=False)

    @pl.when(active & (i >= 2))
    def _():
        row_chunks(scatter_row, i - 2, slot, wait=True)

    @pl.when(active)
    def _():
        x = _unpack_rows(xbuf, slot * buf_rows, MOE_BLOCK).astype(BF16)
        a = jnp.dot(x, wgb[...], preferred_element_type=F32)
        b = jnp.dot(x, wub[...], preferred_element_type=F32)
        act = (a * _sigmoid(a) * b).astype(BF16)
        y = jnp.dot(act, wdb[...], preferred_element_type=F32)
        _pack_rows(ybuf, slot * buf_rows, y.astype(BF16).astype(F32))
        row_chunks(scatter_row, i, slot, wait=False)

    @pl.when(i == n_used - 1)
    def _():
        row_chunks(scatter_row, i, slot, wait=True)

    @pl.when((i == n_used - 1) & (i >= 1))
    def _():
        row_chunks(scatter_row, i - 1, 1 - slot, wait=True)


def _moe_experts(xn, sched, slot_asg, n_used, nblk, w_gate, w_up, w_down, layer):
    n_tok = xn.shape[0] // PACK_SLABS
    buf_rows = 2 * MOE_BLOCK * PACK_SLABS
    any_spec = pl.BlockSpec(memory_space=pl.ANY)
    grid_spec = pltpu.PrefetchScalarGridSpec(
        num_scalar_prefetch=3, grid=(nblk,),
        in_specs=[any_spec, any_spec, any_spec, any_spec], out_specs=any_spec,
        scratch_shapes=[pltpu.VMEM((buf_rows, LANE), jnp.uint32), pltpu.VMEM((buf_rows, LANE), jnp.uint32),
                        pltpu.VMEM((2, D_MODEL, MOE_FF), F32), pltpu.VMEM((2, D_MODEL, MOE_FF), F32),
                        pltpu.VMEM((2, MOE_FF, D_MODEL), F32),
                        pltpu.VMEM((D_MODEL, MOE_FF), BF16), pltpu.VMEM((D_MODEL, MOE_FF), BF16),
                        pltpu.VMEM((MOE_FF, D_MODEL), BF16),
                        pltpu.SemaphoreType.DMA((2,)), pltpu.SemaphoreType.DMA((2,)), pltpu.SemaphoreType.DMA((2,))])
    return pl.pallas_call(
        functools.partial(_moe_kernel, n_tok=n_tok, layer=layer), grid_spec=grid_spec,
        out_shape=jax.ShapeDtypeStruct(((MOE_TOPK * n_tok + 2 * MOE_BLOCK) * PACK_SLABS, LANE), jnp.uint32),
        compiler_params=_cparams(("arbitrary",), 56), name="moe_experts",
    )(sched, slot_asg, n_used, xn, w_gate, w_up, w_down)


def _ple_kernel(h1_ref, y0_ref, y1_ref, rg_ref, p_ref, gp_ref, wg_ref, wp_ref, gfin_ref, o_ref, *, final):
    rg = rg_ref[...]
    tm = rg.shape[0]
    ffn = rg[:, 0:1] * _unpack_rows(y0_ref, 0, tm) + rg[:, 1:2] * _unpack_rows(y1_ref, 0, tm)
    h2 = h1_ref[...] + ffn
    hn = _rms(h2, gp_ref[...]).astype(BF16)
    gate = _sigmoid(jnp.dot(hn, wg_ref[...], preferred_element_type=F32))
    pe = jnp.dot(p_ref[...].astype(BF16), wp_ref[...], preferred_element_type=F32)
    h3 = h2 + pe * gate
    o_ref[...] = _rms(h3, gfin_ref[...]) if final else h3


def _ple(h1, y2, rg, p, gp, wg, wp, gfin, final, layer, tm=256):
    rows = h1.shape[0]
    n_t = rows // tm

    def rspec(n):
        return pl.BlockSpec((tm, n), lambda i: (i, 0))

    def yspec(k):
        return pl.BlockSpec((tm * PACK_SLABS, LANE), lambda i: (k * n_t + i, 0))

    return pl.pallas_call(
        functools.partial(_ple_kernel, final=final), grid=(n_t,),
        in_specs=[rspec(D_MODEL), yspec(0), yspec(1), rspec(ROUTE_W),
                  pl.BlockSpec((None, tm, PLE_DIM), lambda i: (layer, i, 0)), _param(gp, layer), _param(wg, layer),
                  _param(wp, layer), _full(gfin)],
        out_specs=rspec(D_MODEL), out_shape=jax.ShapeDtypeStruct((rows, D_MODEL), F32),
        compiler_params=_cparams(("parallel",), 48), name="moe_combine_ple",
    )(h1, y2, y2, rg, p, gp, wg, wp, gfin)


def _rope_tables(positions, rot_dim, lead, period, reps):
    half = rot_dim // 2
    inv_freq = ROPE_THETA ** (-jnp.arange(0, rot_dim, 2, dtype=F32) / rot_dim)
    ang = positions.astype(F32).reshape(-1, 1) * inv_freq
    cos, sin = jnp.cos(ang), jnp.sin(ang)
    n = cos.shape[0]
    tail = period - lead - rot_dim
    c = jnp.concatenate([jnp.ones((n, lead), F32), cos, cos, jnp.ones((n, tail), F32)], axis=1)
    s1 = jnp.concatenate([jnp.zeros((n, lead), F32), -sin, jnp.zeros((n, half + tail), F32)], axis=1)
    s2 = jnp.concatenate([jnp.zeros((n, lead + half), F32), sin, jnp.zeros((n, tail), F32)], axis=1)
    return tuple(jnp.tile(t, (1, reps)) for t in (c, s1, s2))


def _pad_last(a, lo, hi):
    return jnp.pad(a, [(0, 0)] * (a.ndim - 1) + [(lo, hi)])


def _pad_rows(a, hi):
    return jnp.pad(a, [(0, 0)] * (a.ndim - 2) + [(0, hi), (0, 0)])


def _split_cols(w, sizes):
    out, o = [], 0
    for n in sizes:
        out.append(w[..., o:o + n])
        o += n
    return out


def _in_weights(w_in):
    cb, cc, ch, su, dq, dk, dv, cq, ckv, kr = _split_cols(
        w_in, (CONV_DIM, CONV_DIM, CONV_DIM, SSM_DIM, 3 * DIL_GW, 3 * DIL_GW, 3 * DIL_GW, MLA_Q_RANK, MLA_KV_RANK,
               MLA_ROPE))
    ws = [cb, cc, ch, _pad_last(su, 0, SSM_PAD - SSM_DIM)]
    kinds = ["rows", "rows", "rows", "time_major"]
    for g in range(len(DIL_PATTERNS)):
        ws += [_pad_last(m[..., g * DIL_GW:(g + 1) * DIL_GW], 0, DIL_PLANES * LANE - DIL_GW) for m in (dq, dk, dv)]
        kinds += ["planes_rope_q", "planes_rope_k", "planes"]
    ws += [cq, ckv, _pad_last(kr, MLA_NOPE, MLA_HW - MLA_NOPE - MLA_ROPE)]
    kinds += ["mla_cq", "mla_ckv", "mla_kr"]
    return [w.astype(BF16) for w in ws], kinds


def _mla_weights(w_uq, w_ukv):
    depth = w_uq.shape[0]
    q = w_uq.reshape(depth, MLA_Q_RANK, MLA_HEADS, MLA_QK)
    wq = _pad_last(q, 0, MLA_HW - MLA_QK).reshape(depth, MLA_Q_RANK, MLA_HEADS * MLA_HW)
    kv = w_ukv.reshape(depth, MLA_KV_RANK, MLA_HEADS, MLA_NOPE + MLA_V)
    wk = _pad_last(kv[..., :MLA_NOPE], 0, MLA_HW - MLA_NOPE).reshape(depth, MLA_KV_RANK, -1)
    wv = _pad_last(kv[..., MLA_NOPE:], 0, MLA_HW - MLA_V).reshape(depth, MLA_KV_RANK, -1)
    return wq.astype(BF16), wk.astype(BF16), wv.astype(BF16)


def _out_weights(w_out):
    o_ssm, o_dil, o_mla = CONV_DIM, CONV_DIM + SSM_DIM, CONV_DIM + SSM_DIM + 3 * DIL_GW
    ws = [w_out[:, :o_ssm], _pad_rows(w_out[:, o_ssm:o_dil], SSM_PAD - SSM_DIM)]
    ws += [jnp.concatenate([_pad_rows(w_out[:, o_dil + g * DIL_GW:o_dil + (g + 1) * DIL_GW],
                                      DIL_PLANES * LANE - DIL_GW) for g in range(3)], axis=1)]
    ws += [_pad_rows(w_out[:, o_mla:], MLA_VW - MLA_HEADS * MLA_V)]
    return [w.astype(BF16) for w in ws]


def _router_weights(w_group, b_group, w_router, b_router):
    w = _pad_last(jnp.concatenate([w_group, w_router], axis=-1), 0, ROUTE_W - MOE_GROUPS - N_EXPERTS)
    hi = w.astype(BF16)
    lo = (w - hi.astype(F32)).astype(BF16)
    b = _pad_last(jnp.concatenate([b_group, b_router], axis=-1), 0, ROUTE_W - MOE_GROUPS - N_EXPERTS)
    return jnp.concatenate([hi, lo], axis=-1), b[:, None, :]


def kernel(x, p, positions, norm_mix_g, w_in, conv_w, ssm_lam_re, ssm_lam_im, ssm_log_dt, ssm_b_re, ssm_b_im, ssm_c_re, ssm_c_im, ssm_d, ssm_glu_w, ssm_glu_b, mla_q_norm_g, mla_w_uq, mla_kv_norm_g, mla_w_ukv, w_out, norm_ffn_g, w_group, b_group, w_router, b_router, moe_w_gate, moe_w_up, moe_w_down, norm_ple_g, ple_w_proj, ple_w_gate, final_norm_g):
    batch, seq, _ = x.shape
    n_tok = batch * seq
    depth = w_in.shape[0]
    dil_tabs = _rope_tables(positions, DIL_ROT, 0, DIL_HEAD_DIM, LANE // DIL_HEAD_DIM)
    mla_tabs = _rope_tables(positions, MLA_ROPE, MLA_NOPE, MLA_HW, 1)
    h = x.reshape(n_tok, D_MODEL)
    gains = lambda g: g[:, None, :]
    in_ws, in_kinds = _in_weights(w_in)
    mla_prm = (gains(mla_q_norm_g), gains(mla_kv_norm_g), *_mla_weights(mla_w_uq, mla_w_ukv))
    ssm_prm = _ssm_params(ssm_lam_re, ssm_lam_im, ssm_log_dt, ssm_b_re, ssm_b_im, ssm_c_re, ssm_c_im, ssm_d,
                          ssm_glu_w, ssm_glu_b)
    out_ws = _out_weights(w_out)
    wr, br = _router_weights(w_group, b_group, w_router, b_router)
    ple_wg, ple_wp = ple_w_gate.astype(BF16), ple_w_proj.astype(BF16)
    p_rows = p.reshape(depth, n_tok, PLE_DIM)
    for i in range(depth):
        z = _in_proj(h, gains(norm_mix_g), dil_tabs, mla_tabs, mla_prm, in_ws, in_kinds, batch, seq, i)
        cb, cc, ch, su = z[:4]
        q, k, v = z[13:]
        y_conv = _conv_mixer(cb, cc, ch, conv_w, batch, seq, i)
        y_ssm = _ssm_mixer(su, ssm_prm, batch, seq, i)
        outs, lses = [], []
        for g, (window, dil) in enumerate(DIL_PATTERNS):
            assert window // dil == DIL_BLK and (seq // dil) % DIL_BLK == 0
            o, l = _dil_group(z[4 + 3 * g], z[5 + 3 * g], z[6 + 3 * g], dil, batch, seq)
            outs.append(o)
            lses.append(l)
        y_mla = _mla_attn(q, k, v, batch, seq)
        h1, xn, route_i, route_g = _out_proj(
            (y_conv, y_ssm, outs, lses, y_mla), h, out_ws, gains(norm_ffn_g), wr, br, batch, seq, i)
        sched, slot_asg, n_used, nblk = _moe_plan(route_i, n_tok)
        y2 = _moe_experts(xn, sched, slot_asg, n_used, nblk, moe_w_gate, moe_w_up, moe_w_down, i)
        h = _ple(h1, y2, route_g, p_rows, gains(norm_ple_g), ple_wg, ple_wp, final_norm_g.reshape(1, -1),
                 final=(i == depth - 1), layer=i)
    return h.reshape(batch, seq, D_MODEL)
```

```python
import functools
import math

import jax
import jax.numpy as jnp
from jax import lax
from jax.experimental import pallas as pl
from jax.experimental.pallas import tpu as pltpu

D_MODEL = 2048
PLE_DIM = 256
ROPE_THETA = 500000.0
NORM_EPS = 1e-6

CONV_DIM = 448
CONV_K = 3

SSM_DIM = 448
SSM_GROUP = 16
SSM_GROUPS = SSM_DIM // SSM_GROUP
SSM_STATE = 64
SSM_N = SSM_GROUPS * SSM_STATE
SSM_PAD = 512

DIL_HEAD_DIM = 64
DIL_ROT = DIL_HEAD_DIM // 4
DIL_PATTERNS = ((128, 1), (512, 4), (2048, 16))
DIL_HPG = 3
DIL_GW = DIL_HPG * DIL_HEAD_DIM
DIL_PLANES = 2
DIL_BLK = 128
DIL_UNROLL = 4

MLA_HEADS = 9
MLA_Q_RANK = 384
MLA_KV_RANK = 256
MLA_NOPE = 64
MLA_ROPE = 32
MLA_V = 64
MLA_QK = MLA_NOPE + MLA_ROPE
MLA_HW = 128
MLA_VW = 640

MOE_GROUPS = 8
MOE_EPG = 8
N_EXPERTS = MOE_GROUPS * MOE_EPG
MOE_TOPK = 2
MOE_FF = 512
MOE_BLOCK = 256
ROUTE_W = 128

LANE = 128
PACK_SLABS = D_MODEL // 2 // LANE
ROW_CHUNK = 32
SLOT_SHIFT = 15
NEG_BIG = -1e30

BF16 = jnp.bfloat16
F32 = jnp.float32


def _cparams(sem, vmem_mb):
    return pltpu.CompilerParams(dimension_semantics=sem, vmem_limit_bytes=vmem_mb * 1024 * 1024)


def _rms(x, g):
    ms = jnp.mean(x * x, axis=-1, keepdims=True)
    return (x * lax.rsqrt(ms + NORM_EPS)) * g


def _sigmoid(x):
    return 1.0 / (1.0 + jnp.exp(-x))


def _full(a):
    return pl.BlockSpec(a.shape, lambda *_: (0,) * a.ndim)


def _param(a, layer):
    return pl.BlockSpec((None,) + a.shape[1:], lambda *_: (layer,) + (0,) * (a.ndim - 1),
                        pipeline_mode=pl.Buffered(1))


_HI16 = 0xFFFF0000


def _pack_rows(ref, base, x):
    n = x.shape[0]
    bits = lax.bitcast_convert_type(x, jnp.uint32)
    for c in range(PACK_SLABS):
        lo = bits[:, c * LANE:(c + 1) * LANE] >> 16
        hi = bits[:, D_MODEL // 2 + c * LANE:D_MODEL // 2 + (c + 1) * LANE] & jnp.uint32(_HI16)
        ref[pl.ds(base + c, n, stride=PACK_SLABS), :] = lo | hi


def _unpack_rows(ref, base, n):
    lo, hi = [], []
    for c in range(PACK_SLABS):
        w = ref[pl.ds(base + c, n, stride=PACK_SLABS), :]
        lo.append(lax.bitcast_convert_type(w << 16, F32))
        hi.append(lax.bitcast_convert_type(w & jnp.uint32(_HI16), F32))
    return jnp.concatenate(lo + hi, axis=1)


def _in_proj_kernel(x_ref, g_ref, dc_ref, ds1_ref, ds2_ref, mc_ref, ms1_ref, ms2_ref, gq_ref, gkv_ref,
                    wq_ref, wk_ref, wv_ref, *refs, kinds):
    n_out = len(kinds)
    w_refs, o_refs = refs[:n_out], refs[n_out:]
    xn = _rms(x_ref[...], g_ref[...]).astype(BF16)
    tabs = [jnp.concatenate([t[...]] * DIL_PLANES, axis=1) for t in (dc_ref, ds1_ref, ds2_ref)]
    latent = {}
    outs = iter(o_refs)
    for kind, w_ref in zip(kinds, w_refs):
        res = jnp.dot(xn, w_ref[...], preferred_element_type=F32)
        if kind.startswith("mla"):
            latent[kind] = res
            continue
        o_ref = next(outs)
        if kind.startswith("planes"):
            if kind != "planes":
                res = _rope_lanes(res, *tabs, DIL_ROT // 2, DIL_PLANES * LANE)
            if kind == "planes_rope_q":
                res = res * (1.0 / math.sqrt(DIL_HEAD_DIM))
            for j in range(o_ref.shape[0]):
                o_ref[j] = res[:, j * LANE:(j + 1) * LANE]
        else:
            o_ref[...] = res
    q_ref, k_ref, v_ref = outs
    rope = functools.partial(_rope_lanes, c=mc_ref[...], s1=ms1_ref[...], s2=ms2_ref[...], half=MLA_ROPE // 2,
                             width=MLA_HW)
    qn = _rms(latent["mla_cq"], gq_ref[...]).astype(BF16)
    kvn = _rms(latent["mla_ckv"], gkv_ref[...]).astype(BF16)
    q = jnp.dot(qn, wq_ref[...], preferred_element_type=F32)
    kn = jnp.dot(kvn, wk_ref[...], preferred_element_type=F32)
    lane = lax.broadcasted_iota(jnp.int32, (1, MLA_HEADS * MLA_HW), 1)
    ones_col = (lane % MLA_HW == MLA_V).astype(F32)
    v_ref[...] = (jnp.dot(kvn, wv_ref[...], preferred_element_type=F32) + ones_col).astype(BF16)
    kr = rope(latent["mla_kr"])
    scale = math.log2(math.e) / math.sqrt(MLA_QK)
    for h in range(MLA_HEADS):
        sl = slice(h * MLA_HW, (h + 1) * MLA_HW)
        q_ref[:, sl] = (rope(q[:, sl]) * scale).astype(BF16)
        k_ref[:, sl] = (kn[:, sl] + kr).astype(BF16)


def _in_proj(h, g, dil_tabs, mla_tabs, mla_prm, weights, kinds, batch, seq, layer, tm=256):
    n_s = seq // tm

    def rspec(n):
        return pl.BlockSpec((tm, n), lambda b, i: (b * n_s + i, 0))

    in_specs = [rspec(D_MODEL), _param(g, layer)] + [rspec(LANE)] * 6 + [_param(a, layer) for a in mla_prm]
    in_specs += [_param(w, layer) for w in weights]
    out_shape, out_specs = [], []
    for kind, w in zip(kinds, weights):
        n = w.shape[-1]
        if kind.startswith("mla"):
            continue
        if kind == "time_major":
            out_shape.append(jax.ShapeDtypeStruct((seq, batch * n), F32))
            out_specs.append(pl.BlockSpec((tm, n), lambda b, i: (i, b)))
        elif kind.startswith("planes"):
            out_shape.append(jax.ShapeDtypeStruct((n // LANE, batch * seq, LANE), F32))
            out_specs.append(pl.BlockSpec((n // LANE, tm, LANE), lambda b, i: (0, b * n_s + i, 0)))
        else:
            out_shape.append(jax.ShapeDtypeStruct((batch * seq, n), F32))
            out_specs.append(rspec(n))
    for n in (MLA_HEADS * MLA_HW,) * 3:
        out_shape.append(jax.ShapeDtypeStruct((batch * seq, n), BF16))
        out_specs.append(rspec(n))
    return pl.pallas_call(
        functools.partial(_in_proj_kernel, kinds=tuple(kinds)),
        grid=(batch, n_s), in_specs=in_specs, out_specs=out_specs, out_shape=out_shape,
        compiler_params=_cparams(("parallel", "parallel"), 56), name="in_proj",
    )(h, g, *dil_tabs, *mla_tabs, *mla_prm, *weights)


def _conv_kernel(cb_ref, cc_ref, ch_ref, w_ref, o_ref, u_ref, *, rows):
    seq = cb_ref.shape[0]
    u_ref[0:8, :] = jnp.zeros((8, CONV_DIM), F32)
    for r0 in range(0, seq, rows):
        u_ref[8 + r0:8 + r0 + rows, :] = cc_ref[r0:r0 + rows, :] * ch_ref[r0:r0 + rows, :]
    w0, w1, w2 = w_ref[0:1, :], w_ref[1:2, :], w_ref[2:3, :]
    for r0 in range(0, seq, rows):
        acc = (w2 * u_ref[8 + r0:8 + r0 + rows, :] + w1 * u_ref[7 + r0:7 + r0 + rows, :]
               + w0 * u_ref[6 + r0:6 + r0 + rows, :])
        o_ref[r0:r0 + rows, :] = cb_ref[r0:r0 + rows, :] * acc


def _conv_mixer(cb, cc, ch, conv_w, batch, seq, layer):
    spec = pl.BlockSpec((seq, CONV_DIM), lambda b: (b, 0))
    return pl.pallas_call(
        functools.partial(_conv_kernel, rows=256),
        grid=(batch,), in_specs=[spec, spec, spec, _param(conv_w, layer)], out_specs=spec,
        out_shape=jax.ShapeDtypeStruct((batch * seq, CONV_DIM), F32),
        scratch_shapes=[pltpu.VMEM((seq + 8, CONV_DIM), F32)],
        compiler_params=_cparams(("parallel",), 48), name="conv_mixer",
    )(cb, cc, ch, conv_w)


SSM_NP = SSM_N // LANE
SSM_SCAN_PLANES = 4
_SSM_SPLIT = ((0, 256, 0, 1024), (256, 256, 1024, 768))


def _gelu_tanh(x):
    return 0.5 * x * (1.0 + jnp.tanh(math.sqrt(2.0 / math.pi) * (x + 0.044715 * (x * x * x))))


def _ssm_kernel(u_ref, bf0_ref, bf1_ref, are_ref, aim_ref, cc0_ref, cc1_ref, d_ref, gw_ref, gb_ref, o_ref,
                xs_ref, st_ref, *, batch, tc):
    per_tile = 8 // batch
    npk = SSM_NP // per_tile

    @pl.when(pl.program_id(0) == 0)
    def _():
        st_ref[...] = jnp.zeros_like(st_ref)

    def plane_rows(p, b):
        return p % npk, pl.ds((p // npk) * batch + b, tc, stride=8)

    for b in range(batch):
        for (l0, lw, s0, sw), bf_ref in zip(_SSM_SPLIT, (bf0_ref, bf1_ref)):
            ub = u_ref[:, b * SSM_PAD + l0:b * SSM_PAD + l0 + lw].astype(BF16)
            r = jnp.dot(ub, bf_ref[...], preferred_element_type=F32)
            for q in range(sw // LANE):
                pk, rows = plane_rows(s0 // LANE + q, b)
                xs_ref[pk, rows, :] = r[:, q * LANE:(q + 1) * LANE]
                xs_ref[npk + pk, rows, :] = r[:, sw + q * LANE:sw + (q + 1) * LANE]

    def decay(a_ref, pk):
        return jnp.concatenate([jnp.broadcast_to(a_ref[:, (pk + j * npk) * LANE:(pk + j * npk + 1) * LANE],
                                                 (batch, LANE)) for j in range(per_tile)], axis=0)

    for p0 in range(0, npk, SSM_SCAN_PLANES):
        pks = list(range(p0, min(p0 + SSM_SCAN_PLANES, npk)))
        ar = [decay(are_ref, pk) for pk in pks]
        ai = [decay(aim_ref, pk) for pk in pks]

        def body(t, carry, pks=pks, ar=ar, ai=ai):
            rows = pl.ds(pl.multiple_of(t * 8, 8), 8)
            new = []
            for idx, pk in enumerate(pks):
                sr, si = carry[idx]
                sr, si = (ar[idx] * sr - ai[idx] * si + xs_ref[pk, rows, :],
                          ar[idx] * si + ai[idx] * sr + xs_ref[npk + pk, rows, :])
                xs_ref[pk, rows, :] = sr
                xs_ref[npk + pk, rows, :] = si
                new.append((sr, si))
            return tuple(new)

        init = tuple((st_ref[pk], st_ref[npk + pk]) for pk in pks)
        for pk, (sr, si) in zip(pks, lax.fori_loop(0, tc, body, init, unroll=2)):
            st_ref[pk] = sr
            st_ref[npk + pk] = si

    for b in range(batch):
        ys = []
        for (l0, lw, s0, sw), cc_ref in zip(_SSM_SPLIT, (cc0_ref, cc1_ref)):
            planes = [plane_rows(p, b) for p in range(s0 // LANE, (s0 + sw) // LANE)]
            xb = jnp.concatenate([xs_ref[pk, rows, :] for pk, rows in planes]
                                 + [xs_ref[npk + pk, rows, :] for pk, rows in planes], axis=1)
            ys.append(jnp.dot(xb.astype(BF16), cc_ref[...], preferred_element_type=F32))
        y = jnp.concatenate(ys, axis=1) + d_ref[...] * u_ref[:, b * SSM_PAD:(b + 1) * SSM_PAD]
        g = _gelu_tanh(y)
        o_ref[:, b * SSM_PAD:(b + 1) * SSM_PAD] = g * _sigmoid(
            jnp.dot(g.astype(BF16), gw_ref[...], preferred_element_type=F32) + gb_ref[...])


def _ssm_mixer(su, prm, batch, seq, layer, tc=256):
    spec = pl.BlockSpec((tc, batch * SSM_PAD), lambda c: (c, 0))
    args = (prm["bf0"], prm["bf1"], prm["a_re"], prm["a_im"], prm["cc0"], prm["cc1"], prm["d"], prm["gw"], prm["gb"])
    return pl.pallas_call(
        functools.partial(_ssm_kernel, batch=batch, tc=tc),
        grid=(seq // tc,), in_specs=[spec] + [_param(a, layer) for a in args], out_specs=spec,
        out_shape=jax.ShapeDtypeStruct((seq, batch * SSM_PAD), F32),
        scratch_shapes=[pltpu.VMEM((2 * SSM_NP * batch // 8, tc * 8, LANE), F32),
                        pltpu.VMEM((2 * SSM_NP * batch // 8, 8, LANE), F32)],
        compiler_params=_cparams(("arbitrary",), 48), name="ssm_mixer",
    )(su, *args)


def _ssm_params(lam_re, lam_im, log_dt, b_re, b_im, c_re, c_im, d_skip, glu_w, glu_b):
    depth = lam_re.shape[0]
    dt = jnp.exp(log_dt)[..., None]
    mag = jnp.exp(lam_re * dt)
    a_re = mag * jnp.cos(lam_im * dt)
    a_im = mag * jnp.sin(lam_im * dt)
    nr, ni = a_re - 1.0, a_im
    den = lam_re * lam_re + lam_im * lam_im
    f_re = (nr * lam_re + ni * lam_im) / den
    f_im = (ni * lam_re - nr * lam_im) / den
    bfr = f_re[..., None] * b_re - f_im[..., None] * b_im
    bfi = f_re[..., None] * b_im + f_im[..., None] * b_re
    same_group = (jnp.arange(SSM_DIM)[:, None] // SSM_GROUP == jnp.arange(SSM_N)[None, :] // SSM_STATE).astype(F32)

    def in_blockdiag(m):
        cols = jnp.transpose(m, (0, 3, 1, 2)).reshape(depth, SSM_GROUP, SSM_N)
        return jnp.tile(cols, (1, SSM_GROUPS, 1)) * same_group

    def out_blockdiag(m):
        rows = jnp.transpose(m, (0, 1, 3, 2)).reshape(depth, SSM_N, SSM_GROUP)
        return jnp.tile(rows, (1, 1, SSM_GROUPS)) * same_group.T

    pad = SSM_PAD - SSM_DIM
    b_re_d, b_im_d = (_pad_rows(in_blockdiag(m), pad) for m in (bfr, bfi))
    c_re_d, c_im_d = (_pad_last(out_blockdiag(m), 0, pad) for m in (c_re, -c_im))
    prm = {}
    for c, (l0, lw, s0, sw) in enumerate(_SSM_SPLIT):
        prm[f"bf{c}"] = jnp.concatenate([b_re_d[:, l0:l0 + lw, s0:s0 + sw], b_im_d[:, l0:l0 + lw, s0:s0 + sw]],
                                        axis=2).astype(BF16)
        prm[f"cc{c}"] = jnp.concatenate([c_re_d[:, s0:s0 + sw, l0:l0 + lw], c_im_d[:, s0:s0 + sw, l0:l0 + lw]],
                                        axis=1).astype(BF16)
    return {
        **prm,
        "a_re": a_re.reshape(depth, 1, SSM_N), "a_im": a_im.reshape(depth, 1, SSM_N),
        "d": _pad_last(d_skip, 0, pad)[:, None, :],
        "gw": _pad_last(_pad_rows(glu_w, pad), 0, pad).astype(BF16),
        "gb": _pad_last(glu_b, 0, pad)[:, None, :],
    }


def _rope_lanes(x, c, s1, s2, half, width):
    return x * c + pltpu.roll(x, width - half, 1) * s1 + pltpu.roll(x, half, 1) * s2


def _dil_kernel(q_ref, k_ref, v_ref, o_ref, l_ref, *, dil):
    seq = q_ref.shape[1]
    nb = (seq // dil) // DIL_BLK
    width = DIL_PLANES * LANE
    lane = lax.broadcasted_iota(jnp.int32, (1, width), 1)
    cmask = [(lane // DIL_HEAD_DIM == c).astype(F32) for c in range(DIL_HPG)]
    qi = lax.broadcasted_iota(jnp.int32, (DIL_HPG * DIL_BLK, 2 * DIL_BLK), 0) % DIL_BLK
    ki = lax.broadcasted_iota(jnp.int32, (DIL_HPG * DIL_BLK, 2 * DIL_BLK), 1)
    band = (ki >= qi) & (ki <= qi + DIL_BLK)

    def planes(ref, rows):
        return jnp.concatenate([ref[j, rows, :] for j in range(DIL_PLANES)], axis=1)

    def rows_of(m, blk):
        if dil == 1:
            return pl.ds(pl.multiple_of(blk * DIL_BLK, DIL_BLK), DIL_BLK)
        return pl.ds(blk * DIL_BLK * dil + m, DIL_BLK, stride=dil)

    def blocks(it, carry):
        ids = [it * DIL_UNROLL + u for u in range(DIL_UNROLL)]
        mis = [(idx // nb, idx % nb) for idx in ids]
        rows_q = [rows_of(m, i) for m, i in mis]
        rows_p = [rows_of(m, jnp.maximum(i - 1, 0)) for m, i in mis]
        scores, vcats = [], []
        for rq, rp in zip(rows_q, rows_p):
            q = planes(q_ref, rq)
            kcat = jnp.concatenate([planes(k_ref, rp), planes(k_ref, rq)], axis=0).astype(BF16)
            vcats.append(jnp.concatenate([planes(v_ref, rp), planes(v_ref, rq)], axis=0).astype(BF16))
            qs = jnp.concatenate([q * cm for cm in cmask], axis=0).astype(BF16)
            scores.append(lax.dot_general(qs, kcat, (((1,), (1,)), ((), ())), preferred_element_type=F32))
        soft = []
        for s, (m, i) in zip(scores, mis):
            s = jnp.where(band & ((ki >= DIL_BLK) | (i > 0)), s, NEG_BIG)
            mx = jnp.max(s, axis=-1, keepdims=True)
            p = jnp.exp(s - mx)
            den = jnp.sum(p, axis=-1, keepdims=True)
            soft.append((p.astype(BF16), den, mx + jnp.log(den)))
        for (p, den, lse), vcat, rq in zip(soft, vcats, rows_q):
            o = jnp.dot(p, vcat, preferred_element_type=F32) / den
            out = jnp.zeros((DIL_BLK, width), F32)
            lout = jnp.zeros((DIL_BLK, width), F32)
            for c, cm in enumerate(cmask):
                out = out + cm * o[c * DIL_BLK:(c + 1) * DIL_BLK, :]
                lout = lout + cm * lse[c * DIL_BLK:(c + 1) * DIL_BLK, :]
            for j in range(DIL_PLANES):
                o_ref[j, rq, :] = out[:, j * LANE:(j + 1) * LANE]
                l_ref[j, rq, :] = lout[:, j * LANE:(j + 1) * LANE]
        return carry

    lax.fori_loop(0, dil * nb // DIL_UNROLL, blocks, 0)


def _dil_group(q, k, v, dil, batch, seq):
    spec = pl.BlockSpec((DIL_PLANES, seq, LANE), lambda b: (0, b, 0))
    sds = jax.ShapeDtypeStruct((DIL_PLANES, batch * seq, LANE), F32)
    return pl.pallas_call(
        functools.partial(_dil_kernel, dil=dil), grid=(batch,), in_specs=[spec] * 3,
        out_specs=[spec, spec], out_shape=[sds, sds],
        compiler_params=_cparams(("parallel",), 48), name=f"dil_attn_d{dil}",
    )(q, k, v)


def _mla_attn_kernel(q_ref, k_ref, v_ref, o_ref, *, tq, group):
    qi = pl.program_id(1)
    row = lax.broadcasted_iota(jnp.int32, (tq, tq), 0)
    col = lax.broadcasted_iota(jnp.int32, (tq, tq), 1)
    causal = col <= row

    def heads_out(heads):
        qs = [q_ref[:, h * MLA_HW:(h + 1) * MLA_HW] for h in heads]

        def step(kj, carry, masked):
            rows = pl.ds(pl.multiple_of(kj * tq, tq), tq)
            scores = [lax.dot_general(q, k_ref[rows, h * MLA_HW:(h + 1) * MLA_HW], (((1,), (1,)), ((), ())),
                                      preferred_element_type=F32) for h, q in zip(heads, qs)]
            stats = []
            for s, (m, acc) in zip(scores, carry):
                if masked:
                    s = jnp.where(causal, s, NEG_BIG)
                mn = jnp.maximum(m, jnp.max(s, axis=-1, keepdims=True))
                stats.append((mn, jnp.exp2(m - mn), jnp.exp2(s - mn).astype(BF16)))
            out = []
            for h, (mn, alpha, p), (_, acc) in zip(heads, stats, carry):
                v = v_ref[rows, h * MLA_HW:(h + 1) * MLA_HW]
                out.append((mn, alpha * acc + jnp.dot(p, v, preferred_element_type=F32)))
            return tuple(out)

        init = tuple((jnp.full((tq, 1), NEG_BIG, F32), jnp.zeros((tq, MLA_HW), F32)) for _ in heads)
        carry = lax.fori_loop(0, qi, functools.partial(step, masked=False), init)
        return [acc / acc[:, MLA_V:MLA_V + 1] for _, acc in step(qi, carry, True)]

    for h0 in range(0, MLA_HEADS, group):
        heads = list(range(h0, min(h0 + group, MLA_HEADS)))
        for h, o in zip(heads, heads_out(heads)):
            lo = h * MLA_V
            if h % 2:
                o_ref[:, lo:lo + MLA_V] = pltpu.roll(o, MLA_V, 1)[:, MLA_V:2 * MLA_V]
            else:
                o_ref[:, lo:lo + MLA_V] = o[:, 0:MLA_V]
    if MLA_HEADS % 2:
        o_ref[:, MLA_HEADS * MLA_V:] = jnp.zeros((tq, MLA_VW - MLA_HEADS * MLA_V), F32)


def _mla_attn(q, k, v, batch, seq, tq=256, group=9):
    hw = MLA_HEADS * MLA_HW
    n_q = seq // tq
    return pl.pallas_call(
        functools.partial(_mla_attn_kernel, tq=tq, group=group), grid=(batch, n_q),
        in_specs=[pl.BlockSpec((tq, hw), lambda b, i: (b * n_q + i, 0)),
                  pl.BlockSpec((seq, hw), lambda b, i: (b, 0)),
                  pl.BlockSpec((seq, hw), lambda b, i: (b, 0))],
        out_specs=pl.BlockSpec((tq, MLA_VW), lambda b, i: (b * n_q + i, 0)),
        out_shape=jax.ShapeDtypeStruct((batch * seq, MLA_VW), F32),
        compiler_params=_cparams(("parallel", "arbitrary"), 48), name="mla_attn",
    )(q, k, v)


def _out_proj_kernel(yc_ref, ys_ref, od0_ref, od1_ref, od2_ref, ld0_ref, ld1_ref, ld2_ref, ym_ref, h_ref,
                     wc_ref, ws_ref, wd_ref, wm_ref,
                     gf_ref, wr_ref, br_ref,
                     h1_ref, xn_ref, ri_ref, rg_ref):
    def mm(y, w_ref):
        return jnp.dot(y.astype(BF16), w_ref[...], preferred_element_type=F32)

    y_dil = [None] * (3 * DIL_PLANES)
    for j in range(DIL_PLANES):
        la, lb, lc = ld0_ref[j], ld1_ref[j], ld2_ref[j]
        mx = jnp.maximum(jnp.maximum(la, lb), lc)
        ea, eb, ec = jnp.exp(la - mx), jnp.exp(lb - mx), jnp.exp(lc - mx)
        inv = 1.0 / (ea + eb + ec)
        for g, (o_ref, e) in enumerate(((od0_ref, ea), (od1_ref, eb), (od2_ref, ec))):
            y_dil[g * DIL_PLANES + j] = o_ref[j] * (e * inv)
    mix = (mm(yc_ref[...], wc_ref) + mm(ys_ref[...], ws_ref) + mm(jnp.concatenate(y_dil, axis=1), wd_ref)
           + mm(ym_ref[...], wm_ref))
    h1 = h_ref[...] + mix
    h1_ref[...] = h1
    hn = _rms(h1, gf_ref[...])
    hi = hn.astype(BF16)
    hi32 = hi.astype(F32)
    lo = (hn - hi32).astype(BF16)
    _pack_rows(xn_ref, 0, hi32)
    r_hi = jnp.dot(hi, wr_ref[...], preferred_element_type=F32)
    r_lo = jnp.dot(lo, wr_ref[:, 0:ROUTE_W], preferred_element_type=F32)
    logits = r_hi[:, 0:ROUTE_W] + r_hi[:, ROUTE_W:2 * ROUTE_W] + r_lo + br_ref[...]
    lane = lax.broadcasted_iota(jnp.int32, logits.shape, 1)
    gl = jnp.where(lane < MOE_GROUPS, logits, NEG_BIG)
    gmax = jnp.max(gl, axis=-1, keepdims=True)
    g_top_p = 1.0 / jnp.sum(jnp.exp(gl - gmax), axis=-1, keepdims=True)
    g_idx = jnp.min(jnp.where(gl == gmax, lane, ROUTE_W), axis=-1, keepdims=True)
    in_group = (lane >= MOE_GROUPS) & (lane < MOE_GROUPS + N_EXPERTS) & (((lane - MOE_GROUPS) >> 3) == g_idx)
    el = jnp.where(in_group, logits, NEG_BIG)
    v1 = jnp.max(el, axis=-1, keepdims=True)
    i1 = jnp.min(jnp.where(el == v1, lane, ROUTE_W), axis=-1, keepdims=True)
    el2 = jnp.where(lane == i1, NEG_BIG, el)
    v2 = jnp.max(el2, axis=-1, keepdims=True)
    i2 = jnp.min(jnp.where(el2 == v2, lane, ROUTE_W), axis=-1, keepdims=True)
    e2 = jnp.exp(v2 - v1)
    w1 = g_top_p / (1.0 + e2)
    w2 = g_top_p * e2 / (1.0 + e2)
    ri_ref[...] = jnp.where(lane == 0, i1 - MOE_GROUPS, jnp.where(lane == 1, i2 - MOE_GROUPS, 0))
    rg_ref[...] = jnp.where(lane == 0, w1, jnp.where(lane == 1, w2, 0.0))


def _out_proj(ys, h, ws, gf, wr, br, batch, seq, layer, tm=256):
    rows = batch * seq
    n_s = seq // tm
    yc, ys_tm, dil_o, dil_l, ym = ys

    def rspec(n):
        return pl.BlockSpec((tm, n), lambda b, i: (b * n_s + i, 0))

    pspec = pl.BlockSpec((DIL_PLANES, tm, LANE), lambda b, i: (0, b * n_s + i, 0))
    in_specs = [rspec(CONV_DIM), pl.BlockSpec((tm, SSM_PAD), lambda b, i: (i, b))] + [pspec] * 6
    in_specs += [rspec(MLA_VW), rspec(D_MODEL)]
    in_specs += [_param(a, layer) for a in (*ws, gf, wr, br)]
    return pl.pallas_call(
        _out_proj_kernel, grid=(batch, n_s), in_specs=in_specs,
        out_specs=[rspec(D_MODEL), pl.BlockSpec((tm * PACK_SLABS, LANE), lambda b, i: (b * n_s + i, 0)),
                   rspec(ROUTE_W), rspec(ROUTE_W)],
        out_shape=[jax.ShapeDtypeStruct((rows, D_MODEL), F32),
                   jax.ShapeDtypeStruct((rows * PACK_SLABS, LANE), jnp.uint32),
                   jax.ShapeDtypeStruct((rows, ROUTE_W), jnp.int32), jax.ShapeDtypeStruct((rows, ROUTE_W), F32)],
        compiler_params=_cparams(("parallel", "parallel"), 48), name="out_proj_router",
    )(yc, ys_tm, *dil_o, *dil_l, ym, h, *ws, gf, wr, br)


def _moe_plan(route_i, n_tok):
    n_asg = n_tok * MOE_TOPK
    nblk = (n_asg + N_EXPERTS * (MOE_BLOCK - 1) + MOE_BLOCK - 1) // MOE_BLOCK
    flat_e = route_i[:, :MOE_TOPK].reshape(-1)
    onehot = (flat_e[:, None] == jnp.arange(N_EXPERTS, dtype=jnp.int32)[None, :]).astype(F32)
    oh3 = onehot.reshape(n_asg // MOE_BLOCK, MOE_BLOCK, N_EXPERTS)
    within = jnp.einsum("ij,bje->bie", jnp.tril(jnp.ones((MOE_BLOCK, MOE_BLOCK), F32)), oh3)
    chunk_tot = within[:, -1, :]
    n_chunk = chunk_tot.shape[0]
    chunk_off = jnp.dot(jnp.tril(jnp.ones((n_chunk, n_chunk), F32), -1), chunk_tot)
    rank = jnp.sum((within + chunk_off[:, None, :]) * oh3, axis=-1).reshape(n_asg).astype(jnp.int32) - 1
    counts = (chunk_off[-1] + chunk_tot[-1]).astype(jnp.int32)
    blocks_of = (counts + MOE_BLOCK - 1) // MOE_BLOCK
    upto = jnp.arange(N_EXPERTS)[:, None] <= jnp.arange(N_EXPERTS)[None, :]
    blk_end = jnp.sum(jnp.where(upto, blocks_of[:, None], 0), axis=0)
    blk_start = blk_end - blocks_of
    dest = jnp.dot(onehot, blk_start.astype(F32)).astype(jnp.int32) * MOE_BLOCK + rank
    in_blk = (dest[:, None] // MOE_BLOCK == jnp.arange(nblk, dtype=jnp.int32)[None, :]).astype(F32)
    at_off = (dest[:, None] % MOE_BLOCK == jnp.arange(MOE_BLOCK, dtype=jnp.int32)[None, :]).astype(F32)
    asg = jnp.arange(n_asg, dtype=jnp.int32)
    a_hi, a_lo = (asg // MOE_BLOCK).astype(F32)[:, None], (asg % MOE_BLOCK).astype(F32)[:, None]
    inv = lambda v: jnp.einsum("ab,ao->bo", in_blk * v, at_off)
    slot_a = jnp.where(inv(1.0) > 0.5, inv(a_hi) * MOE_BLOCK + inv(a_lo), -1.0).astype(jnp.int32).reshape(-1)
    slot = jnp.arange(nblk * MOE_BLOCK, dtype=jnp.int32)
    dump = MOE_TOPK * n_tok + ((slot // MOE_BLOCK) % 2) * MOE_BLOCK + slot % MOE_BLOCK
    assert MOE_TOPK * n_tok + 2 * MOE_BLOCK <= 1 << SLOT_SHIFT
    slot_asg = jnp.where(slot_a >= 0, ((slot_a >> 1) << SLOT_SHIFT) | ((slot_a & 1) * n_tok + (slot_a >> 1)), dump)
    n_used = blk_end[-1].astype(jnp.int32)
    blk = jnp.arange(nblk, dtype=jnp.int32)
    block_expert = jnp.minimum(jnp.sum(blk_end[None, :] <= blk[:, None], axis=1), N_EXPERTS - 1).astype(jnp.int32)
    mine = block_expert[:, None] == jnp.arange(N_EXPERTS, dtype=jnp.int32)[None, :]
    lookup = lambda table: jnp.sum(jnp.where(mine, table[None, :], 0), axis=1)
    run_start, run_end = lookup(blk_start), lookup(blk_end)
    is_first = ((blk == run_start) & (blk < n_used)).astype(jnp.int32)
    used = (blocks_of > 0).astype(jnp.int32)
    nth_used = jnp.sum(jnp.where(upto, used[:, None], 0), axis=0) - used
    w_slot = lookup(nth_used) % 2
    starts_at = (blk_start[None, :] == run_end[:, None]) & (blocks_of[None, :] > 0)
    nxt = jnp.where(run_end < n_used, jnp.sum(jnp.where(starts_at, jnp.arange(N_EXPERTS)[None, :], 0), axis=1), -1)
    rows_left = jnp.clip(lookup(counts) - (blk - run_start) * MOE_BLOCK, 0, MOE_BLOCK)
    sched = jnp.stack([block_expert, is_first, w_slot, nxt, jnp.where(blk < n_used, rows_left, 0)])
    sched = jnp.pad(sched.astype(jnp.int32), ((0, 0), (0, 1)))
    return sched, slot_asg, n_used.reshape(1), nblk


def _moe_kernel(sched_ref, asg_ref, nu_ref, x_hbm, wg_hbm, wu_hbm, wd_hbm, y_hbm,
                xbuf, ybuf, wgf, wuf, wdf, wgb, wub, wdb, gsem, ssem, wsem, *, n_tok, layer):
    i = pl.program_id(0)
    n_used = nu_ref[0]
    slot = i % 2
    buf_rows = MOE_BLOCK * PACK_SLABS
    active = i < n_used
    first = active & (sched_ref[1, i] == 1)
    w_slot = sched_ref[2, i]

    def weights(e, ws):
        return [pltpu.make_async_copy(w_hbm.at[layer, e], buf.at[ws], wsem.at[ws])
                for w_hbm, buf in ((wg_hbm, wgf), (wu_hbm, wuf), (wd_hbm, wdf))]

    def slab_rows(row):
        return pl.ds(pl.multiple_of(row * PACK_SLABS, PACK_SLABS), PACK_SLABS)

    def gather_row(blk, sl, r):
        tok = asg_ref[blk * MOE_BLOCK + r] >> SLOT_SHIFT
        return pltpu.make_async_copy(x_hbm.at[slab_rows(tok), :], xbuf.at[slab_rows(sl * MOE_BLOCK + r), :],
                                     gsem.at[sl])

    def scatter_row(blk, sl, r):
        dst = asg_ref[blk * MOE_BLOCK + r] & ((1 << SLOT_SHIFT) - 1)
        return pltpu.make_async_copy(ybuf.at[slab_rows(sl * MOE_BLOCK + r), :], y_hbm.at[slab_rows(dst), :],
                                     ssem.at[sl])

    def row_chunks(make, blk, sl, wait):
        n_valid = sched_ref[4, blk]
        if wait:
            buf, sem = (xbuf, gsem) if make is gather_row else (ybuf, ssem)
            shift = ROW_CHUNK.bit_length() - 1
            n_slabs = (((n_valid + (ROW_CHUNK - 1)) >> shift) << shift) * PACK_SLABS

            @pl.when(n_valid > 0)
            def _():
                pltpu.make_async_copy(buf.at[pl.ds(0, n_slabs), :], buf.at[pl.ds(0, n_slabs), :], sem.at[sl]).wait()
            return
        for c0 in range(0, MOE_BLOCK, ROW_CHUNK):
            @pl.when(c0 < n_valid)
            def _(c0=c0):
                for r in range(c0, c0 + ROW_CHUNK):
                    make(blk, sl, r).start()

    @pl.when(i == 0)
    def _():
        xbuf[...] = jnp.zeros_like(xbuf)
        ybuf[...] = jnp.zeros_like(ybuf)
        dump = pltpu.make_async_copy(ybuf, y_hbm.at[pl.ds(MOE_TOPK * n_tok * PACK_SLABS, 2 * buf_rows), :], ssem.at[0])
        dump.start()
        dump.wait()

    @pl.when((i == 0) & (n_used > 0))
    def _():
        for cp in weights(sched_ref[0, 0], 0):
            cp.start()
        row_chunks(gather_row, 0, 0, wait=False)

    @pl.when(first)
    def _():
        for cp in weights(sched_ref[0, i], w_slot):
            cp.wait()

    @pl.when(first & (sched_ref[3, i] >= 0))
    def _():
        for cp, queue in zip(weights(sched_ref[3, i], 1 - w_slot), (0, 1, 1)):
            cp.start(priority=queue)

    @pl.when(first)
    def _():
        wgb[...] = wgf[w_slot].astype(BF16)
        wub[...] = wuf[w_slot].astype(BF16)
        wdb[...] = wdf[w_slot].astype(BF16)

    @pl.when(active)
    def _():
        row_chunks(gather_row, i, slot, wait=True)
        row_chunks(gather_row, i + 1, 1 - slot, wait=False)

    @pl.when(active & (i >= 2))
    def _():
        row_chunks(scatter_row, i - 2, slot, wait=True)

    @pl.when(active)
    def _():
        x = _unpack_rows(xbuf, slot * buf_rows, MOE_BLOCK).astype(BF16)
        a = jnp.dot(x, wgb[...], preferred_element_type=F32)
        b = jnp.dot(x, wub[...], preferred_element_type=F32)
        act = (a * _sigmoid(a) * b).astype(BF16)
        y = jnp.dot(act, wdb[...], preferred_element_type=F32)
        _pack_rows(ybuf, slot * buf_rows, y.astype(BF16).astype(F32))
        row_chunks(scatter_row, i, slot, wait=False)

    @pl.when(i == n_used - 1)
    def _():
        row_chunks(scatter_row, i, slot, wait=True)

    @pl.when((i == n_used - 1) & (i >= 1))
    def _():
        row_chunks(scatter_row, i - 1, 1 - slot, wait=True)


def _moe_experts(xn, sched, slot_asg, n_used, nblk, w_gate, w_up, w_down, layer):
    n_tok = xn.shape[0] // PACK_SLABS
    buf_rows = 2 * MOE_BLOCK * PACK_SLABS
    any_spec = pl.BlockSpec(memory_space=pl.ANY)
    grid_spec = pltpu.PrefetchScalarGridSpec(
        num_scalar_prefetch=3, grid=(nblk,),
        in_specs=[any_spec, any_spec, any_spec, any_spec], out_specs=any_spec,
        scratch_shapes=[pltpu.VMEM((buf_rows, LANE), jnp.uint32), pltpu.VMEM((buf_rows, LANE), jnp.uint32),
                        pltpu.VMEM((2, D_MODEL, MOE_FF), F32), pltpu.VMEM((2, D_MODEL, MOE_FF), F32),
                        pltpu.VMEM((2, MOE_FF, D_MODEL), F32),
                        pltpu.VMEM((D_MODEL, MOE_FF), BF16), pltpu.VMEM((D_MODEL, MOE_FF), BF16),
                        pltpu.VMEM((MOE_FF, D_MODEL), BF16),
                        pltpu.SemaphoreType.DMA((2,)), pltpu.SemaphoreType.DMA((2,)), pltpu.SemaphoreType.DMA((2,))])
    return pl.pallas_call(
        functools.partial(_moe_kernel, n_tok=n_tok, layer=layer), grid_spec=grid_spec,
        out_shape=jax.ShapeDtypeStruct(((MOE_TOPK * n_tok + 2 * MOE_BLOCK) * PACK_SLABS, LANE), jnp.uint32),
        compiler_params=_cparams(("arbitrary",), 56), name="moe_experts",
    )(sched, slot_asg, n_used, xn, w_gate, w_up, w_down)


def _ple_kernel(h1_ref, y0_ref, y1_ref, rg_ref, p_ref, gp_ref, wg_ref, wp_ref, gfin_ref, o_ref, *, final):
    rg = rg_ref[...]
    tm = rg.shape[0]
    ffn = rg[:, 0:1] * _unpack_rows(y0_ref, 0, tm) + rg[:, 1:2] * _unpack_rows(y1_ref, 0, tm)
    h2 = h1_ref[...] + ffn
    hn = _rms(h2, gp_ref[...]).astype(BF16)
    gate = _sigmoid(jnp.dot(hn, wg_ref[...], preferred_element_type=F32))
    pe = jnp.dot(p_ref[...].astype(BF16), wp_ref[...], preferred_element_type=F32)
    h3 = h2 + pe * gate
    o_ref[...] = _rms(h3, gfin_ref[...]) if final else h3


def _ple(h1, y2, rg, p, gp, wg, wp, gfin, final, layer, tm=256):
    rows = h1.shape[0]
    n_t = rows // tm

    def rspec(n):
        return pl.BlockSpec((tm, n), lambda i: (i, 0))

    def yspec(k):
        return pl.BlockSpec((tm * PACK_SLABS, LANE), lambda i: (k * n_t + i, 0))

    return pl.pallas_call(
        functools.partial(_ple_kernel, final=final), grid=(n_t,),
        in_specs=[rspec(D_MODEL), yspec(0), yspec(1), rspec(ROUTE_W),
                  pl.BlockSpec((None, tm, PLE_DIM), lambda i: (layer, i, 0)), _param(gp, layer), _param(wg, layer),
                  _param(wp, layer), _full(gfin)],
        out_specs=rspec(D_MODEL), out_shape=jax.ShapeDtypeStruct((rows, D_MODEL), F32),
        compiler_params=_cparams(("parallel",), 48), name="moe_combine_ple",
    )(h1, y2, y2, rg, p, gp, wg, wp, gfin)


def _rope_tables(positions, rot_dim, lead, period, reps):
    half = rot_dim // 2
    inv_freq = ROPE_THETA ** (-jnp.arange(0, rot_dim, 2, dtype=F32) / rot_dim)
    ang = positions.astype(F32).reshape(-1, 1) * inv_freq
    cos, sin = jnp.cos(ang), jnp.sin(ang)
    n = cos.shape[0]
    tail = period - lead - rot_dim
    c = jnp.concatenate([jnp.ones((n, lead), F32), cos, cos, jnp.ones((n, tail), F32)], axis=1)
    s1 = jnp.concatenate([jnp.zeros((n, lead), F32), -sin, jnp.zeros((n, half + tail), F32)], axis=1)
    s2 = jnp.concatenate([jnp.zeros((n, lead + half), F32), sin, jnp.zeros((n, tail), F32)], axis=1)
    return tuple(jnp.tile(t, (1, reps)) for t in (c, s1, s2))


def _pad_last(a, lo, hi):
    return jnp.pad(a, [(0, 0)] * (a.ndim - 1) + [(lo, hi)])


def _pad_rows(a, hi):
    return jnp.pad(a, [(0, 0)] * (a.ndim - 2) + [(0, hi), (0, 0)])


def _split_cols(w, sizes):
    out, o = [], 0
    for n in sizes:
        out.append(w[..., o:o + n])
        o += n
    return out


def _in_weights(w_in):
    cb, cc, ch, su, dq, dk, dv, cq, ckv, kr = _split_cols(
        w_in, (CONV_DIM, CONV_DIM, CONV_DIM, SSM_DIM, 3 * DIL_GW, 3 * DIL_GW, 3 * DIL_GW, MLA_Q_RANK, MLA_KV_RANK,
               MLA_ROPE))
    ws = [cb, cc, ch, _pad_last(su, 0, SSM_PAD - SSM_DIM)]
    kinds = ["rows", "rows", "rows", "time_major"]
    for g in range(len(DIL_PATTERNS)):
        ws += [_pad_last(m[..., g * DIL_GW:(g + 1) * DIL_GW], 0, DIL_PLANES * LANE - DIL_GW) for m in (dq, dk, dv)]
        kinds += ["planes_rope_q", "planes_rope_k", "planes"]
    ws += [cq, ckv, _pad_last(kr, MLA_NOPE, MLA_HW - MLA_NOPE - MLA_ROPE)]
    kinds += ["mla_cq", "mla_ckv", "mla_kr"]
    return [w.astype(BF16) for w in ws], kinds


def _mla_weights(w_uq, w_ukv):
    depth = w_uq.shape[0]
    q = w_uq.reshape(depth, MLA_Q_RANK, MLA_HEADS, MLA_QK)
    wq = _pad_last(q, 0, MLA_HW - MLA_QK).reshape(depth, MLA_Q_RANK, MLA_HEADS * MLA_HW)
    kv = w_ukv.reshape(depth, MLA_KV_RANK, MLA_HEADS, MLA_NOPE + MLA_V)
    wk = _pad_last(kv[..., :MLA_NOPE], 0, MLA_HW - MLA_NOPE).reshape(depth, MLA_KV_RANK, -1)
    wv = _pad_last(kv[..., MLA_NOPE:], 0, MLA_HW - MLA_V).reshape(depth, MLA_KV_RANK, -1)
    return wq.astype(BF16), wk.astype(BF16), wv.astype(BF16)


def _out_weights(w_out):
    o_ssm, o_dil, o_mla = CONV_DIM, CONV_DIM + SSM_DIM, CONV_DIM + SSM_DIM + 3 * DIL_GW
    ws = [w_out[:, :o_ssm], _pad_rows(w_out[:, o_ssm:o_dil], SSM_PAD - SSM_DIM)]
    ws += [jnp.concatenate([_pad_rows(w_out[:, o_dil + g * DIL_GW:o_dil + (g + 1) * DIL_GW],
                                      DIL_PLANES * LANE - DIL_GW) for g in range(3)], axis=1)]
    ws += [_pad_rows(w_out[:, o_mla:], MLA_VW - MLA_HEADS * MLA_V)]
    return [w.astype(BF16) for w in ws]


def _router_weights(w_group, b_group, w_router, b_router):
    w = _pad_last(jnp.concatenate([w_group, w_router], axis=-1), 0, ROUTE_W - MOE_GROUPS - N_EXPERTS)
    hi = w.astype(BF16)
    lo = (w - hi.astype(F32)).astype(BF16)
    b = _pad_last(jnp.concatenate([b_group, b_router], axis=-1), 0, ROUTE_W - MOE_GROUPS - N_EXPERTS)
    return jnp.concatenate([hi, lo], axis=-1), b[:, None, :]


def kernel(x, p, positions, norm_mix_g, w_in, conv_w, ssm_lam_re, ssm_lam_im, ssm_log_dt, ssm_b_re, ssm_b_im, ssm_c_re, ssm_c_im, ssm_d, ssm_glu_w, ssm_glu_b, mla_q_norm_g, mla_w_uq, mla_kv_norm_g, mla_w_ukv, w_out, norm_ffn_g, w_group, b_group, w_router, b_router, moe_w_gate, moe_w_up, moe_w_down, norm_ple_g, ple_w_proj, ple_w_gate, final_norm_g):
    batch, seq, _ = x.shape
    n_tok = batch * seq
    depth = w_in.shape[0]
    dil_tabs = _rope_tables(positions, DIL_ROT, 0, DIL_HEAD_DIM, LANE // DIL_HEAD_DIM)
    mla_tabs = _rope_tables(positions, MLA_ROPE, MLA_NOPE, MLA_HW, 1)
    h = x.reshape(n_tok, D_MODEL)
    gains = lambda g: g[:, None, :]
    in_ws, in_kinds = _in_weights(w_in)
    mla_prm = (gains(mla_q_norm_g), gains(mla_kv_norm_g), *_mla_weights(mla_w_uq, mla_w_ukv))
    ssm_prm = _ssm_params(ssm_lam_re, ssm_lam_im, ssm_log_dt, ssm_b_re, ssm_b_im, ssm_c_re, ssm_c_im, ssm_d,
                          ssm_glu_w, ssm_glu_b)
    out_ws = _out_weights(w_out)
    wr, br = _router_weights(w_group, b_group, w_router, b_router)
    ple_wg, ple_wp = ple_w_gate.astype(BF16), ple_w_proj.astype(BF16)
    p_rows = p.reshape(depth, n_tok, PLE_DIM)
    for i in range(depth):
        z = _in_proj(h, gains(norm_mix_g), dil_tabs, mla_tabs, mla_prm, in_ws, in_kinds, batch, seq, i)
        cb, cc, ch, su = z[:4]
        q, k, v = z[13:]
        y_conv = _conv_mixer(cb, cc, ch, conv_w, batch, seq, i)
        y_ssm = _ssm_mixer(su, ssm_prm, batch, seq, i)
        outs, lses = [], []
        for g, (window, dil) in enumerate(DIL_PATTERNS):
            assert window // dil == DIL_BLK and (seq // dil) % DIL_BLK == 0
            o, l = _dil_group(z[4 + 3 * g], z[5 + 3 * g], z[6 + 3 * g], dil, batch, seq)
            outs.append(o)
            lses.append(l)
        y_mla = _mla_attn(q, k, v, batch, seq)
        h1, xn, route_i, route_g = _out_proj(
            (y_conv, y_ssm, outs, lses, y_mla), h, out_ws, gains(norm_ffn_g), wr, br, batch, seq, i)
        sched, slot_asg, n_used, nblk = _moe_plan(route_i, n_tok)
        y2 = _moe_experts(xn, sched, slot_asg, n_used, nblk, moe_w_gate, moe_w_up, moe_w_down, i)
        h = _ple(h1, y2, route_g, p_rows, gains(norm_ple_g), ple_wg, ple_wp, final_norm_g.reshape(1, -1),
                 final=(i == depth - 1), layer=i)
    return h.reshape(batch, seq, D_MODEL)
```

```python
import functools
import math

import jax
import jax.numpy as jnp
from jax import lax
from jax.experimental import pallas as pl
from jax.experimental.pallas import tpu as pltpu

D_MODEL = 2048
PLE_DIM = 256
ROPE_THETA = 500000.0
NORM_EPS = 1e-6

CONV_DIM = 448
CONV_K = 3

SSM_DIM = 448
SSM_GROUP = 16
SSM_GROUPS = SSM_DIM // SSM_GROUP
SSM_STATE = 64
SSM_N = SSM_GROUPS * SSM_STATE
SSM_PAD = 512

DIL_HEAD_DIM = 64
DIL_ROT = DIL_HEAD_DIM // 4
DIL_PATTERNS = ((128, 1), (512, 4), (2048, 16))
DIL_HPG = 3
DIL_GW = DIL_HPG * DIL_HEAD_DIM
DIL_PLANES = 2
DIL_BLK = 128
DIL_UNROLL = 4

MLA_HEADS = 9
MLA_Q_RANK = 384
MLA_KV_RANK = 256
MLA_NOPE = 64
MLA_ROPE = 32
MLA_V = 64
MLA_QK = MLA_NOPE + MLA_ROPE
MLA_HW = 128
MLA_VW = 640

MOE_GROUPS = 8
MOE_EPG = 8
N_EXPERTS = MOE_GROUPS * MOE_EPG
MOE_TOPK = 2
MOE_FF = 512
MOE_BLOCK = 256
ROUTE_W = 128

LANE = 128
PACK_SLABS = D_MODEL // 2 // LANE
ROW_CHUNK = 32
SLOT_SHIFT = 15
NEG_BIG = -1e30

BF16 = jnp.bfloat16
F32 = jnp.float32


def _cparams(sem, vmem_mb):
    return pltpu.CompilerParams(dimension_semantics=sem, vmem_limit_bytes=vmem_mb * 1024 * 1024)


def _rms(x, g):
    ms = jnp.mean(x * x, axis=-1, keepdims=True)
    return (x * lax.rsqrt(ms + NORM_EPS)) * g


def _sigmoid(x):
    return 1.0 / (1.0 + jnp.exp(-x))


def _full(a):
    return pl.BlockSpec(a.shape, lambda *_: (0,) * a.ndim)


def _param(a, layer):
    return pl.BlockSpec((None,) + a.shape[1:], lambda *_: (layer,) + (0,) * (a.ndim - 1),
                        pipeline_mode=pl.Buffered(1))


_HI16 = 0xFFFF0000


def _pack_rows(ref, base, x):
    n = x.shape[0]
    bits = lax.bitcast_convert_type(x, jnp.uint32)
    for c in range(PACK_SLABS):
        lo = bits[:, c * LANE:(c + 1) * LANE] >> 16
        hi = bits[:, D_MODEL // 2 + c * LANE:D_MODEL // 2 + (c + 1) * LANE] & jnp.uint32(_HI16)
        ref[pl.ds(base + c, n, stride=PACK_SLABS), :] = lo | hi


def _unpack_rows(ref, base, n):
    lo, hi = [], []
    for c in range(PACK_SLABS):
        w = ref[pl.ds(base + c, n, stride=PACK_SLABS), :]
        lo.append(lax.bitcast_convert_type(w << 16, F32))
        hi.append(lax.bitcast_convert_type(w & jnp.uint32(_HI16), F32))
    return jnp.concatenate(lo + hi, axis=1)


def _in_proj_kernel(x_ref, g_ref, dc_ref, ds1_ref, ds2_ref, mc_ref, ms1_ref, ms2_ref, gq_ref, gkv_ref,
                    wq_ref, wk_ref, wv_ref, *refs, kinds):
    n_out = len(kinds)
    w_refs, o_refs = refs[:n_out], refs[n_out:]
    xn = _rms(x_ref[...], g_ref[...]).astype(BF16)
    tabs = [jnp.concatenate([t[...]] * DIL_PLANES, axis=1) for t in (dc_ref, ds1_ref, ds2_ref)]
    latent = {}
    outs = iter(o_refs)
    for kind, w_ref in zip(kinds, w_refs):
        res = jnp.dot(xn, w_ref[...], preferred_element_type=F32)
        if kind.startswith("mla"):
            latent[kind] = res
            continue
        o_ref = next(outs)
        if kind.startswith("planes"):
            if kind != "planes":
                res = _rope_lanes(res, *tabs, DIL_ROT // 2, DIL_PLANES * LANE)
            if kind == "planes_rope_q":
                res = res * (1.0 / math.sqrt(DIL_HEAD_DIM))
            for j in range(o_ref.shape[0]):
                o_ref[j] = res[:, j * LANE:(j + 1) * LANE]
        else:
            o_ref[...] = res
    q_ref, k_ref, v_ref = outs
    rope = functools.partial(_rope_lanes, c=mc_ref[...], s1=ms1_ref[...], s2=ms2_ref[...], half=MLA_ROPE // 2,
                             width=MLA_HW)
    qn = _rms(latent["mla_cq"], gq_ref[...]).astype(BF16)
    kvn = _rms(latent["mla_ckv"], gkv_ref[...]).astype(BF16)
    q = jnp.dot(qn, wq_ref[...], preferred_element_type=F32)
    kn = jnp.dot(kvn, wk_ref[...], preferred_element_type=F32)
    lane = lax.broadcasted_iota(jnp.int32, (1, MLA_HEADS * MLA_HW), 1)
    ones_col = (lane % MLA_HW == MLA_V).astype(F32)
    v_ref[...] = (jnp.dot(kvn, wv_ref[...], preferred_element_type=F32) + ones_col).astype(BF16)
    kr = rope(latent["mla_kr"])
    scale = math.log2(math.e) / math.sqrt(MLA_QK)
    for h in range(MLA_HEADS):
        sl = slice(h * MLA_HW, (h + 1) * MLA_HW)
        q_ref[:, sl] = (rope(q[:, sl]) * scale).astype(BF16)
        k_ref[:, sl] = (kn[:, sl] + kr).astype(BF16)


def _in_proj(h, g, dil_tabs, mla_tabs, mla_prm, weights, kinds, batch, seq, layer, tm=256):
    n_s = seq // tm

    def rspec(n):
        return pl.BlockSpec((tm, n), lambda b, i: (b * n_s + i, 0))

    in_specs = [rspec(D_MODEL), _param(g, layer)] + [rspec(LANE)] * 6 + [_param(a, layer) for a in mla_prm]
    in_specs += [_param(w, layer) for w in weights]
    out_shape, out_specs = [], []
    for kind, w in zip(kinds, weights):
        n = w.shape[-1]
        if kind.startswith("mla"):
            continue
        if kind == "time_major":
            out_shape.append(jax.ShapeDtypeStruct((seq, batch * n), F32))
            out_specs.append(pl.BlockSpec((tm, n), lambda b, i: (i, b)))
        elif kind.startswith("planes"):
            out_shape.append(jax.ShapeDtypeStruct((n // LANE, batch * seq, LANE), F32))
            out_specs.append(pl.BlockSpec((n // LANE, tm, LANE), lambda b, i: (0, b * n_s + i, 0)))
        else:
            out_shape.append(jax.ShapeDtypeStruct((batch * seq, n), F32))
            out_specs.append(rspec(n))
    for n in (MLA_HEADS * MLA_HW,) * 3:
        out_shape.append(jax.ShapeDtypeStruct((batch * seq, n), BF16))
        out_specs.append(rspec(n))
    return pl.pallas_call(
        functools.partial(_in_proj_kernel, kinds=tuple(kinds)),
        grid=(batch, n_s), in_specs=in_specs, out_specs=out_specs, out_shape=out_shape,
        compiler_params=_cparams(("parallel", "parallel"), 56), name="in_proj",
    )(h, g, *dil_tabs, *mla_tabs, *mla_prm, *weights)


SSM_NP = SSM_N // LANE
SSM_SCAN_PLANES = 4
_SSM_SPLIT = ((0, 256, 0, 1024), (256, 256, 1024, 768))


def _gelu_tanh(x):
    return 0.5 * x * (1.0 + jnp.tanh(math.sqrt(2.0 / math.pi) * (x + 0.044715 * (x * x * x))))


def _ssm_kernel(u_ref, bf0_ref, bf1_ref, are_ref, aim_ref, cc0_ref, cc1_ref, d_ref, gw_ref, gb_ref, o_ref,
                xs_ref, st_ref, *, batch, tc):
    per_tile = 8 // batch
    npk = SSM_NP // per_tile

    @pl.when(pl.program_id(0) == 0)
    def _():
        st_ref[...] = jnp.zeros_like(st_ref)

    def plane_rows(p, b):
        return p % npk, pl.ds((p // npk) * batch + b, tc, stride=8)

    for b in range(batch):
        for (l0, lw, s0, sw), bf_ref in zip(_SSM_SPLIT, (bf0_ref, bf1_ref)):
            ub = u_ref[:, b * SSM_PAD + l0:b * SSM_PAD + l0 + lw].astype(BF16)
            r = jnp.dot(ub, bf_ref[...], preferred_element_type=F32)
            for q in range(sw // LANE):
                pk, rows = plane_rows(s0 // LANE + q, b)
                xs_ref[pk, rows, :] = r[:, q * LANE:(q + 1) * LANE]
                xs_ref[npk + pk, rows, :] = r[:, sw + q * LANE:sw + (q + 1) * LANE]

    def decay(a_ref, pk):
        return jnp.concatenate([jnp.broadcast_to(a_ref[:, (pk + j * npk) * LANE:(pk + j * npk + 1) * LANE],
                                                 (batch, LANE)) for j in range(per_tile)], axis=0)

    for p0 in range(0, npk, SSM_SCAN_PLANES):
        pks = list(range(p0, min(p0 + SSM_SCAN_PLANES, npk)))
        ar = [decay(are_ref, pk) for pk in pks]
        ai = [decay(aim_ref, pk) for pk in pks]

        def body(t, carry, pks=pks, ar=ar, ai=ai):
            rows = pl.ds(pl.multiple_of(t * 8, 8), 8)
            new = []
            for idx, pk in enumerate(pks):
                sr, si = carry[idx]
                sr, si = (ar[idx] * sr - ai[idx] * si + xs_ref[pk, rows, :],
                          ar[idx] * si + ai[idx] * sr + xs_ref[npk + pk, rows, :])
                xs_ref[pk, rows, :] = sr
                xs_ref[npk + pk, rows, :] = si
                new.append((sr, si))
            return tuple(new)

        init = tuple((st_ref[pk], st_ref[npk + pk]) for pk in pks)
        for pk, (sr, si) in zip(pks, lax.fori_loop(0, tc, body, init, unroll=2)):
            st_ref[pk] = sr
            st_ref[npk + pk] = si

    for b in range(batch):
        ys = []
        for (l0, lw, s0, sw), cc_ref in zip(_SSM_SPLIT, (cc0_ref, cc1_ref)):
            planes = [plane_rows(p, b) for p in range(s0 // LANE, (s0 + sw) // LANE)]
            xb = jnp.concatenate([xs_ref[pk, rows, :] for pk, rows in planes]
                                 + [xs_ref[npk + pk, rows, :] for pk, rows in planes], axis=1)
            ys.append(jnp.dot(xb.astype(BF16), cc_ref[...], preferred_element_type=F32))
        y = jnp.concatenate(ys, axis=1) + d_ref[...] * u_ref[:, b * SSM_PAD:(b + 1) * SSM_PAD]
        g = _gelu_tanh(y)
        o_ref[:, b * SSM_PAD:(b + 1) * SSM_PAD] = g * _sigmoid(
            jnp.dot(g.astype(BF16), gw_ref[...], preferred_element_type=F32) + gb_ref[...])


def _ssm_mixer(su, prm, batch, seq, layer, tc=256):
    spec = pl.BlockSpec((tc, batch * SSM_PAD), lambda c: (c, 0))
    args = (prm["bf0"], prm["bf1"], prm["a_re"], prm["a_im"], prm["cc0"], prm["cc1"], prm["d"], prm["gw"], prm["gb"])
    return pl.pallas_call(
        functools.partial(_ssm_kernel, batch=batch, tc=tc),
        grid=(seq // tc,), in_specs=[spec] + [_param(a, layer) for a in args], out_specs=spec,
        out_shape=jax.ShapeDtypeStruct((seq, batch * SSM_PAD), F32),
        scratch_shapes=[pltpu.VMEM((2 * SSM_NP * batch // 8, tc * 8, LANE), F32),
                        pltpu.VMEM((2 * SSM_NP * batch // 8, 8, LANE), F32)],
        compiler_params=_cparams(("arbitrary",), 48), name="ssm_mixer",
    )(su, *args)


def _ssm_params(lam_re, lam_im, log_dt, b_re, b_im, c_re, c_im, d_skip, glu_w, glu_b):
    depth = lam_re.shape[0]
    dt = jnp.exp(log_dt)[..., None]
    mag = jnp.exp(lam_re * dt)
    a_re = mag * jnp.cos(lam_im * dt)
    a_im = mag * jnp.sin(lam_im * dt)
    nr, ni = a_re - 1.0, a_im
    den = lam_re * lam_re + lam_im * lam_im
    f_re = (nr * lam_re + ni * lam_im) / den
    f_im = (ni * lam_re - nr * lam_im) / den
    bfr = f_re[..., None] * b_re - f_im[..., None] * b_im
    bfi = f_re[..., None] * b_im + f_im[..., None] * b_re
    same_group = (jnp.arange(SSM_DIM)[:, None] // SSM_GROUP == jnp.arange(SSM_N)[None, :] // SSM_STATE).astype(F32)

    def in_blockdiag(m):
        cols = jnp.transpose(m, (0, 3, 1, 2)).reshape(depth, SSM_GROUP, SSM_N)
        return jnp.tile(cols, (1, SSM_GROUPS, 1)) * same_group

    def out_blockdiag(m):
        rows = jnp.transpose(m, (0, 1, 3, 2)).reshape(depth, SSM_N, SSM_GROUP)
        return jnp.tile(rows, (1, 1, SSM_GROUPS)) * same_group.T

    pad = SSM_PAD - SSM_DIM
    b_re_d, b_im_d = (_pad_rows(in_blockdiag(m), pad) for m in (bfr, bfi))
    c_re_d, c_im_d = (_pad_last(out_blockdiag(m), 0, pad) for m in (c_re, -c_im))
    prm = {}
    for c, (l0, lw, s0, sw) in enumerate(_SSM_SPLIT):
        prm[f"bf{c}"] = jnp.concatenate([b_re_d[:, l0:l0 + lw, s0:s0 + sw], b_im_d[:, l0:l0 + lw, s0:s0 + sw]],
                                        axis=2).astype(BF16)
        prm[f"cc{c}"] = jnp.concatenate([c_re_d[:, s0:s0 + sw, l0:l0 + lw], c_im_d[:, s0:s0 + sw, l0:l0 + lw]],
                                        axis=1).astype(BF16)
    return {
        **prm,
        "a_re": a_re.reshape(depth, 1, SSM_N), "a_im": a_im.reshape(depth, 1, SSM_N),
        "d": _pad_last(d_skip, 0, pad)[:, None, :],
        "gw": _pad_last(_pad_rows(glu_w, pad), 0, pad).astype(BF16),
        "gb": _pad_last(glu_b, 0, pad)[:, None, :],
    }


def _rope_lanes(x, c, s1, s2, half, width):
    return x * c + pltpu.roll(x, width - half, 1) * s1 + pltpu.roll(x, half, 1) * s2


def _dil_kernel(q_ref, k_ref, v_ref, o_ref, l_ref, *, dil):
    seq = q_ref.shape[1]
    nb = (seq // dil) // DIL_BLK
    width = DIL_PLANES * LANE
    lane = lax.broadcasted_iota(jnp.int32, (1, width), 1)
    cmask = [(lane // DIL_HEAD_DIM == c).astype(F32) for c in range(DIL_HPG)]
    qi = lax.broadcasted_iota(jnp.int32, (DIL_HPG * DIL_BLK, 2 * DIL_BLK), 0) % DIL_BLK
    ki = lax.broadcasted_iota(jnp.int32, (DIL_HPG * DIL_BLK, 2 * DIL_BLK), 1)
    band = (ki >= qi) & (ki <= qi + DIL_BLK)

    def planes(ref, rows):
        return jnp.concatenate([ref[j, rows, :] for j in range(DIL_PLANES)], axis=1)

    def rows_of(m, blk):
        if dil == 1:
            return pl.ds(pl.multiple_of(blk * DIL_BLK, DIL_BLK), DIL_BLK)
        return pl.ds(blk * DIL_BLK * dil + m, DIL_BLK, stride=dil)

    def blocks(it, carry):
        ids = [it * DIL_UNROLL + u for u in range(DIL_UNROLL)]
        mis = [(idx // nb, idx % nb) for idx in ids]
        rows_q = [rows_of(m, i) for m, i in mis]
        rows_p = [rows_of(m, jnp.maximum(i - 1, 0)) for m, i in mis]
        scores, vcats = [], []
        for rq, rp in zip(rows_q, rows_p):
            q = planes(q_ref, rq)
            kcat = jnp.concatenate([planes(k_ref, rp), planes(k_ref, rq)], axis=0).astype(BF16)
            vcats.append(jnp.concatenate([planes(v_ref, rp), planes(v_ref, rq)], axis=0).astype(BF16))
            qs = jnp.concatenate([q * cm for cm in cmask], axis=0).astype(BF16)
            scores.append(lax.dot_general(qs, kcat, (((1,), (1,)), ((), ())), preferred_element_type=F32))
        soft = []
        for s, (m, i) in zip(scores, mis):
            s = jnp.where(band & ((ki >= DIL_BLK) | (i > 0)), s, NEG_BIG)
            mx = jnp.max(s, axis=-1, keepdims=True)
            p = jnp.exp(s - mx)
            den = jnp.sum(p, axis=-1, keepdims=True)
            soft.append((p.astype(BF16), den, mx + jnp.log(den)))
        for (p, den, lse), vcat, rq in zip(soft, vcats, rows_q):
            o = jnp.dot(p, vcat, preferred_element_type=F32) / den
            out = jnp.zeros((DIL_BLK, width), F32)
            lout = jnp.zeros((DIL_BLK, width), F32)
            for c, cm in enumerate(cmask):
                out = out + cm * o[c * DIL_BLK:(c + 1) * DIL_BLK, :]
                lout = lout + cm * lse[c * DIL_BLK:(c + 1) * DIL_BLK, :]
            for j in range(DIL_PLANES):
                o_ref[j, rq, :] = out[:, j * LANE:(j + 1) * LANE]
                l_ref[j, rq, :] = lout[:, j * LANE:(j + 1) * LANE]
        return carry

    lax.fori_loop(0, dil * nb // DIL_UNROLL, blocks, 0)


def _dil_group(q, k, v, dil, batch, seq):
    spec = pl.BlockSpec((DIL_PLANES, seq, LANE), lambda b: (0, b, 0))
    sds = jax.ShapeDtypeStruct((DIL_PLANES, batch * seq, LANE), F32)
    return pl.pallas_call(
        functools.partial(_dil_kernel, dil=dil), grid=(batch,), in_specs=[spec] * 3,
        out_specs=[spec, spec], out_shape=[sds, sds],
        compiler_params=_cparams(("parallel",), 48), name=f"dil_attn_d{dil}",
    )(q, k, v)


def _mla_attn_kernel(q_ref, k_ref, v_ref, o_ref, *, tq, group):
    qi = pl.program_id(1)
    row = lax.broadcasted_iota(jnp.int32, (tq, tq), 0)
    col = lax.broadcasted_iota(jnp.int32, (tq, tq), 1)
    causal = col <= row

    def heads_out(heads):
        qs = [q_ref[:, h * MLA_HW:(h + 1) * MLA_HW] for h in heads]

        def step(kj, carry, masked):
            rows = pl.ds(pl.multiple_of(kj * tq, tq), tq)
            scores = [lax.dot_general(q, k_ref[rows, h * MLA_HW:(h + 1) * MLA_HW], (((1,), (1,)), ((), ())),
                                      preferred_element_type=F32) for h, q in zip(heads, qs)]
            stats = []
            for s, (m, acc) in zip(scores, carry):
                if masked:
                    s = jnp.where(causal, s, NEG_BIG)
                mn = jnp.maximum(m, jnp.max(s, axis=-1, keepdims=True))
                stats.append((mn, jnp.exp2(m - mn), jnp.exp2(s - mn).astype(BF16)))
            out = []
            for h, (mn, alpha, p), (_, acc) in zip(heads, stats, carry):
                v = v_ref[rows, h * MLA_HW:(h + 1) * MLA_HW]
                out.append((mn, alpha * acc + jnp.dot(p, v, preferred_element_type=F32)))
            return tuple(out)

        init = tuple((jnp.full((tq, 1), NEG_BIG, F32), jnp.zeros((tq, MLA_HW), F32)) for _ in heads)
        carry = lax.fori_loop(0, qi, functools.partial(step, masked=False), init)
        return [acc / acc[:, MLA_V:MLA_V + 1] for _, acc in step(qi, carry, True)]

    for h0 in range(0, MLA_HEADS, group):
        heads = list(range(h0, min(h0 + group, MLA_HEADS)))
        for h, o in zip(heads, heads_out(heads)):
            lo = h * MLA_V
            if h % 2:
                o_ref[:, lo:lo + MLA_V] = pltpu.roll(o, MLA_V, 1)[:, MLA_V:2 * MLA_V]
            else:
                o_ref[:, lo:lo + MLA_V] = o[:, 0:MLA_V]
    if MLA_HEADS % 2:
        o_ref[:, MLA_HEADS * MLA_V:] = jnp.zeros((tq, MLA_VW - MLA_HEADS * MLA_V), F32)


def _mla_attn(q, k, v, batch, seq, tq=256, group=9):
    hw = MLA_HEADS * MLA_HW
    n_q = seq // tq
    return pl.pallas_call(
        functools.partial(_mla_attn_kernel, tq=tq, group=group), grid=(batch, n_q),
        in_specs=[pl.BlockSpec((tq, hw), lambda b, i: (b * n_q + i, 0)),
                  pl.BlockSpec((seq, hw), lambda b, i: (b, 0)),
                  pl.BlockSpec((seq, hw), lambda b, i: (b, 0))],
        out_specs=pl.BlockSpec((tq, MLA_VW), lambda b, i: (b * n_q + i, 0)),
        out_shape=jax.ShapeDtypeStruct((batch * seq, MLA_VW), F32),
        compiler_params=_cparams(("parallel", "arbitrary"), 48), name="mla_attn",
    )(q, k, v)


def _out_proj_kernel(cb_ref, cc_ref, ch_ref, cc_halo_ref, ch_halo_ref, ys_ref, od0_ref, od1_ref, od2_ref,
                     ld0_ref, ld1_ref, ld2_ref, ym_ref, h_ref,
                     cw_ref, wc_ref, ws_ref, wd_ref, wm_ref,
                     gf_ref, wr_ref, br_ref,
                     h1_ref, xn_ref, ri_ref, rg_ref, u_ref):
    def mm(y, w_ref):
        return jnp.dot(y.astype(BF16), w_ref[...], preferred_element_type=F32)

    tm = cb_ref.shape[0]
    u_ref[0:8, :] = jnp.where(pl.program_id(1) > 0, cc_halo_ref[...] * ch_halo_ref[...], 0.0)
    u_ref[8:8 + tm, :] = cc_ref[...] * ch_ref[...]
    y_conv = cb_ref[...] * (cw_ref[2:3, :] * u_ref[8:8 + tm, :] + cw_ref[1:2, :] * u_ref[7:7 + tm, :]
                            + cw_ref[0:1, :] * u_ref[6:6 + tm, :])

    y_dil = [None] * (3 * DIL_PLANES)
    for j in range(DIL_PLANES):
        la, lb, lc = ld0_ref[j], ld1_ref[j], ld2_ref[j]
        mx = jnp.maximum(jnp.maximum(la, lb), lc)
        ea, eb, ec = jnp.exp(la - mx), jnp.exp(lb - mx), jnp.exp(lc - mx)
        inv = 1.0 / (ea + eb + ec)
        for g, (o_ref, e) in enumerate(((od0_ref, ea), (od1_ref, eb), (od2_ref, ec))):
            y_dil[g * DIL_PLANES + j] = o_ref[j] * (e * inv)
    mix = (mm(y_conv, wc_ref) + mm(ys_ref[...], ws_ref) + mm(jnp.concatenate(y_dil, axis=1), wd_ref)
           + mm(ym_ref[...], wm_ref))
    h1 = h_ref[...] + mix
    h1_ref[...] = h1
    hn = _rms(h1, gf_ref[...])
    hi = hn.astype(BF16)
    hi32 = hi.astype(F32)
    lo = (hn - hi32).astype(BF16)
    _pack_rows(xn_ref, 0, hi32)
    r_hi = jnp.dot(hi, wr_ref[...], preferred_element_type=F32)
    r_lo = jnp.dot(lo, wr_ref[:, 0:ROUTE_W], preferred_element_type=F32)
    logits = r_hi[:, 0:ROUTE_W] + r_hi[:, ROUTE_W:2 * ROUTE_W] + r_lo + br_ref[...]
    lane = lax.broadcasted_iota(jnp.int32, logits.shape, 1)
    gl = jnp.where(lane < MOE_GROUPS, logits, NEG_BIG)
    gmax = jnp.max(gl, axis=-1, keepdims=True)
    g_top_p = 1.0 / jnp.sum(jnp.exp(gl - gmax), axis=-1, keepdims=True)
    g_idx = jnp.min(jnp.where(gl == gmax, lane, ROUTE_W), axis=-1, keepdims=True)
    in_group = (lane >= MOE_GROUPS) & (lane < MOE_GROUPS + N_EXPERTS) & (((lane - MOE_GROUPS) >> 3) == g_idx)
    el = jnp.where(in_group, logits, NEG_BIG)
    v1 = jnp.max(el, axis=-1, keepdims=True)
    i1 = jnp.min(jnp.where(el == v1, lane, ROUTE_W), axis=-1, keepdims=True)
    el2 = jnp.where(lane == i1, NEG_BIG, el)
    v2 = jnp.max(el2, axis=-1, keepdims=True)
    i2 = jnp.min(jnp.where(el2 == v2, lane, ROUTE_W), axis=-1, keepdims=True)
    e2 = jnp.exp(v2 - v1)
    w1 = g_top_p / (1.0 + e2)
    w2 = g_top_p * e2 / (1.0 + e2)
    ri_ref[...] = jnp.where(lane == 0, i1 - MOE_GROUPS, jnp.where(lane == 1, i2 - MOE_GROUPS, 0))
    rg_ref[...] = jnp.where(lane == 0, w1, jnp.where(lane == 1, w2, 0.0))


def _out_proj(ys, h, conv_w, ws, gf, wr, br, batch, seq, layer, tm=256):
    rows = batch * seq
    n_s = seq // tm
    (cb, cc, ch), ys_tm, dil_o, dil_l, ym = ys

    def rspec(n):
        return pl.BlockSpec((tm, n), lambda b, i: (b * n_s + i, 0))

    halo = pl.BlockSpec((8, CONV_DIM), lambda b, i: (jnp.maximum((b * n_s + i) * (tm // 8) - 1, 0), 0))
    pspec = pl.BlockSpec((DIL_PLANES, tm, LANE), lambda b, i: (0, b * n_s + i, 0))
    in_specs = [rspec(CONV_DIM)] * 3 + [halo, halo, pl.BlockSpec((tm, SSM_PAD), lambda b, i: (i, b))] + [pspec] * 6
    in_specs += [rspec(MLA_VW), rspec(D_MODEL)]
    in_specs += [_param(a, layer) for a in (conv_w, *ws, gf, wr, br)]
    return pl.pallas_call(
        _out_proj_kernel, grid=(batch, n_s), in_specs=in_specs,
        out_specs=[rspec(D_MODEL), pl.BlockSpec((tm * PACK_SLABS, LANE), lambda b, i: (b * n_s + i, 0)),
                   rspec(ROUTE_W), rspec(ROUTE_W)],
        out_shape=[jax.ShapeDtypeStruct((rows, D_MODEL), F32),
                   jax.ShapeDtypeStruct((rows * PACK_SLABS, LANE), jnp.uint32),
                   jax.ShapeDtypeStruct((rows, ROUTE_W), jnp.int32), jax.ShapeDtypeStruct((rows, ROUTE_W), F32)],
        scratch_shapes=[pltpu.VMEM((tm + 8, CONV_DIM), F32)],
        compiler_params=_cparams(("parallel", "parallel"), 52), name="out_proj_router",
    )(cb, cc, ch, cc, ch, ys_tm, *dil_o, *dil_l, ym, h, conv_w, *ws, gf, wr, br)


def _moe_plan(route_i, n_tok):
    n_asg = n_tok * MOE_TOPK
    nblk = (n_asg + N_EXPERTS * (MOE_BLOCK - 1) + MOE_BLOCK - 1) // MOE_BLOCK
    flat_e = route_i[:, :MOE_TOPK].reshape(-1)
    onehot = (flat_e[:, None] == jnp.arange(N_EXPERTS, dtype=jnp.int32)[None, :]).astype(F32)
    oh3 = onehot.reshape(n_asg // MOE_BLOCK, MOE_BLOCK, N_EXPERTS)
    within = jnp.einsum("ij,bje->bie", jnp.tril(jnp.ones((MOE_BLOCK, MOE_BLOCK), F32)), oh3)
    chunk_tot = within[:, -1, :]
    n_chunk = chunk_tot.shape[0]
    chunk_off = jnp.dot(jnp.tril(jnp.ones((n_chunk, n_chunk), F32), -1), chunk_tot)
    rank = jnp.sum((within + chunk_off[:, None, :]) * oh3, axis=-1).reshape(n_asg).astype(jnp.int32) - 1
    counts = (chunk_off[-1] + chunk_tot[-1]).astype(jnp.int32)
    blocks_of = (counts + MOE_BLOCK - 1) // MOE_BLOCK
    upto = jnp.arange(N_EXPERTS)[:, None] <= jnp.arange(N_EXPERTS)[None, :]
    blk_end = jnp.sum(jnp.where(upto, blocks_of[:, None], 0), axis=0)
    blk_start = blk_end - blocks_of
    dest = jnp.dot(onehot, blk_start.astype(F32)).astype(jnp.int32) * MOE_BLOCK + rank
    in_blk = (dest[:, None] // MOE_BLOCK == jnp.arange(nblk, dtype=jnp.int32)[None, :]).astype(F32)
    at_off = (dest[:, None] % MOE_BLOCK == jnp.arange(MOE_BLOCK, dtype=jnp.int32)[None, :]).astype(F32)
    asg = jnp.arange(n_asg, dtype=jnp.int32)
    a_hi, a_lo = (asg // MOE_BLOCK).astype(F32)[:, None], (asg % MOE_BLOCK).astype(F32)[:, None]
    inv = lambda v: jnp.einsum("ab,ao->bo", in_blk * v, at_off)
    slot_a = jnp.where(inv(1.0) > 0.5, inv(a_hi) * MOE_BLOCK + inv(a_lo), -1.0).astype(jnp.int32).reshape(-1)
    slot = jnp.arange(nblk * MOE_BLOCK, dtype=jnp.int32)
    dump = MOE_TOPK * n_tok + ((slot // MOE_BLOCK) % 2) * MOE_BLOCK + slot % MOE_BLOCK
    assert MOE_TOPK * n_tok + 2 * MOE_BLOCK <= 1 << SLOT_SHIFT
    slot_asg = jnp.where(slot_a >= 0, ((slot_a >> 1) << SLOT_SHIFT) | ((slot_a & 1) * n_tok + (slot_a >> 1)), dump)
    n_used = blk_end[-1].astype(jnp.int32)
    blk = jnp.arange(nblk, dtype=jnp.int32)
    block_expert = jnp.minimum(jnp.sum(blk_end[None, :] <= blk[:, None], axis=1), N_EXPERTS - 1).astype(jnp.int32)
    mine = block_expert[:, None] == jnp.arange(N_EXPERTS, dtype=jnp.int32)[None, :]
    lookup = lambda table: jnp.sum(jnp.where(mine, table[None, :], 0), axis=1)
    run_start, run_end = lookup(blk_start), lookup(blk_end)
    is_first = ((blk == run_start) & (blk < n_used)).astype(jnp.int32)
    used = (blocks_of > 0).astype(jnp.int32)
    nth_used = jnp.sum(jnp.where(upto, used[:, None], 0), axis=0) - used
    w_slot = lookup(nth_used) % 2
    starts_at = (blk_start[None, :] == run_end[:, None]) & (blocks_of[None, :] > 0)
    nxt = jnp.where(run_end < n_used, jnp.sum(jnp.where(starts_at, jnp.arange(N_EXPERTS)[None, :], 0), axis=1), -1)
    rows_left = jnp.clip(lookup(counts) - (blk - run_start) * MOE_BLOCK, 0, MOE_BLOCK)
    sched = jnp.stack([block_expert, is_first, w_slot, nxt, jnp.where(blk < n_used, rows_left, 0)])
    sched = jnp.pad(sched.astype(jnp.int32), ((0, 0), (0, 1)))
    return sched, slot_asg, n_used.reshape(1), nblk


def _moe_kernel(sched_ref, asg_ref, nu_ref, x_hbm, wg_hbm, wu_hbm, wd_hbm, y_hbm,
                xbuf, ybuf, wgf, wuf, wdf, wgb, wub, wdb, gsem, ssem, wsem, *, n_tok, layer):
    i = pl.program_id(0)
    n_used = nu_ref[0]
    slot = i % 2
    buf_rows = MOE_BLOCK * PACK_SLABS
    active = i < n_used
    first = active & (sched_ref[1, i] == 1)
    w_slot = sched_ref[2, i]

    def weights(e, ws):
        return [pltpu.make_async_copy(w_hbm.at[layer, e], buf.at[ws], wsem.at[ws])
                for w_hbm, buf in ((wg_hbm, wgf), (wu_hbm, wuf), (wd_hbm, wdf))]

    def slab_rows(row):
        return pl.ds(pl.multiple_of(row * PACK_SLABS, PACK_SLABS), PACK_SLABS)

    def gather_row(blk, sl, r):
        tok = asg_ref[blk * MOE_BLOCK + r] >> SLOT_SHIFT
        return pltpu.make_async_copy(x_hbm.at[slab_rows(tok), :], xbuf.at[slab_rows(sl * MOE_BLOCK + r), :],
                                     gsem.at[sl])

    def scatter_row(blk, sl, r):
        dst = asg_ref[blk * MOE_BLOCK + r] & ((1 << SLOT_SHIFT) - 1)
        return pltpu.make_async_copy(ybuf.at[slab_rows(sl * MOE_BLOCK + r), :], y_hbm.at[slab_rows(dst), :],
                                     ssem.at[sl])

    def row_chunks(make, blk, sl, wait):
        n_valid = sched_ref[4, blk]
        if wait:
            buf, sem = (xbuf, gsem) if make is gather_row else (ybuf, ssem)
            shift = ROW_CHUNK.bit_length() - 1
            n_slabs = (((n_valid + (ROW_CHUNK - 1)) >> shift) << shift) * PACK_SLABS

            @pl.when(n_valid > 0)
            def _():
                pltpu.make_async_copy(buf.at[pl.ds(0, n_slabs), :], buf.at[pl.ds(0, n_slabs), :], sem.at[sl]).wait()
            return
        for c0 in range(0, MOE_BLOCK, ROW_CHUNK):
            @pl.when(c0 < n_valid)
            def _(c0=c0):
                for r in range(c0, c0 + ROW_CHUNK):
                    make(blk, sl, r).start()

    @pl.when(i == 0)
    def _():
        xbuf[...] = jnp.zeros_like(xbuf)
        ybuf[...] = jnp.zeros_like(ybuf)
        dump = pltpu.make_async_copy(ybuf, y_hbm.at[pl.ds(MOE_TOPK * n_tok * PACK_SLABS, 2 * buf_rows), :], ssem.at[0])
        dump.start()
        dump.wait()

    @pl.when((i == 0) & (n_used > 0))
    def _():
        for cp in weights(sched_ref[0, 0], 0):
            cp.start()
        row_chunks(gather_row, 0, 0, wait=False)

    @pl.when(first)
    def _():
        for cp in weights(sched_ref[0, i], w_slot):
            cp.wait()

    @pl.when(first & (sched_ref[3, i] >= 0))
    def _():
        for cp, queue in zip(weights(sched_ref[3, i], 1 - w_slot), (0, 1, 1)):
            cp.start(priority=queue)

    @pl.when(first)
    def _():
        wgb[...] = wgf[w_slot].astype(BF16)
        wub[...] = wuf[w_slot].astype(BF16)
        wdb[...] = wdf[w_slot].astype(BF16)

    @pl.when(active)
    def _():
        row_chunks(gather_row, i, slot, wait=True)
        row_chunks(gather_row, i + 1, 1 - slot, wait=False)

    @pl.when(active & (i >= 2))
    def _():
        row_chunks(scatter_row, i - 2, slot, wait=True)

    @pl.when(active)
    def _():
        x = _unpack_rows(xbuf, slot * buf_rows, MOE_BLOCK).astype(BF16)
        a = jnp.dot(x, wgb[...], preferred_element_type=F32)
        b = jnp.dot(x, wub[...], preferred_element_type=F32)
        act = (a * _sigmoid(a) * b).astype(BF16)
        y = jnp.dot(act, wdb[...], preferred_element_type=F32)
        _pack_rows(ybuf, slot * buf_rows, y.astype(BF16).astype(F32))
        row_chunks(scatter_row, i, slot, wait=False)

    @pl.when(i == n_used - 1)
    def _():
        row_chunks(scatter_row, i, slot, wait=True)

    @pl.when((i == n_used - 1) & (i >= 1))
    def _():
        row_chunks(scatter_row, i - 1, 1 - slot, wait=True)


def _moe_experts(xn, sched, slot_asg, n_used, nblk, w_gate, w_up, w_down, layer):
    n_tok = xn.shape[0] // PACK_SLABS
    buf_rows = 2 * MOE_BLOCK * PACK_SLABS
    any_spec = pl.BlockSpec(memory_space=pl.ANY)
    grid_spec = pltpu.PrefetchScalarGridSpec(
        num_scalar_prefetch=3, grid=(nblk,),
        in_specs=[any_spec, any_spec, any_spec, any_spec], out_specs=any_spec,
        scratch_shapes=[pltpu.VMEM((buf_rows, LANE), jnp.uint32), pltpu.VMEM((buf_rows, LANE), jnp.uint32),
                        pltpu.VMEM((2, D_MODEL, MOE_FF), F32), pltpu.VMEM((2, D_MODEL, MOE_FF), F32),
                        pltpu.VMEM((2, MOE_FF, D_MODEL), F32),
                        pltpu.VMEM((D_MODEL, MOE_FF), BF16), pltpu.VMEM((D_MODEL, MOE_FF), BF16),
                        pltpu.VMEM((MOE_FF, D_MODEL), BF16),
                        pltpu.SemaphoreType.DMA((2,)), pltpu.SemaphoreType.DMA((2,)), pltpu.SemaphoreType.DMA((2,))])
    return pl.pallas_call(
        functools.partial(_moe_kernel, n_tok=n_tok, layer=layer), grid_spec=grid_spec,
        out_shape=jax.ShapeDtypeStruct(((MOE_TOPK * n_tok + 2 * MOE_BLOCK) * PACK_SLABS, LANE), jnp.uint32),
        compiler_params=_cparams(("arbitrary",), 56), name="moe_experts",
    )(sched, slot_asg, n_used, xn, w_gate, w_up, w_down)


def _ple_kernel(h1_ref, y0_ref, y1_ref, rg_ref, p_ref, gp_ref, wg_ref, wp_ref, gfin_ref, o_ref, *, final):
    rg = rg_ref[...]
    tm = rg.shape[0]
    ffn = rg[:, 0:1] * _unpack_rows(y0_ref, 0, tm) + rg[:, 1:2] * _unpack_rows(y1_ref, 0, tm)
    h2 = h1_ref[...] + ffn
    hn = _rms(h2, gp_ref[...]).astype(BF16)
    gate = _sigmoid(jnp.dot(hn, wg_ref[...], preferred_element_type=F32))
    pe = jnp.dot(p_ref[...].astype(BF16), wp_ref[...], preferred_element_type=F32)
    h3 = h2 + pe * gate
    o_ref[...] = _rms(h3, gfin_ref[...]) if final else h3


def _ple(h1, y2, rg, p, gp, wg, wp, gfin, final, layer, tm=512):
    rows = h1.shape[0]
    n_t = rows // tm

    def rspec(n):
        return pl.BlockSpec((tm, n), lambda i: (i, 0))

    def yspec(k):
        return pl.BlockSpec((tm * PACK_SLABS, LANE), lambda i: (k * n_t + i, 0))

    return pl.pallas_call(
        functools.partial(_ple_kernel, final=final), grid=(n_t,),
        in_specs=[rspec(D_MODEL), yspec(0), yspec(1), rspec(ROUTE_W),
                  pl.BlockSpec((None, tm, PLE_DIM), lambda i: (layer, i, 0)), _param(gp, layer), _param(wg, layer),
                  _param(wp, layer), _full(gfin)],
        out_specs=rspec(D_MODEL), out_shape=jax.ShapeDtypeStruct((rows, D_MODEL), F32),
        compiler_params=_cparams(("parallel",), 48), name="moe_combine_ple",
    )(h1, y2, y2, rg, p, gp, wg, wp, gfin)


def _rope_tables(positions, rot_dim, lead, period, reps):
    half = rot_dim // 2
    inv_freq = ROPE_THETA ** (-jnp.arange(0, rot_dim, 2, dtype=F32) / rot_dim)
    ang = positions.astype(F32).reshape(-1, 1) * inv_freq
    cos, sin = jnp.cos(ang), jnp.sin(ang)
    n = cos.shape[0]
    tail = period - lead - rot_dim
    c = jnp.concatenate([jnp.ones((n, lead), F32), cos, cos, jnp.ones((n, tail), F32)], axis=1)
    s1 = jnp.concatenate([jnp.zeros((n, lead), F32), -sin, jnp.zeros((n, half + tail), F32)], axis=1)
    s2 = jnp.concatenate([jnp.zeros((n, lead + half), F32), sin, jnp.zeros((n, tail), F32)], axis=1)
    return tuple(jnp.tile(t, (1, reps)) for t in (c, s1, s2))


def _pad_last(a, lo, hi):
    return jnp.pad(a, [(0, 0)] * (a.ndim - 1) + [(lo, hi)])


def _pad_rows(a, hi):
    return jnp.pad(a, [(0, 0)] * (a.ndim - 2) + [(0, hi), (0, 0)])


def _split_cols(w, sizes):
    out, o = [], 0
    for n in sizes:
        out.append(w[..., o:o + n])
        o += n
    return out


def _in_weights(w_in):
    cb, cc, ch, su, dq, dk, dv, cq, ckv, kr = _split_cols(
        w_in, (CONV_DIM, CONV_DIM, CONV_DIM, SSM_DIM, 3 * DIL_GW, 3 * DIL_GW, 3 * DIL_GW, MLA_Q_RANK, MLA_KV_RANK,
               MLA_ROPE))
    ws = [cb, cc, ch, _pad_last(su, 0, SSM_PAD - SSM_DIM)]
    kinds = ["rows", "rows", "rows", "time_major"]
    for g in range(len(DIL_PATTERNS)):
        ws += [_pad_last(m[..., g * DIL_GW:(g + 1) * DIL_GW], 0, DIL_PLANES * LANE - DIL_GW) for m in (dq, dk, dv)]
        kinds += ["planes_rope_q", "planes_rope_k", "planes"]
    ws += [cq, ckv, _pad_last(kr, MLA_NOPE, MLA_HW - MLA_NOPE - MLA_ROPE)]
    kinds += ["mla_cq", "mla_ckv", "mla_kr"]
    return [w.astype(BF16) for w in ws], kinds


def _mla_weights(w_uq, w_ukv):
    depth = w_uq.shape[0]
    q = w_uq.reshape(depth, MLA_Q_RANK, MLA_HEADS, MLA_QK)
    wq = _pad_last(q, 0, MLA_HW - MLA_QK).reshape(depth, MLA_Q_RANK, MLA_HEADS * MLA_HW)
    kv = w_ukv.reshape(depth, MLA_KV_RANK, MLA_HEADS, MLA_NOPE + MLA_V)
    wk = _pad_last(kv[..., :MLA_NOPE], 0, MLA_HW - MLA_NOPE).reshape(depth, MLA_KV_RANK, -1)
    wv = _pad_last(kv[..., MLA_NOPE:], 0, MLA_HW - MLA_V).reshape(depth, MLA_KV_RANK, -1)
    return wq.astype(BF16), wk.astype(BF16), wv.astype(BF16)


def _out_weights(w_out):
    o_ssm, o_dil, o_mla = CONV_DIM, CONV_DIM + SSM_DIM, CONV_DIM + SSM_DIM + 3 * DIL_GW
    ws = [w_out[:, :o_ssm], _pad_rows(w_out[:, o_ssm:o_dil], SSM_PAD - SSM_DIM)]
    ws += [jnp.concatenate([_pad_rows(w_out[:, o_dil + g * DIL_GW:o_dil + (g + 1) * DIL_GW],
                                      DIL_PLANES * LANE - DIL_GW) for g in range(3)], axis=1)]
    ws += [_pad_rows(w_out[:, o_mla:], MLA_VW - MLA_HEADS * MLA_V)]
    return [w.astype(BF16) for w in ws]


def _router_weights(w_group, b_group, w_router, b_router):
    w = _pad_last(jnp.concatenate([w_group, w_router], axis=-1), 0, ROUTE_W - MOE_GROUPS - N_EXPERTS)
    hi = w.astype(BF16)
    lo = (w - hi.astype(F32)).astype(BF16)
    b = _pad_last(jnp.concatenate([b_group, b_router], axis=-1), 0, ROUTE_W - MOE_GROUPS - N_EXPERTS)
    return jnp.concatenate([hi, lo], axis=-1), b[:, None, :]


def kernel(x, p, positions, norm_mix_g, w_in, conv_w, ssm_lam_re, ssm_lam_im, ssm_log_dt, ssm_b_re, ssm_b_im, ssm_c_re, ssm_c_im, ssm_d, ssm_glu_w, ssm_glu_b, mla_q_norm_g, mla_w_uq, mla_kv_norm_g, mla_w_ukv, w_out, norm_ffn_g, w_group, b_group, w_router, b_router, moe_w_gate, moe_w_up, moe_w_down, norm_ple_g, ple_w_proj, ple_w_gate, final_norm_g):
    batch, seq, _ = x.shape
    n_tok = batch * seq
    depth = w_in.shape[0]
    dil_tabs = _rope_tables(positions, DIL_ROT, 0, DIL_HEAD_DIM, LANE // DIL_HEAD_DIM)
    mla_tabs = _rope_tables(positions, MLA_ROPE, MLA_NOPE, MLA_HW, 1)
    h = x.reshape(n_tok, D_MODEL)
    gains = lambda g: g[:, None, :]
    in_ws, in_kinds = _in_weights(w_in)
    mla_prm = (gains(mla_q_norm_g), gains(mla_kv_norm_g), *_mla_weights(mla_w_uq, mla_w_ukv))
    ssm_prm = _ssm_params(ssm_lam_re, ssm_lam_im, ssm_log_dt, ssm_b_re, ssm_b_im, ssm_c_re, ssm_c_im, ssm_d,
                          ssm_glu_w, ssm_glu_b)
    out_ws = _out_weights(w_out)
    wr, br = _router_weights(w_group, b_group, w_router, b_router)
    ple_wg, ple_wp = ple_w_gate.astype(BF16), ple_w_proj.astype(BF16)
    p_rows = p.reshape(depth, n_tok, PLE_DIM)
    for i in range(depth):
        z = _in_proj(h, gains(norm_mix_g), dil_tabs, mla_tabs, mla_prm, in_ws, in_kinds, batch, seq, i)
        cb, cc, ch, su = z[:4]
        q, k, v = z[13:]
        y_ssm = _ssm_mixer(su, ssm_prm, batch, seq, i)
        outs, lses = [], []
        for g, (window, dil) in enumerate(DIL_PATTERNS):
            assert window // dil == DIL_BLK and (seq // dil) % DIL_BLK == 0
            o, l = _dil_group(z[4 + 3 * g], z[5 + 3 * g], z[6 + 3 * g], dil, batch, seq)
            outs.append(o)
            lses.append(l)
        y_mla = _mla_attn(q, k, v, batch, seq)
        h1, xn, route_i, route_g = _out_proj(
            ((cb, cc, ch), y_ssm, outs, lses, y_mla), h, conv_w, out_ws, gains(norm_ffn_g), wr, br, batch, seq, i)
        sched, slot_asg, n_used, nblk = _moe_plan(route_i, n_tok)
        y2 = _moe_experts(xn, sched, slot_asg, n_used, nblk, moe_w_gate, moe_w_up, moe_w_down, i)
        h = _ple(h1, y2, route_g, p_rows, gains(norm_ple_g), ple_wg, ple_wp, final_norm_g.reshape(1, -1),
                 final=(i == depth - 1), layer=i)
    return h.reshape(batch, seq, D_MODEL)
```

```python
import functools
import math

import jax
import jax.numpy as jnp
from jax import lax
from jax.experimental import pallas as pl
from jax.experimental.pallas import tpu as pltpu

D_MODEL = 2048
PLE_DIM = 256
ROPE_THETA = 500000.0
NORM_EPS = 1e-6

CONV_DIM = 448
CONV_K = 3

SSM_DIM = 448
SSM_GROUP = 16
SSM_GROUPS = SSM_DIM // SSM_GROUP
SSM_STATE = 64
SSM_N = SSM_GROUPS * SSM_STATE
SSM_PAD = 512

DIL_HEAD_DIM = 64
DIL_ROT = DIL_HEAD_DIM // 4
DIL_PATTERNS = ((128, 1), (512, 4), (2048, 16))
DIL_HPG = 3
DIL_GW = DIL_HPG * DIL_HEAD_DIM
DIL_PLANES = 2
DIL_BLK = 128
DIL_UNROLL = 4

MLA_HEADS = 9
MLA_Q_RANK = 384
MLA_KV_RANK = 256
MLA_NOPE = 64
MLA_ROPE = 32
MLA_V = 64
MLA_QK = MLA_NOPE + MLA_ROPE
MLA_HW = 128
MLA_VW = 640

MOE_GROUPS = 8
MOE_EPG = 8
N_EXPERTS = MOE_GROUPS * MOE_EPG
MOE_TOPK = 2
MOE_FF = 512
MOE_BLOCK = 256
ROUTE_W = 128

LANE = 128
PACK_SLABS = D_MODEL // 2 // LANE
ROW_CHUNK = 32
SLOT_SHIFT = 15
NEG_BIG = -1e30

BF16 = jnp.bfloat16
F32 = jnp.float32


def _cparams(sem, vmem_mb):
    return pltpu.CompilerParams(dimension_semantics=sem, vmem_limit_bytes=vmem_mb * 1024 * 1024)


def _rms(x, g):
    ms = jnp.mean(x * x, axis=-1, keepdims=True)
    return (x * lax.rsqrt(ms + NORM_EPS)) * g


def _sigmoid(x):
    return 1.0 / (1.0 + jnp.exp(-x))


def _full(a):
    return pl.BlockSpec(a.shape, lambda *_: (0,) * a.ndim)


def _param(a, layer):
    return pl.BlockSpec((None,) + a.shape[1:], lambda *_: (layer,) + (0,) * (a.ndim - 1),
                        pipeline_mode=pl.Buffered(1))


_HI16 = 0xFFFF0000


def _pack_rows(ref, base, x):
    n = x.shape[0]
    bits = lax.bitcast_convert_type(x, jnp.uint32)
    for c in range(PACK_SLABS):
        lo = bits[:, c * LANE:(c + 1) * LANE] >> 16
        hi = bits[:, D_MODEL // 2 + c * LANE:D_MODEL // 2 + (c + 1) * LANE] & jnp.uint32(_HI16)
        ref[pl.ds(base + c, n, stride=PACK_SLABS), :] = lo | hi


def _unpack_rows(ref, base, n):
    lo, hi = [], []
    for c in range(PACK_SLABS):
        w = ref[pl.ds(base + c, n, stride=PACK_SLABS), :]
        lo.append(lax.bitcast_convert_type(w << 16, F32))
        hi.append(lax.bitcast_convert_type(w & jnp.uint32(_HI16), F32))
    return jnp.concatenate(lo + hi, axis=1)


def _in_proj_kernel(x_ref, g_ref, dc_ref, ds1_ref, ds2_ref, mc_ref, ms1_ref, ms2_ref, gq_ref, gkv_ref,
                    wq_ref, wk_ref, wv_ref, *refs, kinds):
    n_out = len(kinds)
    w_refs, o_refs = refs[:n_out], refs[n_out:]
    xn = _rms(x_ref[...], g_ref[...]).astype(BF16)
    tabs = [jnp.concatenate([t[...]] * DIL_PLANES, axis=1) for t in (dc_ref, ds1_ref, ds2_ref)]
    latent = {}
    outs = iter(o_refs)
    for kind, w_ref in zip(kinds, w_refs):
        res = jnp.dot(xn, w_ref[...], preferred_element_type=F32)
        if kind.startswith("mla"):
            latent[kind] = res
            continue
        o_ref = next(outs)
        if kind.startswith("planes"):
            if kind != "planes":
                res = _rope_lanes(res, *tabs, DIL_ROT // 2, DIL_PLANES * LANE)
            if kind == "planes_rope_q":
                res = res * (1.0 / math.sqrt(DIL_HEAD_DIM))
            for j in range(o_ref.shape[0]):
                o_ref[j] = res[:, j * LANE:(j + 1) * LANE]
        else:
            o_ref[...] = res
    q_ref, k_ref, v_ref = outs
    rope = functools.partial(_rope_lanes, c=mc_ref[...], s1=ms1_ref[...], s2=ms2_ref[...], half=MLA_ROPE // 2,
                             width=MLA_HW)
    qn = _rms(latent["mla_cq"], gq_ref[...]).astype(BF16)
    kvn = _rms(latent["mla_ckv"], gkv_ref[...]).astype(BF16)
    q = jnp.dot(qn, wq_ref[...], preferred_element_type=F32)
    kn = jnp.dot(kvn, wk_ref[...], preferred_element_type=F32)
    lane = lax.broadcasted_iota(jnp.int32, (1, MLA_HEADS * MLA_HW), 1)
    ones_col = (lane % MLA_HW == MLA_V).astype(F32)
    v_ref[...] = (jnp.dot(kvn, wv_ref[...], preferred_element_type=F32) + ones_col).astype(BF16)
    kr = rope(latent["mla_kr"])
    scale = math.log2(math.e) / math.sqrt(MLA_QK)
    for h in range(MLA_HEADS):
        sl = slice(h * MLA_HW, (h + 1) * MLA_HW)
        q_ref[:, sl] = (rope(q[:, sl]) * scale).astype(BF16)
        k_ref[:, sl] = (kn[:, sl] + kr).astype(BF16)


def _in_proj(h, g, dil_tabs, mla_tabs, mla_prm, weights, kinds, batch, seq, layer, tm=256):
    n_s = seq // tm

    def rspec(n):
        return pl.BlockSpec((tm, n), lambda b, i: (b * n_s + i, 0))

    in_specs = [rspec(D_MODEL), _param(g, layer)] + [rspec(LANE)] * 6 + [_param(a, layer) for a in mla_prm]
    in_specs += [_param(w, layer) for w in weights]
    out_shape, out_specs = [], []
    for kind, w in zip(kinds, weights):
        n = w.shape[-1]
        if kind.startswith("mla"):
            continue
        if kind == "time_major":
            out_shape.append(jax.ShapeDtypeStruct((seq, batch * n), F32))
            out_specs.append(pl.BlockSpec((tm, n), lambda b, i: (i, b)))
        elif kind.startswith("planes"):
            out_shape.append(jax.ShapeDtypeStruct((n // LANE, batch * seq, LANE), F32))
            out_specs.append(pl.BlockSpec((n // LANE, tm, LANE), lambda b, i: (0, b * n_s + i, 0)))
        else:
            out_shape.append(jax.ShapeDtypeStruct((batch * seq, n), F32))
            out_specs.append(rspec(n))
    for n in (MLA_HEADS * MLA_HW,) * 3:
        out_shape.append(jax.ShapeDtypeStruct((batch * seq, n), BF16))
        out_specs.append(rspec(n))
    return pl.pallas_call(
        functools.partial(_in_proj_kernel, kinds=tuple(kinds)),
        grid=(batch, n_s), in_specs=in_specs, out_specs=out_specs, out_shape=out_shape,
        compiler_params=_cparams(("parallel", "parallel"), 56), name="in_proj",
    )(h, g, *dil_tabs, *mla_tabs, *mla_prm, *weights)


SSM_NP = SSM_N // LANE
SSM_SCAN_PLANES = 4
_SSM_SPLIT = ((0, 256, 0, 1024), (256, 256, 1024, 768))


def _gelu_tanh(x):
    return 0.5 * x * (1.0 + jnp.tanh(math.sqrt(2.0 / math.pi) * (x + 0.044715 * (x * x * x))))


def _ssm_kernel(u_ref, bf0_ref, bf1_ref, are_ref, aim_ref, cc0_ref, cc1_ref, d_ref, gw_ref, gb_ref, o_ref,
                xs_ref, st_ref, *, batch, tc):
    per_tile = 8 // batch
    npk = SSM_NP // per_tile

    @pl.when(pl.program_id(0) == 0)
    def _():
        st_ref[...] = jnp.zeros_like(st_ref)

    def plane_rows(p, b):
        return p % npk, pl.ds((p // npk) * batch + b, tc, stride=8)

    for b in range(batch):
        for (l0, lw, s0, sw), bf_ref in zip(_SSM_SPLIT, (bf0_ref, bf1_ref)):
            ub = u_ref[:, b * SSM_PAD + l0:b * SSM_PAD + l0 + lw].astype(BF16)
            r = jnp.dot(ub, bf_ref[...], preferred_element_type=F32)
            for q in range(sw // LANE):
                pk, rows = plane_rows(s0 // LANE + q, b)
                xs_ref[pk, rows, :] = r[:, q * LANE:(q + 1) * LANE]
                xs_ref[npk + pk, rows, :] = r[:, sw + q * LANE:sw + (q + 1) * LANE]

    def decay(a_ref, pk):
        return jnp.concatenate([jnp.broadcast_to(a_ref[:, (pk + j * npk) * LANE:(pk + j * npk + 1) * LANE],
                                                 (batch, LANE)) for j in range(per_tile)], axis=0)

    for p0 in range(0, npk, SSM_SCAN_PLANES):
        pks = list(range(p0, min(p0 + SSM_SCAN_PLANES, npk)))
        ar = [decay(are_ref, pk) for pk in pks]
        ai = [decay(aim_ref, pk) for pk in pks]

        def body(t, carry, pks=pks, ar=ar, ai=ai):
            rows = pl.ds(pl.multiple_of(t * 8, 8), 8)
            new = []
            for idx, pk in enumerate(pks):
                sr, si = carry[idx]
                sr, si = (ar[idx] * sr - ai[idx] * si + xs_ref[pk, rows, :],
                          ar[idx] * si + ai[idx] * sr + xs_ref[npk + pk, rows, :])
                xs_ref[pk, rows, :] = sr
                xs_ref[npk + pk, rows, :] = si
                new.append((sr, si))
            return tuple(new)

        init = tuple((st_ref[pk], st_ref[npk + pk]) for pk in pks)
        for pk, (sr, si) in zip(pks, lax.fori_loop(0, tc, body, init, unroll=2)):
            st_ref[pk] = sr
            st_ref[npk + pk] = si

    for b in range(batch):
        ys = []
        for (l0, lw, s0, sw), cc_ref in zip(_SSM_SPLIT, (cc0_ref, cc1_ref)):
            planes = [plane_rows(p, b) for p in range(s0 // LANE, (s0 + sw) // LANE)]
            xb = jnp.concatenate([xs_ref[pk, rows, :] for pk, rows in planes]
                                 + [xs_ref[npk + pk, rows, :] for pk, rows in planes], axis=1)
            ys.append(jnp.dot(xb.astype(BF16), cc_ref[...], preferred_element_type=F32))
        y = jnp.concatenate(ys, axis=1) + d_ref[...] * u_ref[:, b * SSM_PAD:(b + 1) * SSM_PAD]
        g = _gelu_tanh(y)
        o_ref[:, b * SSM_PAD:(b + 1) * SSM_PAD] = g * _sigmoid(
            jnp.dot(g.astype(BF16), gw_ref[...], preferred_element_type=F32) + gb_ref[...])


def _ssm_mixer(su, prm, batch, seq, layer, tc=256):
    spec = pl.BlockSpec((tc, batch * SSM_PAD), lambda c: (c, 0))
    args = (prm["bf0"], prm["bf1"], prm["a_re"], prm["a_im"], prm["cc0"], prm["cc1"], prm["d"], prm["gw"], prm["gb"])
    return pl.pallas_call(
        functools.partial(_ssm_kernel, batch=batch, tc=tc),
        grid=(seq // tc,), in_specs=[spec] + [_param(a, layer) for a in args], out_specs=spec,
        out_shape=jax.ShapeDtypeStruct((seq, batch * SSM_PAD), F32),
        scratch_shapes=[pltpu.VMEM((2 * SSM_NP * batch // 8, tc * 8, LANE), F32),
                        pltpu.VMEM((2 * SSM_NP * batch // 8, 8, LANE), F32)],
        compiler_params=_cparams(("arbitrary",), 48), name="ssm_mixer",
    )(su, *args)


def _ssm_params(lam_re, lam_im, log_dt, b_re, b_im, c_re, c_im, d_skip, glu_w, glu_b):
    depth = lam_re.shape[0]
    dt = jnp.exp(log_dt)[..., None]
    mag = jnp.exp(lam_re * dt)
    a_re = mag * jnp.cos(lam_im * dt)
    a_im = mag * jnp.sin(lam_im * dt)
    nr, ni = a_re - 1.0, a_im
    den = lam_re * lam_re + lam_im * lam_im
    f_re = (nr * lam_re + ni * lam_im) / den
    f_im = (ni * lam_re - nr * lam_im) / den
    bfr = f_re[..., None] * b_re - f_im[..., None] * b_im
    bfi = f_re[..., None] * b_im + f_im[..., None] * b_re
    same_group = (jnp.arange(SSM_DIM)[:, None] // SSM_GROUP == jnp.arange(SSM_N)[None, :] // SSM_STATE).astype(F32)

    def in_blockdiag(m):
        cols = jnp.transpose(m, (0, 3, 1, 2)).reshape(depth, SSM_GROUP, SSM_N)
        return jnp.tile(cols, (1, SSM_GROUPS, 1)) * same_group

    def out_blockdiag(m):
        rows = jnp.transpose(m, (0, 1, 3, 2)).reshape(depth, SSM_N, SSM_GROUP)
        return jnp.tile(rows, (1, 1, SSM_GROUPS)) * same_group.T

    pad = SSM_PAD - SSM_DIM
    b_re_d, b_im_d = (_pad_rows(in_blockdiag(m), pad) for m in (bfr, bfi))
    c_re_d, c_im_d = (_pad_last(out_blockdiag(m), 0, pad) for m in (c_re, -c_im))
    prm = {}
    for c, (l0, lw, s0, sw) in enumerate(_SSM_SPLIT):
        prm[f"bf{c}"] = jnp.concatenate([b_re_d[:, l0:l0 + lw, s0:s0 + sw], b_im_d[:, l0:l0 + lw, s0:s0 + sw]],
                                        axis=2).astype(BF16)
        prm[f"cc{c}"] = jnp.concatenate([c_re_d[:, s0:s0 + sw, l0:l0 + lw], c_im_d[:, s0:s0 + sw, l0:l0 + lw]],
                                        axis=1).astype(BF16)
    return {
        **prm,
        "a_re": a_re.reshape(depth, 1, SSM_N), "a_im": a_im.reshape(depth, 1, SSM_N),
        "d": _pad_last(d_skip, 0, pad)[:, None, :],
        "gw": _pad_last(_pad_rows(glu_w, pad), 0, pad).astype(BF16),
        "gb": _pad_last(glu_b, 0, pad)[:, None, :],
    }


def _rope_lanes(x, c, s1, s2, half, width):
    return x * c + pltpu.roll(x, width - half, 1) * s1 + pltpu.roll(x, half, 1) * s2


def _dil_kernel(q_ref, k_ref, v_ref, o_ref, l_ref, *, dil):
    seq = q_ref.shape[1]
    nb = (seq // dil) // DIL_BLK
    width = DIL_PLANES * LANE
    lane = lax.broadcasted_iota(jnp.int32, (1, width), 1)
    cmask = [(lane // DIL_HEAD_DIM == c).astype(F32) for c in range(DIL_HPG)]
    qi = lax.broadcasted_iota(jnp.int32, (DIL_HPG * DIL_BLK, 2 * DIL_BLK), 0) % DIL_BLK
    ki = lax.broadcasted_iota(jnp.int32, (DIL_HPG * DIL_BLK, 2 * DIL_BLK), 1)
    band = (ki >= qi) & (ki <= qi + DIL_BLK)

    def planes(ref, rows):
        return jnp.concatenate([ref[j, rows, :] for j in range(DIL_PLANES)], axis=1)

    def rows_of(m, blk):
        if dil == 1:
            return pl.ds(pl.multiple_of(blk * DIL_BLK, DIL_BLK), DIL_BLK)
        return pl.ds(blk * DIL_BLK * dil + m, DIL_BLK, stride=dil)

    def blocks(it, carry):
        ids = [it * DIL_UNROLL + u for u in range(DIL_UNROLL)]
        mis = [(idx // nb, idx % nb) for idx in ids]
        rows_q = [rows_of(m, i) for m, i in mis]
        rows_p = [rows_of(m, jnp.maximum(i - 1, 0)) for m, i in mis]
        scores, vcats = [], []
        for rq, rp in zip(rows_q, rows_p):
            q = planes(q_ref, rq)
            kcat = jnp.concatenate([planes(k_ref, rp), planes(k_ref, rq)], axis=0).astype(BF16)
            vcats.append(jnp.concatenate([planes(v_ref, rp), planes(v_ref, rq)], axis=0).astype(BF16))
            qs = jnp.concatenate([q * cm for cm in cmask], axis=0).astype(BF16)
            scores.append(lax.dot_general(qs, kcat, (((1,), (1,)), ((), ())), preferred_element_type=F32))
        soft = []
        for s, (m, i) in zip(scores, mis):
            s = jnp.where(band & ((ki >= DIL_BLK) | (i > 0)), s, NEG_BIG)
            mx = jnp.max(s, axis=-1, keepdims=True)
            p = jnp.exp(s - mx)
            den = jnp.sum(p, axis=-1, keepdims=True)
            soft.append((p.astype(BF16), den, mx + jnp.log(den)))
        for (p, den, lse), vcat, rq in zip(soft, vcats, rows_q):
            o = jnp.dot(p, vcat, preferred_element_type=F32) / den
            out = jnp.zeros((DIL_BLK, width), F32)
            lout = jnp.zeros((DIL_BLK, width), F32)
            for c, cm in enumerate(cmask):
                out = out + cm * o[c * DIL_BLK:(c + 1) * DIL_BLK, :]
                lout = lout + cm * lse[c * DIL_BLK:(c + 1) * DIL_BLK, :]
            for j in range(DIL_PLANES):
                o_ref[j, rq, :] = out[:, j * LANE:(j + 1) * LANE]
                l_ref[j, rq, :] = lout[:, j * LANE:(j + 1) * LANE]
        return carry

    lax.fori_loop(0, dil * nb // DIL_UNROLL, blocks, 0)


def _dil_group(q, k, v, dil, batch, seq):
    spec = pl.BlockSpec((DIL_PLANES, seq, LANE), lambda b: (0, b, 0))
    sds = jax.ShapeDtypeStruct((DIL_PLANES, batch * seq, LANE), F32)
    return pl.pallas_call(
        functools.partial(_dil_kernel, dil=dil), grid=(batch,), in_specs=[spec] * 3,
        out_specs=[spec, spec], out_shape=[sds, sds],
        compiler_params=_cparams(("parallel",), 48), name=f"dil_attn_d{dil}",
    )(q, k, v)


def _mla_attn_kernel(q_ref, k_ref, v_ref, o_ref, *, tq, group):
    qi = pl.program_id(1)
    row = lax.broadcasted_iota(jnp.int32, (tq, tq), 0)
    col = lax.broadcasted_iota(jnp.int32, (tq, tq), 1)
    causal = col <= row

    def heads_out(heads):
        qs = [q_ref[:, h * MLA_HW:(h + 1) * MLA_HW] for h in heads]

        def step(kj, carry, masked):
            rows = pl.ds(pl.multiple_of(kj * tq, tq), tq)
            scores = [lax.dot_general(q, k_ref[rows, h * MLA_HW:(h + 1) * MLA_HW], (((1,), (1,)), ((), ())),
                                      preferred_element_type=F32) for h, q in zip(heads, qs)]
            stats = []
            for s, (m, acc) in zip(scores, carry):
                if masked:
                    s = jnp.where(causal, s, NEG_BIG)
                mn = jnp.maximum(m, jnp.max(s, axis=-1, keepdims=True))
                stats.append((mn, jnp.exp2(m - mn), jnp.exp2(s - mn).astype(BF16)))
            out = []
            for h, (mn, alpha, p), (_, acc) in zip(heads, stats, carry):
                v = v_ref[rows, h * MLA_HW:(h + 1) * MLA_HW]
                out.append((mn, alpha * acc + jnp.dot(p, v, preferred_element_type=F32)))
            return tuple(out)

        init = tuple((jnp.full((tq, 1), NEG_BIG, F32), jnp.zeros((tq, MLA_HW), F32)) for _ in heads)
        carry = lax.fori_loop(0, qi, functools.partial(step, masked=False), init)
        return [acc / acc[:, MLA_V:MLA_V + 1] for _, acc in step(qi, carry, True)]

    for h0 in range(0, MLA_HEADS, group):
        heads = list(range(h0, min(h0 + group, MLA_HEADS)))
        for h, o in zip(heads, heads_out(heads)):
            lo = h * MLA_V
            if h % 2:
                o_ref[:, lo:lo + MLA_V] = pltpu.roll(o, MLA_V, 1)[:, MLA_V:2 * MLA_V]
            else:
                o_ref[:, lo:lo + MLA_V] = o[:, 0:MLA_V]
    if MLA_HEADS % 2:
        o_ref[:, MLA_HEADS * MLA_V:] = jnp.zeros((tq, MLA_VW - MLA_HEADS * MLA_V), F32)


def _mla_attn(q, k, v, batch, seq, tq=512, group=3):
    hw = MLA_HEADS * MLA_HW
    n_q = seq // tq
    return pl.pallas_call(
        functools.partial(_mla_attn_kernel, tq=tq, group=group), grid=(batch, n_q),
        in_specs=[pl.BlockSpec((tq, hw), lambda b, i: (b * n_q + i, 0)),
                  pl.BlockSpec((seq, hw), lambda b, i: (b, 0)),
                  pl.BlockSpec((seq, hw), lambda b, i: (b, 0))],
        out_specs=pl.BlockSpec((tq, MLA_VW), lambda b, i: (b * n_q + i, 0)),
        out_shape=jax.ShapeDtypeStruct((batch * seq, MLA_VW), F32),
        compiler_params=_cparams(("parallel", "arbitrary"), 48), name="mla_attn",
    )(q, k, v)


def _out_proj_kernel(cb_ref, cc_ref, ch_ref, cc_halo_ref, ch_halo_ref, ys_ref, od0_ref, od1_ref, od2_ref,
                     ld0_ref, ld1_ref, ld2_ref, ym_ref, h_ref,
                     cw_ref, wc_ref, ws_ref, wd_ref, wm_ref,
                     gf_ref, wr_ref, br_ref,
                     h1_ref, xn_ref, ri_ref, rg_ref, u_ref):
    def mm(y, w_ref):
        return jnp.dot(y.astype(BF16), w_ref[...], preferred_element_type=F32)

    tm = cb_ref.shape[0]
    u_ref[0:8, :] = jnp.where(pl.program_id(1) > 0, cc_halo_ref[...] * ch_halo_ref[...], 0.0)
    u_ref[8:8 + tm, :] = cc_ref[...] * ch_ref[...]
    y_conv = cb_ref[...] * (cw_ref[2:3, :] * u_ref[8:8 + tm, :] + cw_ref[1:2, :] * u_ref[7:7 + tm, :]
                            + cw_ref[0:1, :] * u_ref[6:6 + tm, :])

    y_dil = [None] * (3 * DIL_PLANES)
    for j in range(DIL_PLANES):
        la, lb, lc = ld0_ref[j], ld1_ref[j], ld2_ref[j]
        mx = jnp.maximum(jnp.maximum(la, lb), lc)
        ea, eb, ec = jnp.exp(la - mx), jnp.exp(lb - mx), jnp.exp(lc - mx)
        inv = 1.0 / (ea + eb + ec)
        for g, (o_ref, e) in enumerate(((od0_ref, ea), (od1_ref, eb), (od2_ref, ec))):
            y_dil[g * DIL_PLANES + j] = o_ref[j] * (e * inv)
    mix = (mm(y_conv, wc_ref) + mm(ys_ref[...], ws_ref) + mm(jnp.concatenate(y_dil, axis=1), wd_ref)
           + mm(ym_ref[...], wm_ref))
    h1 = h_ref[...] + mix
    h1_ref[...] = h1
    hn = _rms(h1, gf_ref[...])
    hi = hn.astype(BF16)
    hi32 = hi.astype(F32)
    lo = (hn - hi32).astype(BF16)
    _pack_rows(xn_ref, 0, hi32)
    r_hi = jnp.dot(hi, wr_ref[...], preferred_element_type=F32)
    r_lo = jnp.dot(lo, wr_ref[:, 0:ROUTE_W], preferred_element_type=F32)
    logits = r_hi[:, 0:ROUTE_W] + r_hi[:, ROUTE_W:2 * ROUTE_W] + r_lo + br_ref[...]
    lane = lax.broadcasted_iota(jnp.int32, logits.shape, 1)
    gl = jnp.where(lane < MOE_GROUPS, logits, NEG_BIG)
    gmax = jnp.max(gl, axis=-1, keepdims=True)
    g_top_p = 1.0 / jnp.sum(jnp.exp(gl - gmax), axis=-1, keepdims=True)
    g_idx = jnp.min(jnp.where(gl == gmax, lane, ROUTE_W), axis=-1, keepdims=True)
    in_group = (lane >= MOE_GROUPS) & (lane < MOE_GROUPS + N_EXPERTS) & (((lane - MOE_GROUPS) >> 3) == g_idx)
    el = jnp.where(in_group, logits, NEG_BIG)
    v1 = jnp.max(el, axis=-1, keepdims=True)
    i1 = jnp.min(jnp.where(el == v1, lane, ROUTE_W), axis=-1, keepdims=True)
    el2 = jnp.where(lane == i1, NEG_BIG, el)
    v2 = jnp.max(el2, axis=-1, keepdims=True)
    i2 = jnp.min(jnp.where(el2 == v2, lane, ROUTE_W), axis=-1, keepdims=True)
    e2 = jnp.exp(v2 - v1)
    w1 = g_top_p / (1.0 + e2)
    w2 = g_top_p * e2 / (1.0 + e2)
    ri_ref[...] = jnp.where(lane == 0, i1 - MOE_GROUPS, jnp.where(lane == 1, i2 - MOE_GROUPS, 0))
    rg_ref[...] = jnp.where(lane == 0, w1, jnp.where(lane == 1, w2, 0.0))


def _out_proj(ys, h, conv_w, ws, gf, wr, br, batch, seq, layer, tm=256):
    rows = batch * seq
    n_s = seq // tm
    (cb, cc, ch), ys_tm, dil_o, dil_l, ym = ys

    def rspec(n):
        return pl.BlockSpec((tm, n), lambda b, i: (b * n_s + i, 0))

    halo = pl.BlockSpec((8, CONV_DIM), lambda b, i: (jnp.maximum((b * n_s + i) * (tm // 8) - 1, 0), 0))
    pspec = pl.BlockSpec((DIL_PLANES, tm, LANE), lambda b, i: (0, b * n_s + i, 0))
    in_specs = [rspec(CONV_DIM)] * 3 + [halo, halo, pl.BlockSpec((tm, SSM_PAD), lambda b, i: (i, b))] + [pspec] * 6
    in_specs += [rspec(MLA_VW), rspec(D_MODEL)]
    in_specs += [_param(a, layer) for a in (conv_w, *ws, gf, wr, br)]
    return pl.pallas_call(
        _out_proj_kernel, grid=(batch, n_s), in_specs=in_specs,
        out_specs=[rspec(D_MODEL), pl.BlockSpec((tm * PACK_SLABS, LANE), lambda b, i: (b * n_s + i, 0)),
                   rspec(ROUTE_W), rspec(ROUTE_W)],
        out_shape=[jax.ShapeDtypeStruct((rows, D_MODEL), F32),
                   jax.ShapeDtypeStruct((rows * PACK_SLABS, LANE), jnp.uint32),
                   jax.ShapeDtypeStruct((rows, ROUTE_W), jnp.int32), jax.ShapeDtypeStruct((rows, ROUTE_W), F32)],
        scratch_shapes=[pltpu.VMEM((tm + 8, CONV_DIM), F32)],
        compiler_params=_cparams(("parallel", "parallel"), 52), name="out_proj_router",
    )(cb, cc, ch, cc, ch, ys_tm, *dil_o, *dil_l, ym, h, conv_w, *ws, gf, wr, br)


def _moe_plan(route_i, n_tok):
    n_asg = n_tok * MOE_TOPK
    nblk = (n_asg + N_EXPERTS * (MOE_BLOCK - 1) + MOE_BLOCK - 1) // MOE_BLOCK
    flat_e = route_i[:, :MOE_TOPK].reshape(-1)
    onehot = (flat_e[:, None] == jnp.arange(N_EXPERTS, dtype=jnp.int32)[None, :]).astype(F32)
    oh3 = onehot.reshape(n_asg // MOE_BLOCK, MOE_BLOCK, N_EXPERTS)
    within = jnp.einsum("ij,bje->bie", jnp.tril(jnp.ones((MOE_BLOCK, MOE_BLOCK), F32)), oh3)
    chunk_tot = within[:, -1, :]
    n_chunk = chunk_tot.shape[0]
    chunk_off = jnp.dot(jnp.tril(jnp.ones((n_chunk, n_chunk), F32), -1), chunk_tot)
    rank = jnp.sum((within + chunk_off[:, None, :]) * oh3, axis=-1).reshape(n_asg).astype(jnp.int32) - 1
    counts = (chunk_off[-1] + chunk_tot[-1]).astype(jnp.int32)
    blocks_of = (counts + MOE_BLOCK - 1) // MOE_BLOCK
    upto = jnp.arange(N_EXPERTS)[:, None] <= jnp.arange(N_EXPERTS)[None, :]
    blk_end = jnp.sum(jnp.where(upto, blocks_of[:, None], 0), axis=0)
    blk_start = blk_end - blocks_of
    dest = jnp.dot(onehot, blk_start.astype(F32)).astype(jnp.int32) * MOE_BLOCK + rank
    in_blk = (dest[:, None] // MOE_BLOCK == jnp.arange(nblk, dtype=jnp.int32)[None, :]).astype(F32)
    at_off = (dest[:, None] % MOE_BLOCK == jnp.arange(MOE_BLOCK, dtype=jnp.int32)[None, :]).astype(F32)
    asg = jnp.arange(n_asg, dtype=jnp.int32)
    a_hi, a_lo = (asg // MOE_BLOCK).astype(F32)[:, None], (asg % MOE_BLOCK).astype(F32)[:, None]
    inv = lambda v: jnp.einsum("ab,ao->bo", in_blk * v, at_off)
    slot_a = jnp.where(inv(1.0) > 0.5, inv(a_hi) * MOE_BLOCK + inv(a_lo), -1.0).astype(jnp.int32).reshape(-1)
    slot = jnp.arange(nblk * MOE_BLOCK, dtype=jnp.int32)
    dump = MOE_TOPK * n_tok + ((slot // MOE_BLOCK) % 2) * MOE_BLOCK + slot % MOE_BLOCK
    assert MOE_TOPK * n_tok + 2 * MOE_BLOCK <= 1 << SLOT_SHIFT
    slot_asg = jnp.where(slot_a >= 0, ((slot_a >> 1) << SLOT_SHIFT) | ((slot_a & 1) * n_tok + (slot_a >> 1)), dump)
    n_used = blk_end[-1].astype(jnp.int32)
    blk = jnp.arange(nblk, dtype=jnp.int32)
    block_expert = jnp.minimum(jnp.sum(blk_end[None, :] <= blk[:, None], axis=1), N_EXPERTS - 1).astype(jnp.int32)
    mine = block_expert[:, None] == jnp.arange(N_EXPERTS, dtype=jnp.int32)[None, :]
    lookup = lambda table: jnp.sum(jnp.where(mine, table[None, :], 0), axis=1)
    run_start, run_end = lookup(blk_start), lookup(blk_end)
    is_first = ((blk == run_start) & (blk < n_used)).astype(jnp.int32)
    used = (blocks_of > 0).astype(jnp.int32)
    nth_used = jnp.sum(jnp.where(upto, used[:, None], 0), axis=0) - used
    w_slot = lookup(nth_used) % 2
    starts_at = (blk_start[None, :] == run_end[:, None]) & (blocks_of[None, :] > 0)
    nxt = jnp.where(run_end < n_used, jnp.sum(jnp.where(starts_at, jnp.arange(N_EXPERTS)[None, :], 0), axis=1), -1)
    rows_left = jnp.clip(lookup(counts) - (blk - run_start) * MOE_BLOCK, 0, MOE_BLOCK)
    sched = jnp.stack([block_expert, is_first, w_slot, nxt, jnp.where(blk < n_used, rows_left, 0)])
    sched = jnp.pad(sched.astype(jnp.int32), ((0, 0), (0, 1)))
    return sched, slot_asg, n_used.reshape(1), nblk


def _moe_kernel(sched_ref, asg_ref, nu_ref, x_hbm, wg_hbm, wu_hbm, wd_hbm, y_hbm,
                xbuf, ybuf, wgf, wuf, wdf, wgb, wub, wdb, gsem, ssem, wsem, *, n_tok, layer):
    i = pl.program_id(0)
    n_used = nu_ref[0]
    slot = i % 2
    buf_rows = MOE_BLOCK * PACK_SLABS
    active = i < n_used
    first = active & (sched_ref[1, i] == 1)
    w_slot = sched_ref[2, i]

    def weights(e, ws):
        return [pltpu.make_async_copy(w_hbm.at[layer, e], buf.at[ws], wsem.at[ws])
                for w_hbm, buf in ((wg_hbm, wgf), (wu_hbm, wuf), (wd_hbm, wdf))]

    def slab_rows(row):
        return pl.ds(pl.multiple_of(row * PACK_SLABS, PACK_SLABS), PACK_SLABS)

    def gather_row(blk, sl, r):
        tok = asg_ref[blk * MOE_BLOCK + r] >> SLOT_SHIFT
        return pltpu.make_async_copy(x_hbm.at[slab_rows(tok), :], xbuf.at[slab_rows(sl * MOE_BLOCK + r), :],
                                     gsem.at[sl])

    def scatter_row(blk, sl, r):
        dst = asg_ref[blk * MOE_BLOCK + r] & ((1 << SLOT_SHIFT) - 1)
        return pltpu.make_async_copy(ybuf.at[slab_rows(sl * MOE_BLOCK + r), :], y_hbm.at[slab_rows(dst), :],
                                     ssem.at[sl])

    def row_chunks(make, blk, sl, wait):
        n_valid = sched_ref[4, blk]
        if wait:
            buf, sem = (xbuf, gsem) if make is gather_row else (ybuf, ssem)
            shift = ROW_CHUNK.bit_length() - 1
            n_slabs = (((n_valid + (ROW_CHUNK - 1)) >> shift) << shift) * PACK_SLABS

            @pl.when(n_valid > 0)
            def _():
                pltpu.make_async_copy(buf.at[pl.ds(0, n_slabs), :], buf.at[pl.ds(0, n_slabs), :], sem.at[sl]).wait()
            return
        for c0 in range(0, MOE_BLOCK, ROW_CHUNK):
            @pl.when(c0 < n_valid)
            def _(c0=c0):
                for r in range(c0, c0 + ROW_CHUNK):
                    make(blk, sl, r).start()

    @pl.when(i == 0)
    def _():
        xbuf[...] = jnp.zeros_like(xbuf)
        ybuf[...] = jnp.zeros_like(ybuf)
        dump = pltpu.make_async_copy(ybuf, y_hbm.at[pl.ds(MOE_TOPK * n_tok * PACK_SLABS, 2 * buf_rows), :], ssem.at[0])
        dump.start()
        dump.wait()

    @pl.when((i == 0) & (n_used > 0))
    def _():
        for cp in weights(sched_ref[0, 0], 0):
            cp.start()
        row_chunks(gather_row, 0, 0, wait=False)

    @pl.when(first)
    def _():
        for cp in weights(sched_ref[0, i], w_slot):
            cp.wait()

    @pl.when(first & (sched_ref[3, i] >= 0))
    def _():
        for cp, queue in zip(weights(sched_ref[3, i], 1 - w_slot), (0, 1, 1)):
            cp.start(priority=queue)

    @pl.when(first)
    def _():
        wgb[...] = wgf[w_slot].astype(BF16)
        wub[...] = wuf[w_slot].astype(BF16)
        wdb[...] = wdf[w_slot].astype(BF16)

    @pl.when(active)
    def _():
        row_chunks(gather_row, i, slot, wait=True)
        row_chunks(gather_row, i + 1, 1 - slot, wait=False)

    @pl.when(active & (i >= 2))
    def _():
        row_chunks(scatter_row, i - 2, slot, wait=True)

    @pl.when(active)
    def _():
        x = _unpack_rows(xbuf, slot * buf_rows, MOE_BLOCK).astype(BF16)
        a = jnp.dot(x, wgb[...], preferred_element_type=F32)
        b = jnp.dot(x, wub[...], preferred_element_type=F32)
        act = (a * _sigmoid(a) * b).astype(BF16)
        y = jnp.dot(act, wdb[...], preferred_element_type=F32)
        _pack_rows(ybuf, slot * buf_rows, y.astype(BF16).astype(F32))
        row_chunks(scatter_row, i, slot, wait=False)

    @pl.when(i == n_used - 1)
    def _():
        row_chunks(scatter_row, i, slot, wait=True)

    @pl.when((i == n_used - 1) & (i >= 1))
    def _():
        row_chunks(scatter_row, i - 1, 1 - slot, wait=True)


def _moe_experts(xn, sched, slot_asg, n_used, nblk, w_gate, w_up, w_down, layer):
    n_tok = xn.shape[0] // PACK_SLABS
    buf_rows = 2 * MOE_BLOCK * PACK_SLABS
    any_spec = pl.BlockSpec(memory_space=pl.ANY)
    grid_spec = pltpu.PrefetchScalarGridSpec(
        num_scalar_prefetch=3, grid=(nblk,),
        in_specs=[any_spec, any_spec, any_spec, any_spec], out_specs=any_spec,
        scratch_shapes=[pltpu.VMEM((buf_rows, LANE), jnp.uint32), pltpu.VMEM((buf_rows, LANE), jnp.uint32),
                        pltpu.VMEM((2, D_MODEL, MOE_FF), F32), pltpu.VMEM((2, D_MODEL, MOE_FF), F32),
                        pltpu.VMEM((2, MOE_FF, D_MODEL), F32),
                        pltpu.VMEM((D_MODEL, MOE_FF), BF16), pltpu.VMEM((D_MODEL, MOE_FF), BF16),
                        pltpu.VMEM((MOE_FF, D_MODEL), BF16),
                        pltpu.SemaphoreType.DMA((2,)), pltpu.SemaphoreType.DMA((2,)), pltpu.SemaphoreType.DMA((2,))])
    return pl.pallas_call(
        functools.partial(_moe_kernel, n_tok=n_tok, layer=layer), grid_spec=grid_spec,
        out_shape=jax.ShapeDtypeStruct(((MOE_TOPK * n_tok + 2 * MOE_BLOCK) * PACK_SLABS, LANE), jnp.uint32),
        compiler_params=_cparams(("arbitrary",), 56), name="moe_experts",
    )(sched, slot_asg, n_used, xn, w_gate, w_up, w_down)


def _ple_kernel(h1_ref, y0_ref, y1_ref, rg_ref, p_ref, gp_ref, wg_ref, wp_ref, gfin_ref, o_ref, *, final):
    rg = rg_ref[...]
    tm = rg.shape[0]
    ffn = rg[:, 0:1] * _unpack_rows(y0_ref, 0, tm) + rg[:, 1:2] * _unpack_rows(y1_ref, 0, tm)
    h2 = h1_ref[...] + ffn
    hn = _rms(h2, gp_ref[...]).astype(BF16)
    gate = _sigmoid(jnp.dot(hn, wg_ref[...], preferred_element_type=F32))
    pe = jnp.dot(p_ref[...].astype(BF16), wp_ref[...], preferred_element_type=F32)
    h3 = h2 + pe * gate
    o_ref[...] = _rms(h3, gfin_ref[...]) if final else h3


def _ple(h1, y2, rg, p, gp, wg, wp, gfin, final, layer, tm=512):
    rows = h1.shape[0]
    n_t = rows // tm

    def rspec(n):
        return pl.BlockSpec((tm, n), lambda i: (i, 0))

    def yspec(k):
        return pl.BlockSpec((tm * PACK_SLABS, LANE), lambda i: (k * n_t + i, 0))

    return pl.pallas_call(
        functools.partial(_ple_kernel, final=final), grid=(n_t,),
        in_specs=[rspec(D_MODEL), yspec(0), yspec(1), rspec(ROUTE_W),
                  pl.BlockSpec((None, tm, PLE_DIM), lambda i: (layer, i, 0)), _param(gp, layer), _param(wg, layer),
                  _param(wp, layer), _full(gfin)],
        out_specs=rspec(D_MODEL), out_shape=jax.ShapeDtypeStruct((rows, D_MODEL), F32),
        compiler_params=_cparams(("parallel",), 48), name="moe_combine_ple",
    )(h1, y2, y2, rg, p, gp, wg, wp, gfin)


def _rope_tables(positions, rot_dim, lead, period, reps):
    half = rot_dim // 2
    inv_freq = ROPE_THETA ** (-jnp.arange(0, rot_dim, 2, dtype=F32) / rot_dim)
    ang = positions.astype(F32).reshape(-1, 1) * inv_freq
    cos, sin = jnp.cos(ang), jnp.sin(ang)
    n = cos.shape[0]
    tail = period - lead - rot_dim
    c = jnp.concatenate([jnp.ones((n, lead), F32), cos, cos, jnp.ones((n, tail), F32)], axis=1)
    s1 = jnp.concatenate([jnp.zeros((n, lead), F32), -sin, jnp.zeros((n, half + tail), F32)], axis=1)
    s2 = jnp.concatenate([jnp.zeros((n, lead + half), F32), sin, jnp.zeros((n, tail), F32)], axis=1)
    return tuple(jnp.tile(t, (1, reps)) for t in (c, s1, s2))


def _pad_last(a, lo, hi):
    return jnp.pad(a, [(0, 0)] * (a.ndim - 1) + [(lo, hi)])


def _pad_rows(a, hi):
    return jnp.pad(a, [(0, 0)] * (a.ndim - 2) + [(0, hi), (0, 0)])


def _split_cols(w, sizes):
    out, o = [], 0
    for n in sizes:
        out.append(w[..., o:o + n])
        o += n
    return out


def _in_weights(w_in):
    cb, cc, ch, su, dq, dk, dv, cq, ckv, kr = _split_cols(
        w_in, (CONV_DIM, CONV_DIM, CONV_DIM, SSM_DIM, 3 * DIL_GW, 3 * DIL_GW, 3 * DIL_GW, MLA_Q_RANK, MLA_KV_RANK,
               MLA_ROPE))
    ws = [cb, cc, ch, _pad_last(su, 0, SSM_PAD - SSM_DIM)]
    kinds = ["rows", "rows", "rows", "time_major"]
    for g in range(len(DIL_PATTERNS)):
        ws += [_pad_last(m[..., g * DIL_GW:(g + 1) * DIL_GW], 0, DIL_PLANES * LANE - DIL_GW) for m in (dq, dk, dv)]
        kinds += ["planes_rope_q", "planes_rope_k", "planes"]
    ws += [cq, ckv, _pad_last(kr, MLA_NOPE, MLA_HW - MLA_NOPE - MLA_ROPE)]
    kinds += ["mla_cq", "mla_ckv", "mla_kr"]
    return [w.astype(BF16) for w in ws], kinds


def _mla_weights(w_uq, w_ukv):
    depth = w_uq.shape[0]
    q = w_uq.reshape(depth, MLA_Q_RANK, MLA_HEADS, MLA_QK)
    wq = _pad_last(q, 0, MLA_HW - MLA_QK).reshape(depth, MLA_Q_RANK, MLA_HEADS * MLA_HW)
    kv = w_ukv.reshape(depth, MLA_KV_RANK, MLA_HEADS, MLA_NOPE + MLA_V)
    wk = _pad_last(kv[..., :MLA_NOPE], 0, MLA_HW - MLA_NOPE).reshape(depth, MLA_KV_RANK, -1)
    wv = _pad_last(kv[..., MLA_NOPE:], 0, MLA_HW - MLA_V).reshape(depth, MLA_KV_RANK, -1)
    return wq.astype(BF16), wk.astype(BF16), wv.astype(BF16)


def _out_weights(w_out):
    o_ssm, o_dil, o_mla = CONV_DIM, CONV_DIM + SSM_DIM, CONV_DIM + SSM_DIM + 3 * DIL_GW
    ws = [w_out[:, :o_ssm], _pad_rows(w_out[:, o_ssm:o_dil], SSM_PAD - SSM_DIM)]
    ws += [jnp.concatenate([_pad_rows(w_out[:, o_dil + g * DIL_GW:o_dil + (g + 1) * DIL_GW],
                                      DIL_PLANES * LANE - DIL_GW) for g in range(3)], axis=1)]
    ws += [_pad_rows(w_out[:, o_mla:], MLA_VW - MLA_HEADS * MLA_V)]
    return [w.astype(BF16) for w in ws]


def _router_weights(w_group, b_group, w_router, b_router):
    w = _pad_last(jnp.concatenate([w_group, w_router], axis=-1), 0, ROUTE_W - MOE_GROUPS - N_EXPERTS)
    hi = w.astype(BF16)
    lo = (w - hi.astype(F32)).astype(BF16)
    b = _pad_last(jnp.concatenate([b_group, b_router], axis=-1), 0, ROUTE_W - MOE_GROUPS - N_EXPERTS)
    return jnp.concatenate([hi, lo], axis=-1), b[:, None, :]


def kernel(x, p, positions, norm_mix_g, w_in, conv_w, ssm_lam_re, ssm_lam_im, ssm_log_dt, ssm_b_re, ssm_b_im, ssm_c_re, ssm_c_im, ssm_d, ssm_glu_w, ssm_glu_b, mla_q_norm_g, mla_w_uq, mla_kv_norm_g, mla_w_ukv, w_out, norm_ffn_g, w_group, b_group, w_router, b_router, moe_w_gate, moe_w_up, moe_w_down, norm_ple_g, ple_w_proj, ple_w_gate, final_norm_g):
    batch, seq, _ = x.shape
    n_tok = batch * seq
    depth = w_in.shape[0]
    dil_tabs = _rope_tables(positions, DIL_ROT, 0, DIL_HEAD_DIM, LANE // DIL_HEAD_DIM)
    mla_tabs = _rope_tables(positions, MLA_ROPE, MLA_NOPE, MLA_HW, 1)
    h = x.reshape(n_tok, D_MODEL)
    gains = lambda g: g[:, None, :]
    in_ws, in_kinds = _in_weights(w_in)
    mla_prm = (gains(mla_q_norm_g), gains(mla_kv_norm_g), *_mla_weights(mla_w_uq, mla_w_ukv))
    ssm_prm = _ssm_params(ssm_lam_re, ssm_lam_im, ssm_log_dt, ssm_b_re, ssm_b_im, ssm_c_re, ssm_c_im, ssm_d,
                          ssm_glu_w, ssm_glu_b)
    out_ws = _out_weights(w_out)
    wr, br = _router_weights(w_group, b_group, w_router, b_router)
    ple_wg, ple_wp = ple_w_gate.astype(BF16), ple_w_proj.astype(BF16)
    p_rows = p.reshape(depth, n_tok, PLE_DIM)
    for i in range(depth):
        z = _in_proj(h, gains(norm_mix_g), dil_tabs, mla_tabs, mla_prm, in_ws, in_kinds, batch, seq, i)
        cb, cc, ch, su = z[:4]
        q, k, v = z[13:]
        y_ssm = _ssm_mixer(su, ssm_prm, batch, seq, i)
        outs, lses = [], []
        for g, (window, dil) in enumerate(DIL_PATTERNS):
            assert window // dil == DIL_BLK and (seq // dil) % DIL_BLK == 0
            o, l = _dil_group(z[4 + 3 * g], z[5 + 3 * g], z[6 + 3 * g], dil, batch, seq)
            outs.append(o)
            lses.append(l)
        y_mla = _mla_attn(q, k, v, batch, seq)
        h1, xn, route_i, route_g = _out_proj(
            ((cb, cc, ch), y_ssm, outs, lses, y_mla), h, conv_w, out_ws, gains(norm_ffn_g), wr, br, batch, seq, i)
        sched, slot_asg, n_used, nblk = _moe_plan(route_i, n_tok)
        y2 = _moe_experts(xn, sched, slot_asg, n_used, nblk, moe_w_gate, moe_w_up, moe_w_down, i)
        h = _ple(h1, y2, route_g, p_rows, gains(norm_ple_g), ple_wg, ple_wp, final_norm_g.reshape(1, -1),
                 final=(i == depth - 1), layer=i)
    return h.reshape(batch, seq, D_MODEL)
```
